```python
import math
import jax, jax.numpy as jnp
from jax import lax
import numpy as np

D_MODEL = 1024
BATCH = 2
SEQ = 16384
DEPTH = 1
DEC_BATCH = 32
DEC_SEQ = 64
PAST_LEN = 1024

CHUNK = 64
D_MIX = D_MODEL
D_S5 = D_MIX // 2
D_CONV = D_MIX - D_S5
S5_GROUP = 16
S5_GROUPS = D_S5 // S5_GROUP
S5_STATE = 64
CONV_W = 3
N_EXPERTS = 32
TOP_K = 4
D_FF = D_MODEL
SWIGLU_LIMIT = 7.0
SWIGLU_ALPHA = 1.702
MOE_BLOCK = 128
LN_EPS = 1e-5
DEEPNORM_ALPHA = (2.0 * DEPTH) ** 0.25
DEEPNORM_BETA = (8.0 * DEPTH) ** -0.25

kernel_name = 's5_shortconv_moe_streaming_encoder'


def layer_norm(x, g, b):
    xf = x.astype(jnp.float32)
    mu = jnp.mean(xf, axis=-1, keepdims=True)
    var = jnp.mean(jnp.square(xf - mu), axis=-1, keepdims=True)
    y = (xf - mu) * lax.rsqrt(var + LN_EPS)
    return (y * g.astype(jnp.float32) + b.astype(jnp.float32)).astype(x.dtype)


def s5_mixer(u, h0, lam_re, lam_im, log_dt, b_re, b_im, c_re, c_im, d_skip, w_glu, b_glu):
    n, l, _ = u.shape
    f32 = jnp.float32
    uf = u.astype(f32).reshape(n, l, S5_GROUPS, S5_GROUP)
    lam = lax.complex(lam_re.astype(f32), lam_im.astype(f32))
    dt = jnp.exp(log_dt.astype(f32))[:, None]
    lam_dt = lam * dt
    lam_bar = jnp.exp(lam_dt)
    bmat = lax.complex(b_re.astype(f32), b_im.astype(f32))
    b_bar = ((lam_bar - 1.0) / lam)[..., None] * bmat
    bu = jnp.einsum('nlgh,gph->nlgp', uf.astype(jnp.complex64), b_bar)
    a = jnp.broadcast_to(lam_bar, bu.shape)

    def combine(e1, e2):
        a1, b1 = e1
        a2, b2 = e2
        return a1 * a2, a2 * b1 + b2

    _, s = lax.associative_scan(combine, (a, bu), axis=1)
    if h0 is not None:
        t = jnp.arange(1, l + 1, dtype=f32)
        decay = jnp.exp(lam_dt[None] * t[:, None, None])
        s = s + decay[None] * h0[:, None]
    cmat = lax.complex(c_re.astype(f32), c_im.astype(f32))
    y = jnp.einsum('nlgp,ghp->nlgh', s, cmat).real
    y = y + d_skip.astype(f32).reshape(S5_GROUPS, S5_GROUP) * uf
    y = y.reshape(n, l, D_S5)
    z = jax.nn.gelu(y)
    z = z * jax.nn.sigmoid(z @ w_glu.astype(f32) + b_glu.astype(f32))
    return z.astype(u.dtype), s[:, -1]


def short_conv_mixer(gate_b, gate_c, v, buf, conv_w):
    xc = gate_c * v
    n, l, d = xc.shape
    if buf is None:
        buf = jnp.zeros((n, CONV_W - 1, d), xc.dtype)
    xp = jnp.concatenate([buf.astype(xc.dtype), xc], axis=1)
    y = conv_w[0] * xp[:, 0:l]
    for k in range(1, CONV_W):
        y = y + conv_w[k] * xp[:, k:k + l]
    return gate_b * y, xp[:, -(CONV_W - 1):]


def moe_ffn(x2d, w_router, b_router, w_gu, b_gu, w_down, b_down):
    t = x2d.shape[0]
    logits = (x2d @ w_router + b_router).astype(jnp.float32)
    top_vals, top_idx = lax.top_k(logits, TOP_K)
    gates = jax.nn.softmax(top_vals, axis=-1)
    n_assign = t * TOP_K
    flat_e = top_idx.reshape(-1).astype(jnp.int32)
    flat_tok = jnp.repeat(jnp.arange(t, dtype=jnp.int32), TOP_K)
    order = jnp.argsort(flat_e)
    sorted_e = flat_e[order]
    sorted_tok = flat_tok[order]
    sorted_gate = gates.reshape(-1)[order]
    counts = jnp.bincount(flat_e, length=N_EXPERTS).astype(jnp.int32)
    padded = (counts + MOE_BLOCK - 1) // MOE_BLOCK * MOE_BLOCK
    start = jnp.cumsum(counts) - counts
    pad_end = jnp.cumsum(padded)
    pad_start = pad_end - padded
    dest = pad_start[sorted_e] + jnp.arange(n_assign, dtype=jnp.int32) - start[sorted_e]
    n_blocks = (n_assign + N_EXPERTS * (MOE_BLOCK - 1) + MOE_BLOCK - 1) // MOE_BLOCK
    row_tok = jnp.zeros((n_blocks * MOE_BLOCK,), jnp.int32).at[dest].set(sorted_tok)
    block_start = jnp.arange(n_blocks, dtype=jnp.int32) * MOE_BLOCK
    block_e = jnp.minimum(jnp.searchsorted(pad_end, block_start, side='right'), N_EXPERTS - 1)
    xb = x2d[row_tok].reshape(n_blocks, MOE_BLOCK, x2d.shape[1])

    def expert_block(args):
        xblk, e = args
        h = xblk @ w_gu[e] + b_gu[e]
        g = jnp.minimum(h[:, :D_FF], SWIGLU_LIMIT)
        up = jnp.clip(h[:, D_FF:], -SWIGLU_LIMIT, SWIGLU_LIMIT)
        act = (up + 1.0) * (g * jax.nn.sigmoid(SWIGLU_ALPHA * g))
        return act @ w_down[e] + b_down[e]

    yb = lax.map(expert_block, (xb, block_e)).reshape(n_blocks * MOE_BLOCK, x2d.shape[1])
    contrib = yb[dest].astype(jnp.float32) * sorted_gate[:, None]
    out = jax.ops.segment_sum(contrib, sorted_tok, num_segments=t)
    return out.astype(x2d.dtype)


def trunk_layer(x, c, h0, conv_buf, w_ada, b_ada, w_in, lam_re, lam_im, log_dt, b_re, b_im, c_re, c_im,
                d_skip, w_glu, b_glu, conv_w, w_out, ln1_g, ln1_b, w_router, b_router, w_gu, b_gu,
                w_down, b_down, ln2_g, ln2_b):
    n, l, d = x.shape
    mod = (jax.nn.silu(c.astype(jnp.float32)) @ w_ada.astype(jnp.float32) + b_ada.astype(jnp.float32)).astype(x.dtype)
    shift1, scale1, gate1, shift2, scale2, gate2 = [m[:, None, :] for m in jnp.split(mod, 6, axis=-1)]
    h = x * (1.0 + scale1) + shift1
    p = h @ w_in
    u = p[..., :D_S5]
    gate_b = p[..., D_S5:D_S5 + D_CONV]
    gate_c = p[..., D_S5 + D_CONV:D_S5 + 2 * D_CONV]
    v = p[..., D_S5 + 2 * D_CONV:]
    y_a, s_last = s5_mixer(u, h0, lam_re, lam_im, log_dt, b_re, b_im, c_re, c_im, d_skip, w_glu, b_glu)
    y_b, conv_new = short_conv_mixer(gate_b, gate_c, v, conv_buf, conv_w)
    mix = jnp.concatenate([y_a.astype(x.dtype), y_b], axis=-1) @ w_out
    x = layer_norm(DEEPNORM_ALPHA * x + (1.0 + gate1) * mix, ln1_g, ln1_b)
    h2 = x * (1.0 + scale2) + shift2
    ffn = moe_ffn(h2.reshape(n * l, d), w_router, b_router, w_gu, b_gu, w_down, b_down).reshape(n, l, d)
    x = layer_norm(DEEPNORM_ALPHA * x + (1.0 + gate2) * ffn, ln2_g, ln2_b)
    return x, s_last.real, s_last.imag, conv_new


def setup_inputs(seed: int = 0) -> dict:
    key = jax.random.key(seed)
    ks = jax.random.split(key, 40)
    f32 = jnp.float32
    nrm = lambda k, shape, s: jax.random.normal(k, shape, f32) * s
    p_in = D_S5 + 3 * D_CONV
    lam_im0 = math.pi * jnp.arange(S5_STATE, dtype=f32)
    return {
        'x_prompt': nrm(ks[0], (BATCH, SEQ, D_MODEL), 1.0),
        'x_sample': nrm(ks[1], (DEC_BATCH, DEC_SEQ, D_MODEL), 1.0),
        'c_prompt': nrm(ks[2], (BATCH, D_MODEL), 1.0),
        'c_sample': nrm(ks[3], (DEC_BATCH, D_MODEL), 1.0),
        'state_s5_re': nrm(ks[4], (DEPTH, DEC_BATCH, S5_GROUPS, S5_STATE), 0.5),
        'state_s5_im': nrm(ks[5], (DEPTH, DEC_BATCH, S5_GROUPS, S5_STATE), 0.5),
        'state_conv': nrm(ks[6], (DEPTH, DEC_BATCH, CONV_W - 1, D_CONV), 1.0),
        'w_ada': nrm(ks[7], (DEPTH, D_MODEL, 6 * D_MODEL), 0.5 * D_MODEL ** -0.5),
        'b_ada': nrm(ks[8], (DEPTH, 6 * D_MODEL), 0.01),
        'w_in': nrm(ks[9], (DEPTH, D_MODEL, p_in), D_MODEL ** -0.5),
        's5_lam_re': -0.5 + nrm(ks[10], (DEPTH, S5_GROUPS, S5_STATE), 0.01),
        's5_lam_im': lam_im0 + nrm(ks[11], (DEPTH, S5_GROUPS, S5_STATE), 0.01),
        's5_log_dt': jax.random.uniform(ks[12], (DEPTH, S5_GROUPS), f32, math.log(1e-3), math.log(1e-1)),
        's5_b_re': nrm(ks[13], (DEPTH, S5_GROUPS, S5_STATE, S5_GROUP), (2.0 * S5_GROUP) ** -0.5),
        's5_b_im': nrm(ks[14], (DEPTH, S5_GROUPS, S5_STATE, S5_GROUP), (2.0 * S5_GROUP) ** -0.5),
        's5_c_re': nrm(ks[15], (DEPTH, S5_GROUPS, S5_GROUP, S5_STATE), (2.0 * S5_STATE) ** -0.5),
        's5_c_im': nrm(ks[16], (DEPTH, S5_GROUPS, S5_GROUP, S5_STATE), (2.0 * S5_STATE) ** -0.5),
        's5_d': nrm(ks[17], (DEPTH, D_S5), 1.0),
        'w_glu': nrm(ks[18], (DEPTH, D_S5, D_S5), D_S5 ** -0.5),
        'b_glu': nrm(ks[19], (DEPTH, D_S5), 0.01),
        'conv_w': nrm(ks[20], (DEPTH, CONV_W, D_CONV), CONV_W ** -0.5),
        'w_out': nrm(ks[21], (DEPTH, D_MIX, D_MODEL), DEEPNORM_BETA * D_MIX ** -0.5),
        'ln1_g': 1.0 + nrm(ks[22], (DEPTH, D_MODEL), 0.01),
        'ln1_b': nrm(ks[23], (DEPTH, D_MODEL), 0.01),
        'w_router': nrm(ks[24], (DEPTH, D_MODEL, N_EXPERTS), D_MODEL ** -0.5),
        'b_router': nrm(ks[25], (DEPTH, N_EXPERTS), 0.01),
        'w_gu': nrm(ks[26], (DEPTH, N_EXPERTS, D_MODEL, 2 * D_FF), D_MODEL ** -0.5),
        'b_gu': nrm(ks[27], (DEPTH, N_EXPERTS, 2 * D_FF), 0.01),
        'w_down': nrm(ks[28], (DEPTH, N_EXPERTS, D_FF, D_MODEL), DEEPNORM_BETA * D_FF ** -0.5),
        'b_down': nrm(ks[29], (DEPTH, N_EXPERTS, D_MODEL), 0.01),
        'ln2_g': 1.0 + nrm(ks[30], (DEPTH, D_MODEL), 0.01),
        'ln2_b': nrm(ks[31], (DEPTH, D_MODEL), 0.01),
    }


def reference(x_prompt, x_sample, c_prompt, c_sample, state_s5_re, state_s5_im, state_conv,
              w_ada, b_ada, w_in, s5_lam_re, s5_lam_im, s5_log_dt, s5_b_re, s5_b_im, s5_c_re, s5_c_im,
              s5_d, w_glu, b_glu, conv_w, w_out, ln1_g, ln1_b, w_router, b_router, w_gu, b_gu,
              w_down, b_down, ln2_g, ln2_b):
    yp, ys = x_prompt, x_sample
    p_re, p_im, p_conv, s_re, s_im, s_conv = [], [], [], [], [], []
    for i in range(DEPTH):
        params = (w_ada[i], b_ada[i], w_in[i], s5_lam_re[i], s5_lam_im[i], s5_log_dt[i], s5_b_re[i],
                  s5_b_im[i], s5_c_re[i], s5_c_im[i], s5_d[i], w_glu[i], b_glu[i], conv_w[i], w_out[i],
                  ln1_g[i], ln1_b[i], w_router[i], b_router[i], w_gu[i], b_gu[i], w_down[i], b_down[i],
                  ln2_g[i], ln2_b[i])
        yp, r, im, cb = trunk_layer(yp, c_prompt, None, None, *params)
        p_re.append(r)
        p_im.append(im)
        p_conv.append(cb)
        h0 = lax.complex(state_s5_re[i].astype(jnp.float32), state_s5_im[i].astype(jnp.float32))
        ys, r, im, cb = trunk_layer(ys, c_sample, h0, state_conv[i], *params)
        s_re.append(r)
        s_im.append(im)
        s_conv.append(cb)
    return (yp, ys, jnp.stack(p_re), jnp.stack(p_im), jnp.stack(p_conv),
            jnp.stack(s_re), jnp.stack(s_im), jnp.stack(s_conv))
```

```python
import functools
import math

import jax
import jax.numpy as jnp
from jax import lax
from jax.experimental import pallas as pl
from jax.experimental.pallas import tpu as pltpu

F32 = jnp.float32
BF16 = jnp.bfloat16

D_MODEL = 1024
DEPTH = 1
D_S5 = 512
D_CONV = 512
S5_GROUP = 16
S5_GROUPS = 32
S5_STATE = 64
CONV_W = 3
N_EXPERTS = 32
TOP_K = 4
D_FF = 1024
SWIGLU_LIMIT = 7.0
SWIGLU_ALPHA = 1.702
LN_EPS = 1e-5
DEEPNORM_ALPHA = (2.0 * DEPTH) ** 0.25

TL = 256
SUBLANES = 8
LANES = 128
N_SLAB = 4
SLAB_W = 1024
HALF = 512
E_BLK = 256
NEG_BIG = -1e30
VMEM_LIMIT = 56 * 1024 * 1024


def _dot(a, b):
    return jnp.dot(a, b, preferred_element_type=F32)


def _layer_norm(x, g, b):
    mu = jnp.mean(x, axis=-1, keepdims=True)
    xc = x - mu
    var = jnp.mean(xc * xc, axis=-1, keepdims=True)
    return xc * lax.rsqrt(var + LN_EPS) * g + b


def _ada_kernel(c_ref, w_ref, b_ref, o_ref):
    c = c_ref[...]
    s = c * jax.nn.sigmoid(c)
    o_ref[...] = _dot(s.astype(BF16), w_ref[...].astype(BF16)) + b_ref[...]


def _ada_call(c_all, w_ada, b_ada):
    rows = c_all.shape[0]
    n_out = w_ada.shape[1]
    tn = 768
    return pl.pallas_call(
        _ada_kernel,
        grid=(n_out // tn,),
        in_specs=[pl.BlockSpec((rows, D_MODEL), lambda i: (0, 0)),
                  pl.BlockSpec((D_MODEL, tn), lambda i: (0, i)),
                  pl.BlockSpec((1, tn), lambda i: (0, i))],
        out_specs=pl.BlockSpec((rows, tn), lambda i: (0, i)),
        out_shape=jax.ShapeDtypeStruct((rows, n_out), F32),
        compiler_params=pltpu.CompilerParams(dimension_semantics=("arbitrary",),
                                             vmem_limit_bytes=VMEM_LIMIT),
        name="ada_mod",
    )(c_all, w_ada, b_ada)


def _mixer_kernel(nseg, seg, carry_tiles, n_alias,
                  x_ref, mod_ref, h0_ref, cbuf_ref, win_ref, bmat_ref, cmat_ref, tab_ref, d_ref,
                  wglu_ref, bglu_ref, convw_ref, wout_ref, g1_ref, b1_ref, wr_ref, br_ref, *rest):
    rest = rest[n_alias:]
    (x1_ref, h2_ref, gate_ref, idx_ref, sout_ref, cout_ref,
     p_ref, bu_ref, xcs_ref, carry_ref) = rest
    j = pl.program_id(1)

    def mod_rows(k):
        if nseg == 1:
            return mod_ref[0, k:k + 1, :]
        return jnp.concatenate(
            [jnp.broadcast_to(mod_ref[s, k:k + 1, :], (seg, D_MODEL)) for s in range(nseg)], axis=0)

    x = x_ref[0]
    h = x * (1.0 + mod_rows(1)) + mod_rows(0)
    p_ref[...] = _dot(h.astype(BF16), win_ref[...])

    for i in range(N_SLAB):
        u_i = p_ref[:, i * LANES:(i + 1) * LANES].astype(BF16)
        bu_ref[i] = _dot(u_i, bmat_ref[i])

    if carry_tiles:
        @pl.when(j == 0)
        def _():
            carry_ref[...] = h0_ref[0]

    for s in range(nseg):
        if carry_tiles:
            init = tuple(carry_ref[i:i + 1, :] for i in range(N_SLAB))
        else:
            init = tuple(h0_ref[s, i:i + 1, :] for i in range(N_SLAB))

        def scan_body(r, carry, s=s):
            new = []
            for i in range(N_SLAB):
                cr = carry[i][:, :HALF]
                ci = carry[i][:, HALF:]
                for half in range(2):
                    row0 = pl.multiple_of(s * seg + r * (2 * SUBLANES) + half * SUBLANES, SUBLANES)
                    blk = bu_ref[i, pl.ds(row0, SUBLANES), :]
                    xr = blk[:, :HALF]
                    xi = blk[:, HALF:]
                    for k, d in enumerate((1, 2, 4)):
                        lr = tab_ref[i, 2 * k]
                        li = tab_ref[i, 2 * k + 1]
                        rr = pltpu.roll(xr, d, 0)
                        ri = pltpu.roll(xi, d, 0)
                        xr, xi = xr + (lr * rr - li * ri), xi + (lr * ri + li * rr)
                    pr = tab_ref[i, 6]
                    pi_ = tab_ref[i, 7]
                    crb = jnp.broadcast_to(cr, (SUBLANES, HALF))
                    cib = jnp.broadcast_to(ci, (SUBLANES, HALF))
                    xr, xi = xr + (pr * crb - pi_ * cib), xi + (pr * cib + pi_ * crb)
                    bu_ref[i, pl.ds(row0, SUBLANES), :] = jnp.concatenate([xr, xi], axis=1)
                    cr = xr[SUBLANES - 1:SUBLANES, :]
                    ci = xi[SUBLANES - 1:SUBLANES, :]
                new.append(jnp.concatenate([cr, ci], axis=1))
            return tuple(new)

        final = lax.fori_loop(0, seg // (2 * SUBLANES), scan_body, init)
        for i in range(N_SLAB):
            sout_ref[s, i:i + 1, :] = final[i]
            if carry_tiles:
                carry_ref[i:i + 1, :] = final[i]

    y = jnp.concatenate([_dot(bu_ref[i].astype(BF16), cmat_ref[i]) for i in range(N_SLAB)], axis=1)
    y = y + d_ref[...] * p_ref[:, :D_S5]
    z = jax.nn.gelu(y)
    z = z * jax.nn.sigmoid(_dot(z.astype(BF16), wglu_ref[...]) + bglu_ref[...])

    xc = p_ref[:, D_S5 + D_CONV:D_S5 + 2 * D_CONV] * p_ref[:, D_S5 + 2 * D_CONV:]
    w0 = convw_ref[0:1, :]
    w1 = convw_ref[1:2, :]
    w2 = convw_ref[2:3, :]
    conv_parts = []
    for s in range(nseg):
        base = s * (seg + SUBLANES)
        xc_s = xc[s * seg:(s + 1) * seg]
        tail = xc_s[seg - 2:seg]
        xcs_ref[pl.ds(base + SUBLANES, seg), :] = xc_s
        if carry_tiles:
            @pl.when(j == 0)
            def _(s=s, base=base):
                xcs_ref[pl.ds(base + SUBLANES - 2, 2), :] = cbuf_ref[s]
        else:
            xcs_ref[pl.ds(base + SUBLANES - 2, 2), :] = cbuf_ref[s]
        xm1 = xcs_ref[pl.ds(base + SUBLANES - 1, seg), :]
        xm2 = xcs_ref[pl.ds(base + SUBLANES - 2, seg), :]
        conv_parts.append(w0 * xm2 + w1 * xm1 + w2 * xc_s)
        cout_ref[s] = tail
        if carry_tiles:
            xcs_ref[pl.ds(base + SUBLANES - 2, 2), :] = tail
    conv = conv_parts[0] if nseg == 1 else jnp.concatenate(conv_parts, axis=0)
    y_b = p_ref[:, D_S5:D_S5 + D_CONV] * conv

    mix = _dot(jnp.concatenate([z, y_b], axis=1).astype(BF16), wout_ref[...])
    x1 = _layer_norm(DEEPNORM_ALPHA * x + (1.0 + mod_rows(2)) * mix, g1_ref[...], b1_ref[...])
    x1_ref[...] = x1
    h2 = x1 * (1.0 + mod_rows(4)) + mod_rows(3)
    h2_ref[...] = h2

    logits = _dot(h2.astype(BF16), wr_ref[...]) + br_ref[...]
    lane = lax.broadcasted_iota(jnp.int32, logits.shape, 1)
    lane_f = lane.astype(F32)
    vals, idxs = [], []
    cur = logits
    for _ in range(TOP_K):
        m = jnp.max(cur, axis=-1, keepdims=True)
        am = jnp.min(jnp.where(cur == m, lane_f, float(LANES)), axis=-1, keepdims=True)
        vals.append(m)
        idxs.append(am.astype(jnp.int32))
        cur = jnp.where(lane_f == am, -jnp.inf, cur)
    exps = [jnp.exp(v - vals[0]) for v in vals]
    inv = 1.0 / (exps[0] + exps[1] + exps[2] + exps[3])
    gate_out = jnp.zeros(logits.shape, F32)
    idx_out = jnp.zeros(logits.shape, jnp.int32)
    for k in range(TOP_K):
        gate_out = jnp.where(lane == k, exps[k] * inv, gate_out)
        idx_out = jnp.where(lane == k, idxs[k], idx_out)
    gate_ref[...] = gate_out
    idx_ref[...] = idx_out


def _mixer_call(x3, mod, h0, cbuf, weights, nseg, seg, tiles_per_group, tile0, n_tok, aliased):
    groups = x3.shape[0]
    nseq = mod.shape[0]
    carry_tiles = nseg == 1
    n_alias = len(aliased)

    def full(a):
        nd = a.ndim
        return pl.BlockSpec(a.shape, lambda g, j, nd=nd: (0,) * nd)

    def tile_map(g, j):
        return (tile0 + g * tiles_per_group + j, 0)

    in_specs = [pl.BlockSpec((1, TL, D_MODEL), lambda g, j: (g, j, 0)),
                pl.BlockSpec((nseg, 6, D_MODEL), lambda g, j: (g, 0, 0)),
                pl.BlockSpec((nseg, N_SLAB, SLAB_W), lambda g, j: (g, 0, 0)),
                pl.BlockSpec((nseg, CONV_W - 1, D_CONV), lambda g, j: (g, 0, 0))]
    in_specs += [full(w) for w in weights]
    in_specs += [pl.BlockSpec(memory_space=pl.ANY)] * n_alias
    out_shape = [jax.ShapeDtypeStruct((n_tok, D_MODEL), F32),
                 jax.ShapeDtypeStruct((n_tok, D_MODEL), F32),
                 jax.ShapeDtypeStruct((n_tok, LANES), F32),
                 jax.ShapeDtypeStruct((n_tok, LANES), jnp.int32),
                 jax.ShapeDtypeStruct((nseq, N_SLAB, SLAB_W), F32),
                 jax.ShapeDtypeStruct((nseq, CONV_W - 1, D_CONV), F32)]
    out_specs = [pl.BlockSpec((TL, D_MODEL), tile_map),
                 pl.BlockSpec((TL, D_MODEL), tile_map),
                 pl.BlockSpec((TL, LANES), tile_map),
                 pl.BlockSpec((TL, LANES), tile_map),
                 pl.BlockSpec((nseg, N_SLAB, SLAB_W), lambda g, j: (g, 0, 0)),
                 pl.BlockSpec((nseg, CONV_W - 1, D_CONV), lambda g, j: (g, 0, 0))]
    n_in = 4 + len(weights)
    aliases = {n_in + k: k for k in range(n_alias)}
    scratch = [pltpu.VMEM((TL, 2 * D_MODEL), F32),
               pltpu.VMEM((N_SLAB, TL, SLAB_W), F32),
               pltpu.VMEM((nseg * (seg + SUBLANES), D_CONV), F32),
               pltpu.VMEM((N_SLAB, SLAB_W), F32)]
    return pl.pallas_call(
        functools.partial(_mixer_kernel, nseg, seg, carry_tiles, n_alias),
        grid=(groups, tiles_per_group),
        in_specs=in_specs, out_specs=out_specs, out_shape=out_shape,
        scratch_shapes=scratch,
        input_output_aliases=aliases,
        compiler_params=pltpu.CompilerParams(dimension_semantics=("arbitrary", "arbitrary"),
                                             vmem_limit_bytes=VMEM_LIMIT),
        name="mixer_prompt" if carry_tiles else "mixer_sample",
    )(x3, mod, h0, cbuf, *weights, *aliased)


def _expert_kernel(be_ref, nu_ref, rt_cur_ref, rt_nxt_ref, h2_hbm, wgu_ref, bgu_ref, wd_ref, bd_ref,
                   yb_ref, buf_ref, sem):
    i = pl.program_id(0)
    n_used = nu_ref[0]

    def issue(rt_ref, slot):
        def body(r8, c):
            for q in range(SUBLANES):
                r = r8 * SUBLANES + q
                tok = rt_ref[0, r]
                pltpu.make_async_copy(h2_hbm.at[pl.ds(tok, 1), :],
                                      buf_ref.at[slot, pl.ds(r, 1), :], sem.at[slot]).start()
            return c
        lax.fori_loop(0, E_BLK // SUBLANES, body, 0)

    @pl.when(i == 0)
    def _():
        issue(rt_cur_ref, 0)

    @pl.when(i + 1 < n_used)
    def _():
        issue(rt_nxt_ref, (i + 1) % 2)

    @pl.when(i < n_used)
    def _():
        slot = i % 2
        pltpu.make_async_copy(h2_hbm.at[pl.ds(0, E_BLK), :], buf_ref.at[slot], sem.at[slot]).wait()
        xb = buf_ref[slot].astype(BF16)
        hgu = _dot(xb, wgu_ref[0]) + bgu_ref[0]
        g = jnp.minimum(hgu[:, :D_FF], SWIGLU_LIMIT)
        up = jnp.clip(hgu[:, D_FF:], -SWIGLU_LIMIT, SWIGLU_LIMIT)
        act = (up + 1.0) * (g * jax.nn.sigmoid(SWIGLU_ALPHA * g))
        yb_ref[...] = _dot(act.astype(BF16), wd_ref[0]) + bd_ref[0]

    @pl.when(i >= n_used)
    def _():
        yb_ref[...] = jnp.zeros(yb_ref.shape, F32)


def _expert_call(block_e, n_used, row_tok, h2, w_gu, b_gu, w_down, b_down):
    nb = block_e.shape[0]
    rt3 = row_tok.reshape(nb, 1, E_BLK)
    grid_spec = pltpu.PrefetchScalarGridSpec(
        num_scalar_prefetch=2,
        grid=(nb,),
        in_specs=[
            pl.BlockSpec((None, 1, E_BLK), lambda i, be, nu: (i, 0, 0), memory_space=pltpu.SMEM),
            pl.BlockSpec((None, 1, E_BLK), lambda i, be, nu: (jnp.minimum(i + 1, nb - 1), 0, 0),
                         memory_space=pltpu.SMEM),
            pl.BlockSpec(memory_space=pl.ANY),
            pl.BlockSpec((1, D_MODEL, 2 * D_FF), lambda i, be, nu: (be[i], 0, 0)),
            pl.BlockSpec((1, 1, 2 * D_FF), lambda i, be, nu: (be[i], 0, 0)),
            pl.BlockSpec((1, D_FF, D_MODEL), lambda i, be, nu: (be[i], 0, 0)),
            pl.BlockSpec((1, 1, D_MODEL), lambda i, be, nu: (be[i], 0, 0)),
        ],
        out_specs=pl.BlockSpec((E_BLK, D_MODEL), lambda i, be, nu: (i, 0)),
        scratch_shapes=[pltpu.VMEM((2, E_BLK, D_MODEL), F32),
                        pltpu.SemaphoreType.DMA((2,))],
    )
    return pl.pallas_call(
        _expert_kernel,
        grid_spec=grid_spec,
        out_shape=jax.ShapeDtypeStruct((nb * E_BLK, D_MODEL), F32),
        compiler_params=pltpu.CompilerParams(dimension_semantics=("arbitrary",),
                                             vmem_limit_bytes=VMEM_LIMIT),
        name="experts",
    )(block_e, n_used, rt3, rt3, h2, w_gu, b_gu, w_down, b_down)


def _combine_kernel(nseg, seg, pos_cur_ref, pos_nxt_ref, x1_ref, gate_ref, mod_ref, g2_ref, b2_ref,
                    yb_hbm, out_ref, buf_ref, sem):
    i = pl.program_id(0)
    n = pl.num_programs(0)
    n_rows = TOP_K * TL

    def issue(pos_ref, slot):
        def body(r8, c):
            for q in range(SUBLANES):
                r = r8 * SUBLANES + q
                src = pos_ref[0, r]
                pltpu.make_async_copy(yb_hbm.at[pl.ds(src, 1), :],
                                      buf_ref.at[slot, pl.ds(r, 1), :], sem.at[slot]).start()
            return c
        lax.fori_loop(0, n_rows // SUBLANES, body, 0)

    @pl.when(i == 0)
    def _():
        issue(pos_cur_ref, 0)

    @pl.when(i + 1 < n)
    def _():
        issue(pos_nxt_ref, (i + 1) % 2)

    slot = i % 2
    pltpu.make_async_copy(yb_hbm.at[pl.ds(0, n_rows), :], buf_ref.at[slot], sem.at[slot]).wait()
    gates = gate_ref[...]
    ffn = gates[:, 0:1] * buf_ref[slot, pl.ds(0, TL), :]
    for k in range(1, TOP_K):
        ffn = ffn + gates[:, k:k + 1] * buf_ref[slot, pl.ds(k * TL, TL), :]
    if nseg == 1:
        gate2 = mod_ref[0, 5:6, :]
    else:
        gate2 = jnp.concatenate(
            [jnp.broadcast_to(mod_ref[s, 5:6, :], (seg, D_MODEL)) for s in range(nseg)], axis=0)
    xa = DEEPNORM_ALPHA * x1_ref[...] + (1.0 + gate2) * ffn
    out_ref[...] = _layer_norm(xa, g2_ref[...], b2_ref[...])


def _combine_call(pos_tiles, x1, gates, mod, ln2_g, ln2_b, yb, nseg, seg, n_tiles, tile0, tiles_per_group):
    pos3 = pos_tiles
    last = tile0 + n_tiles - 1
    in_specs = [
        pl.BlockSpec((None, 1, TOP_K * TL), lambda i: (tile0 + i, 0, 0), memory_space=pltpu.SMEM),
        pl.BlockSpec((None, 1, TOP_K * TL), lambda i: (jnp.minimum(tile0 + i + 1, last), 0, 0),
                     memory_space=pltpu.SMEM),
        pl.BlockSpec((TL, D_MODEL), lambda i: (tile0 + i, 0)),
        pl.BlockSpec((TL, LANES), lambda i: (tile0 + i, 0)),
        pl.BlockSpec((nseg, 6, D_MODEL), lambda i: (i // tiles_per_group, 0, 0)),
        pl.BlockSpec((1, D_MODEL), lambda i: (0, 0)),
        pl.BlockSpec((1, D_MODEL), lambda i: (0, 0)),
        pl.BlockSpec(memory_space=pl.ANY),
    ]
    return pl.pallas_call(
        functools.partial(_combine_kernel, nseg, seg),
        grid=(n_tiles,),
        in_specs=in_specs,
        out_specs=pl.BlockSpec((TL, D_MODEL), lambda i: (i, 0)),
        out_shape=jax.ShapeDtypeStruct((n_tiles * TL, D_MODEL), F32),
        scratch_shapes=[pltpu.VMEM((2, TOP_K * TL, D_MODEL), F32),
                        pltpu.SemaphoreType.DMA((2,))],
        compiler_params=pltpu.CompilerParams(dimension_semantics=("arbitrary",),
                                             vmem_limit_bytes=VMEM_LIMIT),
        name="combine_prompt" if nseg == 1 else "combine_sample",
    )(pos3, pos3, x1, gates, mod, ln2_g, ln2_b, yb)


def _s5_tables(lam_re, lam_im, log_dt, b_re, b_im, c_re, c_im):
    dt = jnp.exp(log_dt.astype(F32))[:, None]
    lam = lax.complex(lam_re.astype(F32), lam_im.astype(F32))
    lam_dt = lam * dt
    lam_bar = jnp.exp(lam_dt)
    b_bar = ((lam_bar - 1.0) / lam)[..., None] * lax.complex(b_re.astype(F32), b_im.astype(F32))
    gl = S5_GROUPS // N_SLAB
    eye = jnp.eye(gl, dtype=F32)

    def b_slab(part):
        a = part.reshape(N_SLAB, gl, S5_STATE, S5_GROUP)
        return jnp.einsum('sgph,gk->sghkp', a, eye).reshape(N_SLAB, gl * S5_GROUP, gl * S5_STATE)

    bmat = jnp.concatenate([b_slab(b_bar.real), b_slab(b_bar.imag)], axis=-1).astype(BF16)

    def c_slab(part):
        a = part.reshape(N_SLAB, gl, S5_GROUP, S5_STATE)
        return jnp.einsum('sghp,gk->sgpkh', a, eye).reshape(N_SLAB, gl * S5_STATE, gl * S5_GROUP)

    cmat = jnp.concatenate([c_slab(c_re.astype(F32)), -c_slab(c_im.astype(F32))], axis=1).astype(BF16)

    row = jnp.arange(SUBLANES, dtype=F32)[:, None, None]

    def power(k):
        return jnp.exp(lam_dt[None] * k)

    tabs = []
    for d in (1, 2, 4):
        pw = power(jnp.full_like(row, float(d)))
        mask = (row >= d).astype(F32)
        tabs += [pw.real * mask, pw.imag * mask]
    pw = power(row + 1.0)
    tabs += [pw.real, pw.imag]
    tab = jnp.stack(tabs, axis=0)
    tab = tab.reshape(8, SUBLANES, N_SLAB, gl * S5_STATE).transpose(2, 0, 1, 3)
    return bmat, cmat, tab


def _state_to_slab(re, im):
    n = re.shape[0]
    return jnp.concatenate([re.reshape(n, N_SLAB, HALF), im.reshape(n, N_SLAB, HALF)], axis=-1).astype(F32)


def _slab_to_state(s):
    n = s.shape[0]
    re = s[:, :, :HALF].reshape(n, S5_GROUPS, S5_STATE)
    im = s[:, :, HALF:].reshape(n, S5_GROUPS, S5_STATE)
    return re, im


def _routing_plan(idx, n_tok):
    n_assign = n_tok * TOP_K
    flat_e = idx.reshape(-1)
    flat_tok = jnp.repeat(jnp.arange(n_tok, dtype=jnp.int32), TOP_K)
    order = jnp.argsort(flat_e)
    sorted_e = flat_e[order]
    sorted_tok = flat_tok[order]
    counts = jnp.bincount(flat_e, length=N_EXPERTS).astype(jnp.int32)
    padded = (counts + E_BLK - 1) // E_BLK * E_BLK
    start = jnp.cumsum(counts) - counts
    pad_end = jnp.cumsum(padded)
    pad_start = pad_end - padded
    dest = pad_start[sorted_e] + jnp.arange(n_assign, dtype=jnp.int32) - start[sorted_e]
    n_blocks = (n_assign + N_EXPERTS * (E_BLK - 1) + E_BLK - 1) // E_BLK
    row_tok = jnp.zeros((n_blocks * E_BLK,), jnp.int32).at[dest].set(sorted_tok)
    pos = jnp.zeros((n_assign,), jnp.int32).at[order].set(dest).reshape(n_tok, TOP_K)
    block_start = jnp.arange(n_blocks, dtype=jnp.int32) * E_BLK
    block_e = jnp.minimum(jnp.searchsorted(pad_end, block_start, side='right'),
                          N_EXPERTS - 1).astype(jnp.int32)
    n_used = (pad_end[-1] // E_BLK).astype(jnp.int32).reshape(1)
    return row_tok, pos, block_e, n_used


def kernel(x_prompt, x_sample, c_prompt, c_sample, state_s5_re, state_s5_im, state_conv, w_ada, b_ada, w_in,
           s5_lam_re, s5_lam_im, s5_log_dt, s5_b_re, s5_b_im, s5_c_re, s5_c_im, s5_d, w_glu, b_glu, conv_w,
           w_out, ln1_g, ln1_b, w_router, b_router, w_gu, b_gu, w_down, b_down, ln2_g, ln2_b):
    assert DEPTH == 1 and w_ada.shape[0] == 1
    n_p, l_p, _ = x_prompt.shape
    n_s, l_s, _ = x_sample.shape
    t_p, t_s = n_p * l_p, n_s * l_s
    n_tok = t_p + t_s
    tiles_p, tiles_s = t_p // TL, t_s // TL
    seq_per_tile = TL // l_s
    assert l_p % TL == 0 and TL % l_s == 0 and n_s % seq_per_tile == 0

    c_all = jnp.concatenate([c_prompt, c_sample], axis=0).astype(F32)
    pad = (-c_all.shape[0]) % SUBLANES
    c_all = jnp.pad(c_all, ((0, pad), (0, 0)))
    mod = _ada_call(c_all, w_ada[0], b_ada[0].reshape(1, -1)).reshape(-1, 6, D_MODEL)
    mod_p, mod_s = mod[:n_p], mod[n_p:n_p + n_s]

    bmat, cmat, tab = _s5_tables(s5_lam_re[0], s5_lam_im[0], s5_log_dt[0], s5_b_re[0], s5_b_im[0],
                                 s5_c_re[0], s5_c_im[0])
    wr = jnp.pad(w_router[0], ((0, 0), (0, LANES - N_EXPERTS))).astype(BF16)
    br = jnp.pad(b_router[0].astype(F32), (0, LANES - N_EXPERTS), constant_values=NEG_BIG).reshape(1, LANES)
    weights = (w_in[0].astype(BF16), bmat, cmat, tab, s5_d[0].reshape(1, D_S5).astype(F32),
               w_glu[0].astype(BF16), b_glu[0].reshape(1, D_S5).astype(F32), conv_w[0].astype(F32),
               w_out[0].astype(BF16), ln1_g[0].reshape(1, D_MODEL).astype(F32),
               ln1_b[0].reshape(1, D_MODEL).astype(F32), wr, br)

    h0_p = jnp.zeros((n_p, N_SLAB, SLAB_W), F32)
    cb_p = jnp.zeros((n_p, CONV_W - 1, D_CONV), F32)
    h0_s = _state_to_slab(state_s5_re[0], state_s5_im[0])
    cb_s = state_conv[0].astype(F32)

    outs_p = _mixer_call(x_prompt, mod_p, h0_p, cb_p, weights, 1, TL, l_p // TL, 0, n_tok, ())
    x1, h2, gates, idx, s_p, conv_p = outs_p
    xs3 = x_sample.reshape(n_s // seq_per_tile, TL, D_MODEL)
    outs_s = _mixer_call(xs3, mod_s, h0_s, cb_s, weights, seq_per_tile, l_s, 1, tiles_p, n_tok,
                         (x1, h2, gates, idx))
    x1, h2, gates, idx, s_s, conv_s = outs_s

    row_tok, pos, block_e, n_used = _routing_plan(idx[:, :TOP_K], n_tok)
    yb = _expert_call(block_e, n_used, row_tok, h2, w_gu[0].astype(BF16),
                      b_gu[0].reshape(N_EXPERTS, 1, 2 * D_FF).astype(F32), w_down[0].astype(BF16),
                      b_down[0].reshape(N_EXPERTS, 1, D_MODEL).astype(F32))
    pos_tiles = pos.reshape(n_tok // TL, TL, TOP_K).transpose(0, 2, 1).reshape(n_tok // TL, 1, TOP_K * TL)
    g2 = ln2_g[0].reshape(1, D_MODEL).astype(F32)
    b2 = ln2_b[0].reshape(1, D_MODEL).astype(F32)
    y_p = _combine_call(pos_tiles, x1, gates, mod_p, g2, b2, yb, 1, TL, tiles_p, 0, l_p // TL)
    y_s = _combine_call(pos_tiles, x1, gates, mod_s, g2, b2, yb, seq_per_tile, l_s, tiles_s, tiles_p, 1)

    p_re, p_im = _slab_to_state(s_p)
    s_re, s_im = _slab_to_state(s_s)
    return (y_p.reshape(n_p, l_p, D_MODEL), y_s.reshape(n_s, l_s, D_MODEL),
            p_re[None], p_im[None], conv_p[None], s_re[None], s_im[None], conv_s[None])
```

```python
import functools
import math

import jax
import jax.numpy as jnp
from jax import lax
from jax.experimental import pallas as pl
from jax.experimental.pallas import tpu as pltpu

F32 = jnp.float32
BF16 = jnp.bfloat16

D_MODEL = 1024
DEPTH = 1
D_S5 = 512
D_CONV = 512
S5_GROUP = 16
S5_GROUPS = 32
S5_STATE = 64
CONV_W = 3
N_EXPERTS = 32
TOP_K = 4
D_FF = 1024
SWIGLU_LIMIT = 7.0
SWIGLU_ALPHA = 1.702
LN_EPS = 1e-5
DEEPNORM_ALPHA = (2.0 * DEPTH) ** 0.25

TL = 256
SUBLANES = 8
LANES = 128
N_SLAB = 4
SLAB_W = 1024
HALF = 512
E_BLK = 256
NEG_BIG = -1e30
VMEM_LIMIT = 56 * 1024 * 1024


def _dot(a, b):
    return jnp.dot(a, b, preferred_element_type=F32)


def _layer_norm(x, g, b):
    mu = jnp.mean(x, axis=-1, keepdims=True)
    xc = x - mu
    var = jnp.mean(xc * xc, axis=-1, keepdims=True)
    return xc * lax.rsqrt(var + LN_EPS) * g + b


def _ada_kernel(c_ref, w_ref, b_ref, o_ref):
    c = c_ref[...]
    s = c * jax.nn.sigmoid(c)
    o_ref[...] = _dot(s.astype(BF16), w_ref[...].astype(BF16)) + b_ref[...]


def _ada_call(c_all, w_ada, b_ada):
    rows = c_all.shape[0]
    n_out = w_ada.shape[1]
    tn = 768
    return pl.pallas_call(
        _ada_kernel,
        grid=(n_out // tn,),
        in_specs=[pl.BlockSpec((rows, D_MODEL), lambda i: (0, 0)),
                  pl.BlockSpec((D_MODEL, tn), lambda i: (0, i)),
                  pl.BlockSpec((1, tn), lambda i: (0, i))],
        out_specs=pl.BlockSpec((rows, tn), lambda i: (0, i)),
        out_shape=jax.ShapeDtypeStruct((rows, n_out), F32),
        compiler_params=pltpu.CompilerParams(dimension_semantics=("arbitrary",),
                                             vmem_limit_bytes=VMEM_LIMIT),
        name="ada_mod",
    )(c_all, w_ada, b_ada)


def _mixer_kernel(nseg, seg, carry_tiles, n_alias, cap_rows,
                  x_ref, mod_ref, h0_ref, cbuf_ref, win_ref, bmat_ref, cmat_ref, tab_ref, d_ref,
                  wglu_ref, bglu_ref, convw_ref, wout_ref, g1_ref, b1_ref, wr_ref, br_ref, cnt0_ref, *rest):
    rest = rest[n_alias:]
    (x1_ref, gate_ref, post_ref, xs_hbm, cnt_out_ref, sout_ref, cout_ref,
     p_ref, bu_ref, xcs_ref, carry_ref, h2s_ref, posv_ref, poss_ref, cnt_ref, sem_s, sem_v) = rest
    g_id = pl.program_id(0)
    j = pl.program_id(1)
    step = g_id * pl.num_programs(1) + j
    is_last = step == pl.num_programs(0) * pl.num_programs(1) - 1
    slot = step % 2

    def mod_rows(k):
        if nseg == 1:
            return mod_ref[0, k:k + 1, :]
        return jnp.concatenate(
            [jnp.broadcast_to(mod_ref[s, k:k + 1, :], (seg, D_MODEL)) for s in range(nseg)], axis=0)

    x = x_ref[0]
    h = x * (1.0 + mod_rows(1)) + mod_rows(0)
    p_ref[...] = _dot(h.astype(BF16), win_ref[...])

    for i in range(N_SLAB):
        u_i = p_ref[:, i * LANES:(i + 1) * LANES].astype(BF16)
        bu_ref[i] = _dot(u_i, bmat_ref[i])

    if carry_tiles:
        @pl.when(j == 0)
        def _():
            carry_ref[...] = h0_ref[0]

    for s in range(nseg):
        if carry_tiles:
            init = tuple(carry_ref[i:i + 1, :] for i in range(N_SLAB))
        else:
            init = tuple(h0_ref[s, i:i + 1, :] for i in range(N_SLAB))

        def scan_body(r, carry, s=s):
            new = []
            for i in range(N_SLAB):
                cr = carry[i][:, :HALF]
                ci = carry[i][:, HALF:]
                for half in range(2):
                    row0 = pl.multiple_of(s * seg + r * (2 * SUBLANES) + half * SUBLANES, SUBLANES)
                    blk = bu_ref[i, pl.ds(row0, SUBLANES), :]
                    xr = blk[:, :HALF]
                    xi = blk[:, HALF:]
                    for k, d in enumerate((1, 2, 4)):
                        lr = tab_ref[i, 2 * k]
                        li = tab_ref[i, 2 * k + 1]
                        rr = pltpu.roll(xr, d, 0)
                        ri = pltpu.roll(xi, d, 0)
                        xr, xi = xr + (lr * rr - li * ri), xi + (lr * ri + li * rr)
                    pr = tab_ref[i, 6]
                    pi_ = tab_ref[i, 7]
                    crb = jnp.broadcast_to(cr, (SUBLANES, HALF))
                    cib = jnp.broadcast_to(ci, (SUBLANES, HALF))
                    xr, xi = xr + (pr * crb - pi_ * cib), xi + (pr * cib + pi_ * crb)
                    bu_ref[i, pl.ds(row0, SUBLANES), :] = jnp.concatenate([xr, xi], axis=1)
                    cr = xr[SUBLANES - 1:SUBLANES, :]
                    ci = xi[SUBLANES - 1:SUBLANES, :]
                new.append(jnp.concatenate([cr, ci], axis=1))
            return tuple(new)

        final = lax.fori_loop(0, seg // (2 * SUBLANES), scan_body, init)
        for i in range(N_SLAB):
            sout_ref[s, i:i + 1, :] = final[i]
            if carry_tiles:
                carry_ref[i:i + 1, :] = final[i]

    y = jnp.concatenate([_dot(bu_ref[i].astype(BF16), cmat_ref[i]) for i in range(N_SLAB)], axis=1)
    y = y + d_ref[...] * p_ref[:, :D_S5]
    z = jax.nn.gelu(y)
    z = z * jax.nn.sigmoid(_dot(z.astype(BF16), wglu_ref[...]) + bglu_ref[...])

    xc = p_ref[:, D_S5 + D_CONV:D_S5 + 2 * D_CONV] * p_ref[:, D_S5 + 2 * D_CONV:]
    w0 = convw_ref[0:1, :]
    w1 = convw_ref[1:2, :]
    w2 = convw_ref[2:3, :]
    conv_parts = []
    for s in range(nseg):
        base = s * (seg + SUBLANES)
        xc_s = xc[s * seg:(s + 1) * seg]
        tail = xc_s[seg - 2:seg]
        xcs_ref[pl.ds(base + SUBLANES, seg), :] = xc_s
        if carry_tiles:
            @pl.when(j == 0)
            def _(s=s, base=base):
                xcs_ref[pl.ds(base + SUBLANES - 2, 2), :] = cbuf_ref[s]
        else:
            xcs_ref[pl.ds(base + SUBLANES - 2, 2), :] = cbuf_ref[s]
        xm1 = xcs_ref[pl.ds(base + SUBLANES - 1, seg), :]
        xm2 = xcs_ref[pl.ds(base + SUBLANES - 2, seg), :]
        conv_parts.append(w0 * xm2 + w1 * xm1 + w2 * xc_s)
        cout_ref[s] = tail
        if carry_tiles:
            xcs_ref[pl.ds(base + SUBLANES - 2, 2), :] = tail
    conv = conv_parts[0] if nseg == 1 else jnp.concatenate(conv_parts, axis=0)
    y_b = p_ref[:, D_S5:D_S5 + D_CONV] * conv

    mix = _dot(jnp.concatenate([z, y_b], axis=1).astype(BF16), wout_ref[...])
    x1 = _layer_norm(DEEPNORM_ALPHA * x + (1.0 + mod_rows(2)) * mix, g1_ref[...], b1_ref[...])
    x1_ref[...] = x1
    h2 = x1 * (1.0 + mod_rows(4)) + mod_rows(3)
    h2s_ref[slot] = h2

    logits = _dot(h2.astype(BF16), wr_ref[...]) + br_ref[...]
    lane = lax.broadcasted_iota(jnp.int32, logits.shape, 1)
    lane_f = lane.astype(F32)
    vals, sels, hots = [], [], []
    cur = logits
    for _ in range(TOP_K):
        m = jnp.max(cur, axis=-1, keepdims=True)
        am = jnp.min(jnp.where(cur == m, lane_f, float(LANES)), axis=-1, keepdims=True)
        hot = lane_f == am
        vals.append(m)
        sels.append(am)
        hots.append(hot)
        cur = jnp.where(hot, -jnp.inf, cur)
    exps = [jnp.exp(v - vals[0]) for v in vals]
    inv = 1.0 / (exps[0] + exps[1] + exps[2] + exps[3])
    gate_out = jnp.zeros(logits.shape, F32)
    for k in range(TOP_K):
        gate_out = jnp.where(lane == k, exps[k] * inv, gate_out)
    gate_ref[...] = gate_out

    @pl.when(step == 0)
    def _():
        cnt_ref[...] = cnt0_ref[...]

    chosen = jnp.zeros(logits.shape, F32)
    for k in range(TOP_K):
        chosen = jnp.where(hots[k], 1.0, chosen)
    r_i = lax.broadcasted_iota(jnp.int32, (TL, TL), 0)
    c_i = lax.broadcasted_iota(jnp.int32, (TL, TL), 1)
    before = jnp.where(c_i < r_i, 1.0, 0.0).astype(BF16)
    rank_base = _dot(before, chosen.astype(BF16)) + cnt_ref[...]
    pos_mat = jnp.zeros(logits.shape, F32)
    for k in range(TOP_K):
        rank_k = jnp.sum(jnp.where(hots[k], rank_base, 0.0), axis=-1, keepdims=True)
        pos_mat = jnp.where(lane == k, sels[k] * float(cap_rows) + rank_k, pos_mat)
    cnt_ref[...] = cnt_ref[...] + jnp.sum(chosen, axis=0, keepdims=True)
    cnt_out_ref[...] = cnt_ref[...]
    pos_t = pos_mat.T[:SUBLANES, :].astype(jnp.int32)
    post_ref[0] = pos_t
    posv_ref[...] = pos_t
    to_smem = pltpu.make_async_copy(posv_ref, poss_ref, sem_v.at[0])
    to_smem.start()
    to_smem.wait()

    def wait_scatter(sl):
        for _ in range(TOP_K):
            pltpu.make_async_copy(h2s_ref.at[sl], xs_hbm.at[pl.ds(0, TL), :], sem_s.at[sl]).wait()

    @pl.when(step > 0)
    def _():
        wait_scatter(1 - slot)

    def scatter_body(t8, c):
        for q in range(SUBLANES):
            t = t8 * SUBLANES + q
            for k in range(TOP_K):
                dst = poss_ref[k, t]
                pltpu.make_async_copy(h2s_ref.at[slot, pl.ds(t, 1), :], xs_hbm.at[pl.ds(dst, 1), :],
                                      sem_s.at[slot]).start()
        return c
    lax.fori_loop(0, TL // SUBLANES, scatter_body, 0)

    @pl.when(is_last)
    def _():
        wait_scatter(slot)


def _mixer_call(x3, mod, h0, cbuf, weights, cnt0, nseg, seg, tiles_per_group, tile0, n_tok, aliased):
    groups = x3.shape[0]
    nseq = mod.shape[0]
    carry_tiles = nseg == 1
    n_alias = len(aliased)
    n_tiles = n_tok // TL
    cap_rows = n_tok

    def full(a):
        nd = a.ndim
        return pl.BlockSpec(a.shape, lambda g, j, nd=nd: (0,) * nd)

    def tile_map(g, j):
        return (tile0 + g * tiles_per_group + j, 0)

    in_specs = [pl.BlockSpec((1, TL, D_MODEL), lambda g, j: (g, j, 0)),
                pl.BlockSpec((nseg, 6, D_MODEL), lambda g, j: (g, 0, 0)),
                pl.BlockSpec((nseg, N_SLAB, SLAB_W), lambda g, j: (g, 0, 0)),
                pl.BlockSpec((nseg, CONV_W - 1, D_CONV), lambda g, j: (g, 0, 0))]
    in_specs += [full(w) for w in weights]
    in_specs += [full(cnt0)]
    in_specs += [pl.BlockSpec(memory_space=pl.ANY)] * n_alias
    out_shape = [jax.ShapeDtypeStruct((n_tok, D_MODEL), F32),
                 jax.ShapeDtypeStruct((n_tok, LANES), F32),
                 jax.ShapeDtypeStruct((n_tiles, SUBLANES, TL), jnp.int32),
                 jax.ShapeDtypeStruct((N_EXPERTS * cap_rows, D_MODEL), F32),
                 jax.ShapeDtypeStruct((1, LANES), F32),
                 jax.ShapeDtypeStruct((nseq, N_SLAB, SLAB_W), F32),
                 jax.ShapeDtypeStruct((nseq, CONV_W - 1, D_CONV), F32)]
    out_specs = [pl.BlockSpec((TL, D_MODEL), tile_map),
                 pl.BlockSpec((TL, LANES), tile_map),
                 pl.BlockSpec((1, SUBLANES, TL), lambda g, j: tile_map(g, j) + (0,)),
                 pl.BlockSpec(memory_space=pl.ANY),
                 pl.BlockSpec((1, LANES), lambda g, j: (0, 0)),
                 pl.BlockSpec((nseg, N_SLAB, SLAB_W), lambda g, j: (g, 0, 0)),
                 pl.BlockSpec((nseg, CONV_W - 1, D_CONV), lambda g, j: (g, 0, 0))]
    n_in = 4 + len(weights) + 1
    aliases = {n_in + k: k for k in range(n_alias)}
    scratch = [pltpu.VMEM((TL, 2 * D_MODEL), F32),
               pltpu.VMEM((N_SLAB, TL, SLAB_W), F32),
               pltpu.VMEM((nseg * (seg + SUBLANES), D_CONV), F32),
               pltpu.VMEM((N_SLAB, SLAB_W), F32),
               pltpu.VMEM((2, TL, D_MODEL), F32),
               pltpu.VMEM((SUBLANES, TL), jnp.int32),
               pltpu.SMEM((SUBLANES, TL), jnp.int32),
               pltpu.VMEM((1, LANES), F32),
               pltpu.SemaphoreType.DMA((2,)),
               pltpu.SemaphoreType.DMA((1,))]
    return pl.pallas_call(
        functools.partial(_mixer_kernel, nseg, seg, carry_tiles, n_alias, cap_rows),
        grid=(groups, tiles_per_group),
        in_specs=in_specs, out_specs=out_specs, out_shape=out_shape,
        scratch_shapes=scratch,
        input_output_aliases=aliases,
        compiler_params=pltpu.CompilerParams(dimension_semantics=("arbitrary", "arbitrary"),
                                             vmem_limit_bytes=VMEM_LIMIT),
        name="mixer_prompt" if carry_tiles else "mixer_sample",
    )(x3, mod, h0, cbuf, *weights, cnt0, *aliased)


W_CAST_ROWS = 128


def _expert_kernel(row_ref, be_ref, nv_ref, nu_ref, xs_ref, wgu_ref, bgu_ref, wd_ref, bd_ref,
                   yb_ref, wgu_bf, wd_bf):
    i = pl.program_id(0)
    n_used = nu_ref[0]
    prev = jnp.maximum(i - 1, 0)
    new_expert = jnp.logical_or(i == 0, be_ref[i] != be_ref[prev])

    @pl.when(jnp.logical_and(new_expert, i < n_used))
    def _():
        def cast_gu(r, c):
            rows = pl.ds(pl.multiple_of(r * W_CAST_ROWS, W_CAST_ROWS), W_CAST_ROWS)
            wgu_bf[rows, :] = wgu_ref[0, rows, :].astype(BF16)
            return c
        lax.fori_loop(0, D_MODEL // W_CAST_ROWS, cast_gu, 0)

        def cast_d(r, c):
            rows = pl.ds(pl.multiple_of(r * W_CAST_ROWS, W_CAST_ROWS), W_CAST_ROWS)
            wd_bf[rows, :] = wd_ref[0, rows, :].astype(BF16)
            return c
        lax.fori_loop(0, D_FF // W_CAST_ROWS, cast_d, 0)

    @pl.when(i < n_used)
    def _():
        row = lax.broadcasted_iota(jnp.int32, (E_BLK, 1), 0)
        xb = jnp.where(row < nv_ref[i], xs_ref[...], 0.0).astype(BF16)
        hgu = _dot(xb, wgu_bf[...]) + bgu_ref[0]
        g = jnp.minimum(hgu[:, :D_FF], SWIGLU_LIMIT)
        up = jnp.clip(hgu[:, D_FF:], -SWIGLU_LIMIT, SWIGLU_LIMIT)
        act = (up + 1.0) * (g * jax.nn.sigmoid(SWIGLU_ALPHA * g))
        yb_ref[...] = _dot(act.astype(BF16), wd_bf[...]) + bd_ref[0]


def _expert_call(blk_row, blk_e, blk_valid, n_used, xs, w_gu, b_gu, w_down, b_down):
    nb = blk_row.shape[0]
    grid_spec = pltpu.PrefetchScalarGridSpec(
        num_scalar_prefetch=4,
        grid=(nb,),
        in_specs=[
            pl.BlockSpec((E_BLK, D_MODEL), lambda i, br, be, nv, nu: (br[i], 0)),
            pl.BlockSpec((1, D_MODEL, 2 * D_FF), lambda i, br, be, nv, nu: (be[i], 0, 0)),
            pl.BlockSpec((1, 1, 2 * D_FF), lambda i, br, be, nv, nu: (be[i], 0, 0)),
            pl.BlockSpec((1, D_FF, D_MODEL), lambda i, br, be, nv, nu: (be[i], 0, 0)),
            pl.BlockSpec((1, 1, D_MODEL), lambda i, br, be, nv, nu: (be[i], 0, 0)),
        ],
        out_specs=pl.BlockSpec((E_BLK, D_MODEL), lambda i, br, be, nv, nu: (br[i], 0)),
        scratch_shapes=[pltpu.VMEM((D_MODEL, 2 * D_FF), BF16),
                        pltpu.VMEM((D_FF, D_MODEL), BF16)],
    )
    return pl.pallas_call(
        _expert_kernel,
        grid_spec=grid_spec,
        out_shape=jax.ShapeDtypeStruct(xs.shape, F32),
        compiler_params=pltpu.CompilerParams(dimension_semantics=("arbitrary",),
                                             vmem_limit_bytes=VMEM_LIMIT),
        name="experts",
    )(blk_row, blk_e, blk_valid, n_used, xs, w_gu, b_gu, w_down, b_down)


def _combine_kernel(nseg, seg, pos_cur_ref, pos_nxt_ref, x1_ref, gate_ref, mod_ref, g2_ref, b2_ref,
                    yb_hbm, out_ref, buf_ref, sem):
    i = pl.program_id(0)
    n = pl.num_programs(0)
    n_rows = TOP_K * TL

    def issue(pos_ref, slot):
        def body(t8, c):
            for q in range(SUBLANES):
                t = t8 * SUBLANES + q
                for k in range(TOP_K):
                    src = pos_ref[k, t]
                    pltpu.make_async_copy(yb_hbm.at[pl.ds(src, 1), :],
                                          buf_ref.at[slot, pl.ds(k * TL + t, 1), :], sem.at[slot]).start()
            return c
        lax.fori_loop(0, TL // SUBLANES, body, 0)

    @pl.when(i == 0)
    def _():
        issue(pos_cur_ref, 0)

    @pl.when(i + 1 < n)
    def _():
        issue(pos_nxt_ref, (i + 1) % 2)

    slot = i % 2
    pltpu.make_async_copy(yb_hbm.at[pl.ds(0, n_rows), :], buf_ref.at[slot], sem.at[slot]).wait()
    gates = gate_ref[...]
    ffn = gates[:, 0:1] * buf_ref[slot, pl.ds(0, TL), :]
    for k in range(1, TOP_K):
        ffn = ffn + gates[:, k:k + 1] * buf_ref[slot, pl.ds(k * TL, TL), :]
    if nseg == 1:
        gate2 = mod_ref[0, 5:6, :]
    else:
        gate2 = jnp.concatenate(
            [jnp.broadcast_to(mod_ref[s, 5:6, :], (seg, D_MODEL)) for s in range(nseg)], axis=0)
    xa = DEEPNORM_ALPHA * x1_ref[...] + (1.0 + gate2) * ffn
    out_ref[...] = _layer_norm(xa, g2_ref[...], b2_ref[...])


def _combine_call(pos_tiles, x1, gates, mod, ln2_g, ln2_b, yb, nseg, seg, n_tiles, tile0, tiles_per_group):
    pos3 = pos_tiles
    last = tile0 + n_tiles - 1
    in_specs = [
        pl.BlockSpec((None, SUBLANES, TL), lambda i: (tile0 + i, 0, 0), memory_space=pltpu.SMEM),
        pl.BlockSpec((None, SUBLANES, TL), lambda i: (jnp.minimum(tile0 + i + 1, last), 0, 0),
                     memory_space=pltpu.SMEM),
        pl.BlockSpec((TL, D_MODEL), lambda i: (tile0 + i, 0)),
        pl.BlockSpec((TL, LANES), lambda i: (tile0 + i, 0)),
        pl.BlockSpec((nseg, 6, D_MODEL), lambda i: (i // tiles_per_group, 0, 0)),
        pl.BlockSpec((1, D_MODEL), lambda i: (0, 0)),
        pl.BlockSpec((1, D_MODEL), lambda i: (0, 0)),
        pl.BlockSpec(memory_space=pl.ANY),
    ]
    return pl.pallas_call(
        functools.partial(_combine_kernel, nseg, seg),
        grid=(n_tiles,),
        in_specs=in_specs,
        out_specs=pl.BlockSpec((TL, D_MODEL), lambda i: (i, 0)),
        out_shape=jax.ShapeDtypeStruct((n_tiles * TL, D_MODEL), F32),
        scratch_shapes=[pltpu.VMEM((2, TOP_K * TL, D_MODEL), F32),
                        pltpu.SemaphoreType.DMA((2,))],
        compiler_params=pltpu.CompilerParams(dimension_semantics=("arbitrary",),
                                             vmem_limit_bytes=VMEM_LIMIT),
        name="combine_prompt" if nseg == 1 else "combine_sample",
    )(pos3, pos3, x1, gates, mod, ln2_g, ln2_b, yb)


def _s5_tables(lam_re, lam_im, log_dt, b_re, b_im, c_re, c_im):
    dt = jnp.exp(log_dt.astype(F32))[:, None]
    lam = lax.complex(lam_re.astype(F32), lam_im.astype(F32))
    lam_dt = lam * dt
    lam_bar = jnp.exp(lam_dt)
    b_bar = ((lam_bar - 1.0) / lam)[..., None] * lax.complex(b_re.astype(F32), b_im.astype(F32))
    gl = S5_GROUPS // N_SLAB
    eye = jnp.eye(gl, dtype=F32)

    def b_slab(part):
        a = part.reshape(N_SLAB, gl, S5_STATE, S5_GROUP)
        return jnp.einsum('sgph,gk->sghkp', a, eye).reshape(N_SLAB, gl * S5_GROUP, gl * S5_STATE)

    bmat = jnp.concatenate([b_slab(b_bar.real), b_slab(b_bar.imag)], axis=-1).astype(BF16)

    def c_slab(part):
        a = part.reshape(N_SLAB, gl, S5_GROUP, S5_STATE)
        return jnp.einsum('sghp,gk->sgpkh', a, eye).reshape(N_SLAB, gl * S5_STATE, gl * S5_GROUP)

    cmat = jnp.concatenate([c_slab(c_re.astype(F32)), -c_slab(c_im.astype(F32))], axis=1).astype(BF16)

    row = jnp.arange(SUBLANES, dtype=F32)[:, None, None]

    def power(k):
        return jnp.exp(lam_dt[None] * k)

    tabs = []
    for d in (1, 2, 4):
        pw = power(jnp.full_like(row, float(d)))
        mask = (row >= d).astype(F32)
        tabs += [pw.real * mask, pw.imag * mask]
    pw = power(row + 1.0)
    tabs += [pw.real, pw.imag]
    tab = jnp.stack(tabs, axis=0)
    tab = tab.reshape(8, SUBLANES, N_SLAB, gl * S5_STATE).transpose(2, 0, 1, 3)
    return bmat, cmat, tab


def _state_to_slab(re, im):
    n = re.shape[0]
    return jnp.concatenate([re.reshape(n, N_SLAB, HALF), im.reshape(n, N_SLAB, HALF)], axis=-1).astype(F32)


def _slab_to_state(s):
    n = s.shape[0]
    re = s[:, :, :HALF].reshape(n, S5_GROUPS, S5_STATE)
    im = s[:, :, HALF:].reshape(n, S5_GROUPS, S5_STATE)
    return re, im


def _block_table(counts, n_tok):
    n_assign = n_tok * TOP_K
    n_blocks = (n_assign + N_EXPERTS * (E_BLK - 1) + E_BLK - 1) // E_BLK
    cap_blocks = n_tok // E_BLK
    nblk = (counts + E_BLK - 1) // E_BLK
    cum = jnp.cumsum(nblk)
    start = cum - nblk
    n_used = cum[-1]
    i = jnp.arange(n_blocks, dtype=jnp.int32)
    ii = jnp.minimum(i, n_used - 1)
    e = jnp.minimum(jnp.searchsorted(cum, ii, side='right'), N_EXPERTS - 1).astype(jnp.int32)
    b = ii - start[e]
    blk_row = (e * cap_blocks + b).astype(jnp.int32)
    valid = jnp.where(i < n_used, jnp.minimum(E_BLK, counts[e] - b * E_BLK), 0).astype(jnp.int32)
    return blk_row, e, valid, n_used.astype(jnp.int32).reshape(1)


def kernel(x_prompt, x_sample, c_prompt, c_sample, state_s5_re, state_s5_im, state_conv, w_ada, b_ada, w_in,
           s5_lam_re, s5_lam_im, s5_log_dt, s5_b_re, s5_b_im, s5_c_re, s5_c_im, s5_d, w_glu, b_glu, conv_w,
           w_out, ln1_g, ln1_b, w_router, b_router, w_gu, b_gu, w_down, b_down, ln2_g, ln2_b):
    assert DEPTH == 1 and w_ada.shape[0] == 1
    n_p, l_p, _ = x_prompt.shape
    n_s, l_s, _ = x_sample.shape
    t_p, t_s = n_p * l_p, n_s * l_s
    n_tok = t_p + t_s
    tiles_p, tiles_s = t_p // TL, t_s // TL
    seq_per_tile = TL // l_s
    assert l_p % TL == 0 and TL % l_s == 0 and n_s % seq_per_tile == 0

    c_all = jnp.concatenate([c_prompt, c_sample], axis=0).astype(F32)
    pad = (-c_all.shape[0]) % SUBLANES
    c_all = jnp.pad(c_all, ((0, pad), (0, 0)))
    mod = _ada_call(c_all, w_ada[0], b_ada[0].reshape(1, -1)).reshape(-1, 6, D_MODEL)
    mod_p, mod_s = mod[:n_p], mod[n_p:n_p + n_s]

    bmat, cmat, tab = _s5_tables(s5_lam_re[0], s5_lam_im[0], s5_log_dt[0], s5_b_re[0], s5_b_im[0],
                                 s5_c_re[0], s5_c_im[0])
    wr = jnp.pad(w_router[0], ((0, 0), (0, LANES - N_EXPERTS))).astype(BF16)
    br = jnp.pad(b_router[0].astype(F32), (0, LANES - N_EXPERTS), constant_values=NEG_BIG).reshape(1, LANES)
    weights = (w_in[0].astype(BF16), bmat, cmat, tab, s5_d[0].reshape(1, D_S5).astype(F32),
               w_glu[0].astype(BF16), b_glu[0].reshape(1, D_S5).astype(F32), conv_w[0].astype(F32),
               w_out[0].astype(BF16), ln1_g[0].reshape(1, D_MODEL).astype(F32),
               ln1_b[0].reshape(1, D_MODEL).astype(F32), wr, br)

    h0_p = jnp.zeros((n_p, N_SLAB, SLAB_W), F32)
    cb_p = jnp.zeros((n_p, CONV_W - 1, D_CONV), F32)
    h0_s = _state_to_slab(state_s5_re[0], state_s5_im[0])
    cb_s = state_conv[0].astype(F32)

    assert n_tok % E_BLK == 0
    cnt0 = jnp.zeros((1, LANES), F32)
    outs_p = _mixer_call(x_prompt, mod_p, h0_p, cb_p, weights, cnt0, 1, TL, l_p // TL, 0, n_tok, ())
    x1, gates, pos_tiles, xs, cnt_p, s_p, conv_p = outs_p
    xs3 = x_sample.reshape(n_s // seq_per_tile, TL, D_MODEL)
    outs_s = _mixer_call(xs3, mod_s, h0_s, cb_s, weights, cnt_p, seq_per_tile, l_s, 1, tiles_p, n_tok,
                         (x1, gates, pos_tiles, xs))
    x1, gates, pos_tiles, xs, cnt, s_s, conv_s = outs_s

    counts = cnt[0, :N_EXPERTS].astype(jnp.int32)
    blk_row, blk_e, blk_valid, n_used = _block_table(counts, n_tok)
    yb = _expert_call(blk_row, blk_e, blk_valid, n_used, xs, w_gu[0].astype(F32),
                      b_gu[0].reshape(N_EXPERTS, 1, 2 * D_FF).astype(F32), w_down[0].astype(F32),
                      b_down[0].reshape(N_EXPERTS, 1, D_MODEL).astype(F32))
    g2 = ln2_g[0].reshape(1, D_MODEL).astype(F32)
    b2 = ln2_b[0].reshape(1, D_MODEL).astype(F32)
    y_p = _combine_call(pos_tiles, x1, gates, mod_p, g2, b2, yb, 1, TL, tiles_p, 0, l_p // TL)
    y_s = _combine_call(pos_tiles, x1, gates, mod_s, g2, b2, yb, seq_per_tile, l_s, tiles_s, tiles_p, 1)

    p_re, p_im = _slab_to_state(s_p)
    s_re, s_im = _slab_to_state(s_s)
    return (y_p.reshape(n_p, l_p, D_MODEL), y_s.reshape(n_s, l_s, D_MODEL),
            p_re[None], p_im[None], conv_p[None], s_re[None], s_im[None], conv_s[None])
```

```python
import functools
import math

import jax
import jax.numpy as jnp
from jax import lax
from jax.experimental import pallas as pl
from jax.experimental.pallas import tpu as pltpu

F32 = jnp.float32
BF16 = jnp.bfloat16

D_MODEL = 1024
DEPTH = 1
D_S5 = 512
D_CONV = 512
S5_GROUP = 16
S5_GROUPS = 32
S5_STATE = 64
CONV_W = 3
N_EXPERTS = 32
TOP_K = 4
D_FF = 1024
SWIGLU_LIMIT = 7.0
SWIGLU_ALPHA = 1.702
LN_EPS = 1e-5
DEEPNORM_ALPHA = (2.0 * DEPTH) ** 0.25

TL = 256
SUBLANES = 8
LANES = 128
N_SLAB = 4
SLAB_W = 1024
HALF = 512
E_BLK = 256
ROW_TILE = D_MODEL // LANES
DMA_THREADS = 2
NEG_BIG = -1e30
VMEM_LIMIT = 56 * 1024 * 1024


def _dot(a, b):
    return jnp.dot(a, b, preferred_element_type=F32)


def _layer_norm(x, g, b):
    mu = jnp.mean(x, axis=-1, keepdims=True)
    xc = x - mu
    var = jnp.mean(xc * xc, axis=-1, keepdims=True)
    return xc * lax.rsqrt(var + LN_EPS) * g + b


def _ada_kernel(c_ref, w_ref, b_ref, o_ref):
    c = c_ref[...]
    s = c * jax.nn.sigmoid(c)
    o_ref[...] = _dot(s.astype(BF16), w_ref[...].astype(BF16)) + b_ref[...]


def _ada_call(c_all, w_ada, b_ada):
    rows = c_all.shape[0]
    n_out = w_ada.shape[1]
    tn = 768
    return pl.pallas_call(
        _ada_kernel,
        grid=(n_out // tn,),
        in_specs=[pl.BlockSpec((rows, D_MODEL), lambda i: (0, 0)),
                  pl.BlockSpec((D_MODEL, tn), lambda i: (0, i)),
                  pl.BlockSpec((1, tn), lambda i: (0, i))],
        out_specs=pl.BlockSpec((rows, tn), lambda i: (0, i)),
        out_shape=jax.ShapeDtypeStruct((rows, n_out), F32),
        compiler_params=pltpu.CompilerParams(dimension_semantics=("arbitrary",),
                                             vmem_limit_bytes=VMEM_LIMIT),
        name="ada_mod",
    )(c_all, w_ada, b_ada)


def _mixer_kernel(nseg, seg, carry_tiles, n_alias, cap_rows,
                  x_ref, mod_ref, h0_ref, cbuf_ref, win_ref, bmat_ref, cmat_ref, tab_ref, d_ref,
                  wglu_ref, bglu_ref, convw_ref, wout_ref, g1_ref, b1_ref, wr_ref, br_ref, cnt0_ref, *rest):
    rest = rest[n_alias:]
    (x1_ref, gate_ref, post_ref, xs_hbm, cnt_out_ref, sout_ref, cout_ref,
     p_ref, bu_ref, xcs_ref, carry_ref, h2s_ref, posv_ref, poss_ref, cnt_ref, sem_s, sem_v) = rest
    g_id = pl.program_id(0)
    j = pl.program_id(1)
    step = g_id * pl.num_programs(1) + j
    is_last = step == pl.num_programs(0) * pl.num_programs(1) - 1
    slot = step % 2

    def mod_rows(k):
        if nseg == 1:
            return mod_ref[0, k:k + 1, :]
        return jnp.concatenate(
            [jnp.broadcast_to(mod_ref[s, k:k + 1, :], (seg, D_MODEL)) for s in range(nseg)], axis=0)

    x = x_ref[0]
    h = x * (1.0 + mod_rows(1)) + mod_rows(0)
    p_ref[...] = _dot(h.astype(BF16), win_ref[...])

    for i in range(N_SLAB):
        u_i = p_ref[:, i * LANES:(i + 1) * LANES].astype(BF16)
        bu_ref[i] = _dot(u_i, bmat_ref[i])

    if carry_tiles:
        @pl.when(j == 0)
        def _():
            carry_ref[...] = h0_ref[0]

    for s in range(nseg):
        if carry_tiles:
            init = tuple(carry_ref[i:i + 1, :] for i in range(N_SLAB))
        else:
            init = tuple(h0_ref[s, i:i + 1, :] for i in range(N_SLAB))

        def scan_body(r, carry, s=s):
            new = []
            for i in range(N_SLAB):
                cr = carry[i][:, :HALF]
                ci = carry[i][:, HALF:]
                for half in range(2):
                    row0 = pl.multiple_of(s * seg + r * (2 * SUBLANES) + half * SUBLANES, SUBLANES)
                    blk = bu_ref[i, pl.ds(row0, SUBLANES), :]
                    xr = blk[:, :HALF]
                    xi = blk[:, HALF:]
                    for k, d in enumerate((1, 2, 4)):
                        lr = tab_ref[i, 2 * k]
                        li = tab_ref[i, 2 * k + 1]
                        rr = pltpu.roll(xr, d, 0)
                        ri = pltpu.roll(xi, d, 0)
                        xr, xi = xr + (lr * rr - li * ri), xi + (lr * ri + li * rr)
                    pr = tab_ref[i, 6]
                    pi_ = tab_ref[i, 7]
                    crb = jnp.broadcast_to(cr, (SUBLANES, HALF))
                    cib = jnp.broadcast_to(ci, (SUBLANES, HALF))
                    xr, xi = xr + (pr * crb - pi_ * cib), xi + (pr * cib + pi_ * crb)
                    bu_ref[i, pl.ds(row0, SUBLANES), :] = jnp.concatenate([xr, xi], axis=1)
                    cr = xr[SUBLANES - 1:SUBLANES, :]
                    ci = xi[SUBLANES - 1:SUBLANES, :]
                new.append(jnp.concatenate([cr, ci], axis=1))
            return tuple(new)

        final = lax.fori_loop(0, seg // (2 * SUBLANES), scan_body, init)
        for i in range(N_SLAB):
            sout_ref[s, i:i + 1, :] = final[i]
            if carry_tiles:
                carry_ref[i:i + 1, :] = final[i]

    y = jnp.concatenate([_dot(bu_ref[i].astype(BF16), cmat_ref[i]) for i in range(N_SLAB)], axis=1)
    y = y + d_ref[...] * p_ref[:, :D_S5]
    z = jax.nn.gelu(y)
    z = z * jax.nn.sigmoid(_dot(z.astype(BF16), wglu_ref[...]) + bglu_ref[...])

    xc = p_ref[:, D_S5 + D_CONV:D_S5 + 2 * D_CONV] * p_ref[:, D_S5 + 2 * D_CONV:]
    w0 = convw_ref[0:1, :]
    w1 = convw_ref[1:2, :]
    w2 = convw_ref[2:3, :]
    conv_parts = []
    for s in range(nseg):
        base = s * (seg + SUBLANES)
        xc_s = xc[s * seg:(s + 1) * seg]
        tail = xc_s[seg - 2:seg]
        xcs_ref[pl.ds(base + SUBLANES, seg), :] = xc_s
        if carry_tiles:
            @pl.when(j == 0)
            def _(s=s, base=base):
                xcs_ref[pl.ds(base + SUBLANES - 2, 2), :] = cbuf_ref[s]
        else:
            xcs_ref[pl.ds(base + SUBLANES - 2, 2), :] = cbuf_ref[s]
        xm1 = xcs_ref[pl.ds(base + SUBLANES - 1, seg), :]
        xm2 = xcs_ref[pl.ds(base + SUBLANES - 2, seg), :]
        conv_parts.append(w0 * xm2 + w1 * xm1 + w2 * xc_s)
        cout_ref[s] = tail
        if carry_tiles:
            xcs_ref[pl.ds(base + SUBLANES - 2, 2), :] = tail
    conv = conv_parts[0] if nseg == 1 else jnp.concatenate(conv_parts, axis=0)
    y_b = p_ref[:, D_S5:D_S5 + D_CONV] * conv

    mix = _dot(jnp.concatenate([z, y_b], axis=1).astype(BF16), wout_ref[...])
    x1 = _layer_norm(DEEPNORM_ALPHA * x + (1.0 + mod_rows(2)) * mix, g1_ref[...], b1_ref[...])
    x1_ref[...] = x1
    h2 = x1 * (1.0 + mod_rows(4)) + mod_rows(3)
    for c in range(ROW_TILE):
        h2s_ref[slot, pl.ds(c, TL, stride=ROW_TILE), :] = h2[:, c * LANES:(c + 1) * LANES]

    logits = _dot(h2.astype(BF16), wr_ref[...]) + br_ref[...]
    lane = lax.broadcasted_iota(jnp.int32, logits.shape, 1)
    lane_f = lane.astype(F32)
    vals, sels, hots = [], [], []
    cur = logits
    for _ in range(TOP_K):
        m = jnp.max(cur, axis=-1, keepdims=True)
        am = jnp.min(jnp.where(cur == m, lane_f, float(LANES)), axis=-1, keepdims=True)
        hot = lane_f == am
        vals.append(m)
        sels.append(am)
        hots.append(hot)
        cur = jnp.where(hot, -jnp.inf, cur)
    exps = [jnp.exp(v - vals[0]) for v in vals]
    inv = 1.0 / (exps[0] + exps[1] + exps[2] + exps[3])
    gate_out = jnp.zeros(logits.shape, F32)
    for k in range(TOP_K):
        gate_out = jnp.where(lane == k, exps[k] * inv, gate_out)
    gate_ref[...] = gate_out

    @pl.when(step == 0)
    def _():
        cnt_ref[...] = cnt0_ref[...]

    chosen = jnp.zeros(logits.shape, F32)
    for k in range(TOP_K):
        chosen = jnp.where(hots[k], 1.0, chosen)
    r_i = lax.broadcasted_iota(jnp.int32, (TL, TL), 0)
    c_i = lax.broadcasted_iota(jnp.int32, (TL, TL), 1)
    before = jnp.where(c_i < r_i, 1.0, 0.0).astype(BF16)
    rank_base = _dot(before, chosen.astype(BF16)) + cnt_ref[...]
    pos_mat = jnp.zeros(logits.shape, F32)
    for k in range(TOP_K):
        rank_k = jnp.sum(jnp.where(hots[k], rank_base, 0.0), axis=-1, keepdims=True)
        pos_mat = jnp.where(lane == k, (sels[k] * float(cap_rows) + rank_k) * float(ROW_TILE), pos_mat)
    cnt_ref[...] = cnt_ref[...] + jnp.sum(chosen, axis=0, keepdims=True)
    cnt_out_ref[...] = cnt_ref[...]
    pos_t = pos_mat.T[:SUBLANES, :].astype(jnp.int32)
    post_ref[0] = pos_t
    posv_ref[...] = pos_t
    to_smem = pltpu.make_async_copy(posv_ref, poss_ref, sem_v.at[0])
    to_smem.start()
    to_smem.wait()

    def wait_scatter(sl):
        for _ in range(TOP_K):
            pltpu.make_async_copy(h2s_ref.at[sl], xs_hbm.at[pl.ds(0, TL * ROW_TILE), :], sem_s.at[sl]).wait()

    @pl.when(step > 0)
    def _():
        wait_scatter(1 - slot)

    def scatter_body(t8, c):
        for q in range(SUBLANES):
            t = t8 * SUBLANES + q
            for k in range(TOP_K):
                dst = pl.multiple_of(poss_ref[k, t], ROW_TILE)
                src = pl.multiple_of(t * ROW_TILE, ROW_TILE)
                pltpu.make_async_copy(h2s_ref.at[slot, pl.ds(src, ROW_TILE), :],
                                      xs_hbm.at[pl.ds(dst, ROW_TILE), :],
                                      sem_s.at[slot]).start(priority=k % DMA_THREADS)
        return c
    lax.fori_loop(0, TL // SUBLANES, scatter_body, 0)

    @pl.when(is_last)
    def _():
        wait_scatter(slot)


def _mixer_call(x3, mod, h0, cbuf, weights, cnt0, nseg, seg, tiles_per_group, tile0, n_tok, aliased):
    groups = x3.shape[0]
    nseq = mod.shape[0]
    carry_tiles = nseg == 1
    n_alias = len(aliased)
    n_tiles = n_tok // TL
    cap_rows = n_tok

    def full(a):
        nd = a.ndim
        return pl.BlockSpec(a.shape, lambda g, j, nd=nd: (0,) * nd)

    def tile_map(g, j):
        return (tile0 + g * tiles_per_group + j, 0)

    in_specs = [pl.BlockSpec((1, TL, D_MODEL), lambda g, j: (g, j, 0)),
                pl.BlockSpec((nseg, 6, D_MODEL), lambda g, j: (g, 0, 0)),
                pl.BlockSpec((nseg, N_SLAB, SLAB_W), lambda g, j: (g, 0, 0)),
                pl.BlockSpec((nseg, CONV_W - 1, D_CONV), lambda g, j: (g, 0, 0))]
    in_specs += [full(w) for w in weights]
    in_specs += [full(cnt0)]
    in_specs += [pl.BlockSpec(memory_space=pl.ANY)] * n_alias
    out_shape = [jax.ShapeDtypeStruct((n_tok, D_MODEL), F32),
                 jax.ShapeDtypeStruct((n_tok, LANES), F32),
                 jax.ShapeDtypeStruct((n_tiles, SUBLANES, TL), jnp.int32),
                 jax.ShapeDtypeStruct((N_EXPERTS * cap_rows * ROW_TILE, LANES), F32),
                 jax.ShapeDtypeStruct((1, LANES), F32),
                 jax.ShapeDtypeStruct((nseq, N_SLAB, SLAB_W), F32),
                 jax.ShapeDtypeStruct((nseq, CONV_W - 1, D_CONV), F32)]
    out_specs = [pl.BlockSpec((TL, D_MODEL), tile_map),
                 pl.BlockSpec((TL, LANES), tile_map),
                 pl.BlockSpec((1, SUBLANES, TL), lambda g, j: tile_map(g, j) + (0,)),
                 pl.BlockSpec(memory_space=pl.ANY),
                 pl.BlockSpec((1, LANES), lambda g, j: (0, 0)),
                 pl.BlockSpec((nseg, N_SLAB, SLAB_W), lambda g, j: (g, 0, 0)),
                 pl.BlockSpec((nseg, CONV_W - 1, D_CONV), lambda g, j: (g, 0, 0))]
    n_in = 4 + len(weights) + 1
    aliases = {n_in + k: k for k in range(n_alias)}
    scratch = [pltpu.VMEM((TL, 2 * D_MODEL), F32),
               pltpu.VMEM((N_SLAB, TL, SLAB_W), F32),
               pltpu.VMEM((nseg * (seg + SUBLANES), D_CONV), F32),
               pltpu.VMEM((N_SLAB, SLAB_W), F32),
               pltpu.VMEM((2, TL * ROW_TILE, LANES), F32),
               pltpu.VMEM((SUBLANES, TL), jnp.int32),
               pltpu.SMEM((SUBLANES, TL), jnp.int32),
               pltpu.VMEM((1, LANES), F32),
               pltpu.SemaphoreType.DMA((2,)),
               pltpu.SemaphoreType.DMA((1,))]
    return pl.pallas_call(
        functools.partial(_mixer_kernel, nseg, seg, carry_tiles, n_alias, cap_rows),
        grid=(groups, tiles_per_group),
        in_specs=in_specs, out_specs=out_specs, out_shape=out_shape,
        scratch_shapes=scratch,
        input_output_aliases=aliases,
        compiler_params=pltpu.CompilerParams(dimension_semantics=("arbitrary", "arbitrary"),
                                             vmem_limit_bytes=VMEM_LIMIT),
        name="mixer_prompt" if carry_tiles else "mixer_sample",
    )(x3, mod, h0, cbuf, *weights, cnt0, *aliased)


W_CAST_ROWS = 128


def _expert_kernel(row_ref, be_ref, nv_ref, nu_ref, xs_ref, wgu_ref, bgu_ref, wd_ref, bd_ref,
                   yb_ref, wgu_bf, wd_bf):
    i = pl.program_id(0)
    n_used = nu_ref[0]
    prev = jnp.maximum(i - 1, 0)
    new_expert = jnp.logical_or(i == 0, be_ref[i] != be_ref[prev])

    @pl.when(jnp.logical_and(new_expert, i < n_used))
    def _():
        def cast_gu(r, c):
            rows = pl.ds(pl.multiple_of(r * W_CAST_ROWS, W_CAST_ROWS), W_CAST_ROWS)
            wgu_bf[rows, :] = wgu_ref[0, rows, :].astype(BF16)
            return c
        lax.fori_loop(0, D_MODEL // W_CAST_ROWS, cast_gu, 0)

        def cast_d(r, c):
            rows = pl.ds(pl.multiple_of(r * W_CAST_ROWS, W_CAST_ROWS), W_CAST_ROWS)
            wd_bf[rows, :] = wd_ref[0, rows, :].astype(BF16)
            return c
        lax.fori_loop(0, D_FF // W_CAST_ROWS, cast_d, 0)

    @pl.when(i < n_used)
    def _():
        row = lax.broadcasted_iota(jnp.int32, (E_BLK, 1), 0)
        xt = jnp.concatenate([xs_ref[pl.ds(c, E_BLK, stride=ROW_TILE), :] for c in range(ROW_TILE)], axis=1)
        xb = jnp.where(row < nv_ref[i], xt, 0.0).astype(BF16)
        hgu = _dot(xb, wgu_bf[...]) + bgu_ref[0]
        g = jnp.minimum(hgu[:, :D_FF], SWIGLU_LIMIT)
        up = jnp.clip(hgu[:, D_FF:], -SWIGLU_LIMIT, SWIGLU_LIMIT)
        act = (up + 1.0) * (g * jax.nn.sigmoid(SWIGLU_ALPHA * g))
        y = _dot(act.astype(BF16), wd_bf[...]) + bd_ref[0]
        for c in range(ROW_TILE):
            yb_ref[pl.ds(c, E_BLK, stride=ROW_TILE), :] = y[:, c * LANES:(c + 1) * LANES]


def _expert_call(blk_row, blk_e, blk_valid, n_used, xs, w_gu, b_gu, w_down, b_down):
    nb = blk_row.shape[0]
    grid_spec = pltpu.PrefetchScalarGridSpec(
        num_scalar_prefetch=4,
        grid=(nb,),
        in_specs=[
            pl.BlockSpec((E_BLK * ROW_TILE, LANES), lambda i, br, be, nv, nu: (br[i], 0)),
            pl.BlockSpec((1, D_MODEL, 2 * D_FF), lambda i, br, be, nv, nu: (be[i], 0, 0)),
            pl.BlockSpec((1, 1, 2 * D_FF), lambda i, br, be, nv, nu: (be[i], 0, 0)),
            pl.BlockSpec((1, D_FF, D_MODEL), lambda i, br, be, nv, nu: (be[i], 0, 0)),
            pl.BlockSpec((1, 1, D_MODEL), lambda i, br, be, nv, nu: (be[i], 0, 0)),
        ],
        out_specs=pl.BlockSpec((E_BLK * ROW_TILE, LANES), lambda i, br, be, nv, nu: (br[i], 0)),
        scratch_shapes=[pltpu.VMEM((D_MODEL, 2 * D_FF), BF16),
                        pltpu.VMEM((D_FF, D_MODEL), BF16)],
    )
    return pl.pallas_call(
        _expert_kernel,
        grid_spec=grid_spec,
        out_shape=jax.ShapeDtypeStruct(xs.shape, F32),
        compiler_params=pltpu.CompilerParams(dimension_semantics=("arbitrary",),
                                             vmem_limit_bytes=VMEM_LIMIT),
        name="experts",
    )(blk_row, blk_e, blk_valid, n_used, xs, w_gu, b_gu, w_down, b_down)


def _combine_kernel(nseg, seg, pos_cur_ref, pos_nxt_ref, x1_ref, gate_ref, mod_ref, g2_ref, b2_ref,
                    yb_hbm, out_ref, buf_ref, sem):
    i = pl.program_id(0)
    n = pl.num_programs(0)
    n_rows = TOP_K * TL

    def issue(pos_ref, slot):
        def body(t8, c):
            for q in range(SUBLANES):
                t = t8 * SUBLANES + q
                for k in range(TOP_K):
                    src = pl.multiple_of(pos_ref[k, t], ROW_TILE)
                    dst = pl.multiple_of((k * TL + t) * ROW_TILE, ROW_TILE)
                    pltpu.make_async_copy(yb_hbm.at[pl.ds(src, ROW_TILE), :],
                                          buf_ref.at[slot, pl.ds(dst, ROW_TILE), :],
                                          sem.at[slot]).start(priority=k % DMA_THREADS)
            return c
        lax.fori_loop(0, TL // SUBLANES, body, 0)

    @pl.when(i == 0)
    def _():
        issue(pos_cur_ref, 0)

    @pl.when(i + 1 < n)
    def _():
        issue(pos_nxt_ref, (i + 1) % 2)

    slot = i % 2
    pltpu.make_async_copy(yb_hbm.at[pl.ds(0, n_rows * ROW_TILE), :], buf_ref.at[slot], sem.at[slot]).wait()
    gates = gate_ref[...]

    def rows_of(k):
        return jnp.concatenate(
            [buf_ref[slot, pl.ds(k * TL * ROW_TILE + c, TL, stride=ROW_TILE), :] for c in range(ROW_TILE)],
            axis=1)

    ffn = gates[:, 0:1] * rows_of(0)
    for k in range(1, TOP_K):
        ffn = ffn + gates[:, k:k + 1] * rows_of(k)
    if nseg == 1:
        gate2 = mod_ref[0, 5:6, :]
    else:
        gate2 = jnp.concatenate(
            [jnp.broadcast_to(mod_ref[s, 5:6, :], (seg, D_MODEL)) for s in range(nseg)], axis=0)
    xa = DEEPNORM_ALPHA * x1_ref[...] + (1.0 + gate2) * ffn
    out_ref[...] = _layer_norm(xa, g2_ref[...], b2_ref[...])


def _combine_call(pos_tiles, x1, gates, mod, ln2_g, ln2_b, yb, nseg, seg, n_tiles, tile0, tiles_per_group):
    pos3 = pos_tiles
    last = tile0 + n_tiles - 1
    in_specs = [
        pl.BlockSpec((None, SUBLANES, TL), lambda i: (tile0 + i, 0, 0), memory_space=pltpu.SMEM),
        pl.BlockSpec((None, SUBLANES, TL), lambda i: (jnp.minimum(tile0 + i + 1, last), 0, 0),
                     memory_space=pltpu.SMEM),
        pl.BlockSpec((TL, D_MODEL), lambda i: (tile0 + i, 0)),
        pl.BlockSpec((TL, LANES), lambda i: (tile0 + i, 0)),
        pl.BlockSpec((nseg, 6, D_MODEL), lambda i: (i // tiles_per_group, 0, 0)),
        pl.BlockSpec((1, D_MODEL), lambda i: (0, 0)),
        pl.BlockSpec((1, D_MODEL), lambda i: (0, 0)),
        pl.BlockSpec(memory_space=pl.ANY),
    ]
    return pl.pallas_call(
        functools.partial(_combine_kernel, nseg, seg),
        grid=(n_tiles,),
        in_specs=in_specs,
        out_specs=pl.BlockSpec((TL, D_MODEL), lambda i: (i, 0)),
        out_shape=jax.ShapeDtypeStruct((n_tiles * TL, D_MODEL), F32),
        scratch_shapes=[pltpu.VMEM((2, TOP_K * TL * ROW_TILE, LANES), F32),
                        pltpu.SemaphoreType.DMA((2,))],
        compiler_params=pltpu.CompilerParams(dimension_semantics=("arbitrary",),
                                             vmem_limit_bytes=VMEM_LIMIT),
        name="combine_prompt" if nseg == 1 else "combine_sample",
    )(pos3, pos3, x1, gates, mod, ln2_g, ln2_b, yb)


def _s5_tables(lam_re, lam_im, log_dt, b_re, b_im, c_re, c_im):
    dt = jnp.exp(log_dt.astype(F32))[:, None]
    lam = lax.complex(lam_re.astype(F32), lam_im.astype(F32))
    lam_dt = lam * dt
    lam_bar = jnp.exp(lam_dt)
    b_bar = ((lam_bar - 1.0) / lam)[..., None] * lax.complex(b_re.astype(F32), b_im.astype(F32))
    gl = S5_GROUPS // N_SLAB
    eye = jnp.eye(gl, dtype=F32)

    def b_slab(part):
        a = part.reshape(N_SLAB, gl, S5_STATE, S5_GROUP)
        return jnp.einsum('sgph,gk->sghkp', a, eye).reshape(N_SLAB, gl * S5_GROUP, gl * S5_STATE)

    bmat = jnp.concatenate([b_slab(b_bar.real), b_slab(b_bar.imag)], axis=-1).astype(BF16)

    def c_slab(part):
        a = part.reshape(N_SLAB, gl, S5_GROUP, S5_STATE)
        return jnp.einsum('sghp,gk->sgpkh', a, eye).reshape(N_SLAB, gl * S5_STATE, gl * S5_GROUP)

    cmat = jnp.concatenate([c_slab(c_re.astype(F32)), -c_slab(c_im.astype(F32))], axis=1).astype(BF16)

    row = jnp.arange(SUBLANES, dtype=F32)[:, None, None]

    def power(k):
        return jnp.exp(lam_dt[None] * k)

    tabs = []
    for d in (1, 2, 4):
        pw = power(jnp.full_like(row, float(d)))
        mask = (row >= d).astype(F32)
        tabs += [pw.real * mask, pw.imag * mask]
    pw = power(row + 1.0)
    tabs += [pw.real, pw.imag]
    tab = jnp.stack(tabs, axis=0)
    tab = tab.reshape(8, SUBLANES, N_SLAB, gl * S5_STATE).transpose(2, 0, 1, 3)
    return bmat, cmat, tab


def _state_to_slab(re, im):
    n = re.shape[0]
    return jnp.concatenate([re.reshape(n, N_SLAB, HALF), im.reshape(n, N_SLAB, HALF)], axis=-1).astype(F32)


def _slab_to_state(s):
    n = s.shape[0]
    re = s[:, :, :HALF].reshape(n, S5_GROUPS, S5_STATE)
    im = s[:, :, HALF:].reshape(n, S5_GROUPS, S5_STATE)
    return re, im


def _block_table(counts, n_tok):
    n_assign = n_tok * TOP_K
    n_blocks = (n_assign + N_EXPERTS * (E_BLK - 1) + E_BLK - 1) // E_BLK
    cap_blocks = n_tok // E_BLK
    nblk = (counts + E_BLK - 1) // E_BLK
    cum = jnp.cumsum(nblk)
    start = cum - nblk
    n_used = cum[-1]
    i = jnp.arange(n_blocks, dtype=jnp.int32)
    ii = jnp.maximum(jnp.minimum(i, n_used - 1), 0)
    e = jnp.minimum(jnp.sum((cum[None, :] <= ii[:, None]).astype(jnp.int32), axis=1), N_EXPERTS - 1)
    b = ii - start[e]
    blk_row = (e * cap_blocks + b).astype(jnp.int32)
    valid = jnp.where(i < n_used, jnp.minimum(E_BLK, counts[e] - b * E_BLK), 0).astype(jnp.int32)
    return blk_row, e, valid, n_used.astype(jnp.int32).reshape(1)


def kernel(x_prompt, x_sample, c_prompt, c_sample, state_s5_re, state_s5_im, state_conv, w_ada, b_ada, w_in,
           s5_lam_re, s5_lam_im, s5_log_dt, s5_b_re, s5_b_im, s5_c_re, s5_c_im, s5_d, w_glu, b_glu, conv_w,
           w_out, ln1_g, ln1_b, w_router, b_router, w_gu, b_gu, w_down, b_down, ln2_g, ln2_b):
    assert DEPTH == 1 and w_ada.shape[0] == 1
    n_p, l_p, _ = x_prompt.shape
    n_s, l_s, _ = x_sample.shape
    t_p, t_s = n_p * l_p, n_s * l_s
    n_tok = t_p + t_s
    tiles_p, tiles_s = t_p // TL, t_s // TL
    seq_per_tile = TL // l_s
    assert l_p % TL == 0 and TL % l_s == 0 and n_s % seq_per_tile == 0

    c_all = jnp.concatenate([c_prompt, c_sample], axis=0).astype(F32)
    pad = (-c_all.shape[0]) % SUBLANES
    c_all = jnp.pad(c_all, ((0, pad), (0, 0)))
    mod = _ada_call(c_all, w_ada[0], b_ada[0].reshape(1, -1)).reshape(-1, 6, D_MODEL)
    mod_p, mod_s = mod[:n_p], mod[n_p:n_p + n_s]

    bmat, cmat, tab = _s5_tables(s5_lam_re[0], s5_lam_im[0], s5_log_dt[0], s5_b_re[0], s5_b_im[0],
                                 s5_c_re[0], s5_c_im[0])
    wr = jnp.pad(w_router[0], ((0, 0), (0, LANES - N_EXPERTS))).astype(BF16)
    br = jnp.pad(b_router[0].astype(F32), (0, LANES - N_EXPERTS), constant_values=NEG_BIG).reshape(1, LANES)
    weights = (w_in[0].astype(BF16), bmat, cmat, tab, s5_d[0].reshape(1, D_S5).astype(F32),
               w_glu[0].astype(BF16), b_glu[0].reshape(1, D_S5).astype(F32), conv_w[0].astype(F32),
               w_out[0].astype(BF16), ln1_g[0].reshape(1, D_MODEL).astype(F32),
               ln1_b[0].reshape(1, D_MODEL).astype(F32), wr, br)

    h0_p = jnp.zeros((n_p, N_SLAB, SLAB_W), F32)
    cb_p = jnp.zeros((n_p, CONV_W - 1, D_CONV), F32)
    h0_s = _state_to_slab(state_s5_re[0], state_s5_im[0])
    cb_s = state_conv[0].astype(F32)

    assert n_tok % E_BLK == 0
    cnt0 = jnp.zeros((1, LANES), F32)
    outs_p = _mixer_call(x_prompt, mod_p, h0_p, cb_p, weights, cnt0, 1, TL, l_p // TL, 0, n_tok, ())
    x1, gates, pos_tiles, xs, cnt_p, s_p, conv_p = outs_p
    xs3 = x_sample.reshape(n_s // seq_per_tile, TL, D_MODEL)
    outs_s = _mixer_call(xs3, mod_s, h0_s, cb_s, weights, cnt_p, seq_per_tile, l_s, 1, tiles_p, n_tok,
                         (x1, gates, pos_tiles, xs))
    x1, gates, pos_tiles, xs, cnt, s_s, conv_s = outs_s

    counts = cnt[0, :N_EXPERTS].astype(jnp.int32)
    blk_row, blk_e, blk_valid, n_used = _block_table(counts, n_tok)
    yb = _expert_call(blk_row, blk_e, blk_valid, n_used, xs, w_gu[0].astype(F32),
                      b_gu[0].reshape(N_EXPERTS, 1, 2 * D_FF).astype(F32), w_down[0].astype(F32),
                      b_down[0].reshape(N_EXPERTS, 1, D_MODEL).astype(F32))
    g2 = ln2_g[0].reshape(1, D_MODEL).astype(F32)
    b2 = ln2_b[0].reshape(1, D_MODEL).astype(F32)
    y_p = _combine_call(pos_tiles, x1, gates, mod_p, g2, b2, yb, 1, TL, tiles_p, 0, l_p // TL)
    y_s = _combine_call(pos_tiles, x1, gates, mod_s, g2, b2, yb, seq_per_tile, l_s, tiles_s, tiles_p, 1)

    p_re, p_im = _slab_to_state(s_p)
    s_re, s_im = _slab_to_state(s_s)
    return (y_p.reshape(n_p, l_p, D_MODEL), y_s.reshape(n_s, l_s, D_MODEL),
            p_re[None], p_im[None], conv_p[None], s_re[None], s_im[None], conv_s[None])
```

```python
import functools
import math

import jax
import jax.numpy as jnp
from jax import lax
from jax.experimental import pallas as pl
from jax.experimental.pallas import tpu as pltpu

F32 = jnp.float32
BF16 = jnp.bfloat16

D_MODEL = 1024
DEPTH = 1
D_S5 = 512
D_CONV = 512
S5_GROUP = 16
S5_GROUPS = 32
S5_STATE = 64
CONV_W = 3
N_EXPERTS = 32
TOP_K = 4
D_FF = 1024
SWIGLU_LIMIT = 7.0
SWIGLU_ALPHA = 1.702
LN_EPS = 1e-5
DEEPNORM_ALPHA = (2.0 * DEPTH) ** 0.25

TL = 256
SUBLANES = 8
LANES = 128
N_SLAB = 4
SLAB_W = 1024
HALF = 512
E_BLK = 256
E_STEP = 2 * E_BLK
ROW_TILE = D_MODEL // LANES
DMA_THREADS = 2
C_CHUNK = 32
NEG_BIG = -1e30
VMEM_LIMIT = 56 * 1024 * 1024


def _dot(a, b):
    return jnp.dot(a, b, preferred_element_type=F32)


def _layer_norm(x, g, b):
    mu = jnp.mean(x, axis=-1, keepdims=True)
    xc = x - mu
    var = jnp.mean(xc * xc, axis=-1, keepdims=True)
    return xc * lax.rsqrt(var + LN_EPS) * g + b


def _ada_kernel(c_ref, w_ref, b_ref, o_ref):
    c = c_ref[...]
    s = c * jax.nn.sigmoid(c)
    o_ref[...] = _dot(s.astype(BF16), w_ref[...].astype(BF16)) + b_ref[...]


def _ada_call(c_all, w_ada, b_ada):
    rows = c_all.shape[0]
    n_out = w_ada.shape[1]
    tn = 768
    return pl.pallas_call(
        _ada_kernel,
        grid=(n_out // tn,),
        in_specs=[pl.BlockSpec((rows, D_MODEL), lambda i: (0, 0)),
                  pl.BlockSpec((D_MODEL, tn), lambda i: (0, i)),
                  pl.BlockSpec((1, tn), lambda i: (0, i))],
        out_specs=pl.BlockSpec((rows, tn), lambda i: (0, i)),
        out_shape=jax.ShapeDtypeStruct((rows, n_out), F32),
        compiler_params=pltpu.CompilerParams(dimension_semantics=("arbitrary",),
                                             vmem_limit_bytes=VMEM_LIMIT),
        name="ada_mod",
    )(c_all, w_ada, b_ada)


def _mixer_kernel(nseg, seg, carry_tiles, n_alias, cap_rows,
                  x_ref, mod_ref, h0_ref, cbuf_ref, win_ref, bmat_ref, cmat_ref, tab_ref, d_ref,
                  wglu_ref, bglu_ref, convw_ref, wout_ref, g1_ref, b1_ref, wr_ref, br_ref, cnt0_ref, *rest):
    rest = rest[n_alias:]
    (x1_ref, gate_ref, post_ref, xs_hbm, cnt_out_ref, sout_ref, cout_ref,
     p_ref, bu_ref, xcs_ref, carry_ref, h2s_ref, posv_ref, poss_ref, cnt_ref, sem_s, sem_v) = rest
    g_id = pl.program_id(0)
    j = pl.program_id(1)
    step = g_id * pl.num_programs(1) + j
    is_last = step == pl.num_programs(0) * pl.num_programs(1) - 1
    slot = step % 2

    def mod_rows(k):
        if nseg == 1:
            return mod_ref[0, k:k + 1, :]
        return jnp.concatenate(
            [jnp.broadcast_to(mod_ref[s, k:k + 1, :], (seg, D_MODEL)) for s in range(nseg)], axis=0)

    x = x_ref[0]
    h = x * (1.0 + mod_rows(1)) + mod_rows(0)
    p_ref[...] = _dot(h.astype(BF16), win_ref[...])

    for i in range(N_SLAB):
        u_i = p_ref[:, i * LANES:(i + 1) * LANES].astype(BF16)
        bu_ref[i] = _dot(u_i, bmat_ref[i])

    prev = 1 - slot
    smem_copy = pltpu.make_async_copy(posv_ref, poss_ref, sem_v.at[0])

    @pl.when(step == 0)
    def _():
        k_i = lax.broadcasted_iota(jnp.int32, (SUBLANES, TL), 0)
        t_i = lax.broadcasted_iota(jnp.int32, (SUBLANES, TL), 1)
        posv_ref[...] = (N_EXPERTS * cap_rows + k_i * TL + t_i) * ROW_TILE
        smem_copy.start()
        h2s_ref[1] = jnp.zeros((TL * ROW_TILE, LANES), F32)

    smem_copy.wait()

    def issue_rows(src_slot, t0, n_rows):
        for q in range(n_rows):
            t = t0 + q
            src = pl.multiple_of(t * ROW_TILE, ROW_TILE)
            for k in range(TOP_K):
                dst = pl.multiple_of(poss_ref[k, t], ROW_TILE)
                pltpu.make_async_copy(h2s_ref.at[src_slot, pl.ds(src, ROW_TILE), :],
                                      xs_hbm.at[pl.ds(dst, ROW_TILE), :],
                                      sem_s.at[src_slot]).start(priority=k % DMA_THREADS)

    def wait_scatter(sl):
        for _ in range(TOP_K):
            pltpu.make_async_copy(h2s_ref.at[sl], xs_hbm.at[pl.ds(0, TL * ROW_TILE), :], sem_s.at[sl]).wait()

    if carry_tiles:
        @pl.when(j == 0)
        def _():
            carry_ref[...] = h0_ref[0]

    for s in range(nseg):
        if carry_tiles:
            init = tuple(carry_ref[i:i + 1, :] for i in range(N_SLAB))
        else:
            init = tuple(h0_ref[s, i:i + 1, :] for i in range(N_SLAB))

        def scan_body(r, carry, s=s):
            issue_rows(prev, s * seg + r * (2 * SUBLANES), 2 * SUBLANES)
            new = []
            for i in range(N_SLAB):
                cr = carry[i][:, :HALF]
                ci = carry[i][:, HALF:]
                for half in range(2):
                    row0 = pl.multiple_of(s * seg + r * (2 * SUBLANES) + half * SUBLANES, SUBLANES)
                    blk = bu_ref[i, pl.ds(row0, SUBLANES), :]
                    xr = blk[:, :HALF]
                    xi = blk[:, HALF:]
                    for k, d in enumerate((1, 2, 4)):
                        lr = tab_ref[i, 2 * k]
                        li = tab_ref[i, 2 * k + 1]
                        rr = pltpu.roll(xr, d, 0)
                        ri = pltpu.roll(xi, d, 0)
                        xr, xi = xr + (lr * rr - li * ri), xi + (lr * ri + li * rr)
                    pr = tab_ref[i, 6]
                    pi_ = tab_ref[i, 7]
                    crb = jnp.broadcast_to(cr, (SUBLANES, HALF))
                    cib = jnp.broadcast_to(ci, (SUBLANES, HALF))
                    xr, xi = xr + (pr * crb - pi_ * cib), xi + (pr * cib + pi_ * crb)
                    bu_ref[i, pl.ds(row0, SUBLANES), :] = jnp.concatenate([xr, xi], axis=1)
                    cr = xr[SUBLANES - 1:SUBLANES, :]
                    ci = xi[SUBLANES - 1:SUBLANES, :]
                new.append(jnp.concatenate([cr, ci], axis=1))
            return tuple(new)

        final = lax.fori_loop(0, seg // (2 * SUBLANES), scan_body, init)
        for i in range(N_SLAB):
            sout_ref[s, i:i + 1, :] = final[i]
            if carry_tiles:
                carry_ref[i:i + 1, :] = final[i]

    y = jnp.concatenate([_dot(bu_ref[i].astype(BF16), cmat_ref[i]) for i in range(N_SLAB)], axis=1)
    y = y + d_ref[...] * p_ref[:, :D_S5]
    z = jax.nn.gelu(y)
    z = z * jax.nn.sigmoid(_dot(z.astype(BF16), wglu_ref[...]) + bglu_ref[...])

    xc = p_ref[:, D_S5 + D_CONV:D_S5 + 2 * D_CONV] * p_ref[:, D_S5 + 2 * D_CONV:]
    w0 = convw_ref[0:1, :]
    w1 = convw_ref[1:2, :]
    w2 = convw_ref[2:3, :]
    conv_parts = []
    for s in range(nseg):
        base = s * (seg + SUBLANES)
        xc_s = xc[s * seg:(s + 1) * seg]
        tail = xc_s[seg - 2:seg]
        xcs_ref[pl.ds(base + SUBLANES, seg), :] = xc_s
        if carry_tiles:
            @pl.when(j == 0)
            def _(s=s, base=base):
                xcs_ref[pl.ds(base + SUBLANES - 2, 2), :] = cbuf_ref[s]
        else:
            xcs_ref[pl.ds(base + SUBLANES - 2, 2), :] = cbuf_ref[s]
        xm1 = xcs_ref[pl.ds(base + SUBLANES - 1, seg), :]
        xm2 = xcs_ref[pl.ds(base + SUBLANES - 2, seg), :]
        conv_parts.append(w0 * xm2 + w1 * xm1 + w2 * xc_s)
        cout_ref[s] = tail
        if carry_tiles:
            xcs_ref[pl.ds(base + SUBLANES - 2, 2), :] = tail
    conv = conv_parts[0] if nseg == 1 else jnp.concatenate(conv_parts, axis=0)
    y_b = p_ref[:, D_S5:D_S5 + D_CONV] * conv

    mix = _dot(jnp.concatenate([z, y_b], axis=1).astype(BF16), wout_ref[...])
    x1 = _layer_norm(DEEPNORM_ALPHA * x + (1.0 + mod_rows(2)) * mix, g1_ref[...], b1_ref[...])
    x1_ref[...] = x1
    h2 = x1 * (1.0 + mod_rows(4)) + mod_rows(3)
    for c in range(ROW_TILE):
        h2s_ref[slot, pl.ds(c, TL, stride=ROW_TILE), :] = h2[:, c * LANES:(c + 1) * LANES]

    logits = _dot(h2.astype(BF16), wr_ref[...]) + br_ref[...]
    lane = lax.broadcasted_iota(jnp.int32, logits.shape, 1)
    lane_f = lane.astype(F32)
    vals, sels, hots = [], [], []
    cur = logits
    for _ in range(TOP_K):
        m = jnp.max(cur, axis=-1, keepdims=True)
        am = jnp.min(jnp.where(cur == m, lane_f, float(LANES)), axis=-1, keepdims=True)
        hot = lane_f == am
        vals.append(m)
        sels.append(am)
        hots.append(hot)
        cur = jnp.where(hot, -jnp.inf, cur)
    exps = [jnp.exp(v - vals[0]) for v in vals]
    inv = 1.0 / (exps[0] + exps[1] + exps[2] + exps[3])
    gate_out = jnp.zeros(logits.shape, F32)
    for k in range(TOP_K):
        gate_out = jnp.where(lane == k, exps[k] * inv, gate_out)
    gate_ref[...] = gate_out

    @pl.when(step == 0)
    def _():
        cnt_ref[...] = cnt0_ref[...]

    chosen = jnp.zeros(logits.shape, F32)
    for k in range(TOP_K):
        chosen = jnp.where(hots[k], 1.0, chosen)
    r_i = lax.broadcasted_iota(jnp.int32, (TL, TL), 0)
    c_i = lax.broadcasted_iota(jnp.int32, (TL, TL), 1)
    before = jnp.where(c_i < r_i, 1.0, 0.0).astype(BF16)
    rank_base = _dot(before, chosen.astype(BF16)) + cnt_ref[...]
    pos_mat = jnp.zeros(logits.shape, F32)
    for k in range(TOP_K):
        rank_k = jnp.sum(jnp.where(hots[k], rank_base, 0.0), axis=-1, keepdims=True)
        pos_mat = jnp.where(lane == k, (sels[k] * float(cap_rows) + rank_k) * float(ROW_TILE), pos_mat)
    cnt_ref[...] = cnt_ref[...] + jnp.sum(chosen, axis=0, keepdims=True)
    cnt_out_ref[...] = cnt_ref[...]
    pos_t = pos_mat.T[:SUBLANES, :].astype(jnp.int32)
    post_ref[0] = pos_t
    posv_ref[...] = pos_t
    smem_copy.start()
    wait_scatter(prev)

    @pl.when(is_last)
    def _():
        smem_copy.wait()

        def tail_body(t8, c):
            issue_rows(slot, t8 * SUBLANES, SUBLANES)
            return c
        lax.fori_loop(0, TL // SUBLANES, tail_body, 0)
        wait_scatter(slot)


def _mixer_call(x3, mod, h0, cbuf, weights, cnt0, nseg, seg, tiles_per_group, tile0, n_tok, aliased):
    groups = x3.shape[0]
    nseq = mod.shape[0]
    carry_tiles = nseg == 1
    n_alias = len(aliased)
    n_tiles = n_tok // TL
    cap_rows = n_tok

    def full(a):
        nd = a.ndim
        return pl.BlockSpec(a.shape, lambda g, j, nd=nd: (0,) * nd)

    def tile_map(g, j):
        return (tile0 + g * tiles_per_group + j, 0)

    in_specs = [pl.BlockSpec((1, TL, D_MODEL), lambda g, j: (g, j, 0)),
                pl.BlockSpec((nseg, 6, D_MODEL), lambda g, j: (g, 0, 0)),
                pl.BlockSpec((nseg, N_SLAB, SLAB_W), lambda g, j: (g, 0, 0)),
                pl.BlockSpec((nseg, CONV_W - 1, D_CONV), lambda g, j: (g, 0, 0))]
    in_specs += [full(w) for w in weights]
    in_specs += [full(cnt0)]
    in_specs += [pl.BlockSpec(memory_space=pl.ANY)] * n_alias
    out_shape = [jax.ShapeDtypeStruct((n_tok, D_MODEL), F32),
                 jax.ShapeDtypeStruct((n_tok, LANES), F32),
                 jax.ShapeDtypeStruct((n_tiles, SUBLANES, TL), jnp.int32),
                 jax.ShapeDtypeStruct(((N_EXPERTS * cap_rows + TOP_K * TL) * ROW_TILE, LANES), F32),
                 jax.ShapeDtypeStruct((1, LANES), F32),
                 jax.ShapeDtypeStruct((nseq, N_SLAB, SLAB_W), F32),
                 jax.ShapeDtypeStruct((nseq, CONV_W - 1, D_CONV), F32)]
    out_specs = [pl.BlockSpec((TL, D_MODEL), tile_map),
                 pl.BlockSpec((TL, LANES), tile_map),
                 pl.BlockSpec((1, SUBLANES, TL), lambda g, j: tile_map(g, j) + (0,)),
                 pl.BlockSpec(memory_space=pl.ANY),
                 pl.BlockSpec((1, LANES), lambda g, j: (0, 0)),
                 pl.BlockSpec((nseg, N_SLAB, SLAB_W), lambda g, j: (g, 0, 0)),
                 pl.BlockSpec((nseg, CONV_W - 1, D_CONV), lambda g, j: (g, 0, 0))]
    n_in = 4 + len(weights) + 1
    aliases = {n_in + k: k for k in range(n_alias)}
    scratch = [pltpu.VMEM((TL, 2 * D_MODEL), F32),
               pltpu.VMEM((N_SLAB, TL, SLAB_W), F32),
               pltpu.VMEM((nseg * (seg + SUBLANES), D_CONV), F32),
               pltpu.VMEM((N_SLAB, SLAB_W), F32),
               pltpu.VMEM((2, TL * ROW_TILE, LANES), F32),
               pltpu.VMEM((SUBLANES, TL), jnp.int32),
               pltpu.SMEM((SUBLANES, TL), jnp.int32),
               pltpu.VMEM((1, LANES), F32),
               pltpu.SemaphoreType.DMA((2,)),
               pltpu.SemaphoreType.DMA((1,))]
    return pl.pallas_call(
        functools.partial(_mixer_kernel, nseg, seg, carry_tiles, n_alias, cap_rows),
        grid=(groups, tiles_per_group),
        in_specs=in_specs, out_specs=out_specs, out_shape=out_shape,
        scratch_shapes=scratch,
        input_output_aliases=aliases,
        compiler_params=pltpu.CompilerParams(dimension_semantics=("arbitrary", "arbitrary"),
                                             vmem_limit_bytes=VMEM_LIMIT),
        name="mixer_prompt" if carry_tiles else "mixer_sample",
    )(x3, mod, h0, cbuf, *weights, cnt0, *aliased)


W_CAST_ROWS = 128


def _expert_kernel(row_ref, be_ref, nv_ref, nu_ref, xs_ref, wgu_ref, bgu_ref, wd_ref, bd_ref,
                   yb_ref, wgu_bf, wd_bf):
    i = pl.program_id(0)
    n_used = nu_ref[0]
    prev = jnp.maximum(i - 1, 0)
    new_expert = jnp.logical_or(i == 0, be_ref[i] != be_ref[prev])

    @pl.when(jnp.logical_and(new_expert, i < n_used))
    def _():
        def cast_gu(r, c):
            rows = pl.ds(pl.multiple_of(r * W_CAST_ROWS, W_CAST_ROWS), W_CAST_ROWS)
            wgu_bf[rows, :] = wgu_ref[0, rows, :].astype(BF16)
            return c
        lax.fori_loop(0, D_MODEL // W_CAST_ROWS, cast_gu, 0)

        def cast_d(r, c):
            rows = pl.ds(pl.multiple_of(r * W_CAST_ROWS, W_CAST_ROWS), W_CAST_ROWS)
            wd_bf[rows, :] = wd_ref[0, rows, :].astype(BF16)
            return c
        lax.fori_loop(0, D_FF // W_CAST_ROWS, cast_d, 0)

    for sub in range(E_STEP // E_BLK):
        @pl.when(jnp.logical_and(i < n_used, nv_ref[i] > sub * E_BLK))
        def _(sub=sub):
            base = sub * E_BLK * ROW_TILE
            row = lax.broadcasted_iota(jnp.int32, (E_BLK, 1), 0) + sub * E_BLK
            xt = jnp.concatenate(
                [xs_ref[pl.ds(base + c, E_BLK, stride=ROW_TILE), :] for c in range(ROW_TILE)], axis=1)
            xb = jnp.where(row < nv_ref[i], xt, 0.0).astype(BF16)
            hgu = _dot(xb, wgu_bf[...]) + bgu_ref[0]
            g = jnp.minimum(hgu[:, :D_FF], SWIGLU_LIMIT)
            up = jnp.clip(hgu[:, D_FF:], -SWIGLU_LIMIT, SWIGLU_LIMIT)
            act = (up + 1.0) * (g * jax.nn.sigmoid(SWIGLU_ALPHA * g))
            y = _dot(act.astype(BF16), wd_bf[...]) + bd_ref[0]
            for c in range(ROW_TILE):
                yb_ref[pl.ds(base + c, E_BLK, stride=ROW_TILE), :] = y[:, c * LANES:(c + 1) * LANES]


def _expert_call(blk_row, blk_e, blk_valid, n_used, xs, w_gu, b_gu, w_down, b_down):
    nb = blk_row.shape[0]
    grid_spec = pltpu.PrefetchScalarGridSpec(
        num_scalar_prefetch=4,
        grid=(nb,),
        in_specs=[
            pl.BlockSpec((E_STEP * ROW_TILE, LANES), lambda i, br, be, nv, nu: (br[i], 0)),
            pl.BlockSpec((1, D_MODEL, 2 * D_FF), lambda i, br, be, nv, nu: (be[i], 0, 0)),
            pl.BlockSpec((1, 1, 2 * D_FF), lambda i, br, be, nv, nu: (be[i], 0, 0)),
            pl.BlockSpec((1, D_FF, D_MODEL), lambda i, br, be, nv, nu: (be[i], 0, 0)),
            pl.BlockSpec((1, 1, D_MODEL), lambda i, br, be, nv, nu: (be[i], 0, 0)),
        ],
        out_specs=pl.BlockSpec((E_STEP * ROW_TILE, LANES), lambda i, br, be, nv, nu: (br[i], 0)),
        scratch_shapes=[pltpu.VMEM((D_MODEL, 2 * D_FF), BF16),
                        pltpu.VMEM((D_FF, D_MODEL), BF16)],
    )
    return pl.pallas_call(
        _expert_kernel,
        grid_spec=grid_spec,
        out_shape=jax.ShapeDtypeStruct(xs.shape, F32),
        compiler_params=pltpu.CompilerParams(dimension_semantics=("arbitrary",),
                                             vmem_limit_bytes=VMEM_LIMIT),
        name="experts",
    )(blk_row, blk_e, blk_valid, n_used, xs, w_gu, b_gu, w_down, b_down)


def _combine_kernel(nseg, seg, pos_cur_ref, pos_nxt_ref, x1_ref, gate_ref, mod_ref, g2_ref, b2_ref,
                    yb_hbm, out_ref, buf_a, buf_b, sem):
    i = pl.program_id(0)
    n = pl.num_programs(0)
    n_rows = TOP_K * TL
    bufs = (buf_a, buf_b)

    def issue_rows(pos_ref, dst_slot, t0, count):
        for q in range(count):
            t = t0 + q
            for k in range(TOP_K):
                src = pl.multiple_of(pos_ref[k, t], ROW_TILE)
                dst = pl.multiple_of((k * TL + t) * ROW_TILE, ROW_TILE)
                pltpu.make_async_copy(yb_hbm.at[pl.ds(src, ROW_TILE), :],
                                      bufs[dst_slot].at[pl.ds(dst, ROW_TILE), :],
                                      sem.at[dst_slot]).start(priority=k % DMA_THREADS)

    def wait_rows(sl):
        pltpu.make_async_copy(yb_hbm.at[pl.ds(0, n_rows * ROW_TILE), :], bufs[sl], sem.at[sl]).wait()

    @pl.when(i == 0)
    def _():
        def first(t8, c):
            issue_rows(pos_cur_ref, 0, t8 * SUBLANES, SUBLANES)
            return c
        lax.fori_loop(0, TL // SUBLANES, first, 0)

    g2 = g2_ref[...]
    b2 = b2_ref[...]

    def tile(slot):
        nxt = 1 - slot
        cur = bufs[slot]
        wait_rows(slot)

        def body(r, c):
            r0 = pl.multiple_of(r * C_CHUNK, C_CHUNK)
            issue_rows(pos_nxt_ref, nxt, r0, C_CHUNK)
            gates = gate_ref[pl.ds(r0, C_CHUNK), :]
            ffn = None
            for k in range(TOP_K):
                rows = jnp.concatenate(
                    [cur[pl.ds((k * TL + r0) * ROW_TILE + cc, C_CHUNK, stride=ROW_TILE), :]
                     for cc in range(ROW_TILE)], axis=1)
                term = gates[:, k:k + 1] * rows
                ffn = term if ffn is None else ffn + term
            gate2 = mod_ref[0, 5:6, :] if nseg == 1 else mod_ref[r0 // seg, 5:6, :]
            xa = DEEPNORM_ALPHA * x1_ref[pl.ds(r0, C_CHUNK), :] + (1.0 + gate2) * ffn
            out_ref[pl.ds(r0, C_CHUNK), :] = _layer_norm(xa, g2, b2)
            return c
        lax.fori_loop(0, TL // C_CHUNK, body, 0)

        @pl.when(i == n - 1)
        def _():
            wait_rows(nxt)

    for parity in range(2):
        @pl.when(i % 2 == parity)
        def _(parity=parity):
            tile(parity)


def _combine_call(pos_tiles, x1, gates, mod, ln2_g, ln2_b, yb, nseg, seg, n_tiles, tile0, tiles_per_group):
    pos3 = pos_tiles
    last = tile0 + n_tiles - 1
    in_specs = [
        pl.BlockSpec((None, SUBLANES, TL), lambda i: (tile0 + i, 0, 0), memory_space=pltpu.SMEM),
        pl.BlockSpec((None, SUBLANES, TL), lambda i: (jnp.minimum(tile0 + i + 1, last), 0, 0),
                     memory_space=pltpu.SMEM),
        pl.BlockSpec((TL, D_MODEL), lambda i: (tile0 + i, 0)),
        pl.BlockSpec((TL, LANES), lambda i: (tile0 + i, 0)),
        pl.BlockSpec((nseg, 6, D_MODEL), lambda i: (i // tiles_per_group, 0, 0)),
        pl.BlockSpec((1, D_MODEL), lambda i: (0, 0)),
        pl.BlockSpec((1, D_MODEL), lambda i: (0, 0)),
        pl.BlockSpec(memory_space=pl.ANY),
    ]
    return pl.pallas_call(
        functools.partial(_combine_kernel, nseg, seg),
        grid=(n_tiles,),
        in_specs=in_specs,
        out_specs=pl.BlockSpec((TL, D_MODEL), lambda i: (i, 0)),
        out_shape=jax.ShapeDtypeStruct((n_tiles * TL, D_MODEL), F32),
        scratch_shapes=[pltpu.VMEM((TOP_K * TL * ROW_TILE, LANES), F32),
                        pltpu.VMEM((TOP_K * TL * ROW_TILE, LANES), F32),
                        pltpu.SemaphoreType.DMA((2,))],
        compiler_params=pltpu.CompilerParams(dimension_semantics=("arbitrary",),
                                             vmem_limit_bytes=VMEM_LIMIT),
        name="combine_prompt" if nseg == 1 else "combine_sample",
    )(pos3, pos3, x1, gates, mod, ln2_g, ln2_b, yb)


def _s5_tables(lam_re, lam_im, log_dt, b_re, b_im, c_re, c_im):
    dt = jnp.exp(log_dt.astype(F32))[:, None]
    lam = lax.complex(lam_re.astype(F32), lam_im.astype(F32))
    lam_dt = lam * dt
    lam_bar = jnp.exp(lam_dt)
    b_bar = ((lam_bar - 1.0) / lam)[..., None] * lax.complex(b_re.astype(F32), b_im.astype(F32))
    gl = S5_GROUPS // N_SLAB
    eye = jnp.eye(gl, dtype=F32)

    def b_slab(part):
        a = part.reshape(N_SLAB, gl, S5_STATE, S5_GROUP)
        return jnp.einsum('sgph,gk->sghkp', a, eye).reshape(N_SLAB, gl * S5_GROUP, gl * S5_STATE)

    bmat = jnp.concatenate([b_slab(b_bar.real), b_slab(b_bar.imag)], axis=-1).astype(BF16)

    def c_slab(part):
        a = part.reshape(N_SLAB, gl, S5_GROUP, S5_STATE)
        return jnp.einsum('sghp,gk->sgpkh', a, eye).reshape(N_SLAB, gl * S5_STATE, gl * S5_GROUP)

    cmat = jnp.concatenate([c_slab(c_re.astype(F32)), -c_slab(c_im.astype(F32))], axis=1).astype(BF16)

    row = jnp.arange(SUBLANES, dtype=F32)[:, None, None]

    def power(k):
        return jnp.exp(lam_dt[None] * k)

    tabs = []
    for d in (1, 2, 4):
        pw = power(jnp.full_like(row, float(d)))
        mask = (row >= d).astype(F32)
        tabs += [pw.real * mask, pw.imag * mask]
    pw = power(row + 1.0)
    tabs += [pw.real, pw.imag]
    tab = jnp.stack(tabs, axis=0)
    tab = tab.reshape(8, SUBLANES, N_SLAB, gl * S5_STATE).transpose(2, 0, 1, 3)
    return bmat, cmat, tab


def _state_to_slab(re, im):
    n = re.shape[0]
    return jnp.concatenate([re.reshape(n, N_SLAB, HALF), im.reshape(n, N_SLAB, HALF)], axis=-1).astype(F32)


def _slab_to_state(s):
    n = s.shape[0]
    re = s[:, :, :HALF].reshape(n, S5_GROUPS, S5_STATE)
    im = s[:, :, HALF:].reshape(n, S5_GROUPS, S5_STATE)
    return re, im


def _block_table(counts, n_tok):
    n_assign = n_tok * TOP_K
    n_blocks = (n_assign + N_EXPERTS * (E_STEP - 1) + E_STEP - 1) // E_STEP
    cap_blocks = n_tok // E_STEP
    nblk = (counts + E_STEP - 1) // E_STEP
    cum = jnp.cumsum(nblk)
    start = cum - nblk
    n_used = cum[-1]
    i = jnp.arange(n_blocks, dtype=jnp.int32)
    ii = jnp.maximum(jnp.minimum(i, n_used - 1), 0)
    e = jnp.minimum(jnp.sum((cum[None, :] <= ii[:, None]).astype(jnp.int32), axis=1), N_EXPERTS - 1)
    b = ii - start[e]
    blk_row = (e * cap_blocks + b).astype(jnp.int32)
    valid = jnp.where(i < n_used, jnp.minimum(E_STEP, counts[e] - b * E_STEP), 0).astype(jnp.int32)
    return blk_row, e, valid, n_used.astype(jnp.int32).reshape(1)


def kernel(x_prompt, x_sample, c_prompt, c_sample, state_s5_re, state_s5_im, state_conv, w_ada, b_ada, w_in,
           s5_lam_re, s5_lam_im, s5_log_dt, s5_b_re, s5_b_im, s5_c_re, s5_c_im, s5_d, w_glu, b_glu, conv_w,
           w_out, ln1_g, ln1_b, w_router, b_router, w_gu, b_gu, w_down, b_down, ln2_g, ln2_b):
    assert DEPTH == 1 and w_ada.shape[0] == 1
    n_p, l_p, _ = x_prompt.shape
    n_s, l_s, _ = x_sample.shape
    t_p, t_s = n_p * l_p, n_s * l_s
    n_tok = t_p + t_s
    tiles_p, tiles_s = t_p // TL, t_s // TL
    seq_per_tile = TL // l_s
    assert l_p % TL == 0 and TL % l_s == 0 and n_s % seq_per_tile == 0

    c_all = jnp.concatenate([c_prompt, c_sample], axis=0).astype(F32)
    pad = (-c_all.shape[0]) % SUBLANES
    c_all = jnp.pad(c_all, ((0, pad), (0, 0)))
    mod = _ada_call(c_all, w_ada[0], b_ada[0].reshape(1, -1)).reshape(-1, 6, D_MODEL)
    mod_p, mod_s = mod[:n_p], mod[n_p:n_p + n_s]

    bmat, cmat, tab = _s5_tables(s5_lam_re[0], s5_lam_im[0], s5_log_dt[0], s5_b_re[0], s5_b_im[0],
                                 s5_c_re[0], s5_c_im[0])
    wr = jnp.pad(w_router[0], ((0, 0), (0, LANES - N_EXPERTS))).astype(BF16)
    br = jnp.pad(b_router[0].astype(F32), (0, LANES - N_EXPERTS), constant_values=NEG_BIG).reshape(1, LANES)
    weights = (w_in[0].astype(BF16), bmat, cmat, tab, s5_d[0].reshape(1, D_S5).astype(F32),
               w_glu[0].astype(BF16), b_glu[0].reshape(1, D_S5).astype(F32), conv_w[0].astype(F32),
               w_out[0].astype(BF16), ln1_g[0].reshape(1, D_MODEL).astype(F32),
               ln1_b[0].reshape(1, D_MODEL).astype(F32), wr, br)

    h0_p = jnp.zeros((n_p, N_SLAB, SLAB_W), F32)
    cb_p = jnp.zeros((n_p, CONV_W - 1, D_CONV), F32)
    h0_s = _state_to_slab(state_s5_re[0], state_s5_im[0])
    cb_s = state_conv[0].astype(F32)

    assert n_tok % E_STEP == 0
    cnt0 = jnp.zeros((1, LANES), F32)
    outs_p = _mixer_call(x_prompt, mod_p, h0_p, cb_p, weights, cnt0, 1, TL, l_p // TL, 0, n_tok, ())
    x1, gates, pos_tiles, xs, cnt_p, s_p, conv_p = outs_p
    xs3 = x_sample.reshape(n_s // seq_per_tile, TL, D_MODEL)
    outs_s = _mixer_call(xs3, mod_s, h0_s, cb_s, weights, cnt_p, seq_per_tile, l_s, 1, tiles_p, n_tok,
                         (x1, gates, pos_tiles, xs))
    x1, gates, pos_tiles, xs, cnt, s_s, conv_s = outs_s

    counts = cnt[0, :N_EXPERTS].astype(jnp.int32)
    blk_row, blk_e, blk_valid, n_used = _block_table(counts, n_tok)
    yb = _expert_call(blk_row, blk_e, blk_valid, n_used, xs, w_gu[0].astype(F32),
                      b_gu[0].reshape(N_EXPERTS, 1, 2 * D_FF).astype(F32), w_down[0].astype(F32),
                      b_down[0].reshape(N_EXPERTS, 1, D_MODEL).astype(F32))
    g2 = ln2_g[0].reshape(1, D_MODEL).astype(F32)
    b2 = ln2_b[0].reshape(1, D_MODEL).astype(F32)
    y_p = _combine_call(pos_tiles, x1, gates, mod_p, g2, b2, yb, 1, TL, tiles_p, 0, l_p // TL)
    y_s = _combine_call(pos_tiles, x1, gates, mod_s, g2, b2, yb, seq_per_tile, l_s, tiles_s, tiles_p, 1)

    p_re, p_im = _slab_to_state(s_p)
    s_re, s_im = _slab_to_state(s_s)
    return (y_p.reshape(n_p, l_p, D_MODEL), y_s.reshape(n_s, l_s, D_MODEL),
            p_re[None], p_im[None], conv_p[None], s_re[None], s_im[None], conv_s[None])
```

```python
import functools
import math

import jax
import jax.numpy as jnp
from jax import lax
from jax.experimental import pallas as pl
from jax.experimental.pallas import tpu as pltpu

F32 = jnp.float32
BF16 = jnp.bfloat16

D_MODEL = 1024
DEPTH = 1
D_S5 = 512
D_CONV = 512
S5_GROUP = 16
S5_GROUPS = 32
S5_STATE = 64
CONV_W = 3
N_EXPERTS = 32
TOP_K = 4
D_FF = 1024
SWIGLU_LIMIT = 7.0
SWIGLU_ALPHA = 1.702
LN_EPS = 1e-5
DEEPNORM_ALPHA = (2.0 * DEPTH) ** 0.25

TL = 256
SUBLANES = 8
LANES = 128
N_SLAB = 4
SLAB_W = 1024
HALF = 512
E_BLK = 256
ROW_TILE = D_MODEL // LANES
PACK_ROWS = HALF // LANES
HI16_MASK = -65536
DMA_THREADS = 2
NEG_BIG = -1e30
VMEM_LIMIT = 56 * 1024 * 1024


def _dot(a, b):
    return jnp.dot(a, b, preferred_element_type=F32)


def _layer_norm(x, g, b):
    mu = jnp.mean(x, axis=-1, keepdims=True)
    xc = x - mu
    var = jnp.mean(xc * xc, axis=-1, keepdims=True)
    return xc * lax.rsqrt(var + LN_EPS) * g + b


def _ada_kernel(c_ref, w_ref, b_ref, o_ref):
    c = c_ref[...]
    s = c * jax.nn.sigmoid(c)
    o_ref[...] = _dot(s.astype(BF16), w_ref[...].astype(BF16)) + b_ref[...]


def _ada_call(c_all, w_ada, b_ada):
    rows = c_all.shape[0]
    n_out = w_ada.shape[1]
    tn = 768
    return pl.pallas_call(
        _ada_kernel,
        grid=(n_out // tn,),
        in_specs=[pl.BlockSpec((rows, D_MODEL), lambda i: (0, 0)),
                  pl.BlockSpec((D_MODEL, tn), lambda i: (0, i)),
                  pl.BlockSpec((1, tn), lambda i: (0, i))],
        out_specs=pl.BlockSpec((rows, tn), lambda i: (0, i)),
        out_shape=jax.ShapeDtypeStruct((rows, n_out), F32),
        compiler_params=pltpu.CompilerParams(dimension_semantics=("arbitrary",),
                                             vmem_limit_bytes=VMEM_LIMIT),
        name="ada_mod",
    )(c_all, w_ada, b_ada)


def _mixer_kernel(nseg, seg, carry_tiles, n_alias, cap_rows, tile0,
                  x_ref, mod_ref, h0_ref, cbuf_ref, win_ref, bmat_ref, cmat_ref, tab_ref, d_ref,
                  wglu_ref, bglu_ref, convw_ref, wout_ref, g1_ref, b1_ref, wr_ref, br_ref, cnt0_ref, *rest):
    rest = rest[n_alias:]
    (x1_ref, gate_ref, xs_hbm, cnt_out_ref, sout_ref, cout_ref,
     p_ref, bu_ref, xcs_ref, carry_ref, h2s_ref, posv_ref, poss_ref, cnt_ref, sem_s, sem_v) = rest
    g_id = pl.program_id(0)
    j = pl.program_id(1)
    step = g_id * pl.num_programs(1) + j
    is_last = step == pl.num_programs(0) * pl.num_programs(1) - 1
    slot = step % 2

    def mod_rows(k):
        if nseg == 1:
            return mod_ref[0, k:k + 1, :]
        return jnp.concatenate(
            [jnp.broadcast_to(mod_ref[s, k:k + 1, :], (seg, D_MODEL)) for s in range(nseg)], axis=0)

    x = x_ref[0]
    h = x * (1.0 + mod_rows(1)) + mod_rows(0)
    p_ref[...] = _dot(h.astype(BF16), win_ref[...])

    for i in range(N_SLAB):
        u_i = p_ref[:, i * LANES:(i + 1) * LANES].astype(BF16)
        bu_ref[i] = _dot(u_i, bmat_ref[i])

    prev = 1 - slot
    smem_copy = pltpu.make_async_copy(posv_ref, poss_ref, sem_v.at[0])

    @pl.when(step == 0)
    def _():
        k_i = lax.broadcasted_iota(jnp.int32, (SUBLANES, TL), 0)
        t_i = lax.broadcasted_iota(jnp.int32, (SUBLANES, TL), 1)
        posv_ref[...] = (N_EXPERTS * cap_rows + k_i * TL + t_i) * ROW_TILE
        smem_copy.start()
        h2s_ref[...] = jnp.zeros(h2s_ref.shape, F32)

    smem_copy.wait()

    def issue_rows(src_slot, t0, n_rows):
        for q in range(n_rows):
            t = t0 + q
            src = pl.multiple_of(t * ROW_TILE, ROW_TILE)
            for k in range(TOP_K):
                dst = pl.multiple_of(poss_ref[k, t], ROW_TILE)
                pltpu.make_async_copy(h2s_ref.at[src_slot, pl.ds(src, ROW_TILE), :],
                                      xs_hbm.at[pl.ds(dst, ROW_TILE), :],
                                      sem_s.at[src_slot]).start(priority=k % DMA_THREADS)

    def wait_scatter(sl):
        for _ in range(TOP_K):
            pltpu.make_async_copy(h2s_ref.at[sl], xs_hbm.at[pl.ds(0, TL * ROW_TILE), :], sem_s.at[sl]).wait()

    if carry_tiles:
        @pl.when(j == 0)
        def _():
            carry_ref[...] = h0_ref[0]

    for s in range(nseg):
        if carry_tiles:
            init = tuple(carry_ref[i:i + 1, :] for i in range(N_SLAB))
        else:
            init = tuple(h0_ref[s, i:i + 1, :] for i in range(N_SLAB))

        def scan_body(r, carry, s=s):
            issue_rows(prev, s * seg + r * (2 * SUBLANES), 2 * SUBLANES)
            new = []
            for i in range(N_SLAB):
                cr = carry[i][:, :HALF]
                ci = carry[i][:, HALF:]
                for half in range(2):
                    row0 = pl.multiple_of(s * seg + r * (2 * SUBLANES) + half * SUBLANES, SUBLANES)
                    blk = bu_ref[i, pl.ds(row0, SUBLANES), :]
                    xr = blk[:, :HALF]
                    xi = blk[:, HALF:]
                    for k, d in enumerate((1, 2, 4)):
                        lr = tab_ref[i, 2 * k]
                        li = tab_ref[i, 2 * k + 1]
                        rr = pltpu.roll(xr, d, 0)
                        ri = pltpu.roll(xi, d, 0)
                        xr, xi = xr + (lr * rr - li * ri), xi + (lr * ri + li * rr)
                    pr = tab_ref[i, 6]
                    pi_ = tab_ref[i, 7]
                    crb = jnp.broadcast_to(cr, (SUBLANES, HALF))
                    cib = jnp.broadcast_to(ci, (SUBLANES, HALF))
                    xr, xi = xr + (pr * crb - pi_ * cib), xi + (pr * cib + pi_ * crb)
                    bu_ref[i, pl.ds(row0, SUBLANES), :] = jnp.concatenate([xr, xi], axis=1)
                    cr = xr[SUBLANES - 1:SUBLANES, :]
                    ci = xi[SUBLANES - 1:SUBLANES, :]
                new.append(jnp.concatenate([cr, ci], axis=1))
            return tuple(new)

        final = lax.fori_loop(0, seg // (2 * SUBLANES), scan_body, init)
        for i in range(N_SLAB):
            sout_ref[s, i:i + 1, :] = final[i]
            if carry_tiles:
                carry_ref[i:i + 1, :] = final[i]

    y = jnp.concatenate([_dot(bu_ref[i].astype(BF16), cmat_ref[i]) for i in range(N_SLAB)], axis=1)
    y = y + d_ref[...] * p_ref[:, :D_S5]
    z = jax.nn.gelu(y)
    z = z * jax.nn.sigmoid(_dot(z.astype(BF16), wglu_ref[...]) + bglu_ref[...])

    xc = p_ref[:, D_S5 + D_CONV:D_S5 + 2 * D_CONV] * p_ref[:, D_S5 + 2 * D_CONV:]
    w0 = convw_ref[0:1, :]
    w1 = convw_ref[1:2, :]
    w2 = convw_ref[2:3, :]
    conv_parts = []
    for s in range(nseg):
        base = s * (seg + SUBLANES)
        xc_s = xc[s * seg:(s + 1) * seg]
        tail = xc_s[seg - 2:seg]
        xcs_ref[pl.ds(base + SUBLANES, seg), :] = xc_s
        if carry_tiles:
            @pl.when(j == 0)
            def _(s=s, base=base):
                xcs_ref[pl.ds(base + SUBLANES - 2, 2), :] = cbuf_ref[s]
        else:
            xcs_ref[pl.ds(base + SUBLANES - 2, 2), :] = cbuf_ref[s]
        xm1 = xcs_ref[pl.ds(base + SUBLANES - 1, seg), :]
        xm2 = xcs_ref[pl.ds(base + SUBLANES - 2, seg), :]
        conv_parts.append(w0 * xm2 + w1 * xm1 + w2 * xc_s)
        cout_ref[s] = tail
        if carry_tiles:
            xcs_ref[pl.ds(base + SUBLANES - 2, 2), :] = tail
    conv = conv_parts[0] if nseg == 1 else jnp.concatenate(conv_parts, axis=0)
    y_b = p_ref[:, D_S5:D_S5 + D_CONV] * conv

    mix = _dot(jnp.concatenate([z, y_b], axis=1).astype(BF16), wout_ref[...])
    x1 = _layer_norm(DEEPNORM_ALPHA * x + (1.0 + mod_rows(2)) * mix, g1_ref[...], b1_ref[...])
    x1_ref[...] = x1
    h2 = x1 * (1.0 + mod_rows(4)) + mod_rows(3)
    lo_bits = lax.bitcast_convert_type(h2[:, :HALF].astype(BF16).astype(F32), jnp.int32)
    hi_bits = lax.bitcast_convert_type(h2[:, HALF:].astype(BF16).astype(F32), jnp.int32)
    packed = lax.bitcast_convert_type(hi_bits | lax.shift_right_logical(lo_bits, 16), F32)
    for c in range(PACK_ROWS):
        h2s_ref[slot, pl.ds(c, TL, stride=ROW_TILE), :] = packed[:, c * LANES:(c + 1) * LANES]

    logits = _dot(h2.astype(BF16), wr_ref[...]) + br_ref[...]
    lane = lax.broadcasted_iota(jnp.int32, logits.shape, 1)
    lane_f = lane.astype(F32)
    vals, sels, hots = [], [], []
    cur = logits
    for _ in range(TOP_K):
        m = jnp.max(cur, axis=-1, keepdims=True)
        am = jnp.min(jnp.where(cur == m, lane_f, float(LANES)), axis=-1, keepdims=True)
        hot = lane_f == am
        vals.append(m)
        sels.append(am)
        hots.append(hot)
        cur = jnp.where(hot, -jnp.inf, cur)
    exps = [jnp.exp(v - vals[0]) for v in vals]
    inv = 1.0 / (exps[0] + exps[1] + exps[2] + exps[3])
    gate_out = jnp.zeros(logits.shape, F32)
    for k in range(TOP_K):
        gate_out = jnp.where(lane == k, exps[k] * inv, gate_out)
    gate_ref[...] = gate_out

    tok = ((tile0 + step) * TL + lax.broadcasted_iota(jnp.int32, logits.shape, 0)).astype(F32)
    record = jnp.zeros(logits.shape, F32)
    for k in range(TOP_K):
        record = jnp.where(lane == k, sels[k], record)
        record = jnp.where(lane == TOP_K + k, tok + float(k * cap_rows), record)
    h2s_ref[slot, pl.ds(PACK_ROWS, TL, stride=ROW_TILE), :] = record

    @pl.when(step == 0)
    def _():
        cnt_ref[...] = cnt0_ref[...]

    chosen = jnp.zeros(logits.shape, F32)
    for k in range(TOP_K):
        chosen = jnp.where(hots[k], 1.0, chosen)
    r_i = lax.broadcasted_iota(jnp.int32, (TL, TL), 0)
    c_i = lax.broadcasted_iota(jnp.int32, (TL, TL), 1)
    before = jnp.where(c_i < r_i, 1.0, 0.0).astype(BF16)
    rank_base = _dot(before, chosen.astype(BF16)) + cnt_ref[...]
    pos_mat = jnp.zeros(logits.shape, F32)
    for k in range(TOP_K):
        rank_k = jnp.sum(jnp.where(hots[k], rank_base, 0.0), axis=-1, keepdims=True)
        pos_mat = jnp.where(lane == k, (sels[k] * float(cap_rows) + rank_k) * float(ROW_TILE), pos_mat)
    cnt_ref[...] = cnt_ref[...] + jnp.sum(chosen, axis=0, keepdims=True)
    cnt_out_ref[...] = cnt_ref[...]
    pos_t = pos_mat.T[:SUBLANES, :].astype(jnp.int32)
    posv_ref[...] = pos_t
    smem_copy.start()
    wait_scatter(prev)

    @pl.when(is_last)
    def _():
        smem_copy.wait()

        def tail_body(t8, c):
            issue_rows(slot, t8 * SUBLANES, SUBLANES)
            return c
        lax.fori_loop(0, TL // SUBLANES, tail_body, 0)
        wait_scatter(slot)


def _mixer_call(x3, mod, h0, cbuf, weights, cnt0, nseg, seg, tiles_per_group, tile0, n_tok, aliased):
    groups = x3.shape[0]
    nseq = mod.shape[0]
    carry_tiles = nseg == 1
    n_alias = len(aliased)
    n_tiles = n_tok // TL
    cap_rows = n_tok

    def full(a):
        nd = a.ndim
        return pl.BlockSpec(a.shape, lambda g, j, nd=nd: (0,) * nd)

    def tile_map(g, j):
        return (tile0 + g * tiles_per_group + j, 0)

    in_specs = [pl.BlockSpec((1, TL, D_MODEL), lambda g, j: (g, j, 0)),
                pl.BlockSpec((nseg, 6, D_MODEL), lambda g, j: (g, 0, 0)),
                pl.BlockSpec((nseg, N_SLAB, SLAB_W), lambda g, j: (g, 0, 0)),
                pl.BlockSpec((nseg, CONV_W - 1, D_CONV), lambda g, j: (g, 0, 0))]
    in_specs += [full(w) for w in weights]
    in_specs += [full(cnt0)]
    in_specs += [pl.BlockSpec(memory_space=pl.ANY)] * n_alias
    out_shape = [jax.ShapeDtypeStruct((n_tok, D_MODEL), F32),
                 jax.ShapeDtypeStruct((n_tok, LANES), F32),
                 jax.ShapeDtypeStruct(((N_EXPERTS * cap_rows + TOP_K * TL) * ROW_TILE, LANES), F32),
                 jax.ShapeDtypeStruct((1, LANES), F32),
                 jax.ShapeDtypeStruct((nseq, N_SLAB, SLAB_W), F32),
                 jax.ShapeDtypeStruct((nseq, CONV_W - 1, D_CONV), F32)]
    out_specs = [pl.BlockSpec((TL, D_MODEL), tile_map),
                 pl.BlockSpec((TL, LANES), tile_map),
                 pl.BlockSpec(memory_space=pl.ANY),
                 pl.BlockSpec((1, LANES), lambda g, j: (0, 0)),
                 pl.BlockSpec((nseg, N_SLAB, SLAB_W), lambda g, j: (g, 0, 0)),
                 pl.BlockSpec((nseg, CONV_W - 1, D_CONV), lambda g, j: (g, 0, 0))]
    n_in = 4 + len(weights) + 1
    aliases = {n_in + k: k for k in range(n_alias)}
    scratch = [pltpu.VMEM((TL, 2 * D_MODEL), F32),
               pltpu.VMEM((N_SLAB, TL, SLAB_W), F32),
               pltpu.VMEM((nseg * (seg + SUBLANES), D_CONV), F32),
               pltpu.VMEM((N_SLAB, SLAB_W), F32),
               pltpu.VMEM((2, TL * ROW_TILE, LANES), F32),
               pltpu.VMEM((SUBLANES, TL), jnp.int32),
               pltpu.SMEM((SUBLANES, TL), jnp.int32),
               pltpu.VMEM((1, LANES), F32),
               pltpu.SemaphoreType.DMA((2,)),
               pltpu.SemaphoreType.DMA((1,))]
    return pl.pallas_call(
        functools.partial(_mixer_kernel, nseg, seg, carry_tiles, n_alias, cap_rows, tile0),
        grid=(groups, tiles_per_group),
        in_specs=in_specs, out_specs=out_specs, out_shape=out_shape,
        scratch_shapes=scratch,
        input_output_aliases=aliases,
        compiler_params=pltpu.CompilerParams(dimension_semantics=("arbitrary", "arbitrary"),
                                             vmem_limit_bytes=VMEM_LIMIT),
        name="mixer_prompt" if carry_tiles else "mixer_sample",
    )(x3, mod, h0, cbuf, *weights, cnt0, *aliased)


W_CAST_ROWS = 128


def _expert_kernel(n_tok, row_ref, be_ref, nv_ref, nu_ref, xs_ref, wgu_ref, bgu_ref, wd_ref, bd_ref,
                   yk_hbm, wgu_bf, wd_bf, out_a, out_b, slotv_ref, slots_ref, sem_o, sem_v):
    i = pl.program_id(0)
    n_used = nu_ref[0]
    outs = (out_a, out_b)
    prev = jnp.maximum(i - 1, 0)
    new_expert = jnp.logical_or(i == 0, be_ref[i] != be_ref[prev])

    @pl.when(jnp.logical_and(new_expert, i < n_used))
    def _():
        def cast_gu(r, c):
            rows = pl.ds(pl.multiple_of(r * W_CAST_ROWS, W_CAST_ROWS), W_CAST_ROWS)
            wgu_bf[rows, :] = wgu_ref[0, rows, :].astype(BF16)
            return c
        lax.fori_loop(0, D_MODEL // W_CAST_ROWS, cast_gu, 0)

        def cast_d(r, c):
            rows = pl.ds(pl.multiple_of(r * W_CAST_ROWS, W_CAST_ROWS), W_CAST_ROWS)
            wd_bf[rows, :] = wd_ref[0, rows, :].astype(BF16)
            return c
        lax.fori_loop(0, D_FF // W_CAST_ROWS, cast_d, 0)

    def issue_rows(src, t0, count):
        for q in range(count):
            t = t0 + q
            dst = pl.multiple_of(slots_ref[src, 0, t], ROW_TILE)
            row0 = pl.multiple_of(t * ROW_TILE, ROW_TILE)
            pltpu.make_async_copy(outs[src].at[pl.ds(row0, ROW_TILE), :],
                                  yk_hbm.at[pl.ds(dst, ROW_TILE), :],
                                  sem_o.at[src]).start(priority=q % DMA_THREADS)

    def wait_rows(sl):
        pltpu.make_async_copy(outs[sl], yk_hbm.at[pl.ds(0, E_BLK * ROW_TILE), :], sem_o.at[sl]).wait()

    lane = lax.broadcasted_iota(jnp.int32, (E_BLK, LANES), 1)
    row = lax.broadcasted_iota(jnp.int32, (E_BLK, 1), 0)
    spare0 = float(TOP_K * n_tok)

    @pl.when(i == 0)
    def _():
        t_i = lax.broadcasted_iota(jnp.int32, (SUBLANES, E_BLK), 1)
        slotv_ref[...] = (TOP_K * n_tok + t_i) * ROW_TILE
        first = pltpu.make_async_copy(slotv_ref, slots_ref.at[1], sem_v.at[0])
        first.start()
        first.wait()
        out_b[...] = jnp.zeros(out_b.shape, F32)

    def block(cur):
        prv = 1 - cur

        @pl.when(i < n_used)
        def _():
            record = xs_ref[pl.ds(PACK_ROWS, E_BLK, stride=ROW_TILE), :]
            hit = jnp.logical_and(record.astype(jnp.int32) == be_ref[i], lane < TOP_K)
            slot_f = jnp.sum(jnp.where(hit, pltpu.roll(record, LANES - TOP_K, 1), 0.0), axis=-1, keepdims=True)
            live = row < nv_ref[i]
            slot_f = jnp.where(live, slot_f, spare0 + row.astype(F32))
            slot_mat = jnp.where(lane == 0, slot_f * float(ROW_TILE), 0.0)
            slotv_ref[...] = slot_mat.T[:SUBLANES, :].astype(jnp.int32)
            to_smem = pltpu.make_async_copy(slotv_ref, slots_ref.at[cur], sem_v.at[0])
            to_smem.start()

            issue_rows(prv, 0, E_BLK)

            words = [lax.bitcast_convert_type(xs_ref[pl.ds(c, E_BLK, stride=ROW_TILE), :], jnp.int32)
                     for c in range(PACK_ROWS)]
            lo = [lax.bitcast_convert_type(lax.shift_left(w, 16), F32) for w in words]
            hi = [lax.bitcast_convert_type(w & HI16_MASK, F32) for w in words]
            xb = jnp.where(live, jnp.concatenate(lo + hi, axis=1), 0.0).astype(BF16)
            hgu = _dot(xb, wgu_bf[...]) + bgu_ref[0]
            g = jnp.minimum(hgu[:, :D_FF], SWIGLU_LIMIT)
            up = jnp.clip(hgu[:, D_FF:], -SWIGLU_LIMIT, SWIGLU_LIMIT)
            act = (up + 1.0) * (g * jax.nn.sigmoid(SWIGLU_ALPHA * g))
            y = _dot(act.astype(BF16), wd_bf[...]) + bd_ref[0]
            for c in range(ROW_TILE):
                outs[cur][pl.ds(c, E_BLK, stride=ROW_TILE), :] = y[:, c * LANES:(c + 1) * LANES]
            to_smem.wait()
            wait_rows(prv)

        @pl.when(i == n_used)
        def _():
            def tail(t8, c):
                issue_rows(prv, t8 * SUBLANES, SUBLANES)
                return c
            lax.fori_loop(0, E_BLK // SUBLANES, tail, 0)
            wait_rows(prv)

    for parity in range(2):
        @pl.when(i % 2 == parity)
        def _(parity=parity):
            block(parity)


def _expert_call(blk_row, blk_e, blk_valid, n_used, xs, w_gu, b_gu, w_down, b_down, n_tok):
    nb = blk_row.shape[0]
    grid_spec = pltpu.PrefetchScalarGridSpec(
        num_scalar_prefetch=4,
        grid=(nb,),
        in_specs=[
            pl.BlockSpec((E_BLK * ROW_TILE, LANES), lambda i, br, be, nv, nu: (br[i], 0)),
            pl.BlockSpec((1, D_MODEL, 2 * D_FF), lambda i, br, be, nv, nu: (be[i], 0, 0)),
            pl.BlockSpec((1, 1, 2 * D_FF), lambda i, br, be, nv, nu: (be[i], 0, 0)),
            pl.BlockSpec((1, D_FF, D_MODEL), lambda i, br, be, nv, nu: (be[i], 0, 0)),
            pl.BlockSpec((1, 1, D_MODEL), lambda i, br, be, nv, nu: (be[i], 0, 0)),
        ],
        out_specs=pl.BlockSpec(memory_space=pl.ANY),
        scratch_shapes=[pltpu.VMEM((D_MODEL, 2 * D_FF), BF16),
                        pltpu.VMEM((D_FF, D_MODEL), BF16),
                        pltpu.VMEM((E_BLK * ROW_TILE, LANES), F32),
                        pltpu.VMEM((E_BLK * ROW_TILE, LANES), F32),
                        pltpu.VMEM((SUBLANES, E_BLK), jnp.int32),
                        pltpu.SMEM((2, SUBLANES, E_BLK), jnp.int32),
                        pltpu.SemaphoreType.DMA((2,)),
                        pltpu.SemaphoreType.DMA((1,))],
    )
    return pl.pallas_call(
        functools.partial(_expert_kernel, n_tok),
        grid_spec=grid_spec,
        out_shape=jax.ShapeDtypeStruct(((TOP_K * n_tok + E_BLK) * ROW_TILE, LANES), F32),
        compiler_params=pltpu.CompilerParams(dimension_semantics=("arbitrary",),
                                             vmem_limit_bytes=VMEM_LIMIT),
        name="experts",
    )(blk_row, blk_e, blk_valid, n_used, xs, w_gu, b_gu, w_down, b_down)


def _combine_kernel(nseg, seg, y0_ref, y1_ref, y2_ref, y3_ref, x1_ref, gate_ref, mod_ref, g2_ref, b2_ref,
                    out_ref):
    gates = gate_ref[...]
    ffn = None
    for k, y_ref in enumerate((y0_ref, y1_ref, y2_ref, y3_ref)):
        rows = jnp.concatenate(
            [y_ref[pl.ds(c, TL, stride=ROW_TILE), :] for c in range(ROW_TILE)], axis=1)
        term = gates[:, k:k + 1] * rows
        ffn = term if ffn is None else ffn + term
    if nseg == 1:
        gate2 = mod_ref[0, 5:6, :]
    else:
        gate2 = jnp.concatenate(
            [jnp.broadcast_to(mod_ref[s, 5:6, :], (seg, D_MODEL)) for s in range(nseg)], axis=0)
    xa = DEEPNORM_ALPHA * x1_ref[...] + (1.0 + gate2) * ffn
    out_ref[...] = _layer_norm(xa, g2_ref[...], b2_ref[...])


def _combine_call(yk, x1, gates, mod, ln2_g, ln2_b, nseg, seg, n_tiles, tile0, tiles_per_group, n_tok):
    tiles_total = n_tok // TL

    def yk_spec(k):
        return pl.BlockSpec((TL * ROW_TILE, LANES), lambda i, k=k: (k * tiles_total + tile0 + i, 0))

    in_specs = [yk_spec(k) for k in range(TOP_K)] + [
        pl.BlockSpec((TL, D_MODEL), lambda i: (tile0 + i, 0)),
        pl.BlockSpec((TL, LANES), lambda i: (tile0 + i, 0)),
        pl.BlockSpec((nseg, 6, D_MODEL), lambda i: (i // tiles_per_group, 0, 0)),
        pl.BlockSpec((1, D_MODEL), lambda i: (0, 0)),
        pl.BlockSpec((1, D_MODEL), lambda i: (0, 0)),
    ]
    return pl.pallas_call(
        functools.partial(_combine_kernel, nseg, seg),
        grid=(n_tiles,),
        in_specs=in_specs,
        out_specs=pl.BlockSpec((TL, D_MODEL), lambda i: (i, 0)),
        out_shape=jax.ShapeDtypeStruct((n_tiles * TL, D_MODEL), F32),
        compiler_params=pltpu.CompilerParams(dimension_semantics=("arbitrary",),
                                             vmem_limit_bytes=VMEM_LIMIT),
        name="combine_prompt" if nseg == 1 else "combine_sample",
    )(yk, yk, yk, yk, x1, gates, mod, ln2_g, ln2_b)


def _s5_tables(lam_re, lam_im, log_dt, b_re, b_im, c_re, c_im):
    dt = jnp.exp(log_dt.astype(F32))[:, None]
    lam = lax.complex(lam_re.astype(F32), lam_im.astype(F32))
    lam_dt = lam * dt
    lam_bar = jnp.exp(lam_dt)
    b_bar = ((lam_bar - 1.0) / lam)[..., None] * lax.complex(b_re.astype(F32), b_im.astype(F32))
    gl = S5_GROUPS // N_SLAB
    eye = jnp.eye(gl, dtype=F32)

    def b_slab(part):
        a = part.reshape(N_SLAB, gl, S5_STATE, S5_GROUP)
        return jnp.einsum('sgph,gk->sghkp', a, eye).reshape(N_SLAB, gl * S5_GROUP, gl * S5_STATE)

    bmat = jnp.concatenate([b_slab(b_bar.real), b_slab(b_bar.imag)], axis=-1).astype(BF16)

    def c_slab(part):
        a = part.reshape(N_SLAB, gl, S5_GROUP, S5_STATE)
        return jnp.einsum('sghp,gk->sgpkh', a, eye).reshape(N_SLAB, gl * S5_STATE, gl * S5_GROUP)

    cmat = jnp.concatenate([c_slab(c_re.astype(F32)), -c_slab(c_im.astype(F32))], axis=1).astype(BF16)

    row = jnp.arange(SUBLANES, dtype=F32)[:, None, None]

    def power(k):
        return jnp.exp(lam_dt[None] * k)

    tabs = []
    for d in (1, 2, 4):
        pw = power(jnp.full_like(row, float(d)))
        mask = (row >= d).astype(F32)
        tabs += [pw.real * mask, pw.imag * mask]
    pw = power(row + 1.0)
    tabs += [pw.real, pw.imag]
    tab = jnp.stack(tabs, axis=0)
    tab = tab.reshape(8, SUBLANES, N_SLAB, gl * S5_STATE).transpose(2, 0, 1, 3)
    return bmat, cmat, tab


def _state_to_slab(re, im):
    n = re.shape[0]
    return jnp.concatenate([re.reshape(n, N_SLAB, HALF), im.reshape(n, N_SLAB, HALF)], axis=-1).astype(F32)


def _slab_to_state(s):
    n = s.shape[0]
    re = s[:, :, :HALF].reshape(n, S5_GROUPS, S5_STATE)
    im = s[:, :, HALF:].reshape(n, S5_GROUPS, S5_STATE)
    return re, im


def _block_table(counts, n_tok):
    n_assign = n_tok * TOP_K
    n_blocks = (n_assign + N_EXPERTS * (E_BLK - 1) + E_BLK - 1) // E_BLK + 1
    cap_blocks = n_tok // E_BLK
    nblk = (counts + E_BLK - 1) // E_BLK
    cum = jnp.cumsum(nblk)
    start = cum - nblk
    n_used = cum[-1]
    i = jnp.arange(n_blocks, dtype=jnp.int32)
    ii = jnp.maximum(jnp.minimum(i, n_used - 1), 0)
    e = jnp.minimum(jnp.sum((cum[None, :] <= ii[:, None]).astype(jnp.int32), axis=1), N_EXPERTS - 1)
    b = ii - start[e]
    blk_row = (e * cap_blocks + b).astype(jnp.int32)
    valid = jnp.where(i < n_used, jnp.minimum(E_BLK, counts[e] - b * E_BLK), 0).astype(jnp.int32)
    return blk_row, e, valid, n_used.astype(jnp.int32).reshape(1)


def kernel(x_prompt, x_sample, c_prompt, c_sample, state_s5_re, state_s5_im, state_conv, w_ada, b_ada, w_in,
           s5_lam_re, s5_lam_im, s5_log_dt, s5_b_re, s5_b_im, s5_c_re, s5_c_im, s5_d, w_glu, b_glu, conv_w,
           w_out, ln1_g, ln1_b, w_router, b_router, w_gu, b_gu, w_down, b_down, ln2_g, ln2_b):
    assert DEPTH == 1 and w_ada.shape[0] == 1
    n_p, l_p, _ = x_prompt.shape
    n_s, l_s, _ = x_sample.shape
    t_p, t_s = n_p * l_p, n_s * l_s
    n_tok = t_p + t_s
    tiles_p, tiles_s = t_p // TL, t_s // TL
    seq_per_tile = TL // l_s
    assert l_p % TL == 0 and TL % l_s == 0 and n_s % seq_per_tile == 0

    c_all = jnp.concatenate([c_prompt, c_sample], axis=0).astype(F32)
    pad = (-c_all.shape[0]) % SUBLANES
    c_all = jnp.pad(c_all, ((0, pad), (0, 0)))
    mod = _ada_call(c_all, w_ada[0], b_ada[0].reshape(1, -1)).reshape(-1, 6, D_MODEL)
    mod_p, mod_s = mod[:n_p], mod[n_p:n_p + n_s]

    bmat, cmat, tab = _s5_tables(s5_lam_re[0], s5_lam_im[0], s5_log_dt[0], s5_b_re[0], s5_b_im[0],
                                 s5_c_re[0], s5_c_im[0])
    wr = jnp.pad(w_router[0], ((0, 0), (0, LANES - N_EXPERTS))).astype(BF16)
    br = jnp.pad(b_router[0].astype(F32), (0, LANES - N_EXPERTS), constant_values=NEG_BIG).reshape(1, LANES)
    weights = (w_in[0].astype(BF16), bmat, cmat, tab, s5_d[0].reshape(1, D_S5).astype(F32),
               w_glu[0].astype(BF16), b_glu[0].reshape(1, D_S5).astype(F32), conv_w[0].astype(F32),
               w_out[0].astype(BF16), ln1_g[0].reshape(1, D_MODEL).astype(F32),
               ln1_b[0].reshape(1, D_MODEL).astype(F32), wr, br)

    h0_p = jnp.zeros((n_p, N_SLAB, SLAB_W), F32)
    cb_p = jnp.zeros((n_p, CONV_W - 1, D_CONV), F32)
    h0_s = _state_to_slab(state_s5_re[0], state_s5_im[0])
    cb_s = state_conv[0].astype(F32)

    assert n_tok % E_BLK == 0 and E_BLK % TL == 0
    cnt0 = jnp.zeros((1, LANES), F32)
    outs_p = _mixer_call(x_prompt, mod_p, h0_p, cb_p, weights, cnt0, 1, TL, l_p // TL, 0, n_tok, ())
    x1, gates, xs, cnt_p, s_p, conv_p = outs_p
    xs3 = x_sample.reshape(n_s // seq_per_tile, TL, D_MODEL)
    outs_s = _mixer_call(xs3, mod_s, h0_s, cb_s, weights, cnt_p, seq_per_tile, l_s, 1, tiles_p, n_tok,
                         (x1, gates, xs))
    x1, gates, xs, cnt, s_s, conv_s = outs_s

    counts = cnt[0, :N_EXPERTS].astype(jnp.int32)
    blk_row, blk_e, blk_valid, n_used = _block_table(counts, n_tok)
    yk = _expert_call(blk_row, blk_e, blk_valid, n_used, xs, w_gu[0].astype(F32),
                      b_gu[0].reshape(N_EXPERTS, 1, 2 * D_FF).astype(F32), w_down[0].astype(F32),
                      b_down[0].reshape(N_EXPERTS, 1, D_MODEL).astype(F32), n_tok)
    g2 = ln2_g[0].reshape(1, D_MODEL).astype(F32)
    b2 = ln2_b[0].reshape(1, D_MODEL).astype(F32)
    y_p = _combine_call(yk, x1, gates, mod_p, g2, b2, 1, TL, tiles_p, 0, l_p // TL, n_tok)
    y_s = _combine_call(yk, x1, gates, mod_s, g2, b2, seq_per_tile, l_s, tiles_s, tiles_p, 1, n_tok)

    p_re, p_im = _slab_to_state(s_p)
    s_re, s_im = _slab_to_state(s_s)
    return (y_p.reshape(n_p, l_p, D_MODEL), y_s.reshape(n_s, l_s, D_MODEL),
            p_re[None], p_im[None], conv_p[None], s_re[None], s_im[None], conv_s[None])
```

```python
import functools
import math

import jax
import jax.numpy as jnp
from jax import lax
from jax.experimental import pallas as pl
from jax.experimental.pallas import tpu as pltpu

F32 = jnp.float32
BF16 = jnp.bfloat16

D_MODEL = 1024
DEPTH = 1
D_S5 = 512
D_CONV = 512
S5_GROUP = 16
S5_GROUPS = 32
S5_STATE = 64
CONV_W = 3
N_EXPERTS = 32
TOP_K = 4
D_FF = 1024
SWIGLU_LIMIT = 7.0
SWIGLU_ALPHA = 1.702
LN_EPS = 1e-5
DEEPNORM_ALPHA = (2.0 * DEPTH) ** 0.25

TL = 256
SUBLANES = 8
LANES = 128
N_SLAB = 4
SLAB_W = 1024
HALF = 512
E_BLK = 256
ROW_TILE = D_MODEL // LANES
PACK_ROWS = HALF // LANES
HI16_MASK = -65536
DMA_THREADS = 2
NEG_BIG = -1e30
VMEM_LIMIT = 56 * 1024 * 1024


def _dot(a, b):
    return jnp.dot(a, b, preferred_element_type=F32)


def _layer_norm(x, g, b):
    mu = jnp.mean(x, axis=-1, keepdims=True)
    xc = x - mu
    var = jnp.mean(xc * xc, axis=-1, keepdims=True)
    return xc * lax.rsqrt(var + LN_EPS) * g + b


def _ada_kernel(c_ref, w_ref, b_ref, o_ref):
    c = c_ref[...]
    s = c * jax.nn.sigmoid(c)
    o_ref[...] = _dot(s.astype(BF16), w_ref[...].astype(BF16)) + b_ref[...]


def _ada_call(c_all, w_ada, b_ada):
    rows = c_all.shape[0]
    n_out = w_ada.shape[1]
    tn = 768
    return pl.pallas_call(
        _ada_kernel,
        grid=(n_out // tn,),
        in_specs=[pl.BlockSpec((rows, D_MODEL), lambda i: (0, 0)),
                  pl.BlockSpec((D_MODEL, tn), lambda i: (0, i)),
                  pl.BlockSpec((1, tn), lambda i: (0, i))],
        out_specs=pl.BlockSpec((rows, tn), lambda i: (0, i)),
        out_shape=jax.ShapeDtypeStruct((rows, n_out), F32),
        compiler_params=pltpu.CompilerParams(dimension_semantics=("arbitrary",),
                                             vmem_limit_bytes=VMEM_LIMIT),
        name="ada_mod",
    )(c_all, w_ada, b_ada)


def _mixer_kernel(nseg, seg, carry_tiles, n_alias, cap_rows, tile0,
                  x_ref, mod_ref, h0_ref, cbuf_ref, win_ref, bmat_ref, cmat_ref, tab_ref, d_ref,
                  wglu_ref, bglu_ref, convw_ref, wout_ref, g1_ref, b1_ref, wr_ref, br_ref, cnt0_ref, *rest):
    rest = rest[n_alias:]
    (x1_ref, gate_ref, xs_hbm, cnt_out_ref, sout_ref, cout_ref,
     p_ref, bu_ref, xcs_ref, carry_ref, h2s_ref, posv_ref, poss_ref, cnt_ref, sem_s, sem_v) = rest
    g_id = pl.program_id(0)
    j = pl.program_id(1)
    step = g_id * pl.num_programs(1) + j
    is_last = step == pl.num_programs(0) * pl.num_programs(1) - 1
    slot = step % 2

    def mod_rows(k):
        if nseg == 1:
            return mod_ref[0, k:k + 1, :]
        return jnp.concatenate(
            [jnp.broadcast_to(mod_ref[s, k:k + 1, :], (seg, D_MODEL)) for s in range(nseg)], axis=0)

    x = x_ref[0]
    h = x * (1.0 + mod_rows(1)) + mod_rows(0)
    p_ref[...] = _dot(h.astype(BF16), win_ref[...])

    for i in range(N_SLAB):
        u_i = p_ref[:, i * LANES:(i + 1) * LANES].astype(BF16)
        bu_ref[i] = _dot(u_i, bmat_ref[i])

    prev = 1 - slot
    smem_copy = pltpu.make_async_copy(posv_ref, poss_ref, sem_v.at[0])

    @pl.when(step == 0)
    def _():
        k_i = lax.broadcasted_iota(jnp.int32, (SUBLANES, TL), 0)
        t_i = lax.broadcasted_iota(jnp.int32, (SUBLANES, TL), 1)
        posv_ref[...] = (N_EXPERTS * cap_rows + k_i * TL + t_i) * ROW_TILE
        smem_copy.start()
        h2s_ref[...] = jnp.zeros(h2s_ref.shape, F32)

    smem_copy.wait()

    def issue_rows(src_slot, t0, n_rows):
        for q in range(n_rows):
            t = t0 + q
            src = pl.multiple_of(t * ROW_TILE, ROW_TILE)
            for k in range(TOP_K):
                dst = pl.multiple_of(poss_ref[k, t], ROW_TILE)
                pltpu.make_async_copy(h2s_ref.at[src_slot, pl.ds(src, ROW_TILE), :],
                                      xs_hbm.at[pl.ds(dst, ROW_TILE), :],
                                      sem_s.at[src_slot]).start(priority=k % DMA_THREADS)

    def wait_scatter(sl):
        for _ in range(TOP_K):
            pltpu.make_async_copy(h2s_ref.at[sl], xs_hbm.at[pl.ds(0, TL * ROW_TILE), :], sem_s.at[sl]).wait()

    if carry_tiles:
        @pl.when(j == 0)
        def _():
            carry_ref[...] = h0_ref[0]

    for s in range(nseg):
        if carry_tiles:
            init = tuple(carry_ref[i:i + 1, :] for i in range(N_SLAB))
        else:
            init = tuple(h0_ref[s, i:i + 1, :] for i in range(N_SLAB))

        def scan_body(r, carry, s=s):
            issue_rows(prev, s * seg + r * (2 * SUBLANES), 2 * SUBLANES)
            new = []
            for i in range(N_SLAB):
                cr = carry[i][:, :HALF]
                ci = carry[i][:, HALF:]
                for half in range(2):
                    row0 = pl.multiple_of(s * seg + r * (2 * SUBLANES) + half * SUBLANES, SUBLANES)
                    blk = bu_ref[i, pl.ds(row0, SUBLANES), :]
                    xr = blk[:, :HALF]
                    xi = blk[:, HALF:]
                    for k, d in enumerate((1, 2, 4)):
                        lr = tab_ref[i, 2 * k]
                        li = tab_ref[i, 2 * k + 1]
                        rr = pltpu.roll(xr, d, 0)
                        ri = pltpu.roll(xi, d, 0)
                        xr, xi = xr + (lr * rr - li * ri), xi + (lr * ri + li * rr)
                    pr = tab_ref[i, 6]
                    pi_ = tab_ref[i, 7]
                    crb = jnp.broadcast_to(cr, (SUBLANES, HALF))
                    cib = jnp.broadcast_to(ci, (SUBLANES, HALF))
                    xr, xi = xr + (pr * crb - pi_ * cib), xi + (pr * cib + pi_ * crb)
                    bu_ref[i, pl.ds(row0, SUBLANES), :] = jnp.concatenate([xr, xi], axis=1)
                    cr = xr[SUBLANES - 1:SUBLANES, :]
                    ci = xi[SUBLANES - 1:SUBLANES, :]
                new.append(jnp.concatenate([cr, ci], axis=1))
            return tuple(new)

        final = lax.fori_loop(0, seg // (2 * SUBLANES), scan_body, init)
        for i in range(N_SLAB):
            sout_ref[s, i:i + 1, :] = final[i]
            if carry_tiles:
                carry_ref[i:i + 1, :] = final[i]

    y = jnp.concatenate([_dot(bu_ref[i].astype(BF16), cmat_ref[i]) for i in range(N_SLAB)], axis=1)
    y = y + d_ref[...] * p_ref[:, :D_S5]
    z = jax.nn.gelu(y)
    z = z * jax.nn.sigmoid(_dot(z.astype(BF16), wglu_ref[...]) + bglu_ref[...])

    xc = p_ref[:, D_S5 + D_CONV:D_S5 + 2 * D_CONV] * p_ref[:, D_S5 + 2 * D_CONV:]
    w0 = convw_ref[0:1, :]
    w1 = convw_ref[1:2, :]
    w2 = convw_ref[2:3, :]
    conv_parts = []
    for s in range(nseg):
        base = s * (seg + SUBLANES)
        xc_s = xc[s * seg:(s + 1) * seg]
        tail = xc_s[seg - 2:seg]
        xcs_ref[pl.ds(base + SUBLANES, seg), :] = xc_s
        if carry_tiles:
            @pl.when(j == 0)
            def _(s=s, base=base):
                xcs_ref[pl.ds(base + SUBLANES - 2, 2), :] = cbuf_ref[s]
        else:
            xcs_ref[pl.ds(base + SUBLANES - 2, 2), :] = cbuf_ref[s]
        xm1 = xcs_ref[pl.ds(base + SUBLANES - 1, seg), :]
        xm2 = xcs_ref[pl.ds(base + SUBLANES - 2, seg), :]
        conv_parts.append(w0 * xm2 + w1 * xm1 + w2 * xc_s)
        cout_ref[s] = tail
        if carry_tiles:
            xcs_ref[pl.ds(base + SUBLANES - 2, 2), :] = tail
    conv = conv_parts[0] if nseg == 1 else jnp.concatenate(conv_parts, axis=0)
    y_b = p_ref[:, D_S5:D_S5 + D_CONV] * conv

    mix = _dot(jnp.concatenate([z, y_b], axis=1).astype(BF16), wout_ref[...])
    x1 = _layer_norm(DEEPNORM_ALPHA * x + (1.0 + mod_rows(2)) * mix, g1_ref[...], b1_ref[...])
    x1_ref[...] = x1
    h2 = x1 * (1.0 + mod_rows(4)) + mod_rows(3)
    lo_bits = lax.bitcast_convert_type(h2[:, :HALF].astype(BF16).astype(F32), jnp.int32)
    hi_bits = lax.bitcast_convert_type(h2[:, HALF:].astype(BF16).astype(F32), jnp.int32)
    packed = lax.bitcast_convert_type(hi_bits | lax.shift_right_logical(lo_bits, 16), F32)
    for c in range(PACK_ROWS):
        h2s_ref[slot, pl.ds(c, TL, stride=ROW_TILE), :] = packed[:, c * LANES:(c + 1) * LANES]

    logits = _dot(h2.astype(BF16), wr_ref[...]) + br_ref[...]
    lane = lax.broadcasted_iota(jnp.int32, logits.shape, 1)
    lane_f = lane.astype(F32)
    vals, sels, hots = [], [], []
    cur = logits
    for _ in range(TOP_K):
        m = jnp.max(cur, axis=-1, keepdims=True)
        am = jnp.min(jnp.where(cur == m, lane_f, float(LANES)), axis=-1, keepdims=True)
        hot = lane_f == am
        vals.append(m)
        sels.append(am)
        hots.append(hot)
        cur = jnp.where(hot, -jnp.inf, cur)
    exps = [jnp.exp(v - vals[0]) for v in vals]
    inv = 1.0 / (exps[0] + exps[1] + exps[2] + exps[3])
    gate_out = jnp.zeros(logits.shape, F32)
    for k in range(TOP_K):
        gate_out = jnp.where(lane == k, exps[k] * inv, gate_out)
    gate_ref[...] = gate_out

    tok = ((tile0 + step) * TL + lax.broadcasted_iota(jnp.int32, logits.shape, 0)).astype(F32)
    record = jnp.zeros(logits.shape, F32)
    for k in range(TOP_K):
        record = jnp.where(lane == k, sels[k], record)
        record = jnp.where(lane == TOP_K + k, tok + float(k * cap_rows), record)
    h2s_ref[slot, pl.ds(PACK_ROWS, TL, stride=ROW_TILE), :] = record

    @pl.when(step == 0)
    def _():
        cnt_ref[...] = cnt0_ref[...]

    chosen = jnp.zeros(logits.shape, F32)
    for k in range(TOP_K):
        chosen = jnp.where(hots[k], 1.0, chosen)
    r_i = lax.broadcasted_iota(jnp.int32, (TL, TL), 0)
    c_i = lax.broadcasted_iota(jnp.int32, (TL, TL), 1)
    before = jnp.where(c_i < r_i, 1.0, 0.0).astype(BF16)
    rank_base = _dot(before, chosen.astype(BF16)) + cnt_ref[...]
    pos_mat = jnp.zeros(logits.shape, F32)
    for k in range(TOP_K):
        rank_k = jnp.sum(jnp.where(hots[k], rank_base, 0.0), axis=-1, keepdims=True)
        pos_mat = jnp.where(lane == k, (sels[k] * float(cap_rows) + rank_k) * float(ROW_TILE), pos_mat)
    cnt_ref[...] = cnt_ref[...] + jnp.sum(chosen, axis=0, keepdims=True)
    cnt_out_ref[...] = cnt_ref[...]
    pos_t = pos_mat.T[:SUBLANES, :].astype(jnp.int32)
    posv_ref[...] = pos_t
    smem_copy.start()
    wait_scatter(prev)

    @pl.when(is_last)
    def _():
        smem_copy.wait()

        def tail_body(t8, c):
            issue_rows(slot, t8 * SUBLANES, SUBLANES)
            return c
        lax.fori_loop(0, TL // SUBLANES, tail_body, 0)
        wait_scatter(slot)


def _mixer_call(x3, mod, h0, cbuf, weights, cnt0, nseg, seg, tiles_per_group, tile0, n_tok, aliased):
    groups = x3.shape[0]
    nseq = mod.shape[0]
    carry_tiles = nseg == 1
    n_alias = len(aliased)
    n_tiles = n_tok // TL
    cap_rows = n_tok

    def full(a):
        nd = a.ndim
        return pl.BlockSpec(a.shape, lambda g, j, nd=nd: (0,) * nd)

    def tile_map(g, j):
        return (tile0 + g * tiles_per_group + j, 0)

    in_specs = [pl.BlockSpec((1, TL, D_MODEL), lambda g, j: (g, j, 0)),
                pl.BlockSpec((nseg, 6, D_MODEL), lambda g, j: (g, 0, 0)),
                pl.BlockSpec((nseg, N_SLAB, SLAB_W), lambda g, j: (g, 0, 0)),
                pl.BlockSpec((nseg, CONV_W - 1, D_CONV), lambda g, j: (g, 0, 0))]
    in_specs += [full(w) for w in weights]
    in_specs += [full(cnt0)]
    in_specs += [pl.BlockSpec(memory_space=pl.ANY)] * n_alias
    out_shape = [jax.ShapeDtypeStruct((n_tok, D_MODEL), F32),
                 jax.ShapeDtypeStruct((n_tok, LANES), F32),
                 jax.ShapeDtypeStruct(((N_EXPERTS * cap_rows + TOP_K * TL) * ROW_TILE, LANES), F32),
                 jax.ShapeDtypeStruct((1, LANES), F32),
                 jax.ShapeDtypeStruct((nseq, N_SLAB, SLAB_W), F32),
                 jax.ShapeDtypeStruct((nseq, CONV_W - 1, D_CONV), F32)]
    out_specs = [pl.BlockSpec((TL, D_MODEL), tile_map),
                 pl.BlockSpec((TL, LANES), tile_map),
                 pl.BlockSpec(memory_space=pl.ANY),
                 pl.BlockSpec((1, LANES), lambda g, j: (0, 0)),
                 pl.BlockSpec((nseg, N_SLAB, SLAB_W), lambda g, j: (g, 0, 0)),
                 pl.BlockSpec((nseg, CONV_W - 1, D_CONV), lambda g, j: (g, 0, 0))]
    n_in = 4 + len(weights) + 1
    aliases = {n_in + k: k for k in range(n_alias)}
    scratch = [pltpu.VMEM((TL, 2 * D_MODEL), F32),
               pltpu.VMEM((N_SLAB, TL, SLAB_W), F32),
               pltpu.VMEM((nseg * (seg + SUBLANES), D_CONV), F32),
               pltpu.VMEM((N_SLAB, SLAB_W), F32),
               pltpu.VMEM((2, TL * ROW_TILE, LANES), F32),
               pltpu.VMEM((SUBLANES, TL), jnp.int32),
               pltpu.SMEM((SUBLANES, TL), jnp.int32),
               pltpu.VMEM((1, LANES), F32),
               pltpu.SemaphoreType.DMA((2,)),
               pltpu.SemaphoreType.DMA((1,))]
    return pl.pallas_call(
        functools.partial(_mixer_kernel, nseg, seg, carry_tiles, n_alias, cap_rows, tile0),
        grid=(groups, tiles_per_group),
        in_specs=in_specs, out_specs=out_specs, out_shape=out_shape,
        scratch_shapes=scratch,
        input_output_aliases=aliases,
        compiler_params=pltpu.CompilerParams(dimension_semantics=("arbitrary", "arbitrary"),
                                             vmem_limit_bytes=VMEM_LIMIT),
        name="mixer_prompt" if carry_tiles else "mixer_sample",
    )(x3, mod, h0, cbuf, *weights, cnt0, *aliased)


W_CAST_ROWS = 128


def _expert_kernel(n_tok, row_ref, be_ref, nv_ref, nu_ref, xs_ref, wgu_ref, bgu_ref, wd_ref, bd_ref,
                   yk_hbm, wgu_bf, wd_bf, out_a, out_b, slotv_ref, slots_ref, sem_o, sem_v):
    i = pl.program_id(0)
    n_used = nu_ref[0]
    outs = (out_a, out_b)
    prev = jnp.maximum(i - 1, 0)
    new_expert = jnp.logical_or(i == 0, be_ref[i] != be_ref[prev])

    @pl.when(jnp.logical_and(new_expert, i < n_used))
    def _():
        def cast_gu(r, c):
            rows = pl.ds(pl.multiple_of(r * W_CAST_ROWS, W_CAST_ROWS), W_CAST_ROWS)
            wgu_bf[rows, :] = wgu_ref[0, rows, :].astype(BF16)
            return c
        lax.fori_loop(0, D_MODEL // W_CAST_ROWS, cast_gu, 0)

        def cast_d(r, c):
            rows = pl.ds(pl.multiple_of(r * W_CAST_ROWS, W_CAST_ROWS), W_CAST_ROWS)
            wd_bf[rows, :] = wd_ref[0, rows, :].astype(BF16)
            return c
        lax.fori_loop(0, D_FF // W_CAST_ROWS, cast_d, 0)

    def issue_rows(src, t0, count):
        for q in range(count):
            t = t0 + q
            dst = pl.multiple_of(slots_ref[src, 0, t], ROW_TILE)
            row0 = pl.multiple_of(t * ROW_TILE, ROW_TILE)
            pltpu.make_async_copy(outs[src].at[pl.ds(row0, ROW_TILE), :],
                                  yk_hbm.at[pl.ds(dst, ROW_TILE), :],
                                  sem_o.at[src]).start(priority=q % DMA_THREADS)

    def wait_rows(sl):
        pltpu.make_async_copy(outs[sl], yk_hbm.at[pl.ds(0, E_BLK * ROW_TILE), :], sem_o.at[sl]).wait()

    lane = lax.broadcasted_iota(jnp.int32, (E_BLK, LANES), 1)
    row = lax.broadcasted_iota(jnp.int32, (E_BLK, 1), 0)
    spare0 = float(TOP_K * n_tok)

    @pl.when(i == 0)
    def _():
        t_i = lax.broadcasted_iota(jnp.int32, (SUBLANES, E_BLK), 1)
        slotv_ref[...] = (TOP_K * n_tok + t_i) * ROW_TILE
        pltpu.make_async_copy(slotv_ref, slots_ref.at[1], sem_v.at[0]).start()
        out_b[...] = jnp.zeros(out_b.shape, F32)

    def block(cur):
        prv = 1 - cur

        def smem_copy(dst):
            return pltpu.make_async_copy(slotv_ref, slots_ref.at[dst], sem_v.at[0])

        @pl.when(i < n_used)
        def _():
            smem_copy(prv).wait()
            issue_rows(prv, 0, E_BLK)

            record = xs_ref[pl.ds(PACK_ROWS, E_BLK, stride=ROW_TILE), :]
            hit = jnp.logical_and(record.astype(jnp.int32) == be_ref[i], lane < TOP_K)
            slot_f = jnp.sum(jnp.where(hit, pltpu.roll(record, LANES - TOP_K, 1), 0.0), axis=-1, keepdims=True)
            live = row < nv_ref[i]
            slot_f = jnp.where(live, slot_f, spare0 + row.astype(F32))
            slot_mat = jnp.where(lane == 0, slot_f * float(ROW_TILE), 0.0)
            slotv_ref[...] = slot_mat.T[:SUBLANES, :].astype(jnp.int32)
            smem_copy(cur).start()

            words = [lax.bitcast_convert_type(xs_ref[pl.ds(c, E_BLK, stride=ROW_TILE), :], jnp.int32)
                     for c in range(PACK_ROWS)]
            lo = [lax.bitcast_convert_type(lax.shift_left(w, 16), F32) for w in words]
            hi = [lax.bitcast_convert_type(w & HI16_MASK, F32) for w in words]
            xb = jnp.where(live, jnp.concatenate(lo + hi, axis=1), 0.0).astype(BF16)
            hgu = _dot(xb, wgu_bf[...]) + bgu_ref[0]
            g = jnp.minimum(hgu[:, :D_FF], SWIGLU_LIMIT)
            up = jnp.clip(hgu[:, D_FF:], -SWIGLU_LIMIT, SWIGLU_LIMIT)
            act = (up + 1.0) * (g * jax.nn.sigmoid(SWIGLU_ALPHA * g))
            y = _dot(act.astype(BF16), wd_bf[...]) + bd_ref[0]

            @pl.when(i > 0)
            def _():
                wait_rows(cur)

            for c in range(ROW_TILE):
                outs[cur][pl.ds(c, E_BLK, stride=ROW_TILE), :] = y[:, c * LANES:(c + 1) * LANES]

        @pl.when(i == n_used)
        def _():
            smem_copy(prv).wait()

            def tail(t8, c):
                issue_rows(prv, t8 * SUBLANES, SUBLANES)
                return c
            lax.fori_loop(0, E_BLK // SUBLANES, tail, 0)
            wait_rows(cur)
            wait_rows(prv)

    for parity in range(2):
        @pl.when(i % 2 == parity)
        def _(parity=parity):
            block(parity)


def _expert_call(blk_row, blk_e, blk_valid, n_used, xs, w_gu, b_gu, w_down, b_down, n_tok):
    nb = blk_row.shape[0]
    grid_spec = pltpu.PrefetchScalarGridSpec(
        num_scalar_prefetch=4,
        grid=(nb,),
        in_specs=[
            pl.BlockSpec((E_BLK * ROW_TILE, LANES), lambda i, br, be, nv, nu: (br[i], 0)),
            pl.BlockSpec((1, D_MODEL, 2 * D_FF), lambda i, br, be, nv, nu: (be[i], 0, 0)),
            pl.BlockSpec((1, 1, 2 * D_FF), lambda i, br, be, nv, nu: (be[i], 0, 0)),
            pl.BlockSpec((1, D_FF, D_MODEL), lambda i, br, be, nv, nu: (be[i], 0, 0)),
            pl.BlockSpec((1, 1, D_MODEL), lambda i, br, be, nv, nu: (be[i], 0, 0)),
        ],
        out_specs=pl.BlockSpec(memory_space=pl.ANY),
        scratch_shapes=[pltpu.VMEM((D_MODEL, 2 * D_FF), BF16),
                        pltpu.VMEM((D_FF, D_MODEL), BF16),
                        pltpu.VMEM((E_BLK * ROW_TILE, LANES), F32),
                        pltpu.VMEM((E_BLK * ROW_TILE, LANES), F32),
                        pltpu.VMEM((SUBLANES, E_BLK), jnp.int32),
                        pltpu.SMEM((2, SUBLANES, E_BLK), jnp.int32),
                        pltpu.SemaphoreType.DMA((2,)),
                        pltpu.SemaphoreType.DMA((1,))],
    )
    return pl.pallas_call(
        functools.partial(_expert_kernel, n_tok),
        grid_spec=grid_spec,
        out_shape=jax.ShapeDtypeStruct(((TOP_K * n_tok + E_BLK) * ROW_TILE, LANES), F32),
        compiler_params=pltpu.CompilerParams(dimension_semantics=("arbitrary",),
                                             vmem_limit_bytes=VMEM_LIMIT),
        name="experts",
    )(blk_row, blk_e, blk_valid, n_used, xs, w_gu, b_gu, w_down, b_down)


def _combine_kernel(nseg, seg, y0_ref, y1_ref, y2_ref, y3_ref, x1_ref, gate_ref, mod_ref, g2_ref, b2_ref,
                    out_ref):
    gates = gate_ref[...]
    ffn = None
    for k, y_ref in enumerate((y0_ref, y1_ref, y2_ref, y3_ref)):
        rows = jnp.concatenate(
            [y_ref[pl.ds(c, TL, stride=ROW_TILE), :] for c in range(ROW_TILE)], axis=1)
        term = gates[:, k:k + 1] * rows
        ffn = term if ffn is None else ffn + term
    if nseg == 1:
        gate2 = mod_ref[0, 5:6, :]
    else:
        gate2 = jnp.concatenate(
            [jnp.broadcast_to(mod_ref[s, 5:6, :], (seg, D_MODEL)) for s in range(nseg)], axis=0)
    xa = DEEPNORM_ALPHA * x1_ref[...] + (1.0 + gate2) * ffn
    out_ref[...] = _layer_norm(xa, g2_ref[...], b2_ref[...])


def _combine_call(yk, x1, gates, mod, ln2_g, ln2_b, nseg, seg, n_tiles, tile0, tiles_per_group, n_tok):
    tiles_total = n_tok // TL

    def yk_spec(k):
        return pl.BlockSpec((TL * ROW_TILE, LANES), lambda i, k=k: (k * tiles_total + tile0 + i, 0))

    in_specs = [yk_spec(k) for k in range(TOP_K)] + [
        pl.BlockSpec((TL, D_MODEL), lambda i: (tile0 + i, 0)),
        pl.BlockSpec((TL, LANES), lambda i: (tile0 + i, 0)),
        pl.BlockSpec((nseg, 6, D_MODEL), lambda i: (i // tiles_per_group, 0, 0)),
        pl.BlockSpec((1, D_MODEL), lambda i: (0, 0)),
        pl.BlockSpec((1, D_MODEL), lambda i: (0, 0)),
    ]
    return pl.pallas_call(
        functools.partial(_combine_kernel, nseg, seg),
        grid=(n_tiles,),
        in_specs=in_specs,
        out_specs=pl.BlockSpec((TL, D_MODEL), lambda i: (i, 0)),
        out_shape=jax.ShapeDtypeStruct((n_tiles * TL, D_MODEL), F32),
        compiler_params=pltpu.CompilerParams(dimension_semantics=("arbitrary",),
                                             vmem_limit_bytes=VMEM_LIMIT),
        name="combine_prompt" if nseg == 1 else "combine_sample",
    )(yk, yk, yk, yk, x1, gates, mod, ln2_g, ln2_b)


def _s5_tables(lam_re, lam_im, log_dt, b_re, b_im, c_re, c_im):
    dt = jnp.exp(log_dt.astype(F32))[:, None]
    lam = lax.complex(lam_re.astype(F32), lam_im.astype(F32))
    lam_dt = lam * dt
    lam_bar = jnp.exp(lam_dt)
    b_bar = ((lam_bar - 1.0) / lam)[..., None] * lax.complex(b_re.astype(F32), b_im.astype(F32))
    gl = S5_GROUPS // N_SLAB
    eye = jnp.eye(gl, dtype=F32)

    def b_slab(part):
        a = part.reshape(N_SLAB, gl, S5_STATE, S5_GROUP)
        return jnp.einsum('sgph,gk->sghkp', a, eye).reshape(N_SLAB, gl * S5_GROUP, gl * S5_STATE)

    bmat = jnp.concatenate([b_slab(b_bar.real), b_slab(b_bar.imag)], axis=-1).astype(BF16)

    def c_slab(part):
        a = part.reshape(N_SLAB, gl, S5_GROUP, S5_STATE)
        return jnp.einsum('sghp,gk->sgpkh', a, eye).reshape(N_SLAB, gl * S5_STATE, gl * S5_GROUP)

    cmat = jnp.concatenate([c_slab(c_re.astype(F32)), -c_slab(c_im.astype(F32))], axis=1).astype(BF16)

    row = jnp.arange(SUBLANES, dtype=F32)[:, None, None]

    def power(k):
        return jnp.exp(lam_dt[None] * k)

    tabs = []
    for d in (1, 2, 4):
        pw = power(jnp.full_like(row, float(d)))
        mask = (row >= d).astype(F32)
        tabs += [pw.real * mask, pw.imag * mask]
    pw = power(row + 1.0)
    tabs += [pw.real, pw.imag]
    tab = jnp.stack(tabs, axis=0)
    tab = tab.reshape(8, SUBLANES, N_SLAB, gl * S5_STATE).transpose(2, 0, 1, 3)
    return bmat, cmat, tab


def _state_to_slab(re, im):
    n = re.shape[0]
    return jnp.concatenate([re.reshape(n, N_SLAB, HALF), im.reshape(n, N_SLAB, HALF)], axis=-1).astype(F32)


def _slab_to_state(s):
    n = s.shape[0]
    re = s[:, :, :HALF].reshape(n, S5_GROUPS, S5_STATE)
    im = s[:, :, HALF:].reshape(n, S5_GROUPS, S5_STATE)
    return re, im


def _block_table(counts, n_tok):
    n_assign = n_tok * TOP_K
    n_blocks = (n_assign + N_EXPERTS * (E_BLK - 1) + E_BLK - 1) // E_BLK + 1
    cap_blocks = n_tok // E_BLK
    nblk = (counts + E_BLK - 1) // E_BLK
    cum = jnp.cumsum(nblk)
    start = cum - nblk
    n_used = cum[-1]
    i = jnp.arange(n_blocks, dtype=jnp.int32)
    ii = jnp.maximum(jnp.minimum(i, n_used - 1), 0)
    e = jnp.minimum(jnp.sum((cum[None, :] <= ii[:, None]).astype(jnp.int32), axis=1), N_EXPERTS - 1)
    b = ii - start[e]
    blk_row = (e * cap_blocks + b).astype(jnp.int32)
    valid = jnp.where(i < n_used, jnp.minimum(E_BLK, counts[e] - b * E_BLK), 0).astype(jnp.int32)
    return blk_row, e, valid, n_used.astype(jnp.int32).reshape(1)


def kernel(x_prompt, x_sample, c_prompt, c_sample, state_s5_re, state_s5_im, state_conv, w_ada, b_ada, w_in,
           s5_lam_re, s5_lam_im, s5_log_dt, s5_b_re, s5_b_im, s5_c_re, s5_c_im, s5_d, w_glu, b_glu, conv_w,
           w_out, ln1_g, ln1_b, w_router, b_router, w_gu, b_gu, w_down, b_down, ln2_g, ln2_b):
    assert DEPTH == 1 and w_ada.shape[0] == 1
    n_p, l_p, _ = x_prompt.shape
    n_s, l_s, _ = x_sample.shape
    t_p, t_s = n_p * l_p, n_s * l_s
    n_tok = t_p + t_s
    tiles_p, tiles_s = t_p // TL, t_s // TL
    seq_per_tile = TL // l_s
    assert l_p % TL == 0 and TL % l_s == 0 and n_s % seq_per_tile == 0

    c_all = jnp.concatenate([c_prompt, c_sample], axis=0).astype(F32)
    pad = (-c_all.shape[0]) % SUBLANES
    c_all = jnp.pad(c_all, ((0, pad), (0, 0)))
    mod = _ada_call(c_all, w_ada[0], b_ada[0].reshape(1, -1)).reshape(-1, 6, D_MODEL)
    mod_p, mod_s = mod[:n_p], mod[n_p:n_p + n_s]

    bmat, cmat, tab = _s5_tables(s5_lam_re[0], s5_lam_im[0], s5_log_dt[0], s5_b_re[0], s5_b_im[0],
                                 s5_c_re[0], s5_c_im[0])
    wr = jnp.pad(w_router[0], ((0, 0), (0, LANES - N_EXPERTS))).astype(BF16)
    br = jnp.pad(b_router[0].astype(F32), (0, LANES - N_EXPERTS), constant_values=NEG_BIG).reshape(1, LANES)
    weights = (w_in[0].astype(BF16), bmat, cmat, tab, s5_d[0].reshape(1, D_S5).astype(F32),
               w_glu[0].astype(BF16), b_glu[0].reshape(1, D_S5).astype(F32), conv_w[0].astype(F32),
               w_out[0].astype(BF16), ln1_g[0].reshape(1, D_MODEL).astype(F32),
               ln1_b[0].reshape(1, D_MODEL).astype(F32), wr, br)

    h0_p = jnp.zeros((n_p, N_SLAB, SLAB_W), F32)
    cb_p = jnp.zeros((n_p, CONV_W - 1, D_CONV), F32)
    h0_s = _state_to_slab(state_s5_re[0], state_s5_im[0])
    cb_s = state_conv[0].astype(F32)

    assert n_tok % E_BLK == 0 and E_BLK % TL == 0
    cnt0 = jnp.zeros((1, LANES), F32)
    outs_p = _mixer_call(x_prompt, mod_p, h0_p, cb_p, weights, cnt0, 1, TL, l_p // TL, 0, n_tok, ())
    x1, gates, xs, cnt_p, s_p, conv_p = outs_p
    xs3 = x_sample.reshape(n_s // seq_per_tile, TL, D_MODEL)
    outs_s = _mixer_call(xs3, mod_s, h0_s, cb_s, weights, cnt_p, seq_per_tile, l_s, 1, tiles_p, n_tok,
                         (x1, gates, xs))
    x1, gates, xs, cnt, s_s, conv_s = outs_s

    counts = cnt[0, :N_EXPERTS].astype(jnp.int32)
    blk_row, blk_e, blk_valid, n_used = _block_table(counts, n_tok)
    yk = _expert_call(blk_row, blk_e, blk_valid, n_used, xs, w_gu[0].astype(F32),
                      b_gu[0].reshape(N_EXPERTS, 1, 2 * D_FF).astype(F32), w_down[0].astype(F32),
                      b_down[0].reshape(N_EXPERTS, 1, D_MODEL).astype(F32), n_tok)
    g2 = ln2_g[0].reshape(1, D_MODEL).astype(F32)
    b2 = ln2_b[0].reshape(1, D_MODEL).astype(F32)
    y_p = _combine_call(yk, x1, gates, mod_p, g2, b2, 1, TL, tiles_p, 0, l_p // TL, n_tok)
    y_s = _combine_call(yk, x1, gates, mod_s, g2, b2, seq_per_tile, l_s, tiles_s, tiles_p, 1, n_tok)

    p_re, p_im = _slab_to_state(s_p)
    s_re, s_im = _slab_to_state(s_s)
    return (y_p.reshape(n_p, l_p, D_MODEL), y_s.reshape(n_s, l_s, D_MODEL),
            p_re[None], p_im[None], conv_p[None], s_re[None], s_im[None], conv_s[None])
```

```python
import functools
import math

import jax
import jax.numpy as jnp
from jax import lax
from jax.experimental import pallas as pl
from jax.experimental.pallas import tpu as pltpu

F32 = jnp.float32
BF16 = jnp.bfloat16

D_MODEL = 1024
DEPTH = 1
D_S5 = 512
D_CONV = 512
S5_GROUP = 16
S5_GROUPS = 32
S5_STATE = 64
CONV_W = 3
N_EXPERTS = 32
TOP_K = 4
D_FF = 1024
SWIGLU_LIMIT = 7.0
SWIGLU_ALPHA = 1.702
LN_EPS = 1e-5
DEEPNORM_ALPHA = (2.0 * DEPTH) ** 0.25

TL = 256
SUBLANES = 8
LANES = 128
N_SLAB = 4
SLAB_W = 1024
HALF = 512
E_BLK = 256
ROW_TILE = D_MODEL // LANES
PACK_ROWS = HALF // LANES
HI16_MASK = -65536
DMA_THREADS = 2
NEG_BIG = -1e30
VMEM_LIMIT = 56 * 1024 * 1024


def _dot(a, b):
    return jnp.dot(a, b, preferred_element_type=F32)


def _layer_norm(x, g, b):
    mu = jnp.mean(x, axis=-1, keepdims=True)
    xc = x - mu
    var = jnp.mean(xc * xc, axis=-1, keepdims=True)
    return xc * lax.rsqrt(var + LN_EPS) * g + b


def _ada_kernel(c_ref, w_ref, b_ref, o_ref):
    c = c_ref[...]
    s = c * jax.nn.sigmoid(c)
    o_ref[...] = _dot(s.astype(BF16), w_ref[...].astype(BF16)) + b_ref[...]


def _ada_call(c_all, w_ada, b_ada):
    rows = c_all.shape[0]
    n_out = w_ada.shape[1]
    tn = 768
    return pl.pallas_call(
        _ada_kernel,
        grid=(n_out // tn,),
        in_specs=[pl.BlockSpec((rows, D_MODEL), lambda i: (0, 0)),
                  pl.BlockSpec((D_MODEL, tn), lambda i: (0, i)),
                  pl.BlockSpec((1, tn), lambda i: (0, i))],
        out_specs=pl.BlockSpec((rows, tn), lambda i: (0, i)),
        out_shape=jax.ShapeDtypeStruct((rows, n_out), F32),
        compiler_params=pltpu.CompilerParams(dimension_semantics=("arbitrary",),
                                             vmem_limit_bytes=VMEM_LIMIT),
        name="ada_mod",
    )(c_all, w_ada, b_ada)


def _mixer_kernel(nseg, seg, carry_tiles, n_alias, cap_rows, tile0,
                  x_ref, mod_ref, h0_ref, cbuf_ref, win_ref, bmat_ref, cmat_ref, tab_ref, d_ref,
                  wglu_ref, bglu_ref, convw_ref, wout_ref, g1_ref, b1_ref, wr_ref, br_ref, cnt0_ref, *rest):
    rest = rest[n_alias:]
    (x1_ref, gate_ref, xs_hbm, cnt_out_ref, sout_ref, cout_ref,
     p_ref, bu_ref, xcs_ref, carry_ref, h2s_ref, posv_ref, poss_ref, cnt_ref, sem_s, sem_v) = rest
    g_id = pl.program_id(0)
    j = pl.program_id(1)
    step = g_id * pl.num_programs(1) + j
    is_last = step == pl.num_programs(0) * pl.num_programs(1) - 1
    slot = step % 2

    def mod_rows(k):
        if nseg == 1:
            return mod_ref[0, k:k + 1, :]
        return jnp.concatenate(
            [jnp.broadcast_to(mod_ref[s, k:k + 1, :], (seg, D_MODEL)) for s in range(nseg)], axis=0)

    x = x_ref[0]
    h = x * (1.0 + mod_rows(1)) + mod_rows(0)
    p_ref[...] = _dot(h.astype(BF16), win_ref[...])

    for i in range(N_SLAB):
        u_i = p_ref[:, i * LANES:(i + 1) * LANES].astype(BF16)
        bu_ref[i] = _dot(u_i, bmat_ref[i])

    prev = 1 - slot
    smem_copy = pltpu.make_async_copy(posv_ref, poss_ref, sem_v.at[0])

    @pl.when(step == 0)
    def _():
        k_i = lax.broadcasted_iota(jnp.int32, (SUBLANES, TL), 0)
        t_i = lax.broadcasted_iota(jnp.int32, (SUBLANES, TL), 1)
        posv_ref[...] = (N_EXPERTS * cap_rows + k_i * TL + t_i) * ROW_TILE
        smem_copy.start()
        h2s_ref[...] = jnp.zeros(h2s_ref.shape, F32)

    smem_copy.wait()

    def issue_rows(src_slot, t0, n_rows):
        for q in range(n_rows):
            t = t0 + q
            src = pl.multiple_of(t * ROW_TILE, ROW_TILE)
            for k in range(TOP_K):
                dst = pl.multiple_of(poss_ref[k, t], ROW_TILE)
                pltpu.make_async_copy(h2s_ref.at[src_slot, pl.ds(src, ROW_TILE), :],
                                      xs_hbm.at[pl.ds(dst, ROW_TILE), :],
                                      sem_s.at[src_slot]).start(priority=k % DMA_THREADS)

    def wait_scatter(sl):
        for _ in range(TOP_K):
            pltpu.make_async_copy(h2s_ref.at[sl], xs_hbm.at[pl.ds(0, TL * ROW_TILE), :], sem_s.at[sl]).wait()

    if carry_tiles:
        @pl.when(j == 0)
        def _():
            carry_ref[...] = h0_ref[0]

    for s in range(nseg):
        if carry_tiles:
            init = tuple(carry_ref[i:i + 1, :] for i in range(N_SLAB))
        else:
            init = tuple(h0_ref[s, i:i + 1, :] for i in range(N_SLAB))

        def scan_body(r, carry, s=s):
            issue_rows(prev, s * seg + r * (2 * SUBLANES), 2 * SUBLANES)
            new = []
            for i in range(N_SLAB):
                cr = carry[i][:, :HALF]
                ci = carry[i][:, HALF:]
                for half in range(2):
                    row0 = pl.multiple_of(s * seg + r * (2 * SUBLANES) + half * SUBLANES, SUBLANES)
                    blk = bu_ref[i, pl.ds(row0, SUBLANES), :]
                    xr = blk[:, :HALF]
                    xi = blk[:, HALF:]
                    for k, d in enumerate((1, 2, 4)):
                        lr = tab_ref[i, 2 * k]
                        li = tab_ref[i, 2 * k + 1]
                        rr = pltpu.roll(xr, d, 0)
                        ri = pltpu.roll(xi, d, 0)
                        xr, xi = xr + (lr * rr - li * ri), xi + (lr * ri + li * rr)
                    pr = tab_ref[i, 6]
                    pi_ = tab_ref[i, 7]
                    crb = jnp.broadcast_to(cr, (SUBLANES, HALF))
                    cib = jnp.broadcast_to(ci, (SUBLANES, HALF))
                    xr, xi = xr + (pr * crb - pi_ * cib), xi + (pr * cib + pi_ * crb)
                    bu_ref[i, pl.ds(row0, SUBLANES), :] = jnp.concatenate([xr, xi], axis=1)
                    cr = xr[SUBLANES - 1:SUBLANES, :]
                    ci = xi[SUBLANES - 1:SUBLANES, :]
                new.append(jnp.concatenate([cr, ci], axis=1))
            return tuple(new)

        final = lax.fori_loop(0, seg // (2 * SUBLANES), scan_body, init)
        for i in range(N_SLAB):
            sout_ref[s, i:i + 1, :] = final[i]
            if carry_tiles:
                carry_ref[i:i + 1, :] = final[i]

    y = jnp.concatenate([_dot(bu_ref[i].astype(BF16), cmat_ref[i]) for i in range(N_SLAB)], axis=1)
    y = y + d_ref[...] * p_ref[:, :D_S5]
    z = jax.nn.gelu(y)
    z = z * jax.nn.sigmoid(_dot(z.astype(BF16), wglu_ref[...]) + bglu_ref[...])

    xc = p_ref[:, D_S5 + D_CONV:D_S5 + 2 * D_CONV] * p_ref[:, D_S5 + 2 * D_CONV:]
    w0 = convw_ref[0:1, :]
    w1 = convw_ref[1:2, :]
    w2 = convw_ref[2:3, :]
    conv_parts = []
    for s in range(nseg):
        base = s * (seg + SUBLANES)
        xc_s = xc[s * seg:(s + 1) * seg]
        tail = xc_s[seg - 2:seg]
        xcs_ref[pl.ds(base + SUBLANES, seg), :] = xc_s
        if carry_tiles:
            @pl.when(j == 0)
            def _(s=s, base=base):
                xcs_ref[pl.ds(base + SUBLANES - 2, 2), :] = cbuf_ref[s]
        else:
            xcs_ref[pl.ds(base + SUBLANES - 2, 2), :] = cbuf_ref[s]
        xm1 = xcs_ref[pl.ds(base + SUBLANES - 1, seg), :]
        xm2 = xcs_ref[pl.ds(base + SUBLANES - 2, seg), :]
        conv_parts.append(w0 * xm2 + w1 * xm1 + w2 * xc_s)
        cout_ref[s] = tail
        if carry_tiles:
            xcs_ref[pl.ds(base + SUBLANES - 2, 2), :] = tail
    conv = conv_parts[0] if nseg == 1 else jnp.concatenate(conv_parts, axis=0)
    y_b = p_ref[:, D_S5:D_S5 + D_CONV] * conv

    mix = _dot(jnp.concatenate([z, y_b], axis=1).astype(BF16), wout_ref[...])
    x1 = _layer_norm(DEEPNORM_ALPHA * x + (1.0 + mod_rows(2)) * mix, g1_ref[...], b1_ref[...])
    x1_ref[...] = x1
    h2 = x1 * (1.0 + mod_rows(4)) + mod_rows(3)
    lo_bits = lax.bitcast_convert_type(h2[:, :HALF].astype(BF16).astype(F32), jnp.int32)
    hi_bits = lax.bitcast_convert_type(h2[:, HALF:].astype(BF16).astype(F32), jnp.int32)
    packed = lax.bitcast_convert_type(hi_bits | lax.shift_right_logical(lo_bits, 16), F32)
    for c in range(PACK_ROWS):
        h2s_ref[slot, pl.ds(c, TL, stride=ROW_TILE), :] = packed[:, c * LANES:(c + 1) * LANES]

    logits = _dot(h2.astype(BF16), wr_ref[...]) + br_ref[...]
    lane = lax.broadcasted_iota(jnp.int32, logits.shape, 1)
    lane_f = lane.astype(F32)
    vals, sels, hots = [], [], []
    cur = logits
    for _ in range(TOP_K):
        m = jnp.max(cur, axis=-1, keepdims=True)
        am = jnp.min(jnp.where(cur == m, lane_f, float(LANES)), axis=-1, keepdims=True)
        hot = lane_f == am
        vals.append(m)
        sels.append(am)
        hots.append(hot)
        cur = jnp.where(hot, -jnp.inf, cur)
    exps = [jnp.exp(v - vals[0]) for v in vals]
    inv = 1.0 / (exps[0] + exps[1] + exps[2] + exps[3])
    gate_out = jnp.zeros(logits.shape, F32)
    for k in range(TOP_K):
        gate_out = jnp.where(lane == k, exps[k] * inv, gate_out)
    gate_ref[...] = gate_out

    tok = ((tile0 + step) * TL + lax.broadcasted_iota(jnp.int32, logits.shape, 0)).astype(F32)
    record = jnp.zeros(logits.shape, F32)
    for k in range(TOP_K):
        record = jnp.where(lane == k, sels[k], record)
        record = jnp.where(lane == TOP_K + k, tok + float(k * cap_rows), record)
    h2s_ref[slot, pl.ds(PACK_ROWS, TL, stride=ROW_TILE), :] = record

    @pl.when(step == 0)
    def _():
        cnt_ref[...] = cnt0_ref[...]

    chosen = jnp.zeros(logits.shape, F32)
    for k in range(TOP_K):
        chosen = jnp.where(hots[k], 1.0, chosen)
    r_i = lax.broadcasted_iota(jnp.int32, (TL, TL), 0)
    c_i = lax.broadcasted_iota(jnp.int32, (TL, TL), 1)
    before = jnp.where(c_i < r_i, 1.0, 0.0).astype(BF16)
    rank_base = _dot(before, chosen.astype(BF16)) + cnt_ref[...]
    pos_mat = jnp.zeros(logits.shape, F32)
    for k in range(TOP_K):
        rank_k = jnp.sum(jnp.where(hots[k], rank_base, 0.0), axis=-1, keepdims=True)
        pos_mat = jnp.where(lane == k, (sels[k] * float(cap_rows) + rank_k) * float(ROW_TILE), pos_mat)
    cnt_ref[...] = cnt_ref[...] + jnp.sum(chosen, axis=0, keepdims=True)
    cnt_out_ref[...] = cnt_ref[...]
    pos_t = pos_mat.T[:SUBLANES, :].astype(jnp.int32)
    posv_ref[...] = pos_t
    smem_copy.start()
    wait_scatter(prev)

    @pl.when(is_last)
    def _():
        smem_copy.wait()

        def tail_body(t8, c):
            issue_rows(slot, t8 * SUBLANES, SUBLANES)
            return c
        lax.fori_loop(0, TL // SUBLANES, tail_body, 0)
        wait_scatter(slot)


def _mixer_call(x3, mod, h0, cbuf, weights, cnt0, nseg, seg, tiles_per_group, tile0, n_tok, aliased):
    groups = x3.shape[0]
    nseq = mod.shape[0]
    carry_tiles = nseg == 1
    n_alias = len(aliased)
    n_tiles = n_tok // TL
    cap_rows = n_tok

    def full(a):
        nd = a.ndim
        return pl.BlockSpec(a.shape, lambda g, j, nd=nd: (0,) * nd)

    def tile_map(g, j):
        return (tile0 + g * tiles_per_group + j, 0)

    in_specs = [pl.BlockSpec((1, TL, D_MODEL), lambda g, j: (g, j, 0)),
                pl.BlockSpec((nseg, 6, D_MODEL), lambda g, j: (g, 0, 0)),
                pl.BlockSpec((nseg, N_SLAB, SLAB_W), lambda g, j: (g, 0, 0)),
                pl.BlockSpec((nseg, CONV_W - 1, D_CONV), lambda g, j: (g, 0, 0))]
    in_specs += [full(w) for w in weights]
    in_specs += [full(cnt0)]
    in_specs += [pl.BlockSpec(memory_space=pl.ANY)] * n_alias
    out_shape = [jax.ShapeDtypeStruct((n_tok, D_MODEL), F32),
                 jax.ShapeDtypeStruct((n_tok, LANES), F32),
                 jax.ShapeDtypeStruct(((N_EXPERTS * cap_rows + TOP_K * TL) * ROW_TILE, LANES), F32),
                 jax.ShapeDtypeStruct((1, LANES), F32),
                 jax.ShapeDtypeStruct((nseq, N_SLAB, SLAB_W), F32),
                 jax.ShapeDtypeStruct((nseq, CONV_W - 1, D_CONV), F32)]
    out_specs = [pl.BlockSpec((TL, D_MODEL), tile_map),
                 pl.BlockSpec((TL, LANES), tile_map),
                 pl.BlockSpec(memory_space=pl.ANY),
                 pl.BlockSpec((1, LANES), lambda g, j: (0, 0)),
                 pl.BlockSpec((nseg, N_SLAB, SLAB_W), lambda g, j: (g, 0, 0)),
                 pl.BlockSpec((nseg, CONV_W - 1, D_CONV), lambda g, j: (g, 0, 0))]
    n_in = 4 + len(weights) + 1
    aliases = {n_in + k: k for k in range(n_alias)}
    scratch = [pltpu.VMEM((TL, 2 * D_MODEL), F32),
               pltpu.VMEM((N_SLAB, TL, SLAB_W), F32),
               pltpu.VMEM((nseg * (seg + SUBLANES), D_CONV), F32),
               pltpu.VMEM((N_SLAB, SLAB_W), F32),
               pltpu.VMEM((2, TL * ROW_TILE, LANES), F32),
               pltpu.VMEM((SUBLANES, TL), jnp.int32),
               pltpu.SMEM((SUBLANES, TL), jnp.int32),
               pltpu.VMEM((1, LANES), F32),
               pltpu.SemaphoreType.DMA((2,)),
               pltpu.SemaphoreType.DMA((1,))]
    return pl.pallas_call(
        functools.partial(_mixer_kernel, nseg, seg, carry_tiles, n_alias, cap_rows, tile0),
        grid=(groups, tiles_per_group),
        in_specs=in_specs, out_specs=out_specs, out_shape=out_shape,
        scratch_shapes=scratch,
        input_output_aliases=aliases,
        compiler_params=pltpu.CompilerParams(dimension_semantics=("arbitrary", "arbitrary"),
                                             vmem_limit_bytes=VMEM_LIMIT),
        name="mixer_prompt" if carry_tiles else "mixer_sample",
    )(x3, mod, h0, cbuf, *weights, cnt0, *aliased)


W_CAST_ROWS = 128


def _expert_kernel(n_tok, row_ref, be_ref, nv_ref, nu_ref, xs_ref, wgu_ref, bgu_ref, wd_ref, bd_ref,
                   yk_hbm, wgu_bf, wd_bf, out_a, out_b, slotv_ref, slots_ref, sem_o, sem_v):
    i = pl.program_id(0)
    n_used = nu_ref[0]
    outs = (out_a, out_b)
    prev = jnp.maximum(i - 1, 0)
    new_expert = jnp.logical_or(i == 0, be_ref[i] != be_ref[prev])

    @pl.when(jnp.logical_and(new_expert, i < n_used))
    def _():
        def cast_gu(r, c):
            rows = pl.ds(pl.multiple_of(r * W_CAST_ROWS, W_CAST_ROWS), W_CAST_ROWS)
            wgu_bf[rows, :] = wgu_ref[0, rows, :].astype(BF16)
            return c
        lax.fori_loop(0, D_MODEL // W_CAST_ROWS, cast_gu, 0)

        def cast_d(r, c):
            rows = pl.ds(pl.multiple_of(r * W_CAST_ROWS, W_CAST_ROWS), W_CAST_ROWS)
            wd_bf[rows, :] = wd_ref[0, rows, :].astype(BF16)
            return c
        lax.fori_loop(0, D_FF // W_CAST_ROWS, cast_d, 0)

    def issue_rows(src, t0, count):
        for q in range(count):
            t = t0 + q
            dst = pl.multiple_of(slots_ref[src, 0, t], ROW_TILE)
            row0 = pl.multiple_of(t * ROW_TILE, ROW_TILE)
            pltpu.make_async_copy(outs[src].at[pl.ds(row0, ROW_TILE), :],
                                  yk_hbm.at[pl.ds(dst, ROW_TILE), :],
                                  sem_o.at[src]).start(priority=q % DMA_THREADS)

    def wait_rows(sl):
        pltpu.make_async_copy(outs[sl], yk_hbm.at[pl.ds(0, E_BLK * ROW_TILE), :], sem_o.at[sl]).wait()

    lane = lax.broadcasted_iota(jnp.int32, (E_BLK, LANES), 1)
    row = lax.broadcasted_iota(jnp.int32, (E_BLK, 1), 0)
    spare0 = float(TOP_K * n_tok)

    @pl.when(i == 0)
    def _():
        t_i = lax.broadcasted_iota(jnp.int32, (SUBLANES, E_BLK), 1)
        slotv_ref[...] = (TOP_K * n_tok + t_i) * ROW_TILE
        pltpu.make_async_copy(slotv_ref, slots_ref.at[1], sem_v.at[0]).start()
        out_b[...] = jnp.zeros(out_b.shape, F32)

    def block(cur):
        prv = 1 - cur

        def smem_copy(dst):
            return pltpu.make_async_copy(slotv_ref, slots_ref.at[dst], sem_v.at[0])

        @pl.when(i < n_used)
        def _():
            smem_copy(prv).wait()
            record = xs_ref[pl.ds(PACK_ROWS, E_BLK, stride=ROW_TILE), :]
            hit = jnp.logical_and(record.astype(jnp.int32) == be_ref[i], lane < TOP_K)
            slot_f = jnp.sum(jnp.where(hit, pltpu.roll(record, LANES - TOP_K, 1), 0.0), axis=-1, keepdims=True)
            slot_f = jnp.where(row < nv_ref[i], slot_f, spare0 + row.astype(F32))
            slot_mat = jnp.where(lane == 0, slot_f * float(ROW_TILE), 0.0)
            slotv_ref[...] = slot_mat.T[:SUBLANES, :].astype(jnp.int32)
            smem_copy(cur).start()

        @pl.when(i < n_used)
        def _():
            issue_rows(prv, 0, E_BLK)
            live = row < nv_ref[i]
            words = [lax.bitcast_convert_type(xs_ref[pl.ds(c, E_BLK, stride=ROW_TILE), :], jnp.int32)
                     for c in range(PACK_ROWS)]
            lo = [lax.bitcast_convert_type(lax.shift_left(w, 16), F32) for w in words]
            hi = [lax.bitcast_convert_type(w & HI16_MASK, F32) for w in words]
            xb = jnp.where(live, jnp.concatenate(lo + hi, axis=1), 0.0).astype(BF16)
            hgu = _dot(xb, wgu_bf[...]) + bgu_ref[0]
            g = jnp.minimum(hgu[:, :D_FF], SWIGLU_LIMIT)
            up = jnp.clip(hgu[:, D_FF:], -SWIGLU_LIMIT, SWIGLU_LIMIT)
            act = (up + 1.0) * (g * jax.nn.sigmoid(SWIGLU_ALPHA * g))
            y = _dot(act.astype(BF16), wd_bf[...]) + bd_ref[0]

            @pl.when(i > 0)
            def _():
                wait_rows(cur)

            for c in range(ROW_TILE):
                outs[cur][pl.ds(c, E_BLK, stride=ROW_TILE), :] = y[:, c * LANES:(c + 1) * LANES]

        @pl.when(i == n_used)
        def _():
            smem_copy(prv).wait()

            def tail(t8, c):
                issue_rows(prv, t8 * SUBLANES, SUBLANES)
                return c
            lax.fori_loop(0, E_BLK // SUBLANES, tail, 0)
            wait_rows(cur)
            wait_rows(prv)

    for parity in range(2):
        @pl.when(i % 2 == parity)
        def _(parity=parity):
            block(parity)


def _expert_call(blk_row, blk_e, blk_valid, n_used, xs, w_gu, b_gu, w_down, b_down, n_tok):
    nb = blk_row.shape[0]
    grid_spec = pltpu.PrefetchScalarGridSpec(
        num_scalar_prefetch=4,
        grid=(nb,),
        in_specs=[
            pl.BlockSpec((E_BLK * ROW_TILE, LANES), lambda i, br, be, nv, nu: (br[i], 0)),
            pl.BlockSpec((1, D_MODEL, 2 * D_FF), lambda i, br, be, nv, nu: (be[i], 0, 0)),
            pl.BlockSpec((1, 1, 2 * D_FF), lambda i, br, be, nv, nu: (be[i], 0, 0)),
            pl.BlockSpec((1, D_FF, D_MODEL), lambda i, br, be, nv, nu: (be[i], 0, 0)),
            pl.BlockSpec((1, 1, D_MODEL), lambda i, br, be, nv, nu: (be[i], 0, 0)),
        ],
        out_specs=pl.BlockSpec(memory_space=pl.ANY),
        scratch_shapes=[pltpu.VMEM((D_MODEL, 2 * D_FF), BF16),
                        pltpu.VMEM((D_FF, D_MODEL), BF16),
                        pltpu.VMEM((E_BLK * ROW_TILE, LANES), F32),
                        pltpu.VMEM((E_BLK * ROW_TILE, LANES), F32),
                        pltpu.VMEM((SUBLANES, E_BLK), jnp.int32),
                        pltpu.SMEM((2, SUBLANES, E_BLK), jnp.int32),
                        pltpu.SemaphoreType.DMA((2,)),
                        pltpu.SemaphoreType.DMA((1,))],
    )
    return pl.pallas_call(
        functools.partial(_expert_kernel, n_tok),
        grid_spec=grid_spec,
        out_shape=jax.ShapeDtypeStruct(((TOP_K * n_tok + E_BLK) * ROW_TILE, LANES), F32),
        compiler_params=pltpu.CompilerParams(dimension_semantics=("arbitrary",),
                                             vmem_limit_bytes=VMEM_LIMIT),
        name="experts",
    )(blk_row, blk_e, blk_valid, n_used, xs, w_gu, b_gu, w_down, b_down)


def _combine_kernel(nseg, seg, y0_ref, y1_ref, y2_ref, y3_ref, x1_ref, gate_ref, mod_ref, g2_ref, b2_ref,
                    out_ref):
    gates = gate_ref[...]
    ffn = None
    for k, y_ref in enumerate((y0_ref, y1_ref, y2_ref, y3_ref)):
        rows = jnp.concatenate(
            [y_ref[pl.ds(c, TL, stride=ROW_TILE), :] for c in range(ROW_TILE)], axis=1)
        term = gates[:, k:k + 1] * rows
        ffn = term if ffn is None else ffn + term
    if nseg == 1:
        gate2 = mod_ref[0, 5:6, :]
    else:
        gate2 = jnp.concatenate(
            [jnp.broadcast_to(mod_ref[s, 5:6, :], (seg, D_MODEL)) for s in range(nseg)], axis=0)
    xa = DEEPNORM_ALPHA * x1_ref[...] + (1.0 + gate2) * ffn
    out_ref[...] = _layer_norm(xa, g2_ref[...], b2_ref[...])


def _combine_call(yk, x1, gates, mod, ln2_g, ln2_b, nseg, seg, n_tiles, tile0, tiles_per_group, n_tok):
    tiles_total = n_tok // TL

    def yk_spec(k):
        return pl.BlockSpec((TL * ROW_TILE, LANES), lambda i, k=k: (k * tiles_total + tile0 + i, 0))

    in_specs = [yk_spec(k) for k in range(TOP_K)] + [
        pl.BlockSpec((TL, D_MODEL), lambda i: (tile0 + i, 0)),
        pl.BlockSpec((TL, LANES), lambda i: (tile0 + i, 0)),
        pl.BlockSpec((nseg, 6, D_MODEL), lambda i: (i // tiles_per_group, 0, 0)),
        pl.BlockSpec((1, D_MODEL), lambda i: (0, 0)),
        pl.BlockSpec((1, D_MODEL), lambda i: (0, 0)),
    ]
    return pl.pallas_call(
        functools.partial(_combine_kernel, nseg, seg),
        grid=(n_tiles,),
        in_specs=in_specs,
        out_specs=pl.BlockSpec((TL, D_MODEL), lambda i: (i, 0)),
        out_shape=jax.ShapeDtypeStruct((n_tiles * TL, D_MODEL), F32),
        compiler_params=pltpu.CompilerParams(dimension_semantics=("arbitrary",),
                                             vmem_limit_bytes=VMEM_LIMIT),
        name="combine_prompt" if nseg == 1 else "combine_sample",
    )(yk, yk, yk, yk, x1, gates, mod, ln2_g, ln2_b)


def _s5_tables(lam_re, lam_im, log_dt, b_re, b_im, c_re, c_im):
    dt = jnp.exp(log_dt.astype(F32))[:, None]
    lam = lax.complex(lam_re.astype(F32), lam_im.astype(F32))
    lam_dt = lam * dt
    lam_bar = jnp.exp(lam_dt)
    b_bar = ((lam_bar - 1.0) / lam)[..., None] * lax.complex(b_re.astype(F32), b_im.astype(F32))
    gl = S5_GROUPS // N_SLAB
    eye = jnp.eye(gl, dtype=F32)

    def b_slab(part):
        a = part.reshape(N_SLAB, gl, S5_STATE, S5_GROUP)
        return jnp.einsum('sgph,gk->sghkp', a, eye).reshape(N_SLAB, gl * S5_GROUP, gl * S5_STATE)

    bmat = jnp.concatenate([b_slab(b_bar.real), b_slab(b_bar.imag)], axis=-1).astype(BF16)

    def c_slab(part):
        a = part.reshape(N_SLAB, gl, S5_GROUP, S5_STATE)
        return jnp.einsum('sghp,gk->sgpkh', a, eye).reshape(N_SLAB, gl * S5_STATE, gl * S5_GROUP)

    cmat = jnp.concatenate([c_slab(c_re.astype(F32)), -c_slab(c_im.astype(F32))], axis=1).astype(BF16)

    row = jnp.arange(SUBLANES, dtype=F32)[:, None, None]

    def power(k):
        return jnp.exp(lam_dt[None] * k)

    tabs = []
    for d in (1, 2, 4):
        pw = power(jnp.full_like(row, float(d)))
        mask = (row >= d).astype(F32)
        tabs += [pw.real * mask, pw.imag * mask]
    pw = power(row + 1.0)
    tabs += [pw.real, pw.imag]
    tab = jnp.stack(tabs, axis=0)
    tab = tab.reshape(8, SUBLANES, N_SLAB, gl * S5_STATE).transpose(2, 0, 1, 3)
    return bmat, cmat, tab


def _state_to_slab(re, im):
    n = re.shape[0]
    return jnp.concatenate([re.reshape(n, N_SLAB, HALF), im.reshape(n, N_SLAB, HALF)], axis=-1).astype(F32)


def _slab_to_state(s):
    n = s.shape[0]
    re = s[:, :, :HALF].reshape(n, S5_GROUPS, S5_STATE)
    im = s[:, :, HALF:].reshape(n, S5_GROUPS, S5_STATE)
    return re, im


def _block_table(counts, n_tok):
    n_assign = n_tok * TOP_K
    n_blocks = (n_assign + N_EXPERTS * (E_BLK - 1) + E_BLK - 1) // E_BLK + 1
    cap_blocks = n_tok // E_BLK
    nblk = (counts + E_BLK - 1) // E_BLK
    cum = jnp.cumsum(nblk)
    start = cum - nblk
    n_used = cum[-1]
    i = jnp.arange(n_blocks, dtype=jnp.int32)
    ii = jnp.maximum(jnp.minimum(i, n_used - 1), 0)
    e = jnp.minimum(jnp.sum((cum[None, :] <= ii[:, None]).astype(jnp.int32), axis=1), N_EXPERTS - 1)
    b = ii - start[e]
    blk_row = (e * cap_blocks + b).astype(jnp.int32)
    valid = jnp.where(i < n_used, jnp.minimum(E_BLK, counts[e] - b * E_BLK), 0).astype(jnp.int32)
    return blk_row, e, valid, n_used.astype(jnp.int32).reshape(1)


def kernel(x_prompt, x_sample, c_prompt, c_sample, state_s5_re, state_s5_im, state_conv, w_ada, b_ada, w_in,
           s5_lam_re, s5_lam_im, s5_log_dt, s5_b_re, s5_b_im, s5_c_re, s5_c_im, s5_d, w_glu, b_glu, conv_w,
           w_out, ln1_g, ln1_b, w_router, b_router, w_gu, b_gu, w_down, b_down, ln2_g, ln2_b):
    assert DEPTH == 1 and w_ada.shape[0] == 1
    n_p, l_p, _ = x_prompt.shape
    n_s, l_s, _ = x_sample.shape
    t_p, t_s = n_p * l_p, n_s * l_s
    n_tok = t_p + t_s
    tiles_p, tiles_s = t_p // TL, t_s // TL
    seq_per_tile = TL // l_s
    assert l_p % TL == 0 and TL % l_s == 0 and n_s % seq_per_tile == 0

    c_all = jnp.concatenate([c_prompt, c_sample], axis=0).astype(F32)
    pad = (-c_all.shape[0]) % SUBLANES
    c_all = jnp.pad(c_all, ((0, pad), (0, 0)))
    mod = _ada_call(c_all, w_ada[0], b_ada[0].reshape(1, -1)).reshape(-1, 6, D_MODEL)
    mod_p, mod_s = mod[:n_p], mod[n_p:n_p + n_s]

    bmat, cmat, tab = _s5_tables(s5_lam_re[0], s5_lam_im[0], s5_log_dt[0], s5_b_re[0], s5_b_im[0],
                                 s5_c_re[0], s5_c_im[0])
    wr = jnp.pad(w_router[0], ((0, 0), (0, LANES - N_EXPERTS))).astype(BF16)
    br = jnp.pad(b_router[0].astype(F32), (0, LANES - N_EXPERTS), constant_values=NEG_BIG).reshape(1, LANES)
    weights = (w_in[0].astype(BF16), bmat, cmat, tab, s5_d[0].reshape(1, D_S5).astype(F32),
               w_glu[0].astype(BF16), b_glu[0].reshape(1, D_S5).astype(F32), conv_w[0].astype(F32),
               w_out[0].astype(BF16), ln1_g[0].reshape(1, D_MODEL).astype(F32),
               ln1_b[0].reshape(1, D_MODEL).astype(F32), wr, br)

    h0_p = jnp.zeros((n_p, N_SLAB, SLAB_W), F32)
    cb_p = jnp.zeros((n_p, CONV_W - 1, D_CONV), F32)
    h0_s = _state_to_slab(state_s5_re[0], state_s5_im[0])
    cb_s = state_conv[0].astype(F32)

    assert n_tok % E_BLK == 0 and E_BLK % TL == 0
    cnt0 = jnp.zeros((1, LANES), F32)
    outs_p = _mixer_call(x_prompt, mod_p, h0_p, cb_p, weights, cnt0, 1, TL, l_p // TL, 0, n_tok, ())
    x1, gates, xs, cnt_p, s_p, conv_p = outs_p
    xs3 = x_sample.reshape(n_s // seq_per_tile, TL, D_MODEL)
    outs_s = _mixer_call(xs3, mod_s, h0_s, cb_s, weights, cnt_p, seq_per_tile, l_s, 1, tiles_p, n_tok,
                         (x1, gates, xs))
    x1, gates, xs, cnt, s_s, conv_s = outs_s

    counts = cnt[0, :N_EXPERTS].astype(jnp.int32)
    blk_row, blk_e, blk_valid, n_used = _block_table(counts, n_tok)
    yk = _expert_call(blk_row, blk_e, blk_valid, n_used, xs, w_gu[0].astype(F32),
                      b_gu[0].reshape(N_EXPERTS, 1, 2 * D_FF).astype(F32), w_down[0].astype(F32),
                      b_down[0].reshape(N_EXPERTS, 1, D_MODEL).astype(F32), n_tok)
    g2 = ln2_g[0].reshape(1, D_MODEL).astype(F32)
    b2 = ln2_b[0].reshape(1, D_MODEL).astype(F32)
    y_p = _combine_call(yk, x1, gates, mod_p, g2, b2, 1, TL, tiles_p, 0, l_p // TL, n_tok)
    y_s = _combine_call(yk, x1, gates, mod_s, g2, b2, seq_per_tile, l_s, tiles_s, tiles_p, 1, n_tok)

    p_re, p_im = _slab_to_state(s_p)
    s_re, s_im = _slab_to_state(s_s)
    return (y_p.reshape(n_p, l_p, D_MODEL), y_s.reshape(n_s, l_s, D_MODEL),
            p_re[None], p_im[None], conv_p[None], s_re[None], s_im[None], conv_s[None])
```

```python
import functools
import math

import jax
import jax.numpy as jnp
from jax import lax
from jax.experimental import pallas as pl
from jax.experimental.pallas import tpu as pltpu

F32 = jnp.float32
BF16 = jnp.bfloat16

D_MODEL = 1024
DEPTH = 1
D_S5 = 512
D_CONV = 512
S5_GROUP = 16
S5_GROUPS = 32
S5_STATE = 64
CONV_W = 3
N_EXPERTS = 32
TOP_K = 4
D_FF = 1024
SWIGLU_LIMIT = 7.0
SWIGLU_ALPHA = 1.702
LN_EPS = 1e-5
DEEPNORM_ALPHA = (2.0 * DEPTH) ** 0.25

TL = 256
SUBLANES = 8
LANES = 128
N_SLAB = 4
SLAB_W = 1024
HALF = 512
E_BLK = 256
ROW_TILE = D_MODEL // LANES
PACK_ROWS = HALF // LANES
HI16_MASK = -65536
DMA_THREADS = 2
NEG_BIG = -1e30
VMEM_LIMIT = 56 * 1024 * 1024


def _dot(a, b):
    return jnp.dot(a, b, preferred_element_type=F32)


def _layer_norm(x, g, b):
    mu = jnp.mean(x, axis=-1, keepdims=True)
    xc = x - mu
    var = jnp.mean(xc * xc, axis=-1, keepdims=True)
    return xc * lax.rsqrt(var + LN_EPS) * g + b


def _ada_kernel(c_ref, w_ref, b_ref, o_ref):
    c = c_ref[...]
    s = c * jax.nn.sigmoid(c)
    o_ref[...] = _dot(s.astype(BF16), w_ref[...].astype(BF16)) + b_ref[...]


def _ada_call(c_all, w_ada, b_ada):
    rows = c_all.shape[0]
    n_out = w_ada.shape[1]
    tn = 768
    return pl.pallas_call(
        _ada_kernel,
        grid=(n_out // tn,),
        in_specs=[pl.BlockSpec((rows, D_MODEL), lambda i: (0, 0)),
                  pl.BlockSpec((D_MODEL, tn), lambda i: (0, i)),
                  pl.BlockSpec((1, tn), lambda i: (0, i))],
        out_specs=pl.BlockSpec((rows, tn), lambda i: (0, i)),
        out_shape=jax.ShapeDtypeStruct((rows, n_out), F32),
        compiler_params=pltpu.CompilerParams(dimension_semantics=("arbitrary",),
                                             vmem_limit_bytes=VMEM_LIMIT),
        name="ada_mod",
    )(c_all, w_ada, b_ada)


def _mixer_kernel(nseg, seg, carry_tiles, n_alias, cap_rows, tile0,
                  x_ref, mod_ref, h0_ref, cbuf_ref, win_ref, bmat_ref, cmat_ref, tab_ref, d_ref,
                  wglu_ref, bglu_ref, convw_ref, wout_ref, g1_ref, b1_ref, wr_ref, br_ref, cnt0_ref, *rest):
    rest = rest[n_alias:]
    (x1_ref, gate_ref, xs_hbm, cnt_out_ref, sout_ref, cout_ref,
     p_ref, bu_ref, xcs_ref, carry_ref, h2s_ref, posv_ref, poss_ref, cnt_ref, sem_s, sem_v) = rest
    g_id = pl.program_id(0)
    j = pl.program_id(1)
    step = g_id * pl.num_programs(1) + j
    is_last = step == pl.num_programs(0) * pl.num_programs(1) - 1
    slot = step % 2

    def mod_rows(k):
        if nseg == 1:
            return mod_ref[0, k:k + 1, :]
        return jnp.concatenate(
            [jnp.broadcast_to(mod_ref[s, k:k + 1, :], (seg, D_MODEL)) for s in range(nseg)], axis=0)

    x = x_ref[0]
    h = x * (1.0 + mod_rows(1)) + mod_rows(0)
    p_ref[...] = _dot(h.astype(BF16), win_ref[...])

    for i in range(N_SLAB):
        u_i = p_ref[:, i * LANES:(i + 1) * LANES].astype(BF16)
        bu_ref[i] = _dot(u_i, bmat_ref[i])

    prev = 1 - slot
    smem_copy = pltpu.make_async_copy(posv_ref, poss_ref, sem_v.at[0])

    @pl.when(step == 0)
    def _():
        k_i = lax.broadcasted_iota(jnp.int32, (SUBLANES, TL), 0)
        t_i = lax.broadcasted_iota(jnp.int32, (SUBLANES, TL), 1)
        posv_ref[...] = (N_EXPERTS * cap_rows + k_i * TL + t_i) * ROW_TILE
        smem_copy.start()
        h2s_ref[...] = jnp.zeros(h2s_ref.shape, F32)

    smem_copy.wait()

    def issue_rows(src_slot, t0, n_rows):
        for q in range(n_rows):
            t = t0 + q
            src = pl.multiple_of(t * ROW_TILE, ROW_TILE)
            for k in range(TOP_K):
                dst = pl.multiple_of(poss_ref[k, t], ROW_TILE)
                pltpu.make_async_copy(h2s_ref.at[src_slot, pl.ds(src, ROW_TILE), :],
                                      xs_hbm.at[pl.ds(dst, ROW_TILE), :],
                                      sem_s.at[src_slot]).start(priority=k % DMA_THREADS)

    def wait_scatter(sl):
        for _ in range(TOP_K):
            pltpu.make_async_copy(h2s_ref.at[sl], xs_hbm.at[pl.ds(0, TL * ROW_TILE), :], sem_s.at[sl]).wait()

    if carry_tiles:
        @pl.when(j == 0)
        def _():
            carry_ref[...] = h0_ref[0]

    for s in range(nseg):
        if carry_tiles:
            init = tuple(carry_ref[i:i + 1, :] for i in range(N_SLAB))
        else:
            init = tuple(h0_ref[s, i:i + 1, :] for i in range(N_SLAB))

        def scan_body(r, carry, s=s):
            issue_rows(prev, s * seg + r * (2 * SUBLANES), 2 * SUBLANES)
            new = []
            for i in range(N_SLAB):
                cr = carry[i][:, :HALF]
                ci = carry[i][:, HALF:]
                for half in range(2):
                    row0 = pl.multiple_of(s * seg + r * (2 * SUBLANES) + half * SUBLANES, SUBLANES)
                    blk = bu_ref[i, pl.ds(row0, SUBLANES), :]
                    xr = blk[:, :HALF]
                    xi = blk[:, HALF:]
                    for k, d in enumerate((1, 2, 4)):
                        lr = tab_ref[i, 2 * k]
                        li = tab_ref[i, 2 * k + 1]
                        rr = pltpu.roll(xr, d, 0)
                        ri = pltpu.roll(xi, d, 0)
                        xr, xi = xr + (lr * rr - li * ri), xi + (lr * ri + li * rr)
                    pr = tab_ref[i, 6]
                    pi_ = tab_ref[i, 7]
                    crb = jnp.broadcast_to(cr, (SUBLANES, HALF))
                    cib = jnp.broadcast_to(ci, (SUBLANES, HALF))
                    xr, xi = xr + (pr * crb - pi_ * cib), xi + (pr * cib + pi_ * crb)
                    bu_ref[i, pl.ds(row0, SUBLANES), :] = jnp.concatenate([xr, xi], axis=1)
                    cr = xr[SUBLANES - 1:SUBLANES, :]
                    ci = xi[SUBLANES - 1:SUBLANES, :]
                new.append(jnp.concatenate([cr, ci], axis=1))
            return tuple(new)

        final = lax.fori_loop(0, seg // (2 * SUBLANES), scan_body, init)
        for i in range(N_SLAB):
            sout_ref[s, i:i + 1, :] = final[i]
            if carry_tiles:
                carry_ref[i:i + 1, :] = final[i]

    y = jnp.concatenate([_dot(bu_ref[i].astype(BF16), cmat_ref[i]) for i in range(N_SLAB)], axis=1)
    y = y + d_ref[...] * p_ref[:, :D_S5]
    z = jax.nn.gelu(y)
    z = z * jax.nn.sigmoid(_dot(z.astype(BF16), wglu_ref[...]) + bglu_ref[...])

    xc = p_ref[:, D_S5 + D_CONV:D_S5 + 2 * D_CONV] * p_ref[:, D_S5 + 2 * D_CONV:]
    w0 = convw_ref[0:1, :]
    w1 = convw_ref[1:2, :]
    w2 = convw_ref[2:3, :]
    conv_parts = []
    for s in range(nseg):
        base = s * (seg + SUBLANES)
        xc_s = xc[s * seg:(s + 1) * seg]
        tail = xc_s[seg - 2:seg]
        xcs_ref[pl.ds(base + SUBLANES, seg), :] = xc_s
        if carry_tiles:
            @pl.when(j == 0)
            def _(s=s, base=base):
                xcs_ref[pl.ds(base + SUBLANES - 2, 2), :] = cbuf_ref[s]
        else:
            xcs_ref[pl.ds(base + SUBLANES - 2, 2), :] = cbuf_ref[s]
        xm1 = xcs_ref[pl.ds(base + SUBLANES - 1, seg), :]
        xm2 = xcs_ref[pl.ds(base + SUBLANES - 2, seg), :]
        conv_parts.append(w0 * xm2 + w1 * xm1 + w2 * xc_s)
        cout_ref[s] = tail
        if carry_tiles:
            xcs_ref[pl.ds(base + SUBLANES - 2, 2), :] = tail
    conv = conv_parts[0] if nseg == 1 else jnp.concatenate(conv_parts, axis=0)
    y_b = p_ref[:, D_S5:D_S5 + D_CONV] * conv

    mix = _dot(jnp.concatenate([z, y_b], axis=1).astype(BF16), wout_ref[...])
    x1 = _layer_norm(DEEPNORM_ALPHA * x + (1.0 + mod_rows(2)) * mix, g1_ref[...], b1_ref[...])
    x1_ref[...] = x1
    h2 = x1 * (1.0 + mod_rows(4)) + mod_rows(3)
    lo_bits = lax.bitcast_convert_type(h2[:, :HALF].astype(BF16).astype(F32), jnp.int32)
    hi_bits = lax.bitcast_convert_type(h2[:, HALF:].astype(BF16).astype(F32), jnp.int32)
    packed = lax.bitcast_convert_type(hi_bits | lax.shift_right_logical(lo_bits, 16), F32)
    for c in range(PACK_ROWS):
        h2s_ref[slot, pl.ds(c, TL, stride=ROW_TILE), :] = packed[:, c * LANES:(c + 1) * LANES]

    logits = _dot(h2.astype(BF16), wr_ref[...]) + br_ref[...]
    lane = lax.broadcasted_iota(jnp.int32, logits.shape, 1)
    lane_f = lane.astype(F32)
    vals, sels, hots = [], [], []
    cur = logits
    for _ in range(TOP_K):
        m = jnp.max(cur, axis=-1, keepdims=True)
        am = jnp.min(jnp.where(cur == m, lane_f, float(LANES)), axis=-1, keepdims=True)
        hot = lane_f == am
        vals.append(m)
        sels.append(am)
        hots.append(hot)
        cur = jnp.where(hot, -jnp.inf, cur)
    exps = [jnp.exp(v - vals[0]) for v in vals]
    inv = 1.0 / (exps[0] + exps[1] + exps[2] + exps[3])
    gate_out = jnp.zeros(logits.shape, F32)
    for k in range(TOP_K):
        gate_out = jnp.where(lane == k, exps[k] * inv, gate_out)
    gate_ref[...] = gate_out

    tok = ((tile0 + step) * TL + lax.broadcasted_iota(jnp.int32, logits.shape, 0)).astype(F32)
    record = jnp.zeros(logits.shape, F32)
    for k in range(TOP_K):
        record = jnp.where(lane == k, sels[k], record)
        record = jnp.where(lane == TOP_K + k, tok + float(k * cap_rows), record)
    h2s_ref[slot, pl.ds(PACK_ROWS, TL, stride=ROW_TILE), :] = record

    @pl.when(step == 0)
    def _():
        cnt_ref[...] = cnt0_ref[...]

    chosen = jnp.zeros(logits.shape, F32)
    for k in range(TOP_K):
        chosen = jnp.where(hots[k], 1.0, chosen)
    r_i = lax.broadcasted_iota(jnp.int32, (TL, TL), 0)
    c_i = lax.broadcasted_iota(jnp.int32, (TL, TL), 1)
    before = jnp.where(c_i < r_i, 1.0, 0.0).astype(BF16)
    rank_base = _dot(before, chosen.astype(BF16)) + cnt_ref[...]
    pos_mat = jnp.zeros(logits.shape, F32)
    for k in range(TOP_K):
        rank_k = jnp.sum(jnp.where(hots[k], rank_base, 0.0), axis=-1, keepdims=True)
        pos_mat = jnp.where(lane == k, (sels[k] * float(cap_rows) + rank_k) * float(ROW_TILE), pos_mat)
    cnt_ref[...] = cnt_ref[...] + jnp.sum(chosen, axis=0, keepdims=True)
    cnt_out_ref[...] = cnt_ref[...]
    pos_t = pos_mat.T[:SUBLANES, :].astype(jnp.int32)
    posv_ref[...] = pos_t
    smem_copy.start()
    wait_scatter(prev)

    @pl.when(is_last)
    def _():
        smem_copy.wait()

        def tail_body(t8, c):
            issue_rows(slot, t8 * SUBLANES, SUBLANES)
            return c
        lax.fori_loop(0, TL // SUBLANES, tail_body, 0)
        wait_scatter(slot)


def _mixer_call(x3, mod, h0, cbuf, weights, cnt0, nseg, seg, tiles_per_group, tile0, n_tok, aliased):
    groups = x3.shape[0]
    nseq = mod.shape[0]
    carry_tiles = nseg == 1
    n_alias = len(aliased)
    n_tiles = n_tok // TL
    cap_rows = n_tok

    def full(a):
        nd = a.ndim
        return pl.BlockSpec(a.shape, lambda g, j, nd=nd: (0,) * nd)

    def tile_map(g, j):
        return (tile0 + g * tiles_per_group + j, 0)

    in_specs = [pl.BlockSpec((1, TL, D_MODEL), lambda g, j: (g, j, 0)),
                pl.BlockSpec((nseg, 6, D_MODEL), lambda g, j: (g, 0, 0)),
                pl.BlockSpec((nseg, N_SLAB, SLAB_W), lambda g, j: (g, 0, 0)),
                pl.BlockSpec((nseg, CONV_W - 1, D_CONV), lambda g, j: (g, 0, 0))]
    in_specs += [full(w) for w in weights]
    in_specs += [full(cnt0)]
    in_specs += [pl.BlockSpec(memory_space=pl.ANY)] * n_alias
    out_shape = [jax.ShapeDtypeStruct((n_tok, D_MODEL), F32),
                 jax.ShapeDtypeStruct((n_tok, LANES), F32),
                 jax.ShapeDtypeStruct(((N_EXPERTS * cap_rows + TOP_K * TL) * ROW_TILE, LANES), F32),
                 jax.ShapeDtypeStruct((1, LANES), F32),
                 jax.ShapeDtypeStruct((nseq, N_SLAB, SLAB_W), F32),
                 jax.ShapeDtypeStruct((nseq, CONV_W - 1, D_CONV), F32)]
    out_specs = [pl.BlockSpec((TL, D_MODEL), tile_map),
                 pl.BlockSpec((TL, LANES), tile_map),
                 pl.BlockSpec(memory_space=pl.ANY),
                 pl.BlockSpec((1, LANES), lambda g, j: (0, 0)),
                 pl.BlockSpec((nseg, N_SLAB, SLAB_W), lambda g, j: (g, 0, 0)),
                 pl.BlockSpec((nseg, CONV_W - 1, D_CONV), lambda g, j: (g, 0, 0))]
    n_in = 4 + len(weights) + 1
    aliases = {n_in + k: k for k in range(n_alias)}
    scratch = [pltpu.VMEM((TL, 2 * D_MODEL), F32),
               pltpu.VMEM((N_SLAB, TL, SLAB_W), F32),
               pltpu.VMEM((nseg * (seg + SUBLANES), D_CONV), F32),
               pltpu.VMEM((N_SLAB, SLAB_W), F32),
               pltpu.VMEM((2, TL * ROW_TILE, LANES), F32),
               pltpu.VMEM((SUBLANES, TL), jnp.int32),
               pltpu.SMEM((SUBLANES, TL), jnp.int32),
               pltpu.VMEM((1, LANES), F32),
               pltpu.SemaphoreType.DMA((2,)),
               pltpu.SemaphoreType.DMA((1,))]
    return pl.pallas_call(
        functools.partial(_mixer_kernel, nseg, seg, carry_tiles, n_alias, cap_rows, tile0),
        grid=(groups, tiles_per_group),
        in_specs=in_specs, out_specs=out_specs, out_shape=out_shape,
        scratch_shapes=scratch,
        input_output_aliases=aliases,
        compiler_params=pltpu.CompilerParams(dimension_semantics=("arbitrary", "arbitrary"),
                                             vmem_limit_bytes=VMEM_LIMIT),
        name="mixer_prompt" if carry_tiles else "mixer_sample",
    )(x3, mod, h0, cbuf, *weights, cnt0, *aliased)


W_CAST_ROWS = 128
COPY_SPLIT = (64, 160)


def _expert_kernel(n_tok, row_ref, be_ref, nv_ref, nu_ref, xs_ref, wgu_ref, bgu_ref, wd_ref, bd_ref,
                   yk_hbm, wgu_bf, wd_bf, act_ref, out_a, out_b, slotv_ref, slots_ref, sem_o, sem_v):
    i = pl.program_id(0)
    n_used = nu_ref[0]
    outs = (out_a, out_b)
    prev = jnp.maximum(i - 1, 0)
    new_expert = jnp.logical_or(i == 0, be_ref[i] != be_ref[prev])

    @pl.when(jnp.logical_and(new_expert, i < n_used))
    def _():
        def cast_gu(r, c):
            rows = pl.ds(pl.multiple_of(r * W_CAST_ROWS, W_CAST_ROWS), W_CAST_ROWS)
            wgu_bf[rows, :] = wgu_ref[0, rows, :].astype(BF16)
            return c
        lax.fori_loop(0, D_MODEL // W_CAST_ROWS, cast_gu, 0)

        def cast_d(r, c):
            rows = pl.ds(pl.multiple_of(r * W_CAST_ROWS, W_CAST_ROWS), W_CAST_ROWS)
            wd_bf[rows, :] = wd_ref[0, rows, :].astype(BF16)
            return c
        lax.fori_loop(0, D_FF // W_CAST_ROWS, cast_d, 0)

    def issue_rows(src, t0, count):
        for q in range(count):
            t = t0 + q
            dst = pl.multiple_of(slots_ref[src, 0, t], ROW_TILE)
            row0 = pl.multiple_of(t * ROW_TILE, ROW_TILE)
            pltpu.make_async_copy(outs[src].at[pl.ds(row0, ROW_TILE), :],
                                  yk_hbm.at[pl.ds(dst, ROW_TILE), :],
                                  sem_o.at[src]).start(priority=q % DMA_THREADS)

    def wait_rows(sl):
        pltpu.make_async_copy(outs[sl], yk_hbm.at[pl.ds(0, E_BLK * ROW_TILE), :], sem_o.at[sl]).wait()

    lane = lax.broadcasted_iota(jnp.int32, (E_BLK, LANES), 1)
    row = lax.broadcasted_iota(jnp.int32, (E_BLK, 1), 0)
    spare0 = float(TOP_K * n_tok)

    @pl.when(i == 0)
    def _():
        t_i = lax.broadcasted_iota(jnp.int32, (SUBLANES, E_BLK), 1)
        slotv_ref[...] = (TOP_K * n_tok + t_i) * ROW_TILE
        pltpu.make_async_copy(slotv_ref, slots_ref.at[1], sem_v.at[0]).start()
        out_b[...] = jnp.zeros(out_b.shape, F32)

    def block(cur):
        prv = 1 - cur

        def smem_copy(dst):
            return pltpu.make_async_copy(slotv_ref, slots_ref.at[dst], sem_v.at[0])

        @pl.when(i < n_used)
        def _():
            smem_copy(prv).wait()
            record = xs_ref[pl.ds(PACK_ROWS, E_BLK, stride=ROW_TILE), :]
            hit = jnp.logical_and(record.astype(jnp.int32) == be_ref[i], lane < TOP_K)
            slot_f = jnp.sum(jnp.where(hit, pltpu.roll(record, LANES - TOP_K, 1), 0.0), axis=-1, keepdims=True)
            slot_f = jnp.where(row < nv_ref[i], slot_f, spare0 + row.astype(F32))
            slot_mat = jnp.where(lane == 0, slot_f * float(ROW_TILE), 0.0)
            slotv_ref[...] = slot_mat.T[:SUBLANES, :].astype(jnp.int32)
            smem_copy(cur).start()
            issue_rows(prv, 0, COPY_SPLIT[0])

        @pl.when(nv_ref[i] > 0)
        def _():
            live = row < nv_ref[i]
            words = [lax.bitcast_convert_type(xs_ref[pl.ds(c, E_BLK, stride=ROW_TILE), :], jnp.int32)
                     for c in range(PACK_ROWS)]
            lo = [lax.bitcast_convert_type(lax.shift_left(w, 16), F32) for w in words]
            hi = [lax.bitcast_convert_type(w & HI16_MASK, F32) for w in words]
            xb = jnp.where(live, jnp.concatenate(lo + hi, axis=1), 0.0).astype(BF16)
            issue_rows(prv, COPY_SPLIT[0], COPY_SPLIT[1] - COPY_SPLIT[0])
            hgu = _dot(xb, wgu_bf[...]) + bgu_ref[0]
            g = jnp.minimum(hgu[:, :D_FF], SWIGLU_LIMIT)
            up = jnp.clip(hgu[:, D_FF:], -SWIGLU_LIMIT, SWIGLU_LIMIT)
            act_ref[...] = ((up + 1.0) * (g * jax.nn.sigmoid(SWIGLU_ALPHA * g))).astype(BF16)

        @pl.when(n_used - i > 0)
        def _():
            issue_rows(prv, COPY_SPLIT[1], E_BLK - COPY_SPLIT[1])
            y = _dot(act_ref[...], wd_bf[...]) + bd_ref[0]

            @pl.when(i > 0)
            def _():
                wait_rows(cur)

            for c in range(ROW_TILE):
                outs[cur][pl.ds(c, E_BLK, stride=ROW_TILE), :] = y[:, c * LANES:(c + 1) * LANES]

        @pl.when(i == n_used)
        def _():
            smem_copy(prv).wait()

            def tail(t8, c):
                issue_rows(prv, t8 * SUBLANES, SUBLANES)
                return c
            lax.fori_loop(0, E_BLK // SUBLANES, tail, 0)
            wait_rows(cur)
            wait_rows(prv)

    for parity in range(2):
        @pl.when(i % 2 == parity)
        def _(parity=parity):
            block(parity)


def _expert_call(blk_row, blk_e, blk_valid, n_used, xs, w_gu, b_gu, w_down, b_down, n_tok):
    nb = blk_row.shape[0]
    grid_spec = pltpu.PrefetchScalarGridSpec(
        num_scalar_prefetch=4,
        grid=(nb,),
        in_specs=[
            pl.BlockSpec((E_BLK * ROW_TILE, LANES), lambda i, br, be, nv, nu: (br[i], 0)),
            pl.BlockSpec((1, D_MODEL, 2 * D_FF), lambda i, br, be, nv, nu: (be[i], 0, 0)),
            pl.BlockSpec((1, 1, 2 * D_FF), lambda i, br, be, nv, nu: (be[i], 0, 0)),
            pl.BlockSpec((1, D_FF, D_MODEL), lambda i, br, be, nv, nu: (be[i], 0, 0)),
            pl.BlockSpec((1, 1, D_MODEL), lambda i, br, be, nv, nu: (be[i], 0, 0)),
        ],
        out_specs=pl.BlockSpec(memory_space=pl.ANY),
        scratch_shapes=[pltpu.VMEM((D_MODEL, 2 * D_FF), BF16),
                        pltpu.VMEM((D_FF, D_MODEL), BF16),
                        pltpu.VMEM((E_BLK, D_FF), BF16),
                        pltpu.VMEM((E_BLK * ROW_TILE, LANES), F32),
                        pltpu.VMEM((E_BLK * ROW_TILE, LANES), F32),
                        pltpu.VMEM((SUBLANES, E_BLK), jnp.int32),
                        pltpu.SMEM((2, SUBLANES, E_BLK), jnp.int32),
                        pltpu.SemaphoreType.DMA((2,)),
                        pltpu.SemaphoreType.DMA((1,))],
    )
    return pl.pallas_call(
        functools.partial(_expert_kernel, n_tok),
        grid_spec=grid_spec,
        out_shape=jax.ShapeDtypeStruct(((TOP_K * n_tok + E_BLK) * ROW_TILE, LANES), F32),
        compiler_params=pltpu.CompilerParams(dimension_semantics=("arbitrary",),
                                             vmem_limit_bytes=VMEM_LIMIT),
        name="experts",
    )(blk_row, blk_e, blk_valid, n_used, xs, w_gu, b_gu, w_down, b_down)


def _combine_kernel(nseg, seg, y0_ref, y1_ref, y2_ref, y3_ref, x1_ref, gate_ref, mod_ref, g2_ref, b2_ref,
                    out_ref):
    gates = gate_ref[...]
    ffn = None
    for k, y_ref in enumerate((y0_ref, y1_ref, y2_ref, y3_ref)):
        rows = jnp.concatenate(
            [y_ref[pl.ds(c, TL, stride=ROW_TILE), :] for c in range(ROW_TILE)], axis=1)
        term = gates[:, k:k + 1] * rows
        ffn = term if ffn is None else ffn + term
    if nseg == 1:
        gate2 = mod_ref[0, 5:6, :]
    else:
        gate2 = jnp.concatenate(
            [jnp.broadcast_to(mod_ref[s, 5:6, :], (seg, D_MODEL)) for s in range(nseg)], axis=0)
    xa = DEEPNORM_ALPHA * x1_ref[...] + (1.0 + gate2) * ffn
    out_ref[...] = _layer_norm(xa, g2_ref[...], b2_ref[...])


def _combine_call(yk, x1, gates, mod, ln2_g, ln2_b, nseg, seg, n_tiles, tile0, tiles_per_group, n_tok):
    tiles_total = n_tok // TL

    def yk_spec(k):
        return pl.BlockSpec((TL * ROW_TILE, LANES), lambda i, k=k: (k * tiles_total + tile0 + i, 0))

    in_specs = [yk_spec(k) for k in range(TOP_K)] + [
        pl.BlockSpec((TL, D_MODEL), lambda i: (tile0 + i, 0)),
        pl.BlockSpec((TL, LANES), lambda i: (tile0 + i, 0)),
        pl.BlockSpec((nseg, 6, D_MODEL), lambda i: (i // tiles_per_group, 0, 0)),
        pl.BlockSpec((1, D_MODEL), lambda i: (0, 0)),
        pl.BlockSpec((1, D_MODEL), lambda i: (0, 0)),
    ]
    return pl.pallas_call(
        functools.partial(_combine_kernel, nseg, seg),
        grid=(n_tiles,),
        in_specs=in_specs,
        out_specs=pl.BlockSpec((TL, D_MODEL), lambda i: (i, 0)),
        out_shape=jax.ShapeDtypeStruct((n_tiles * TL, D_MODEL), F32),
        compiler_params=pltpu.CompilerParams(dimension_semantics=("arbitrary",),
                                             vmem_limit_bytes=VMEM_LIMIT),
        name="combine_prompt" if nseg == 1 else "combine_sample",
    )(yk, yk, yk, yk, x1, gates, mod, ln2_g, ln2_b)


def _s5_tables(lam_re, lam_im, log_dt, b_re, b_im, c_re, c_im):
    dt = jnp.exp(log_dt.astype(F32))[:, None]
    lam = lax.complex(lam_re.astype(F32), lam_im.astype(F32))
    lam_dt = lam * dt
    lam_bar = jnp.exp(lam_dt)
    b_bar = ((lam_bar - 1.0) / lam)[..., None] * lax.complex(b_re.astype(F32), b_im.astype(F32))
    gl = S5_GROUPS // N_SLAB
    eye = jnp.eye(gl, dtype=F32)

    def b_slab(part):
        a = part.reshape(N_SLAB, gl, S5_STATE, S5_GROUP)
        return jnp.einsum('sgph,gk->sghkp', a, eye).reshape(N_SLAB, gl * S5_GROUP, gl * S5_STATE)

    bmat = jnp.concatenate([b_slab(b_bar.real), b_slab(b_bar.imag)], axis=-1).astype(BF16)

    def c_slab(part):
        a = part.reshape(N_SLAB, gl, S5_GROUP, S5_STATE)
        return jnp.einsum('sghp,gk->sgpkh', a, eye).reshape(N_SLAB, gl * S5_STATE, gl * S5_GROUP)

    cmat = jnp.concatenate([c_slab(c_re.astype(F32)), -c_slab(c_im.astype(F32))], axis=1).astype(BF16)

    row = jnp.arange(SUBLANES, dtype=F32)[:, None, None]

    def power(k):
        return jnp.exp(lam_dt[None] * k)

    tabs = []
    for d in (1, 2, 4):
        pw = power(jnp.full_like(row, float(d)))
        mask = (row >= d).astype(F32)
        tabs += [pw.real * mask, pw.imag * mask]
    pw = power(row + 1.0)
    tabs += [pw.real, pw.imag]
    tab = jnp.stack(tabs, axis=0)
    tab = tab.reshape(8, SUBLANES, N_SLAB, gl * S5_STATE).transpose(2, 0, 1, 3)
    return bmat, cmat, tab


def _state_to_slab(re, im):
    n = re.shape[0]
    return jnp.concatenate([re.reshape(n, N_SLAB, HALF), im.reshape(n, N_SLAB, HALF)], axis=-1).astype(F32)


def _slab_to_state(s):
    n = s.shape[0]
    re = s[:, :, :HALF].reshape(n, S5_GROUPS, S5_STATE)
    im = s[:, :, HALF:].reshape(n, S5_GROUPS, S5_STATE)
    return re, im


def _block_table(counts, n_tok):
    n_assign = n_tok * TOP_K
    n_blocks = (n_assign + N_EXPERTS * (E_BLK - 1) + E_BLK - 1) // E_BLK + 1
    cap_blocks = n_tok // E_BLK
    nblk = (counts + E_BLK - 1) // E_BLK
    cum = jnp.cumsum(nblk)
    start = cum - nblk
    n_used = cum[-1]
    i = jnp.arange(n_blocks, dtype=jnp.int32)
    ii = jnp.maximum(jnp.minimum(i, n_used - 1), 0)
    e = jnp.minimum(jnp.sum((cum[None, :] <= ii[:, None]).astype(jnp.int32), axis=1), N_EXPERTS - 1)
    b = ii - start[e]
    blk_row = (e * cap_blocks + b).astype(jnp.int32)
    valid = jnp.where(i < n_used, jnp.minimum(E_BLK, counts[e] - b * E_BLK), 0).astype(jnp.int32)
    return blk_row, e, valid, n_used.astype(jnp.int32).reshape(1)


def kernel(x_prompt, x_sample, c_prompt, c_sample, state_s5_re, state_s5_im, state_conv, w_ada, b_ada, w_in,
           s5_lam_re, s5_lam_im, s5_log_dt, s5_b_re, s5_b_im, s5_c_re, s5_c_im, s5_d, w_glu, b_glu, conv_w,
           w_out, ln1_g, ln1_b, w_router, b_router, w_gu, b_gu, w_down, b_down, ln2_g, ln2_b):
    assert DEPTH == 1 and w_ada.shape[0] == 1
    n_p, l_p, _ = x_prompt.shape
    n_s, l_s, _ = x_sample.shape
    t_p, t_s = n_p * l_p, n_s * l_s
    n_tok = t_p + t_s
    tiles_p, tiles_s = t_p // TL, t_s // TL
    seq_per_tile = TL // l_s
    assert l_p % TL == 0 and TL % l_s == 0 and n_s % seq_per_tile == 0

    c_all = jnp.concatenate([c_prompt, c_sample], axis=0).astype(F32)
    pad = (-c_all.shape[0]) % SUBLANES
    c_all = jnp.pad(c_all, ((0, pad), (0, 0)))
    mod = _ada_call(c_all, w_ada[0], b_ada[0].reshape(1, -1)).reshape(-1, 6, D_MODEL)
    mod_p, mod_s = mod[:n_p], mod[n_p:n_p + n_s]

    bmat, cmat, tab = _s5_tables(s5_lam_re[0], s5_lam_im[0], s5_log_dt[0], s5_b_re[0], s5_b_im[0],
                                 s5_c_re[0], s5_c_im[0])
    wr = jnp.pad(w_router[0], ((0, 0), (0, LANES - N_EXPERTS))).astype(BF16)
    br = jnp.pad(b_router[0].astype(F32), (0, LANES - N_EXPERTS), constant_values=NEG_BIG).reshape(1, LANES)
    weights = (w_in[0].astype(BF16), bmat, cmat, tab, s5_d[0].reshape(1, D_S5).astype(F32),
               w_glu[0].astype(BF16), b_glu[0].reshape(1, D_S5).astype(F32), conv_w[0].astype(F32),
               w_out[0].astype(BF16), ln1_g[0].reshape(1, D_MODEL).astype(F32),
               ln1_b[0].reshape(1, D_MODEL).astype(F32), wr, br)

    h0_p = jnp.zeros((n_p, N_SLAB, SLAB_W), F32)
    cb_p = jnp.zeros((n_p, CONV_W - 1, D_CONV), F32)
    h0_s = _state_to_slab(state_s5_re[0], state_s5_im[0])
    cb_s = state_conv[0].astype(F32)

    assert n_tok % E_BLK == 0 and E_BLK % TL == 0
    cnt0 = jnp.zeros((1, LANES), F32)
    outs_p = _mixer_call(x_prompt, mod_p, h0_p, cb_p, weights, cnt0, 1, TL, l_p // TL, 0, n_tok, ())
    x1, gates, xs, cnt_p, s_p, conv_p = outs_p
    xs3 = x_sample.reshape(n_s // seq_per_tile, TL, D_MODEL)
    outs_s = _mixer_call(xs3, mod_s, h0_s, cb_s, weights, cnt_p, seq_per_tile, l_s, 1, tiles_p, n_tok,
                         (x1, gates, xs))
    x1, gates, xs, cnt, s_s, conv_s = outs_s

    counts = cnt[0, :N_EXPERTS].astype(jnp.int32)
    blk_row, blk_e, blk_valid, n_used = _block_table(counts, n_tok)
    yk = _expert_call(blk_row, blk_e, blk_valid, n_used, xs, w_gu[0].astype(F32),
                      b_gu[0].reshape(N_EXPERTS, 1, 2 * D_FF).astype(F32), w_down[0].astype(F32),
                      b_down[0].reshape(N_EXPERTS, 1, D_MODEL).astype(F32), n_tok)
    g2 = ln2_g[0].reshape(1, D_MODEL).astype(F32)
    b2 = ln2_b[0].reshape(1, D_MODEL).astype(F32)
    y_p = _combine_call(yk, x1, gates, mod_p, g2, b2, 1, TL, tiles_p, 0, l_p // TL, n_tok)
    y_s = _combine_call(yk, x1, gates, mod_s, g2, b2, seq_per_tile, l_s, tiles_s, tiles_p, 1, n_tok)

    p_re, p_im = _slab_to_state(s_p)
    s_re, s_im = _slab_to_state(s_s)
    return (y_p.reshape(n_p, l_p, D_MODEL), y_s.reshape(n_s, l_s, D_MODEL),
            p_re[None], p_im[None], conv_p[None], s_re[None], s_im[None], conv_s[None])
```

```python
import functools
import math

import jax
import jax.numpy as jnp
from jax import lax
from jax.experimental import pallas as pl
from jax.experimental.pallas import tpu as pltpu

F32 = jnp.float32
BF16 = jnp.bfloat16

D_MODEL = 1024
DEPTH = 1
D_S5 = 512
D_CONV = 512
S5_GROUP = 16
S5_GROUPS = 32
S5_STATE = 64
CONV_W = 3
N_EXPERTS = 32
TOP_K = 4
D_FF = 1024
SWIGLU_LIMIT = 7.0
SWIGLU_ALPHA = 1.702
LN_EPS = 1e-5
DEEPNORM_ALPHA = (2.0 * DEPTH) ** 0.25

TL = 256
SUBLANES = 8
LANES = 128
N_SLAB = 4
SLAB_W = 1024
HALF = 512
E_BLK = 512
ROW_TILE = D_MODEL // LANES
PACK_ROWS = HALF // LANES
HI16_MASK = -65536
DMA_THREADS = 2
NEG_BIG = -1e30
VMEM_LIMIT = 56 * 1024 * 1024


def _dot(a, b):
    return jnp.dot(a, b, preferred_element_type=F32)


def _layer_norm(x, g, b):
    mu = jnp.mean(x, axis=-1, keepdims=True)
    xc = x - mu
    var = jnp.mean(xc * xc, axis=-1, keepdims=True)
    return xc * lax.rsqrt(var + LN_EPS) * g + b


def _ada_kernel(c_ref, w_ref, b_ref, o_ref):
    c = c_ref[...]
    s = c * jax.nn.sigmoid(c)
    o_ref[...] = _dot(s.astype(BF16), w_ref[...].astype(BF16)) + b_ref[...]


def _ada_call(c_all, w_ada, b_ada):
    rows = c_all.shape[0]
    n_out = w_ada.shape[1]
    tn = 768
    return pl.pallas_call(
        _ada_kernel,
        grid=(n_out // tn,),
        in_specs=[pl.BlockSpec((rows, D_MODEL), lambda i: (0, 0)),
                  pl.BlockSpec((D_MODEL, tn), lambda i: (0, i)),
                  pl.BlockSpec((1, tn), lambda i: (0, i))],
        out_specs=pl.BlockSpec((rows, tn), lambda i: (0, i)),
        out_shape=jax.ShapeDtypeStruct((rows, n_out), F32),
        compiler_params=pltpu.CompilerParams(dimension_semantics=("arbitrary",),
                                             vmem_limit_bytes=VMEM_LIMIT),
        name="ada_mod",
    )(c_all, w_ada, b_ada)


def _mixer_kernel(nseg, seg, carry_tiles, n_alias, cap_rows, tile0,
                  x_ref, mod_ref, h0_ref, cbuf_ref, win_ref, bmat_ref, cmat_ref, tab_ref, d_ref,
                  wglu_ref, bglu_ref, convw_ref, wout_ref, g1_ref, b1_ref, wr_ref, br_ref, cnt0_ref, *rest):
    rest = rest[n_alias:]
    (x1_ref, gate_ref, xs_hbm, cnt_out_ref, sout_ref, cout_ref,
     p_ref, bu_ref, xcs_ref, carry_ref, h2s_ref, posv_ref, poss_ref, cnt_ref, sem_s, sem_v) = rest
    g_id = pl.program_id(0)
    j = pl.program_id(1)
    step = g_id * pl.num_programs(1) + j
    is_last = step == pl.num_programs(0) * pl.num_programs(1) - 1
    slot = step % 2

    def mod_rows(k):
        if nseg == 1:
            return mod_ref[0, k:k + 1, :]
        return jnp.concatenate(
            [jnp.broadcast_to(mod_ref[s, k:k + 1, :], (seg, D_MODEL)) for s in range(nseg)], axis=0)

    x = x_ref[0]
    h = x * (1.0 + mod_rows(1)) + mod_rows(0)
    p_ref[...] = _dot(h.astype(BF16), win_ref[...])

    for i in range(N_SLAB):
        u_i = p_ref[:, i * LANES:(i + 1) * LANES].astype(BF16)
        bu_ref[i] = _dot(u_i, bmat_ref[i])

    prev = 1 - slot
    smem_copy = pltpu.make_async_copy(posv_ref, poss_ref, sem_v.at[0])

    @pl.when(step == 0)
    def _():
        k_i = lax.broadcasted_iota(jnp.int32, (SUBLANES, TL), 0)
        t_i = lax.broadcasted_iota(jnp.int32, (SUBLANES, TL), 1)
        posv_ref[...] = (N_EXPERTS * cap_rows + k_i * TL + t_i) * ROW_TILE
        smem_copy.start()
        h2s_ref[...] = jnp.zeros(h2s_ref.shape, F32)

    smem_copy.wait()

    def issue_rows(src_slot, t0, n_rows):
        for q in range(n_rows):
            t = t0 + q
            src = pl.multiple_of(t * ROW_TILE, ROW_TILE)
            for k in range(TOP_K):
                dst = pl.multiple_of(poss_ref[k, t], ROW_TILE)
                pltpu.make_async_copy(h2s_ref.at[src_slot, pl.ds(src, ROW_TILE), :],
                                      xs_hbm.at[pl.ds(dst, ROW_TILE), :],
                                      sem_s.at[src_slot]).start(priority=k % DMA_THREADS)

    def wait_scatter(sl):
        for _ in range(TOP_K):
            pltpu.make_async_copy(h2s_ref.at[sl], xs_hbm.at[pl.ds(0, TL * ROW_TILE), :], sem_s.at[sl]).wait()

    if carry_tiles:
        @pl.when(j == 0)
        def _():
            carry_ref[...] = h0_ref[0]

    for s in range(nseg):
        if carry_tiles:
            init = tuple(carry_ref[i:i + 1, :] for i in range(N_SLAB))
        else:
            init = tuple(h0_ref[s, i:i + 1, :] for i in range(N_SLAB))

        def scan_body(r, carry, s=s):
            issue_rows(prev, s * seg + r * (2 * SUBLANES), 2 * SUBLANES)
            new = []
            for i in range(N_SLAB):
                cr = carry[i][:, :HALF]
                ci = carry[i][:, HALF:]
                for half in range(2):
                    row0 = pl.multiple_of(s * seg + r * (2 * SUBLANES) + half * SUBLANES, SUBLANES)
                    blk = bu_ref[i, pl.ds(row0, SUBLANES), :]
                    xr = blk[:, :HALF]
                    xi = blk[:, HALF:]
                    for k, d in enumerate((1, 2, 4)):
                        lr = tab_ref[i, 2 * k]
                        li = tab_ref[i, 2 * k + 1]
                        rr = pltpu.roll(xr, d, 0)
                        ri = pltpu.roll(xi, d, 0)
                        xr, xi = xr + (lr * rr - li * ri), xi + (lr * ri + li * rr)
                    pr = tab_ref[i, 6]
                    pi_ = tab_ref[i, 7]
                    crb = jnp.broadcast_to(cr, (SUBLANES, HALF))
                    cib = jnp.broadcast_to(ci, (SUBLANES, HALF))
                    xr, xi = xr + (pr * crb - pi_ * cib), xi + (pr * cib + pi_ * crb)
                    bu_ref[i, pl.ds(row0, SUBLANES), :] = jnp.concatenate([xr, xi], axis=1)
                    cr = xr[SUBLANES - 1:SUBLANES, :]
                    ci = xi[SUBLANES - 1:SUBLANES, :]
                new.append(jnp.concatenate([cr, ci], axis=1))
            return tuple(new)

        final = lax.fori_loop(0, seg // (2 * SUBLANES), scan_body, init)
        for i in range(N_SLAB):
            sout_ref[s, i:i + 1, :] = final[i]
            if carry_tiles:
                carry_ref[i:i + 1, :] = final[i]

    y = jnp.concatenate([_dot(bu_ref[i].astype(BF16), cmat_ref[i]) for i in range(N_SLAB)], axis=1)
    y = y + d_ref[...] * p_ref[:, :D_S5]
    z = jax.nn.gelu(y)
    z = z * jax.nn.sigmoid(_dot(z.astype(BF16), wglu_ref[...]) + bglu_ref[...])

    xc = p_ref[:, D_S5 + D_CONV:D_S5 + 2 * D_CONV] * p_ref[:, D_S5 + 2 * D_CONV:]
    w0 = convw_ref[0:1, :]
    w1 = convw_ref[1:2, :]
    w2 = convw_ref[2:3, :]
    conv_parts = []
    for s in range(nseg):
        base = s * (seg + SUBLANES)
        xc_s = xc[s * seg:(s + 1) * seg]
        tail = xc_s[seg - 2:seg]
        xcs_ref[pl.ds(base + SUBLANES, seg), :] = xc_s
        if carry_tiles:
            @pl.when(j == 0)
            def _(s=s, base=base):
                xcs_ref[pl.ds(base + SUBLANES - 2, 2), :] = cbuf_ref[s]
        else:
            xcs_ref[pl.ds(base + SUBLANES - 2, 2), :] = cbuf_ref[s]
        xm1 = xcs_ref[pl.ds(base + SUBLANES - 1, seg), :]
        xm2 = xcs_ref[pl.ds(base + SUBLANES - 2, seg), :]
        conv_parts.append(w0 * xm2 + w1 * xm1 + w2 * xc_s)
        cout_ref[s] = tail
        if carry_tiles:
            xcs_ref[pl.ds(base + SUBLANES - 2, 2), :] = tail
    conv = conv_parts[0] if nseg == 1 else jnp.concatenate(conv_parts, axis=0)
    y_b = p_ref[:, D_S5:D_S5 + D_CONV] * conv

    mix = _dot(jnp.concatenate([z, y_b], axis=1).astype(BF16), wout_ref[...])
    x1 = _layer_norm(DEEPNORM_ALPHA * x + (1.0 + mod_rows(2)) * mix, g1_ref[...], b1_ref[...])
    x1_ref[...] = x1
    h2 = x1 * (1.0 + mod_rows(4)) + mod_rows(3)
    lo_bits = lax.bitcast_convert_type(h2[:, :HALF].astype(BF16).astype(F32), jnp.int32)
    hi_bits = lax.bitcast_convert_type(h2[:, HALF:].astype(BF16).astype(F32), jnp.int32)
    packed = lax.bitcast_convert_type(hi_bits | lax.shift_right_logical(lo_bits, 16), F32)
    for c in range(PACK_ROWS):
        h2s_ref[slot, pl.ds(c, TL, stride=ROW_TILE), :] = packed[:, c * LANES:(c + 1) * LANES]

    logits = _dot(h2.astype(BF16), wr_ref[...]) + br_ref[...]
    lane = lax.broadcasted_iota(jnp.int32, logits.shape, 1)
    lane_f = lane.astype(F32)
    vals, sels, hots = [], [], []
    cur = logits
    for _ in range(TOP_K):
        m = jnp.max(cur, axis=-1, keepdims=True)
        am = jnp.min(jnp.where(cur == m, lane_f, float(LANES)), axis=-1, keepdims=True)
        hot = lane_f == am
        vals.append(m)
        sels.append(am)
        hots.append(hot)
        cur = jnp.where(hot, -jnp.inf, cur)
    exps = [jnp.exp(v - vals[0]) for v in vals]
    inv = 1.0 / (exps[0] + exps[1] + exps[2] + exps[3])
    gate_out = jnp.zeros(logits.shape, F32)
    for k in range(TOP_K):
        gate_out = jnp.where(lane == k, exps[k] * inv, gate_out)
    gate_ref[...] = gate_out

    tok = ((tile0 + step) * TL + lax.broadcasted_iota(jnp.int32, logits.shape, 0)).astype(F32)
    record = jnp.zeros(logits.shape, F32)
    for k in range(TOP_K):
        record = jnp.where(lane == k, sels[k], record)
        record = jnp.where(lane == TOP_K + k, tok + float(k * cap_rows), record)
    h2s_ref[slot, pl.ds(PACK_ROWS, TL, stride=ROW_TILE), :] = record

    @pl.when(step == 0)
    def _():
        cnt_ref[...] = cnt0_ref[...]

    chosen = jnp.zeros(logits.shape, F32)
    for k in range(TOP_K):
        chosen = jnp.where(hots[k], 1.0, chosen)
    r_i = lax.broadcasted_iota(jnp.int32, (TL, TL), 0)
    c_i = lax.broadcasted_iota(jnp.int32, (TL, TL), 1)
    before = jnp.where(c_i < r_i, 1.0, 0.0).astype(BF16)
    rank_base = _dot(before, chosen.astype(BF16)) + cnt_ref[...]
    pos_mat = jnp.zeros(logits.shape, F32)
    for k in range(TOP_K):
        rank_k = jnp.sum(jnp.where(hots[k], rank_base, 0.0), axis=-1, keepdims=True)
        pos_mat = jnp.where(lane == k, (sels[k] * float(cap_rows) + rank_k) * float(ROW_TILE), pos_mat)
    cnt_ref[...] = cnt_ref[...] + jnp.sum(chosen, axis=0, keepdims=True)
    cnt_out_ref[...] = cnt_ref[...]
    pos_t = pos_mat.T[:SUBLANES, :].astype(jnp.int32)
    posv_ref[...] = pos_t
    smem_copy.start()
    wait_scatter(prev)

    @pl.when(is_last)
    def _():
        smem_copy.wait()

        def tail_body(t8, c):
            issue_rows(slot, t8 * SUBLANES, SUBLANES)
            return c
        lax.fori_loop(0, TL // SUBLANES, tail_body, 0)
        wait_scatter(slot)


def _mixer_call(x3, mod, h0, cbuf, weights, cnt0, nseg, seg, tiles_per_group, tile0, n_tok, aliased):
    groups = x3.shape[0]
    nseq = mod.shape[0]
    carry_tiles = nseg == 1
    n_alias = len(aliased)
    n_tiles = n_tok // TL
    cap_rows = n_tok

    def full(a):
        nd = a.ndim
        return pl.BlockSpec(a.shape, lambda g, j, nd=nd: (0,) * nd)

    def tile_map(g, j):
        return (tile0 + g * tiles_per_group + j, 0)

    in_specs = [pl.BlockSpec((1, TL, D_MODEL), lambda g, j: (g, j, 0)),
                pl.BlockSpec((nseg, 6, D_MODEL), lambda g, j: (g, 0, 0)),
                pl.BlockSpec((nseg, N_SLAB, SLAB_W), lambda g, j: (g, 0, 0)),
                pl.BlockSpec((nseg, CONV_W - 1, D_CONV), lambda g, j: (g, 0, 0))]
    in_specs += [full(w) for w in weights]
    in_specs += [full(cnt0)]
    in_specs += [pl.BlockSpec(memory_space=pl.ANY)] * n_alias
    out_shape = [jax.ShapeDtypeStruct((n_tok, D_MODEL), F32),
                 jax.ShapeDtypeStruct((n_tok, LANES), F32),
                 jax.ShapeDtypeStruct(((N_EXPERTS * cap_rows + TOP_K * TL) * ROW_TILE, LANES), F32),
                 jax.ShapeDtypeStruct((1, LANES), F32),
                 jax.ShapeDtypeStruct((nseq, N_SLAB, SLAB_W), F32),
                 jax.ShapeDtypeStruct((nseq, CONV_W - 1, D_CONV), F32)]
    out_specs = [pl.BlockSpec((TL, D_MODEL), tile_map),
                 pl.BlockSpec((TL, LANES), tile_map),
                 pl.BlockSpec(memory_space=pl.ANY),
                 pl.BlockSpec((1, LANES), lambda g, j: (0, 0)),
                 pl.BlockSpec((nseg, N_SLAB, SLAB_W), lambda g, j: (g, 0, 0)),
                 pl.BlockSpec((nseg, CONV_W - 1, D_CONV), lambda g, j: (g, 0, 0))]
    n_in = 4 + len(weights) + 1
    aliases = {n_in + k: k for k in range(n_alias)}
    scratch = [pltpu.VMEM((TL, 2 * D_MODEL), F32),
               pltpu.VMEM((N_SLAB, TL, SLAB_W), F32),
               pltpu.VMEM((nseg * (seg + SUBLANES), D_CONV), F32),
               pltpu.VMEM((N_SLAB, SLAB_W), F32),
               pltpu.VMEM((2, TL * ROW_TILE, LANES), F32),
               pltpu.VMEM((SUBLANES, TL), jnp.int32),
               pltpu.SMEM((SUBLANES, TL), jnp.int32),
               pltpu.VMEM((1, LANES), F32),
               pltpu.SemaphoreType.DMA((2,)),
               pltpu.SemaphoreType.DMA((1,))]
    return pl.pallas_call(
        functools.partial(_mixer_kernel, nseg, seg, carry_tiles, n_alias, cap_rows, tile0),
        grid=(groups, tiles_per_group),
        in_specs=in_specs, out_specs=out_specs, out_shape=out_shape,
        scratch_shapes=scratch,
        input_output_aliases=aliases,
        compiler_params=pltpu.CompilerParams(dimension_semantics=("arbitrary", "arbitrary"),
                                             vmem_limit_bytes=VMEM_LIMIT),
        name="mixer_prompt" if carry_tiles else "mixer_sample",
    )(x3, mod, h0, cbuf, *weights, cnt0, *aliased)


W_CAST_ROWS = 128

def _expert_kernel(n_tok, row_ref, be_ref, nv_ref, nu_ref, xs_ref, wgu_ref, bgu_ref, wd_ref, bd_ref,
                   yk_hbm, wgu_bf, wd_bf, out_a, out_b, slotv_ref, slots_ref, sem_o, sem_v):
    i = pl.program_id(0)
    n_used = nu_ref[0]
    outs = (out_a, out_b)
    prev = jnp.maximum(i - 1, 0)
    new_expert = jnp.logical_or(i == 0, be_ref[i] != be_ref[prev])

    @pl.when(jnp.logical_and(new_expert, i < n_used))
    def _():
        def cast_gu(r, c):
            rows = pl.ds(pl.multiple_of(r * W_CAST_ROWS, W_CAST_ROWS), W_CAST_ROWS)
            wgu_bf[rows, :] = wgu_ref[0, rows, :].astype(BF16)
            return c
        lax.fori_loop(0, D_MODEL // W_CAST_ROWS, cast_gu, 0)

        def cast_d(r, c):
            rows = pl.ds(pl.multiple_of(r * W_CAST_ROWS, W_CAST_ROWS), W_CAST_ROWS)
            wd_bf[rows, :] = wd_ref[0, rows, :].astype(BF16)
            return c
        lax.fori_loop(0, D_FF // W_CAST_ROWS, cast_d, 0)

    def issue_rows(src, t0, count):
        for q in range(count):
            t = t0 + q
            dst = pl.multiple_of(slots_ref[src, 0, t], ROW_TILE)
            row0 = pl.multiple_of(t * ROW_TILE, ROW_TILE)
            pltpu.make_async_copy(outs[src].at[pl.ds(row0, ROW_TILE), :],
                                  yk_hbm.at[pl.ds(dst, ROW_TILE), :],
                                  sem_o.at[src]).start(priority=q % DMA_THREADS)

    def wait_rows(sl):
        pltpu.make_async_copy(outs[sl], yk_hbm.at[pl.ds(0, E_BLK * ROW_TILE), :], sem_o.at[sl]).wait()

    lane = lax.broadcasted_iota(jnp.int32, (E_BLK, LANES), 1)
    row = lax.broadcasted_iota(jnp.int32, (E_BLK, 1), 0)
    spare0 = float(TOP_K * n_tok)

    @pl.when(i == 0)
    def _():
        t_i = lax.broadcasted_iota(jnp.int32, (SUBLANES, E_BLK), 1)
        slotv_ref[...] = (TOP_K * n_tok + t_i) * ROW_TILE
        pltpu.make_async_copy(slotv_ref, slots_ref.at[1], sem_v.at[0]).start()
        out_b[...] = jnp.zeros(out_b.shape, F32)

    def block(cur):
        prv = 1 - cur

        def smem_copy(dst):
            return pltpu.make_async_copy(slotv_ref, slots_ref.at[dst], sem_v.at[0])

        @pl.when(i < n_used)
        def _():
            smem_copy(prv).wait()
            record = xs_ref[pl.ds(PACK_ROWS, E_BLK, stride=ROW_TILE), :]
            hit = jnp.logical_and(record.astype(jnp.int32) == be_ref[i], lane < TOP_K)
            slot_f = jnp.sum(jnp.where(hit, pltpu.roll(record, LANES - TOP_K, 1), 0.0), axis=-1, keepdims=True)
            slot_f = jnp.where(row < nv_ref[i], slot_f, spare0 + row.astype(F32))
            slot_mat = jnp.where(lane == 0, slot_f * float(ROW_TILE), 0.0)
            slotv_ref[...] = slot_mat.T[:SUBLANES, :].astype(jnp.int32)
            smem_copy(cur).start()

        @pl.when(i < n_used)
        def _():
            issue_rows(prv, 0, E_BLK)
            live = row < nv_ref[i]
            words = [lax.bitcast_convert_type(xs_ref[pl.ds(c, E_BLK, stride=ROW_TILE), :], jnp.int32)
                     for c in range(PACK_ROWS)]
            lo = [lax.bitcast_convert_type(lax.shift_left(w, 16), F32) for w in words]
            hi = [lax.bitcast_convert_type(w & HI16_MASK, F32) for w in words]
            xb = jnp.where(live, jnp.concatenate(lo + hi, axis=1), 0.0).astype(BF16)
            hgu = _dot(xb, wgu_bf[...]) + bgu_ref[0]
            g = jnp.minimum(hgu[:, :D_FF], SWIGLU_LIMIT)
            up = jnp.clip(hgu[:, D_FF:], -SWIGLU_LIMIT, SWIGLU_LIMIT)
            act = (up + 1.0) * (g * jax.nn.sigmoid(SWIGLU_ALPHA * g))
            y = _dot(act.astype(BF16), wd_bf[...]) + bd_ref[0]

            @pl.when(i > 0)
            def _():
                wait_rows(cur)

            for c in range(ROW_TILE):
                outs[cur][pl.ds(c, E_BLK, stride=ROW_TILE), :] = y[:, c * LANES:(c + 1) * LANES]

        @pl.when(i == n_used)
        def _():
            smem_copy(prv).wait()

            def tail(t8, c):
                issue_rows(prv, t8 * SUBLANES, SUBLANES)
                return c
            lax.fori_loop(0, E_BLK // SUBLANES, tail, 0)
            wait_rows(cur)
            wait_rows(prv)

    for parity in range(2):
        @pl.when(i % 2 == parity)
        def _(parity=parity):
            block(parity)


def _expert_call(blk_row, blk_e, blk_valid, n_used, xs, w_gu, b_gu, w_down, b_down, n_tok):
    nb = blk_row.shape[0]
    grid_spec = pltpu.PrefetchScalarGridSpec(
        num_scalar_prefetch=4,
        grid=(nb,),
        in_specs=[
            pl.BlockSpec((E_BLK * ROW_TILE, LANES), lambda i, br, be, nv, nu: (br[i], 0)),
            pl.BlockSpec((1, D_MODEL, 2 * D_FF), lambda i, br, be, nv, nu: (be[i], 0, 0)),
            pl.BlockSpec((1, 1, 2 * D_FF), lambda i, br, be, nv, nu: (be[i], 0, 0)),
            pl.BlockSpec((1, D_FF, D_MODEL), lambda i, br, be, nv, nu: (be[i], 0, 0)),
            pl.BlockSpec((1, 1, D_MODEL), lambda i, br, be, nv, nu: (be[i], 0, 0)),
        ],
        out_specs=pl.BlockSpec(memory_space=pl.ANY),
        scratch_shapes=[pltpu.VMEM((D_MODEL, 2 * D_FF), BF16),
                        pltpu.VMEM((D_FF, D_MODEL), BF16),
                        pltpu.VMEM((E_BLK * ROW_TILE, LANES), F32),
                        pltpu.VMEM((E_BLK * ROW_TILE, LANES), F32),
                        pltpu.VMEM((SUBLANES, E_BLK), jnp.int32),
                        pltpu.SMEM((2, SUBLANES, E_BLK), jnp.int32),
                        pltpu.SemaphoreType.DMA((2,)),
                        pltpu.SemaphoreType.DMA((1,))],
    )
    return pl.pallas_call(
        functools.partial(_expert_kernel, n_tok),
        grid_spec=grid_spec,
        out_shape=jax.ShapeDtypeStruct(((TOP_K * n_tok + E_BLK) * ROW_TILE, LANES), F32),
        compiler_params=pltpu.CompilerParams(dimension_semantics=("arbitrary",),
                                             vmem_limit_bytes=VMEM_LIMIT),
        name="experts",
    )(blk_row, blk_e, blk_valid, n_used, xs, w_gu, b_gu, w_down, b_down)


def _combine_kernel(nseg, seg, y0_ref, y1_ref, y2_ref, y3_ref, x1_ref, gate_ref, mod_ref, g2_ref, b2_ref,
                    out_ref):
    gates = gate_ref[...]
    ffn = None
    for k, y_ref in enumerate((y0_ref, y1_ref, y2_ref, y3_ref)):
        rows = jnp.concatenate(
            [y_ref[pl.ds(c, TL, stride=ROW_TILE), :] for c in range(ROW_TILE)], axis=1)
        term = gates[:, k:k + 1] * rows
        ffn = term if ffn is None else ffn + term
    if nseg == 1:
        gate2 = mod_ref[0, 5:6, :]
    else:
        gate2 = jnp.concatenate(
            [jnp.broadcast_to(mod_ref[s, 5:6, :], (seg, D_MODEL)) for s in range(nseg)], axis=0)
    xa = DEEPNORM_ALPHA * x1_ref[...] + (1.0 + gate2) * ffn
    out_ref[...] = _layer_norm(xa, g2_ref[...], b2_ref[...])


def _combine_call(yk, x1, gates, mod, ln2_g, ln2_b, nseg, seg, n_tiles, tile0, tiles_per_group, n_tok):
    tiles_total = n_tok // TL

    def yk_spec(k):
        return pl.BlockSpec((TL * ROW_TILE, LANES), lambda i, k=k: (k * tiles_total + tile0 + i, 0))

    in_specs = [yk_spec(k) for k in range(TOP_K)] + [
        pl.BlockSpec((TL, D_MODEL), lambda i: (tile0 + i, 0)),
        pl.BlockSpec((TL, LANES), lambda i: (tile0 + i, 0)),
        pl.BlockSpec((nseg, 6, D_MODEL), lambda i: (i // tiles_per_group, 0, 0)),
        pl.BlockSpec((1, D_MODEL), lambda i: (0, 0)),
        pl.BlockSpec((1, D_MODEL), lambda i: (0, 0)),
    ]
    return pl.pallas_call(
        functools.partial(_combine_kernel, nseg, seg),
        grid=(n_tiles,),
        in_specs=in_specs,
        out_specs=pl.BlockSpec((TL, D_MODEL), lambda i: (i, 0)),
        out_shape=jax.ShapeDtypeStruct((n_tiles * TL, D_MODEL), F32),
        compiler_params=pltpu.CompilerParams(dimension_semantics=("arbitrary",),
                                             vmem_limit_bytes=VMEM_LIMIT),
        name="combine_prompt" if nseg == 1 else "combine_sample",
    )(yk, yk, yk, yk, x1, gates, mod, ln2_g, ln2_b)


def _s5_tables(lam_re, lam_im, log_dt, b_re, b_im, c_re, c_im):
    dt = jnp.exp(log_dt.astype(F32))[:, None]
    lam = lax.complex(lam_re.astype(F32), lam_im.astype(F32))
    lam_dt = lam * dt
    lam_bar = jnp.exp(lam_dt)
    b_bar = ((lam_bar - 1.0) / lam)[..., None] * lax.complex(b_re.astype(F32), b_im.astype(F32))
    gl = S5_GROUPS // N_SLAB
    eye = jnp.eye(gl, dtype=F32)

    def b_slab(part):
        a = part.reshape(N_SLAB, gl, S5_STATE, S5_GROUP)
        return jnp.einsum('sgph,gk->sghkp', a, eye).reshape(N_SLAB, gl * S5_GROUP, gl * S5_STATE)

    bmat = jnp.concatenate([b_slab(b_bar.real), b_slab(b_bar.imag)], axis=-1).astype(BF16)

    def c_slab(part):
        a = part.reshape(N_SLAB, gl, S5_GROUP, S5_STATE)
        return jnp.einsum('sghp,gk->sgpkh', a, eye).reshape(N_SLAB, gl * S5_STATE, gl * S5_GROUP)

    cmat = jnp.concatenate([c_slab(c_re.astype(F32)), -c_slab(c_im.astype(F32))], axis=1).astype(BF16)

    row = jnp.arange(SUBLANES, dtype=F32)[:, None, None]

    def power(k):
        return jnp.exp(lam_dt[None] * k)

    tabs = []
    for d in (1, 2, 4):
        pw = power(jnp.full_like(row, float(d)))
        mask = (row >= d).astype(F32)
        tabs += [pw.real * mask, pw.imag * mask]
    pw = power(row + 1.0)
    tabs += [pw.real, pw.imag]
    tab = jnp.stack(tabs, axis=0)
    tab = tab.reshape(8, SUBLANES, N_SLAB, gl * S5_STATE).transpose(2, 0, 1, 3)
    return bmat, cmat, tab


def _state_to_slab(re, im):
    n = re.shape[0]
    return jnp.concatenate([re.reshape(n, N_SLAB, HALF), im.reshape(n, N_SLAB, HALF)], axis=-1).astype(F32)


def _slab_to_state(s):
    n = s.shape[0]
    re = s[:, :, :HALF].reshape(n, S5_GROUPS, S5_STATE)
    im = s[:, :, HALF:].reshape(n, S5_GROUPS, S5_STATE)
    return re, im


def _block_table(counts, n_tok):
    n_assign = n_tok * TOP_K
    n_blocks = (n_assign + N_EXPERTS * (E_BLK - 1) + E_BLK - 1) // E_BLK + 1
    cap_blocks = n_tok // E_BLK
    nblk = (counts + E_BLK - 1) // E_BLK
    cum = jnp.cumsum(nblk)
    start = cum - nblk
    n_used = cum[-1]
    i = jnp.arange(n_blocks, dtype=jnp.int32)
    ii = jnp.maximum(jnp.minimum(i, n_used - 1), 0)
    e = jnp.minimum(jnp.sum((cum[None, :] <= ii[:, None]).astype(jnp.int32), axis=1), N_EXPERTS - 1)
    b = ii - start[e]
    blk_row = (e * cap_blocks + b).astype(jnp.int32)
    valid = jnp.where(i < n_used, jnp.minimum(E_BLK, counts[e] - b * E_BLK), 0).astype(jnp.int32)
    return blk_row, e, valid, n_used.astype(jnp.int32).reshape(1)


def kernel(x_prompt, x_sample, c_prompt, c_sample, state_s5_re, state_s5_im, state_conv, w_ada, b_ada, w_in,
           s5_lam_re, s5_lam_im, s5_log_dt, s5_b_re, s5_b_im, s5_c_re, s5_c_im, s5_d, w_glu, b_glu, conv_w,
           w_out, ln1_g, ln1_b, w_router, b_router, w_gu, b_gu, w_down, b_down, ln2_g, ln2_b):
    assert DEPTH == 1 and w_ada.shape[0] == 1
    n_p, l_p, _ = x_prompt.shape
    n_s, l_s, _ = x_sample.shape
    t_p, t_s = n_p * l_p, n_s * l_s
    n_tok = t_p + t_s
    tiles_p, tiles_s = t_p // TL, t_s // TL
    seq_per_tile = TL // l_s
    assert l_p % TL == 0 and TL % l_s == 0 and n_s % seq_per_tile == 0

    c_all = jnp.concatenate([c_prompt, c_sample], axis=0).astype(F32)
    pad = (-c_all.shape[0]) % SUBLANES
    c_all = jnp.pad(c_all, ((0, pad), (0, 0)))
    mod = _ada_call(c_all, w_ada[0], b_ada[0].reshape(1, -1)).reshape(-1, 6, D_MODEL)
    mod_p, mod_s = mod[:n_p], mod[n_p:n_p + n_s]

    bmat, cmat, tab = _s5_tables(s5_lam_re[0], s5_lam_im[0], s5_log_dt[0], s5_b_re[0], s5_b_im[0],
                                 s5_c_re[0], s5_c_im[0])
    wr = jnp.pad(w_router[0], ((0, 0), (0, LANES - N_EXPERTS))).astype(BF16)
    br = jnp.pad(b_router[0].astype(F32), (0, LANES - N_EXPERTS), constant_values=NEG_BIG).reshape(1, LANES)
    weights = (w_in[0].astype(BF16), bmat, cmat, tab, s5_d[0].reshape(1, D_S5).astype(F32),
               w_glu[0].astype(BF16), b_glu[0].reshape(1, D_S5).astype(F32), conv_w[0].astype(F32),
               w_out[0].astype(BF16), ln1_g[0].reshape(1, D_MODEL).astype(F32),
               ln1_b[0].reshape(1, D_MODEL).astype(F32), wr, br)

    h0_p = jnp.zeros((n_p, N_SLAB, SLAB_W), F32)
    cb_p = jnp.zeros((n_p, CONV_W - 1, D_CONV), F32)
    h0_s = _state_to_slab(state_s5_re[0], state_s5_im[0])
    cb_s = state_conv[0].astype(F32)

    assert n_tok % E_BLK == 0 and E_BLK % TL == 0
    cnt0 = jnp.zeros((1, LANES), F32)
    outs_p = _mixer_call(x_prompt, mod_p, h0_p, cb_p, weights, cnt0, 1, TL, l_p // TL, 0, n_tok, ())
    x1, gates, xs, cnt_p, s_p, conv_p = outs_p
    xs3 = x_sample.reshape(n_s // seq_per_tile, TL, D_MODEL)
    outs_s = _mixer_call(xs3, mod_s, h0_s, cb_s, weights, cnt_p, seq_per_tile, l_s, 1, tiles_p, n_tok,
                         (x1, gates, xs))
    x1, gates, xs, cnt, s_s, conv_s = outs_s

    counts = cnt[0, :N_EXPERTS].astype(jnp.int32)
    blk_row, blk_e, blk_valid, n_used = _block_table(counts, n_tok)
    yk = _expert_call(blk_row, blk_e, blk_valid, n_used, xs, w_gu[0].astype(F32),
                      b_gu[0].reshape(N_EXPERTS, 1, 2 * D_FF).astype(F32), w_down[0].astype(F32),
                      b_down[0].reshape(N_EXPERTS, 1, D_MODEL).astype(F32), n_tok)
    g2 = ln2_g[0].reshape(1, D_MODEL).astype(F32)
    b2 = ln2_b[0].reshape(1, D_MODEL).astype(F32)
    y_p = _combine_call(yk, x1, gates, mod_p, g2, b2, 1, TL, tiles_p, 0, l_p // TL, n_tok)
    y_s = _combine_call(yk, x1, gates, mod_s, g2, b2, seq_per_tile, l_s, tiles_s, tiles_p, 1, n_tok)

    p_re, p_im = _slab_to_state(s_p)
    s_re, s_im = _slab_to_state(s_s)
    return (y_p.reshape(n_p, l_p, D_MODEL), y_s.reshape(n_s, l_s, D_MODEL),
            p_re[None], p_im[None], conv_p[None], s_re[None], s_im[None], conv_s[None])
```

```python
import functools
import math

import jax
import jax.numpy as jnp
from jax import lax
from jax.experimental import pallas as pl
from jax.experimental.pallas import tpu as pltpu

F32 = jnp.float32
BF16 = jnp.bfloat16

D_MODEL = 1024
DEPTH = 1
D_S5 = 512
D_CONV = 512
S5_GROUP = 16
S5_GROUPS = 32
S5_STATE = 64
CONV_W = 3
N_EXPERTS = 32
TOP_K = 4
D_FF = 1024
SWIGLU_LIMIT = 7.0
SWIGLU_ALPHA = 1.702
LN_EPS = 1e-5
DEEPNORM_ALPHA = (2.0 * DEPTH) ** 0.25

TL = 256
SUBLANES = 8
LANES = 128
N_SLAB = 4
SLAB_W = 1024
HALF = 512
E_BLK = 512
ROW_TILE = D_MODEL // LANES
PACK_ROWS = HALF // LANES
HI16_MASK = -65536
DMA_THREADS = 2
NEG_BIG = -1e30
VMEM_LIMIT = 56 * 1024 * 1024


def _dot(a, b):
    return jnp.dot(a, b, preferred_element_type=F32)


def _layer_norm(x, g, b):
    mu = jnp.mean(x, axis=-1, keepdims=True)
    xc = x - mu
    var = jnp.mean(xc * xc, axis=-1, keepdims=True)
    return xc * lax.rsqrt(var + LN_EPS) * g + b


def _ada_kernel(c_ref, w_ref, b_ref, o_ref):
    c = c_ref[...]
    s = c * jax.nn.sigmoid(c)
    o_ref[...] = _dot(s.astype(BF16), w_ref[...].astype(BF16)) + b_ref[...]


def _ada_call(c_all, w_ada, b_ada):
    rows = c_all.shape[0]
    n_out = w_ada.shape[1]
    tn = 768
    return pl.pallas_call(
        _ada_kernel,
        grid=(n_out // tn,),
        in_specs=[pl.BlockSpec((rows, D_MODEL), lambda i: (0, 0)),
                  pl.BlockSpec((D_MODEL, tn), lambda i: (0, i)),
                  pl.BlockSpec((1, tn), lambda i: (0, i))],
        out_specs=pl.BlockSpec((rows, tn), lambda i: (0, i)),
        out_shape=jax.ShapeDtypeStruct((rows, n_out), F32),
        compiler_params=pltpu.CompilerParams(dimension_semantics=("arbitrary",),
                                             vmem_limit_bytes=VMEM_LIMIT),
        name="ada_mod",
    )(c_all, w_ada, b_ada)


def _mixer_kernel(nseg, seg, carry_tiles, n_alias, cap_rows, tile0,
                  x_ref, xn_ref, mod_ref, modn_ref, h0_ref, cbuf_ref, win_ref, bmat_ref, cmat_ref, tab_ref, d_ref,
                  wglu_ref, bglu_ref, convw_ref, wout_ref, g1_ref, b1_ref, wr_ref, br_ref, cnt0_ref, *rest):
    rest = rest[n_alias:]
    (x1_ref, gate_ref, xs_hbm, cnt_out_ref, sout_ref, cout_ref,
     p_ref, bu_ref, hs_ref, xcs_ref, carry_ref, h2s_ref, posv_ref, poss_ref, cnt_ref, sem_s, sem_v) = rest
    g_id = pl.program_id(0)
    j = pl.program_id(1)
    step = g_id * pl.num_programs(1) + j
    is_last = step == pl.num_programs(0) * pl.num_programs(1) - 1
    slot = step % 2

    def rows_of(m_ref, k):
        if nseg == 1:
            return m_ref[0, k:k + 1, :]
        return jnp.concatenate(
            [jnp.broadcast_to(m_ref[s, k:k + 1, :], (seg, D_MODEL)) for s in range(nseg)], axis=0)

    def mod_rows(k):
        return rows_of(mod_ref, k)

    def front_pieces(xt_ref, m_ref, dst):
        def adaln():
            h = xt_ref[0] * (1.0 + rows_of(m_ref, 1)) + rows_of(m_ref, 0)
            hs_ref[...] = h.astype(BF16)

        def in_proj(c):
            cols = slice(c * HALF, (c + 1) * HALF)
            p_ref[dst, :, cols] = _dot(hs_ref[...], win_ref[:, cols])

        def s5_in(i):
            u_i = p_ref[dst, :, i * LANES:(i + 1) * LANES].astype(BF16)
            bu_ref[dst * N_SLAB + i] = _dot(u_i, bmat_ref[i])

        return ([adaln] + [functools.partial(in_proj, c) for c in range(2 * D_MODEL // HALF)]
                + [functools.partial(s5_in, i) for i in range(N_SLAB)])

    @pl.when(step == 0)
    def _():
        for piece in front_pieces(x_ref, mod_ref, 0):
            piece()

    nxt_front = front_pieces(xn_ref, modn_ref, 1 - slot)

    def run_front(n):
        for _ in range(n):
            nxt_front.pop(0)()

    x = x_ref[0]
    bu0 = slot * N_SLAB

    prev = 1 - slot
    smem_copy = pltpu.make_async_copy(posv_ref, poss_ref, sem_v.at[0])

    @pl.when(step == 0)
    def _():
        k_i = lax.broadcasted_iota(jnp.int32, (SUBLANES, TL), 0)
        t_i = lax.broadcasted_iota(jnp.int32, (SUBLANES, TL), 1)
        posv_ref[...] = (N_EXPERTS * cap_rows + k_i * TL + t_i) * ROW_TILE
        smem_copy.start()
        h2s_ref[...] = jnp.zeros(h2s_ref.shape, F32)

    smem_copy.wait()

    def issue_rows(src_slot, t0, n_rows):
        for q in range(n_rows):
            t = t0 + q
            src = pl.multiple_of(t * ROW_TILE, ROW_TILE)
            for k in range(TOP_K):
                dst = pl.multiple_of(poss_ref[k, t], ROW_TILE)
                pltpu.make_async_copy(h2s_ref.at[src_slot, pl.ds(src, ROW_TILE), :],
                                      xs_hbm.at[pl.ds(dst, ROW_TILE), :],
                                      sem_s.at[src_slot]).start(priority=k % DMA_THREADS)

    def wait_scatter(sl):
        for _ in range(TOP_K):
            pltpu.make_async_copy(h2s_ref.at[sl], xs_hbm.at[pl.ds(0, TL * ROW_TILE), :], sem_s.at[sl]).wait()

    if carry_tiles:
        @pl.when(j == 0)
        def _():
            carry_ref[...] = h0_ref[0]

    for s in range(nseg):
        if carry_tiles:
            init = tuple(carry_ref[i:i + 1, :] for i in range(N_SLAB))
        else:
            init = tuple(h0_ref[s, i:i + 1, :] for i in range(N_SLAB))

        def scan_body(r, carry, s=s):
            issue_rows(prev, s * seg + r * (2 * SUBLANES), 2 * SUBLANES)
            new = []
            for i in range(N_SLAB):
                cr = carry[i][:, :HALF]
                ci = carry[i][:, HALF:]
                for half in range(2):
                    row0 = pl.multiple_of(s * seg + r * (2 * SUBLANES) + half * SUBLANES, SUBLANES)
                    blk = bu_ref[bu0 + i, pl.ds(row0, SUBLANES), :]
                    xr = blk[:, :HALF]
                    xi = blk[:, HALF:]
                    for k, d in enumerate((1, 2, 4)):
                        lr = tab_ref[i, 2 * k]
                        li = tab_ref[i, 2 * k + 1]
                        rr = pltpu.roll(xr, d, 0)
                        ri = pltpu.roll(xi, d, 0)
                        xr, xi = xr + (lr * rr - li * ri), xi + (lr * ri + li * rr)
                    pr = tab_ref[i, 6]
                    pi_ = tab_ref[i, 7]
                    crb = jnp.broadcast_to(cr, (SUBLANES, HALF))
                    cib = jnp.broadcast_to(ci, (SUBLANES, HALF))
                    xr, xi = xr + (pr * crb - pi_ * cib), xi + (pr * cib + pi_ * crb)
                    bu_ref[bu0 + i, pl.ds(row0, SUBLANES), :] = jnp.concatenate([xr, xi], axis=1)
                    cr = xr[SUBLANES - 1:SUBLANES, :]
                    ci = xi[SUBLANES - 1:SUBLANES, :]
                new.append(jnp.concatenate([cr, ci], axis=1))
            return tuple(new)

        final = lax.fori_loop(0, seg // (2 * SUBLANES), scan_body, init)
        for i in range(N_SLAB):
            sout_ref[s, i:i + 1, :] = final[i]
            if carry_tiles:
                carry_ref[i:i + 1, :] = final[i]

    y = jnp.concatenate([_dot(bu_ref[bu0 + i].astype(BF16), cmat_ref[i]) for i in range(N_SLAB)], axis=1)
    y = y + d_ref[...] * p_ref[slot, :, :D_S5]
    z = jax.nn.gelu(y)
    z = z * jax.nn.sigmoid(_dot(z.astype(BF16), wglu_ref[...]) + bglu_ref[...])

    xc = p_ref[slot, :, D_S5 + D_CONV:D_S5 + 2 * D_CONV] * p_ref[slot, :, D_S5 + 2 * D_CONV:]
    w0 = convw_ref[0:1, :]
    w1 = convw_ref[1:2, :]
    w2 = convw_ref[2:3, :]
    conv_parts = []
    for s in range(nseg):
        base = s * (seg + SUBLANES)
        xc_s = xc[s * seg:(s + 1) * seg]
        tail = xc_s[seg - 2:seg]
        xcs_ref[pl.ds(base + SUBLANES, seg), :] = xc_s
        if carry_tiles:
            @pl.when(j == 0)
            def _(s=s, base=base):
                xcs_ref[pl.ds(base + SUBLANES - 2, 2), :] = cbuf_ref[s]
        else:
            xcs_ref[pl.ds(base + SUBLANES - 2, 2), :] = cbuf_ref[s]
        xm1 = xcs_ref[pl.ds(base + SUBLANES - 1, seg), :]
        xm2 = xcs_ref[pl.ds(base + SUBLANES - 2, seg), :]
        conv_parts.append(w0 * xm2 + w1 * xm1 + w2 * xc_s)
        cout_ref[s] = tail
        if carry_tiles:
            xcs_ref[pl.ds(base + SUBLANES - 2, 2), :] = tail
    conv = conv_parts[0] if nseg == 1 else jnp.concatenate(conv_parts, axis=0)
    y_b = p_ref[slot, :, D_S5:D_S5 + D_CONV] * conv

    run_front(2)
    mix = _dot(jnp.concatenate([z, y_b], axis=1).astype(BF16), wout_ref[...])
    x1 = _layer_norm(DEEPNORM_ALPHA * x + (1.0 + mod_rows(2)) * mix, g1_ref[...], b1_ref[...])
    x1_ref[...] = x1
    run_front(1)
    h2 = x1 * (1.0 + mod_rows(4)) + mod_rows(3)
    lo_bits = lax.bitcast_convert_type(h2[:, :HALF].astype(BF16).astype(F32), jnp.int32)
    hi_bits = lax.bitcast_convert_type(h2[:, HALF:].astype(BF16).astype(F32), jnp.int32)
    packed = lax.bitcast_convert_type(hi_bits | lax.shift_right_logical(lo_bits, 16), F32)
    for c in range(PACK_ROWS):
        h2s_ref[slot, pl.ds(c, TL, stride=ROW_TILE), :] = packed[:, c * LANES:(c + 1) * LANES]

    logits = _dot(h2.astype(BF16), wr_ref[...]) + br_ref[...]
    lane = lax.broadcasted_iota(jnp.int32, logits.shape, 1)
    lane_f = lane.astype(F32)
    vals, sels, hots = [], [], []
    cur = logits
    for _ in range(TOP_K):
        m = jnp.max(cur, axis=-1, keepdims=True)
        am = jnp.min(jnp.where(cur == m, lane_f, float(LANES)), axis=-1, keepdims=True)
        hot = lane_f == am
        vals.append(m)
        sels.append(am)
        hots.append(hot)
        cur = jnp.where(hot, -jnp.inf, cur)
    exps = [jnp.exp(v - vals[0]) for v in vals]
    inv = 1.0 / (exps[0] + exps[1] + exps[2] + exps[3])
    gate_out = jnp.zeros(logits.shape, F32)
    for k in range(TOP_K):
        gate_out = jnp.where(lane == k, exps[k] * inv, gate_out)
    gate_ref[...] = gate_out
    run_front(1)

    tok = ((tile0 + step) * TL + lax.broadcasted_iota(jnp.int32, logits.shape, 0)).astype(F32)
    record = jnp.zeros(logits.shape, F32)
    for k in range(TOP_K):
        record = jnp.where(lane == k, sels[k], record)
        record = jnp.where(lane == TOP_K + k, tok + float(k * cap_rows), record)
    h2s_ref[slot, pl.ds(PACK_ROWS, TL, stride=ROW_TILE), :] = record
    run_front(2)

    @pl.when(step == 0)
    def _():
        cnt_ref[...] = cnt0_ref[...]

    chosen = jnp.zeros(logits.shape, F32)
    for k in range(TOP_K):
        chosen = jnp.where(hots[k], 1.0, chosen)
    r_i = lax.broadcasted_iota(jnp.int32, (TL, TL), 0)
    c_i = lax.broadcasted_iota(jnp.int32, (TL, TL), 1)
    before = jnp.where(c_i < r_i, 1.0, 0.0).astype(BF16)
    rank_base = _dot(before, chosen.astype(BF16)) + cnt_ref[...]
    pos_mat = jnp.zeros(logits.shape, F32)
    for k in range(TOP_K):
        rank_k = jnp.sum(jnp.where(hots[k], rank_base, 0.0), axis=-1, keepdims=True)
        pos_mat = jnp.where(lane == k, (sels[k] * float(cap_rows) + rank_k) * float(ROW_TILE), pos_mat)
    cnt_ref[...] = cnt_ref[...] + jnp.sum(chosen, axis=0, keepdims=True)
    cnt_out_ref[...] = cnt_ref[...]
    run_front(len(nxt_front))
    pos_t =pos_mat.T[:SUBLANES, :].astype(jnp.int32)
    posv_ref[...] = pos_t
    smem_copy.start()
    wait_scatter(prev)

    @pl.when(is_last)
    def _():
        smem_copy.wait()

        def tail_body(t8, c):
            issue_rows(slot, t8 * SUBLANES, SUBLANES)
            return c
        lax.fori_loop(0, TL // SUBLANES, tail_body, 0)
        wait_scatter(slot)


def _mixer_call(x3, mod, h0, cbuf, weights, cnt0, nseg, seg, tiles_per_group, tile0, n_tok, aliased):
    groups = x3.shape[0]
    nseq = mod.shape[0]
    carry_tiles = nseg == 1
    n_alias = len(aliased)
    n_tiles = n_tok // TL
    cap_rows = n_tok

    def full(a):
        nd = a.ndim
        return pl.BlockSpec(a.shape, lambda g, j, nd=nd: (0,) * nd)

    def tile_map(g, j):
        return (tile0 + g * tiles_per_group + j, 0)

    def next_g(g, j):
        return jnp.where(j + 1 < tiles_per_group, g, jnp.minimum(g + 1, groups - 1))

    def next_j(g, j):
        return jnp.where(j + 1 < tiles_per_group, j + 1, 0)

    in_specs = [pl.BlockSpec((1, TL, D_MODEL), lambda g, j: (g, j, 0)),
                pl.BlockSpec((1, TL, D_MODEL), lambda g, j: (next_g(g, j), next_j(g, j), 0)),
                pl.BlockSpec((nseg, 6, D_MODEL), lambda g, j: (g, 0, 0)),
                pl.BlockSpec((nseg, 6, D_MODEL), lambda g, j: (next_g(g, j), 0, 0)),
                pl.BlockSpec((nseg, N_SLAB, SLAB_W), lambda g, j: (g, 0, 0)),
                pl.BlockSpec((nseg, CONV_W - 1, D_CONV), lambda g, j: (g, 0, 0))]
    in_specs += [full(w) for w in weights]
    in_specs += [full(cnt0)]
    in_specs += [pl.BlockSpec(memory_space=pl.ANY)] * n_alias
    out_shape = [jax.ShapeDtypeStruct((n_tok, D_MODEL), F32),
                 jax.ShapeDtypeStruct((n_tok, LANES), F32),
                 jax.ShapeDtypeStruct(((N_EXPERTS * cap_rows + TOP_K * TL) * ROW_TILE, LANES), F32),
                 jax.ShapeDtypeStruct((1, LANES), F32),
                 jax.ShapeDtypeStruct((nseq, N_SLAB, SLAB_W), F32),
                 jax.ShapeDtypeStruct((nseq, CONV_W - 1, D_CONV), F32)]
    out_specs = [pl.BlockSpec((TL, D_MODEL), tile_map),
                 pl.BlockSpec((TL, LANES), tile_map),
                 pl.BlockSpec(memory_space=pl.ANY),
                 pl.BlockSpec((1, LANES), lambda g, j: (0, 0)),
                 pl.BlockSpec((nseg, N_SLAB, SLAB_W), lambda g, j: (g, 0, 0)),
                 pl.BlockSpec((nseg, CONV_W - 1, D_CONV), lambda g, j: (g, 0, 0))]
    n_in = 6 + len(weights) + 1
    aliases = {n_in + k: k for k in range(n_alias)}
    scratch = [pltpu.VMEM((2, TL, 2 * D_MODEL), F32),
               pltpu.VMEM((2 * N_SLAB, TL, SLAB_W), F32),
               pltpu.VMEM((TL, D_MODEL), BF16),
               pltpu.VMEM((nseg * (seg + SUBLANES), D_CONV), F32),
               pltpu.VMEM((N_SLAB, SLAB_W), F32),
               pltpu.VMEM((2, TL * ROW_TILE, LANES), F32),
               pltpu.VMEM((SUBLANES, TL), jnp.int32),
               pltpu.SMEM((SUBLANES, TL), jnp.int32),
               pltpu.VMEM((1, LANES), F32),
               pltpu.SemaphoreType.DMA((2,)),
               pltpu.SemaphoreType.DMA((1,))]
    return pl.pallas_call(
        functools.partial(_mixer_kernel, nseg, seg, carry_tiles, n_alias, cap_rows, tile0),
        grid=(groups, tiles_per_group),
        in_specs=in_specs, out_specs=out_specs, out_shape=out_shape,
        scratch_shapes=scratch,
        input_output_aliases=aliases,
        compiler_params=pltpu.CompilerParams(dimension_semantics=("arbitrary", "arbitrary"),
                                             vmem_limit_bytes=VMEM_LIMIT),
        name="mixer_prompt" if carry_tiles else "mixer_sample",
    )(x3, x3, mod, mod, h0, cbuf, *weights, cnt0, *aliased)


W_CAST_ROWS = 128

def _expert_kernel(n_tok, row_ref, be_ref, nv_ref, nu_ref, xs_ref, wgu_ref, bgu_ref, wd_ref, bd_ref,
                   yk_hbm, wgu_bf, wd_bf, out_a, out_b, slotv_ref, slots_ref, sem_o, sem_v):
    i = pl.program_id(0)
    n_used = nu_ref[0]
    outs = (out_a, out_b)
    prev = jnp.maximum(i - 1, 0)
    new_expert = jnp.logical_or(i == 0, be_ref[i] != be_ref[prev])

    @pl.when(jnp.logical_and(new_expert, i < n_used))
    def _():
        def cast_gu(r, c):
            rows = pl.ds(pl.multiple_of(r * W_CAST_ROWS, W_CAST_ROWS), W_CAST_ROWS)
            wgu_bf[rows, :] = wgu_ref[0, rows, :].astype(BF16)
            return c
        lax.fori_loop(0, D_MODEL // W_CAST_ROWS, cast_gu, 0)

        def cast_d(r, c):
            rows = pl.ds(pl.multiple_of(r * W_CAST_ROWS, W_CAST_ROWS), W_CAST_ROWS)
            wd_bf[rows, :] = wd_ref[0, rows, :].astype(BF16)
            return c
        lax.fori_loop(0, D_FF // W_CAST_ROWS, cast_d, 0)

    def issue_rows(src, t0, count):
        for q in range(count):
            t = t0 + q
            dst = pl.multiple_of(slots_ref[src, 0, t], ROW_TILE)
            row0 = pl.multiple_of(t * ROW_TILE, ROW_TILE)
            pltpu.make_async_copy(outs[src].at[pl.ds(row0, ROW_TILE), :],
                                  yk_hbm.at[pl.ds(dst, ROW_TILE), :],
                                  sem_o.at[src]).start(priority=q % DMA_THREADS)

    def wait_rows(sl):
        pltpu.make_async_copy(outs[sl], yk_hbm.at[pl.ds(0, E_BLK * ROW_TILE), :], sem_o.at[sl]).wait()

    lane = lax.broadcasted_iota(jnp.int32, (E_BLK, LANES), 1)
    row = lax.broadcasted_iota(jnp.int32, (E_BLK, 1), 0)
    spare0 = float(TOP_K * n_tok)

    @pl.when(i == 0)
    def _():
        t_i = lax.broadcasted_iota(jnp.int32, (SUBLANES, E_BLK), 1)
        slotv_ref[...] = (TOP_K * n_tok + t_i) * ROW_TILE
        pltpu.make_async_copy(slotv_ref, slots_ref.at[1], sem_v.at[0]).start()
        out_b[...] = jnp.zeros(out_b.shape, F32)

    def block(cur):
        prv = 1 - cur

        def smem_copy(dst):
            return pltpu.make_async_copy(slotv_ref, slots_ref.at[dst], sem_v.at[0])

        @pl.when(i < n_used)
        def _():
            smem_copy(prv).wait()
            record = xs_ref[pl.ds(PACK_ROWS, E_BLK, stride=ROW_TILE), :]
            hit = jnp.logical_and(record.astype(jnp.int32) == be_ref[i], lane < TOP_K)
            slot_f = jnp.sum(jnp.where(hit, pltpu.roll(record, LANES - TOP_K, 1), 0.0), axis=-1, keepdims=True)
            slot_f = jnp.where(row < nv_ref[i], slot_f, spare0 + row.astype(F32))
            slot_mat = jnp.where(lane == 0, slot_f * float(ROW_TILE), 0.0)
            slotv_ref[...] = slot_mat.T[:SUBLANES, :].astype(jnp.int32)
            smem_copy(cur).start()

        @pl.when(i < n_used)
        def _():
            live = row < nv_ref[i]
            words = [lax.bitcast_convert_type(xs_ref[pl.ds(c, E_BLK, stride=ROW_TILE), :], jnp.int32)
                     for c in range(PACK_ROWS)]
            lo = [lax.bitcast_convert_type(lax.shift_left(w, 16), F32) for w in words]
            hi = [lax.bitcast_convert_type(w & HI16_MASK, F32) for w in words]
            xb = jnp.where(live, jnp.concatenate(lo + hi, axis=1), 0.0).astype(BF16)

            issue_rows(prv, 0, E_BLK)
            hgu = _dot(xb, wgu_bf[...]) + bgu_ref[0]
            g = jnp.minimum(hgu[:, :D_FF], SWIGLU_LIMIT)
            up = jnp.clip(hgu[:, D_FF:], -SWIGLU_LIMIT, SWIGLU_LIMIT)
            act = (up + 1.0) * (g * jax.nn.sigmoid(SWIGLU_ALPHA * g))
            y = _dot(act.astype(BF16), wd_bf[...]) + bd_ref[0]

            @pl.when(i > 0)
            def _():
                wait_rows(cur)

            for c in range(ROW_TILE):
                outs[cur][pl.ds(c, E_BLK, stride=ROW_TILE), :] = y[:, c * LANES:(c + 1) * LANES]

        @pl.when(i == n_used)
        def _():
            smem_copy(prv).wait()

            def tail(t8, c):
                issue_rows(prv, t8 * SUBLANES, SUBLANES)
                return c
            lax.fori_loop(0, E_BLK // SUBLANES, tail, 0)
            wait_rows(cur)
            wait_rows(prv)

    for parity in range(2):
        @pl.when(i % 2 == parity)
        def _(parity=parity):
            block(parity)


def _expert_call(blk_row, blk_e, blk_valid, n_used, xs, w_gu, b_gu, w_down, b_down, n_tok):
    nb = blk_row.shape[0]
    grid_spec = pltpu.PrefetchScalarGridSpec(
        num_scalar_prefetch=4,
        grid=(nb,),
        in_specs=[
            pl.BlockSpec((E_BLK * ROW_TILE, LANES), lambda i, br, be, nv, nu: (br[i], 0)),
            pl.BlockSpec((1, D_MODEL, 2 * D_FF), lambda i, br, be, nv, nu: (be[i], 0, 0)),
            pl.BlockSpec((1, 1, 2 * D_FF), lambda i, br, be, nv, nu: (be[i], 0, 0)),
            pl.BlockSpec((1, D_FF, D_MODEL), lambda i, br, be, nv, nu: (be[i], 0, 0)),
            pl.BlockSpec((1, 1, D_MODEL), lambda i, br, be, nv, nu: (be[i], 0, 0)),
        ],
        out_specs=pl.BlockSpec(memory_space=pl.ANY),
        scratch_shapes=[pltpu.VMEM((D_MODEL, 2 * D_FF), BF16),
                        pltpu.VMEM((D_FF, D_MODEL), BF16),
                        pltpu.VMEM((E_BLK * ROW_TILE, LANES), F32),
                        pltpu.VMEM((E_BLK * ROW_TILE, LANES), F32),
                        pltpu.VMEM((SUBLANES, E_BLK), jnp.int32),
                        pltpu.SMEM((2, SUBLANES, E_BLK), jnp.int32),
                        pltpu.SemaphoreType.DMA((2,)),
                        pltpu.SemaphoreType.DMA((1,))],
    )
    return pl.pallas_call(
        functools.partial(_expert_kernel, n_tok),
        grid_spec=grid_spec,
        out_shape=jax.ShapeDtypeStruct(((TOP_K * n_tok + E_BLK) * ROW_TILE, LANES), F32),
        compiler_params=pltpu.CompilerParams(dimension_semantics=("arbitrary",),
                                             vmem_limit_bytes=VMEM_LIMIT),
        name="experts",
    )(blk_row, blk_e, blk_valid, n_used, xs, w_gu, b_gu, w_down, b_down)


def _combine_kernel(nseg, seg, y0_ref, y1_ref, y2_ref, y3_ref, x1_ref, gate_ref, mod_ref, g2_ref, b2_ref,
                    out_ref):
    gates = gate_ref[...]
    ffn = None
    for k, y_ref in enumerate((y0_ref, y1_ref, y2_ref, y3_ref)):
        rows = jnp.concatenate(
            [y_ref[pl.ds(c, TL, stride=ROW_TILE), :] for c in range(ROW_TILE)], axis=1)
        term = gates[:, k:k + 1] * rows
        ffn = term if ffn is None else ffn + term
    if nseg == 1:
        gate2 = mod_ref[0, 5:6, :]
    else:
        gate2 = jnp.concatenate(
            [jnp.broadcast_to(mod_ref[s, 5:6, :], (seg, D_MODEL)) for s in range(nseg)], axis=0)
    xa = DEEPNORM_ALPHA * x1_ref[...] + (1.0 + gate2) * ffn
    out_ref[...] = _layer_norm(xa, g2_ref[...], b2_ref[...])


def _combine_call(yk, x1, gates, mod, ln2_g, ln2_b, nseg, seg, n_tiles, tile0, tiles_per_group, n_tok):
    tiles_total = n_tok // TL

    def yk_spec(k):
        return pl.BlockSpec((TL * ROW_TILE, LANES), lambda i, k=k: (k * tiles_total + tile0 + i, 0))

    in_specs = [yk_spec(k) for k in range(TOP_K)] + [
        pl.BlockSpec((TL, D_MODEL), lambda i: (tile0 + i, 0)),
        pl.BlockSpec((TL, LANES), lambda i: (tile0 + i, 0)),
        pl.BlockSpec((nseg, 6, D_MODEL), lambda i: (i // tiles_per_group, 0, 0)),
        pl.BlockSpec((1, D_MODEL), lambda i: (0, 0)),
        pl.BlockSpec((1, D_MODEL), lambda i: (0, 0)),
    ]
    return pl.pallas_call(
        functools.partial(_combine_kernel, nseg, seg),
        grid=(n_tiles,),
        in_specs=in_specs,
        out_specs=pl.BlockSpec((TL, D_MODEL), lambda i: (i, 0)),
        out_shape=jax.ShapeDtypeStruct((n_tiles * TL, D_MODEL), F32),
        compiler_params=pltpu.CompilerParams(dimension_semantics=("arbitrary",),
                                             vmem_limit_bytes=VMEM_LIMIT),
        name="combine_prompt" if nseg == 1 else "combine_sample",
    )(yk, yk, yk, yk, x1, gates, mod, ln2_g, ln2_b)


def _s5_tables(lam_re, lam_im, log_dt, b_re, b_im, c_re, c_im):
    dt = jnp.exp(log_dt.astype(F32))[:, None]
    lam = lax.complex(lam_re.astype(F32), lam_im.astype(F32))
    lam_dt = lam * dt
    lam_bar = jnp.exp(lam_dt)
    b_bar = ((lam_bar - 1.0) / lam)[..., None] * lax.complex(b_re.astype(F32), b_im.astype(F32))
    gl = S5_GROUPS // N_SLAB
    eye = jnp.eye(gl, dtype=F32)

    def b_slab(part):
        a = part.reshape(N_SLAB, gl, S5_STATE, S5_GROUP)
        return jnp.einsum('sgph,gk->sghkp', a, eye).reshape(N_SLAB, gl * S5_GROUP, gl * S5_STATE)

    bmat = jnp.concatenate([b_slab(b_bar.real), b_slab(b_bar.imag)], axis=-1).astype(BF16)

    def c_slab(part):
        a = part.reshape(N_SLAB, gl, S5_GROUP, S5_STATE)
        return jnp.einsum('sghp,gk->sgpkh', a, eye).reshape(N_SLAB, gl * S5_STATE, gl * S5_GROUP)

    cmat = jnp.concatenate([c_slab(c_re.astype(F32)), -c_slab(c_im.astype(F32))], axis=1).astype(BF16)

    row = jnp.arange(SUBLANES, dtype=F32)[:, None, None]

    def power(k):
        return jnp.exp(lam_dt[None] * k)

    tabs = []
    for d in (1, 2, 4):
        pw = power(jnp.full_like(row, float(d)))
        mask = (row >= d).astype(F32)
        tabs += [pw.real * mask, pw.imag * mask]
    pw = power(row + 1.0)
    tabs += [pw.real, pw.imag]
    tab = jnp.stack(tabs, axis=0)
    tab = tab.reshape(8, SUBLANES, N_SLAB, gl * S5_STATE).transpose(2, 0, 1, 3)
    return bmat, cmat, tab


def _state_to_slab(re, im):
    n = re.shape[0]
    return jnp.concatenate([re.reshape(n, N_SLAB, HALF), im.reshape(n, N_SLAB, HALF)], axis=-1).astype(F32)


def _slab_to_state(s):
    n = s.shape[0]
    re = s[:, :, :HALF].reshape(n, S5_GROUPS, S5_STATE)
    im = s[:, :, HALF:].reshape(n, S5_GROUPS, S5_STATE)
    return re, im


def _block_table(counts, n_tok):
    n_assign = n_tok * TOP_K
    n_blocks = (n_assign + N_EXPERTS * (E_BLK - 1) + E_BLK - 1) // E_BLK + 1
    cap_blocks = n_tok // E_BLK
    nblk = (counts + E_BLK - 1) // E_BLK
    cum = jnp.cumsum(nblk)
    start = cum - nblk
    n_used = cum[-1]
    i = jnp.arange(n_blocks, dtype=jnp.int32)
    ii = jnp.maximum(jnp.minimum(i, n_used - 1), 0)
    e = jnp.minimum(jnp.sum((cum[None, :] <= ii[:, None]).astype(jnp.int32), axis=1), N_EXPERTS - 1)
    b = ii - start[e]
    blk_row = (e * cap_blocks + b).astype(jnp.int32)
    valid = jnp.where(i < n_used, jnp.minimum(E_BLK, counts[e] - b * E_BLK), 0).astype(jnp.int32)
    return blk_row, e, valid, n_used.astype(jnp.int32).reshape(1)


def kernel(x_prompt, x_sample, c_prompt, c_sample, state_s5_re, state_s5_im, state_conv, w_ada, b_ada, w_in,
           s5_lam_re, s5_lam_im, s5_log_dt, s5_b_re, s5_b_im, s5_c_re, s5_c_im, s5_d, w_glu, b_glu, conv_w,
           w_out, ln1_g, ln1_b, w_router, b_router, w_gu, b_gu, w_down, b_down, ln2_g, ln2_b):
    assert DEPTH == 1 and w_ada.shape[0] == 1
    n_p, l_p, _ = x_prompt.shape
    n_s, l_s, _ = x_sample.shape
    t_p, t_s = n_p * l_p, n_s * l_s
    n_tok = t_p + t_s
    tiles_p, tiles_s = t_p // TL, t_s // TL
    seq_per_tile = TL // l_s
    assert l_p % TL == 0 and TL % l_s == 0 and n_s % seq_per_tile == 0

    c_all = jnp.concatenate([c_prompt, c_sample], axis=0).astype(F32)
    pad = (-c_all.shape[0]) % SUBLANES
    c_all = jnp.pad(c_all, ((0, pad), (0, 0)))
    mod = _ada_call(c_all, w_ada[0], b_ada[0].reshape(1, -1)).reshape(-1, 6, D_MODEL)
    mod_p, mod_s = mod[:n_p], mod[n_p:n_p + n_s]

    bmat, cmat, tab = _s5_tables(s5_lam_re[0], s5_lam_im[0], s5_log_dt[0], s5_b_re[0], s5_b_im[0],
                                 s5_c_re[0], s5_c_im[0])
    wr = jnp.pad(w_router[0], ((0, 0), (0, LANES - N_EXPERTS))).astype(BF16)
    br = jnp.pad(b_router[0].astype(F32), (0, LANES - N_EXPERTS), constant_values=NEG_BIG).reshape(1, LANES)
    weights = (w_in[0].astype(BF16), bmat, cmat, tab, s5_d[0].reshape(1, D_S5).astype(F32),
               w_glu[0].astype(BF16), b_glu[0].reshape(1, D_S5).astype(F32), conv_w[0].astype(F32),
               w_out[0].astype(BF16), ln1_g[0].reshape(1, D_MODEL).astype(F32),
               ln1_b[0].reshape(1, D_MODEL).astype(F32), wr, br)

    h0_p = jnp.zeros((n_p, N_SLAB, SLAB_W), F32)
    cb_p = jnp.zeros((n_p, CONV_W - 1, D_CONV), F32)
    h0_s = _state_to_slab(state_s5_re[0], state_s5_im[0])
    cb_s = state_conv[0].astype(F32)

    assert n_tok % E_BLK == 0 and E_BLK % TL == 0
    cnt0 = jnp.zeros((1, LANES), F32)
    outs_p = _mixer_call(x_prompt, mod_p, h0_p, cb_p, weights, cnt0, 1, TL, l_p // TL, 0, n_tok, ())
    x1, gates, xs, cnt_p, s_p, conv_p = outs_p
    xs3 = x_sample.reshape(n_s // seq_per_tile, TL, D_MODEL)
    outs_s = _mixer_call(xs3, mod_s, h0_s, cb_s, weights, cnt_p, seq_per_tile, l_s, 1, tiles_p, n_tok,
                         (x1, gates, xs))
    x1, gates, xs, cnt, s_s, conv_s = outs_s

    counts = cnt[0, :N_EXPERTS].astype(jnp.int32)
    blk_row, blk_e, blk_valid, n_used = _block_table(counts, n_tok)
    yk = _expert_call(blk_row, blk_e, blk_valid, n_used, xs, w_gu[0].astype(F32),
                      b_gu[0].reshape(N_EXPERTS, 1, 2 * D_FF).astype(F32), w_down[0].astype(F32),
                      b_down[0].reshape(N_EXPERTS, 1, D_MODEL).astype(F32), n_tok)
    g2 = ln2_g[0].reshape(1, D_MODEL).astype(F32)
    b2 = ln2_b[0].reshape(1, D_MODEL).astype(F32)
    y_p = _combine_call(yk, x1, gates, mod_p, g2, b2, 1, TL, tiles_p, 0, l_p // TL, n_tok)
    y_s = _combine_call(yk, x1, gates, mod_s, g2, b2, seq_per_tile, l_s, tiles_s, tiles_p, 1, n_tok)

    p_re, p_im = _slab_to_state(s_p)
    s_re, s_im = _slab_to_state(s_s)
    return (y_p.reshape(n_p, l_p, D_MODEL), y_s.reshape(n_s, l_s, D_MODEL),
            p_re[None], p_im[None], conv_p[None], s_re[None], s_im[None], conv_s[None])
```

```python
import functools
import math

import jax
import jax.numpy as jnp
from jax import lax
from jax.experimental import pallas as pl
from jax.experimental.pallas import tpu as pltpu

F32 = jnp.float32
BF16 = jnp.bfloat16

D_MODEL = 1024
DEPTH = 1
D_S5 = 512
D_CONV = 512
S5_GROUP = 16
S5_GROUPS = 32
S5_STATE = 64
CONV_W = 3
N_EXPERTS = 32
TOP_K = 4
D_FF = 1024
SWIGLU_LIMIT = 7.0
SWIGLU_ALPHA = 1.702
LN_EPS = 1e-5
DEEPNORM_ALPHA = (2.0 * DEPTH) ** 0.25

TL = 256
SUBLANES = 8
LANES = 128
N_SLAB = 4
SLAB_W = 1024
HALF = 512
E_BLK = 512
ROW_TILE = D_MODEL // LANES
PACK_ROWS = HALF // LANES
HI16_MASK = -65536
DMA_THREADS = 2
NEG_BIG = -1e30
VMEM_LIMIT = 56 * 1024 * 1024


def _dot(a, b):
    return jnp.dot(a, b, preferred_element_type=F32)


def _layer_norm(x, g, b):
    mu = jnp.mean(x, axis=-1, keepdims=True)
    xc = x - mu
    var = jnp.mean(xc * xc, axis=-1, keepdims=True)
    return xc * lax.rsqrt(var + LN_EPS) * g + b


def _ada_kernel(c_ref, w_ref, b_ref, o_ref):
    c = c_ref[...]
    s = c * jax.nn.sigmoid(c)
    o_ref[...] = _dot(s.astype(BF16), w_ref[...].astype(BF16)) + b_ref[...]


def _ada_call(c_all, w_ada, b_ada):
    rows = c_all.shape[0]
    n_out = w_ada.shape[1]
    tn = 768
    return pl.pallas_call(
        _ada_kernel,
        grid=(n_out // tn,),
        in_specs=[pl.BlockSpec((rows, D_MODEL), lambda i: (0, 0)),
                  pl.BlockSpec((D_MODEL, tn), lambda i: (0, i)),
                  pl.BlockSpec((1, tn), lambda i: (0, i))],
        out_specs=pl.BlockSpec((rows, tn), lambda i: (0, i)),
        out_shape=jax.ShapeDtypeStruct((rows, n_out), F32),
        compiler_params=pltpu.CompilerParams(dimension_semantics=("arbitrary",),
                                             vmem_limit_bytes=VMEM_LIMIT),
        name="ada_mod",
    )(c_all, w_ada, b_ada)


def _mixer_kernel(nseg, seg, carry_tiles, n_alias, cap_rows, tile0,
                  x_ref, xn_ref, mod_ref, modn_ref, h0_ref, cbuf_ref, win_ref, bmat_ref, cmat_ref, tab_ref, d_ref,
                  wglu_ref, bglu_ref, convw_ref, wout_ref, g1_ref, b1_ref, wr_ref, br_ref, cnt0_ref, *rest):
    rest = rest[n_alias:]
    (x1_ref, gate_ref, xs_hbm, cnt_out_ref, sout_ref, cout_ref,
     p_ref, bu_ref, hs_ref, xcs_ref, carry_ref, h2s_ref, posv_ref, poss_ref, cnt_ref, sem_s, sem_v) = rest
    g_id = pl.program_id(0)
    j = pl.program_id(1)
    step = g_id * pl.num_programs(1) + j
    is_last = step == pl.num_programs(0) * pl.num_programs(1) - 1
    slot = step % 2

    def rows_of(m_ref, k):
        if nseg == 1:
            return m_ref[0, k:k + 1, :]
        return jnp.concatenate(
            [jnp.broadcast_to(m_ref[s, k:k + 1, :], (seg, D_MODEL)) for s in range(nseg)], axis=0)

    def mod_rows(k):
        return rows_of(mod_ref, k)

    def front_pieces(xt_ref, m_ref, dst):
        def adaln():
            h = xt_ref[0] * (1.0 + rows_of(m_ref, 1)) + rows_of(m_ref, 0)
            hs_ref[...] = h.astype(BF16)

        def in_proj(c):
            cols = slice(c * HALF, (c + 1) * HALF)
            p_ref[dst, :, cols] = _dot(hs_ref[...], win_ref[:, cols])

        def s5_in(i):
            u_i = p_ref[dst, :, i * LANES:(i + 1) * LANES]
            row_in_tile = lax.broadcasted_iota(jnp.int32, u_i.shape, 0) % SUBLANES
            u_prev = jnp.where(row_in_tile == 0, 0.0, pltpu.roll(u_i, 1, 0))
            lhs = jnp.concatenate([u_i, u_prev], axis=1).astype(BF16)
            bu_ref[dst * N_SLAB + i] = _dot(lhs, bmat_ref[i])

        return ([adaln] + [functools.partial(in_proj, c) for c in range(2 * D_MODEL // HALF)]
                + [functools.partial(s5_in, i) for i in range(N_SLAB)])

    @pl.when(step == 0)
    def _():
        for piece in front_pieces(x_ref, mod_ref, 0):
            piece()

    nxt_front = front_pieces(xn_ref, modn_ref, 1 - slot)

    def run_front(n):
        for _ in range(n):
            nxt_front.pop(0)()

    x = x_ref[0]
    bu0 = slot * N_SLAB

    prev = 1 - slot
    smem_copy = pltpu.make_async_copy(posv_ref, poss_ref, sem_v.at[0])

    @pl.when(step == 0)
    def _():
        k_i = lax.broadcasted_iota(jnp.int32, (SUBLANES, TL), 0)
        t_i = lax.broadcasted_iota(jnp.int32, (SUBLANES, TL), 1)
        posv_ref[...] = (N_EXPERTS * cap_rows + k_i * TL + t_i) * ROW_TILE
        smem_copy.start()
        h2s_ref[...] = jnp.zeros(h2s_ref.shape, F32)

    smem_copy.wait()

    def issue_rows(src_slot, t0, n_rows):
        for q in range(n_rows):
            t = t0 + q
            src = pl.multiple_of(t * ROW_TILE, ROW_TILE)
            for k in range(TOP_K):
                dst = pl.multiple_of(poss_ref[k, t], ROW_TILE)
                pltpu.make_async_copy(h2s_ref.at[src_slot, pl.ds(src, ROW_TILE), :],
                                      xs_hbm.at[pl.ds(dst, ROW_TILE), :],
                                      sem_s.at[src_slot]).start(priority=k % DMA_THREADS)

    def wait_scatter(sl):
        for _ in range(TOP_K):
            pltpu.make_async_copy(h2s_ref.at[sl], xs_hbm.at[pl.ds(0, TL * ROW_TILE), :], sem_s.at[sl]).wait()

    if carry_tiles:
        @pl.when(j == 0)
        def _():
            carry_ref[...] = h0_ref[0]

    for s in range(nseg):
        if carry_tiles:
            init = tuple(carry_ref[i:i + 1, :] for i in range(N_SLAB))
        else:
            init = tuple(h0_ref[s, i:i + 1, :] for i in range(N_SLAB))

        def scan_body(r, carry, s=s):
            issue_rows(prev, s * seg + r * (2 * SUBLANES), 2 * SUBLANES)
            new = []
            for i in range(N_SLAB):
                cr = carry[i][:, :HALF]
                ci = carry[i][:, HALF:]
                for half in range(2):
                    row0 = pl.multiple_of(s * seg + r * (2 * SUBLANES) + half * SUBLANES, SUBLANES)
                    blk = bu_ref[bu0 + i, pl.ds(row0, SUBLANES), :]
                    xr = blk[:, :HALF]
                    xi = blk[:, HALF:]
                    for k, d in enumerate((2, 4)):
                        lr = tab_ref[i, 2 * k]
                        li = tab_ref[i, 2 * k + 1]
                        rr = pltpu.roll(xr, d, 0)
                        ri = pltpu.roll(xi, d, 0)
                        xr, xi = xr + (lr * rr - li * ri), xi + (lr * ri + li * rr)
                    pr = tab_ref[i, 4]
                    pi_ = tab_ref[i, 5]
                    crb = jnp.broadcast_to(cr, (SUBLANES, HALF))
                    cib = jnp.broadcast_to(ci, (SUBLANES, HALF))
                    xr, xi = xr + (pr * crb - pi_ * cib), xi + (pr * cib + pi_ * crb)
                    bu_ref[bu0 + i, pl.ds(row0, SUBLANES), :] = jnp.concatenate([xr, xi], axis=1)
                    cr = xr[SUBLANES - 1:SUBLANES, :]
                    ci = xi[SUBLANES - 1:SUBLANES, :]
                new.append(jnp.concatenate([cr, ci], axis=1))
            return tuple(new)

        final = lax.fori_loop(0, seg // (2 * SUBLANES), scan_body, init)
        for i in range(N_SLAB):
            sout_ref[s, i:i + 1, :] = final[i]
            if carry_tiles:
                carry_ref[i:i + 1, :] = final[i]

    y = jnp.concatenate([_dot(bu_ref[bu0 + i].astype(BF16), cmat_ref[i]) for i in range(N_SLAB)], axis=1)
    y = y + d_ref[...] * p_ref[slot, :, :D_S5]
    z = jax.nn.gelu(y)
    z = z * jax.nn.sigmoid(_dot(z.astype(BF16), wglu_ref[...]) + bglu_ref[...])

    xc = p_ref[slot, :, D_S5 + D_CONV:D_S5 + 2 * D_CONV] * p_ref[slot, :, D_S5 + 2 * D_CONV:]
    w0 = convw_ref[0:1, :]
    w1 = convw_ref[1:2, :]
    w2 = convw_ref[2:3, :]
    conv_parts = []
    for s in range(nseg):
        base = s * (seg + SUBLANES)
        xc_s = xc[s * seg:(s + 1) * seg]
        tail = xc_s[seg - 2:seg]
        xcs_ref[pl.ds(base + SUBLANES, seg), :] = xc_s
        if carry_tiles:
            @pl.when(j == 0)
            def _(s=s, base=base):
                xcs_ref[pl.ds(base + SUBLANES - 2, 2), :] = cbuf_ref[s]
        else:
            xcs_ref[pl.ds(base + SUBLANES - 2, 2), :] = cbuf_ref[s]
        xm1 = xcs_ref[pl.ds(base + SUBLANES - 1, seg), :]
        xm2 = xcs_ref[pl.ds(base + SUBLANES - 2, seg), :]
        conv_parts.append(w0 * xm2 + w1 * xm1 + w2 * xc_s)
        cout_ref[s] = tail
        if carry_tiles:
            xcs_ref[pl.ds(base + SUBLANES - 2, 2), :] = tail
    conv = conv_parts[0] if nseg == 1 else jnp.concatenate(conv_parts, axis=0)
    y_b = p_ref[slot, :, D_S5:D_S5 + D_CONV] * conv

    run_front(2)
    mix = _dot(jnp.concatenate([z, y_b], axis=1).astype(BF16), wout_ref[...])
    x1 = _layer_norm(DEEPNORM_ALPHA * x + (1.0 + mod_rows(2)) * mix, g1_ref[...], b1_ref[...])
    x1_ref[...] = x1
    run_front(1)
    h2 = x1 * (1.0 + mod_rows(4)) + mod_rows(3)
    lo_bits = lax.bitcast_convert_type(h2[:, :HALF].astype(BF16).astype(F32), jnp.int32)
    hi_bits = lax.bitcast_convert_type(h2[:, HALF:].astype(BF16).astype(F32), jnp.int32)
    packed = lax.bitcast_convert_type(hi_bits | lax.shift_right_logical(lo_bits, 16), F32)
    for c in range(PACK_ROWS):
        h2s_ref[slot, pl.ds(c, TL, stride=ROW_TILE), :] = packed[:, c * LANES:(c + 1) * LANES]

    logits = _dot(h2.astype(BF16), wr_ref[...]) + br_ref[...]
    lane = lax.broadcasted_iota(jnp.int32, logits.shape, 1)
    lane_f = lane.astype(F32)
    vals, sels, hots = [], [], []
    cur = logits
    for _ in range(TOP_K):
        m = jnp.max(cur, axis=-1, keepdims=True)
        am = jnp.min(jnp.where(cur == m, lane_f, float(LANES)), axis=-1, keepdims=True)
        hot = lane_f == am
        vals.append(m)
        sels.append(am)
        hots.append(hot)
        cur = jnp.where(hot, -jnp.inf, cur)
    exps = [jnp.exp(v - vals[0]) for v in vals]
    inv = 1.0 / (exps[0] + exps[1] + exps[2] + exps[3])
    gate_out = jnp.zeros(logits.shape, F32)
    for k in range(TOP_K):
        gate_out = jnp.where(lane == k, exps[k] * inv, gate_out)
    gate_ref[...] = gate_out
    run_front(1)

    tok = ((tile0 + step) * TL + lax.broadcasted_iota(jnp.int32, logits.shape, 0)).astype(F32)
    record = jnp.zeros(logits.shape, F32)
    for k in range(TOP_K):
        record = jnp.where(lane == k, sels[k], record)
        record = jnp.where(lane == TOP_K + k, tok + float(k * cap_rows), record)
    h2s_ref[slot, pl.ds(PACK_ROWS, TL, stride=ROW_TILE), :] = record
    run_front(2)

    @pl.when(step == 0)
    def _():
        cnt_ref[...] = cnt0_ref[...]

    chosen = jnp.zeros(logits.shape, F32)
    for k in range(TOP_K):
        chosen = jnp.where(hots[k], 1.0, chosen)
    r_i = lax.broadcasted_iota(jnp.int32, (TL, TL), 0)
    c_i = lax.broadcasted_iota(jnp.int32, (TL, TL), 1)
    before = jnp.where(c_i < r_i, 1.0, 0.0).astype(BF16)
    rank_base = _dot(before, chosen.astype(BF16)) + cnt_ref[...]
    pos_mat = jnp.zeros(logits.shape, F32)
    for k in range(TOP_K):
        rank_k = jnp.sum(jnp.where(hots[k], rank_base, 0.0), axis=-1, keepdims=True)
        pos_mat = jnp.where(lane == k, (sels[k] * float(cap_rows) + rank_k) * float(ROW_TILE), pos_mat)
    cnt_ref[...] = cnt_ref[...] + jnp.sum(chosen, axis=0, keepdims=True)
    cnt_out_ref[...] = cnt_ref[...]
    run_front(len(nxt_front))
    pos_t =pos_mat.T[:SUBLANES, :].astype(jnp.int32)
    posv_ref[...] = pos_t
    smem_copy.start()
    wait_scatter(prev)

    @pl.when(is_last)
    def _():
        smem_copy.wait()

        def tail_body(t8, c):
            issue_rows(slot, t8 * SUBLANES, SUBLANES)
            return c
        lax.fori_loop(0, TL // SUBLANES, tail_body, 0)
        wait_scatter(slot)


def _mixer_call(x3, mod, h0, cbuf, weights, cnt0, nseg, seg, tiles_per_group, tile0, n_tok, aliased):
    groups = x3.shape[0]
    nseq = mod.shape[0]
    carry_tiles = nseg == 1
    n_alias = len(aliased)
    n_tiles = n_tok // TL
    cap_rows = n_tok

    def full(a):
        nd = a.ndim
        return pl.BlockSpec(a.shape, lambda g, j, nd=nd: (0,) * nd)

    def tile_map(g, j):
        return (tile0 + g * tiles_per_group + j, 0)

    def next_g(g, j):
        return jnp.where(j + 1 < tiles_per_group, g, jnp.minimum(g + 1, groups - 1))

    def next_j(g, j):
        return jnp.where(j + 1 < tiles_per_group, j + 1, 0)

    in_specs = [pl.BlockSpec((1, TL, D_MODEL), lambda g, j: (g, j, 0)),
                pl.BlockSpec((1, TL, D_MODEL), lambda g, j: (next_g(g, j), next_j(g, j), 0)),
                pl.BlockSpec((nseg, 6, D_MODEL), lambda g, j: (g, 0, 0)),
                pl.BlockSpec((nseg, 6, D_MODEL), lambda g, j: (next_g(g, j), 0, 0)),
                pl.BlockSpec((nseg, N_SLAB, SLAB_W), lambda g, j: (g, 0, 0)),
                pl.BlockSpec((nseg, CONV_W - 1, D_CONV), lambda g, j: (g, 0, 0))]
    in_specs += [full(w) for w in weights]
    in_specs += [full(cnt0)]
    in_specs += [pl.BlockSpec(memory_space=pl.ANY)] * n_alias
    out_shape = [jax.ShapeDtypeStruct((n_tok, D_MODEL), F32),
                 jax.ShapeDtypeStruct((n_tok, LANES), F32),
                 jax.ShapeDtypeStruct(((N_EXPERTS * cap_rows + TOP_K * TL) * ROW_TILE, LANES), F32),
                 jax.ShapeDtypeStruct((1, LANES), F32),
                 jax.ShapeDtypeStruct((nseq, N_SLAB, SLAB_W), F32),
                 jax.ShapeDtypeStruct((nseq, CONV_W - 1, D_CONV), F32)]
    out_specs = [pl.BlockSpec((TL, D_MODEL), tile_map),
                 pl.BlockSpec((TL, LANES), tile_map),
                 pl.BlockSpec(memory_space=pl.ANY),
                 pl.BlockSpec((1, LANES), lambda g, j: (0, 0)),
                 pl.BlockSpec((nseg, N_SLAB, SLAB_W), lambda g, j: (g, 0, 0)),
                 pl.BlockSpec((nseg, CONV_W - 1, D_CONV), lambda g, j: (g, 0, 0))]
    n_in = 6 + len(weights) + 1
    aliases = {n_in + k: k for k in range(n_alias)}
    scratch = [pltpu.VMEM((2, TL, 2 * D_MODEL), F32),
               pltpu.VMEM((2 * N_SLAB, TL, SLAB_W), F32),
               pltpu.VMEM((TL, D_MODEL), BF16),
               pltpu.VMEM((nseg * (seg + SUBLANES), D_CONV), F32),
               pltpu.VMEM((N_SLAB, SLAB_W), F32),
               pltpu.VMEM((2, TL * ROW_TILE, LANES), F32),
               pltpu.VMEM((SUBLANES, TL), jnp.int32),
               pltpu.SMEM((SUBLANES, TL), jnp.int32),
               pltpu.VMEM((1, LANES), F32),
               pltpu.SemaphoreType.DMA((2,)),
               pltpu.SemaphoreType.DMA((1,))]
    return pl.pallas_call(
        functools.partial(_mixer_kernel, nseg, seg, carry_tiles, n_alias, cap_rows, tile0),
        grid=(groups, tiles_per_group),
        in_specs=in_specs, out_specs=out_specs, out_shape=out_shape,
        scratch_shapes=scratch,
        input_output_aliases=aliases,
        compiler_params=pltpu.CompilerParams(dimension_semantics=("arbitrary", "arbitrary"),
                                             vmem_limit_bytes=VMEM_LIMIT),
        name="mixer_prompt" if carry_tiles else "mixer_sample",
    )(x3, x3, mod, mod, h0, cbuf, *weights, cnt0, *aliased)


W_CAST_ROWS = 128

def _expert_kernel(n_tok, row_ref, be_ref, nv_ref, nu_ref, xs_ref, wgu_ref, bgu_ref, wd_ref, bd_ref,
                   yk_hbm, wgu_bf, wd_bf, out_a, out_b, slotv_ref, slots_ref, sem_o, sem_v):
    i = pl.program_id(0)
    n_used = nu_ref[0]
    outs = (out_a, out_b)
    prev = jnp.maximum(i - 1, 0)
    new_expert = jnp.logical_or(i == 0, be_ref[i] != be_ref[prev])

    @pl.when(jnp.logical_and(new_expert, i < n_used))
    def _():
        def cast_gu(r, c):
            rows = pl.ds(pl.multiple_of(r * W_CAST_ROWS, W_CAST_ROWS), W_CAST_ROWS)
            wgu_bf[rows, :] = wgu_ref[0, rows, :].astype(BF16)
            return c
        lax.fori_loop(0, D_MODEL // W_CAST_ROWS, cast_gu, 0)

        def cast_d(r, c):
            rows = pl.ds(pl.multiple_of(r * W_CAST_ROWS, W_CAST_ROWS), W_CAST_ROWS)
            wd_bf[rows, :] = wd_ref[0, rows, :].astype(BF16)
            return c
        lax.fori_loop(0, D_FF // W_CAST_ROWS, cast_d, 0)

    def issue_rows(src, t0, count):
        for q in range(count):
            t = t0 + q
            dst = pl.multiple_of(slots_ref[src, 0, t], ROW_TILE)
            row0 = pl.multiple_of(t * ROW_TILE, ROW_TILE)
            pltpu.make_async_copy(outs[src].at[pl.ds(row0, ROW_TILE), :],
                                  yk_hbm.at[pl.ds(dst, ROW_TILE), :],
                                  sem_o.at[src]).start(priority=q % DMA_THREADS)

    def wait_rows(sl):
        pltpu.make_async_copy(outs[sl], yk_hbm.at[pl.ds(0, E_BLK * ROW_TILE), :], sem_o.at[sl]).wait()

    lane = lax.broadcasted_iota(jnp.int32, (E_BLK, LANES), 1)
    row = lax.broadcasted_iota(jnp.int32, (E_BLK, 1), 0)
    spare0 = float(TOP_K * n_tok)

    @pl.when(i == 0)
    def _():
        t_i = lax.broadcasted_iota(jnp.int32, (SUBLANES, E_BLK), 1)
        slotv_ref[...] = (TOP_K * n_tok + t_i) * ROW_TILE
        pltpu.make_async_copy(slotv_ref, slots_ref.at[1], sem_v.at[0]).start()
        out_b[...] = jnp.zeros(out_b.shape, F32)

    def block(cur):
        prv = 1 - cur

        def smem_copy(dst):
            return pltpu.make_async_copy(slotv_ref, slots_ref.at[dst], sem_v.at[0])

        @pl.when(i < n_used)
        def _():
            smem_copy(prv).wait()
            record = xs_ref[pl.ds(PACK_ROWS, E_BLK, stride=ROW_TILE), :]
            hit = jnp.logical_and(record.astype(jnp.int32) == be_ref[i], lane < TOP_K)
            slot_f = jnp.sum(jnp.where(hit, pltpu.roll(record, LANES - TOP_K, 1), 0.0), axis=-1, keepdims=True)
            slot_f = jnp.where(row < nv_ref[i], slot_f, spare0 + row.astype(F32))
            slot_mat = jnp.where(lane == 0, slot_f * float(ROW_TILE), 0.0)
            slotv_ref[...] = slot_mat.T[:SUBLANES, :].astype(jnp.int32)
            smem_copy(cur).start()

        @pl.when(i < n_used)
        def _():
            live = row < nv_ref[i]
            words = [lax.bitcast_convert_type(xs_ref[pl.ds(c, E_BLK, stride=ROW_TILE), :], jnp.int32)
                     for c in range(PACK_ROWS)]
            lo = [lax.bitcast_convert_type(lax.shift_left(w, 16), F32) for w in words]
            hi = [lax.bitcast_convert_type(w & HI16_MASK, F32) for w in words]
            xb = jnp.where(live, jnp.concatenate(lo + hi, axis=1), 0.0).astype(BF16)

            issue_rows(prv, 0, E_BLK)
            hgu = _dot(xb, wgu_bf[...]) + bgu_ref[0]
            g = jnp.minimum(hgu[:, :D_FF], SWIGLU_LIMIT)
            up = jnp.clip(hgu[:, D_FF:], -SWIGLU_LIMIT, SWIGLU_LIMIT)
            act = (up + 1.0) * (g * jax.nn.sigmoid(SWIGLU_ALPHA * g))
            y = _dot(act.astype(BF16), wd_bf[...]) + bd_ref[0]

            @pl.when(i > 0)
            def _():
                wait_rows(cur)

            for c in range(ROW_TILE):
                outs[cur][pl.ds(c, E_BLK, stride=ROW_TILE), :] = y[:, c * LANES:(c + 1) * LANES]

        @pl.when(i == n_used)
        def _():
            smem_copy(prv).wait()

            def tail(t8, c):
                issue_rows(prv, t8 * SUBLANES, SUBLANES)
                return c
            lax.fori_loop(0, E_BLK // SUBLANES, tail, 0)
            wait_rows(cur)
            wait_rows(prv)

    for parity in range(2):
        @pl.when(i % 2 == parity)
        def _(parity=parity):
            block(parity)


def _expert_call(blk_row, blk_e, blk_valid, n_used, xs, w_gu, b_gu, w_down, b_down, n_tok):
    nb = blk_row.shape[0]
    grid_spec = pltpu.PrefetchScalarGridSpec(
        num_scalar_prefetch=4,
        grid=(nb,),
        in_specs=[
            pl.BlockSpec((E_BLK * ROW_TILE, LANES), lambda i, br, be, nv, nu: (br[i], 0)),
            pl.BlockSpec((1, D_MODEL, 2 * D_FF), lambda i, br, be, nv, nu: (be[i], 0, 0)),
            pl.BlockSpec((1, 1, 2 * D_FF), lambda i, br, be, nv, nu: (be[i], 0, 0)),
            pl.BlockSpec((1, D_FF, D_MODEL), lambda i, br, be, nv, nu: (be[i], 0, 0)),
            pl.BlockSpec((1, 1, D_MODEL), lambda i, br, be, nv, nu: (be[i], 0, 0)),
        ],
        out_specs=pl.BlockSpec(memory_space=pl.ANY),
        scratch_shapes=[pltpu.VMEM((D_MODEL, 2 * D_FF), BF16),
                        pltpu.VMEM((D_FF, D_MODEL), BF16),
                        pltpu.VMEM((E_BLK * ROW_TILE, LANES), F32),
                        pltpu.VMEM((E_BLK * ROW_TILE, LANES), F32),
                        pltpu.VMEM((SUBLANES, E_BLK), jnp.int32),
                        pltpu.SMEM((2, SUBLANES, E_BLK), jnp.int32),
                        pltpu.SemaphoreType.DMA((2,)),
                        pltpu.SemaphoreType.DMA((1,))],
    )
    return pl.pallas_call(
        functools.partial(_expert_kernel, n_tok),
        grid_spec=grid_spec,
        out_shape=jax.ShapeDtypeStruct(((TOP_K * n_tok + E_BLK) * ROW_TILE, LANES), F32),
        compiler_params=pltpu.CompilerParams(dimension_semantics=("arbitrary",),
                                             vmem_limit_bytes=VMEM_LIMIT),
        name="experts",
    )(blk_row, blk_e, blk_valid, n_used, xs, w_gu, b_gu, w_down, b_down)


def _combine_kernel(nseg, seg, y0_ref, y1_ref, y2_ref, y3_ref, x1_ref, gate_ref, mod_ref, g2_ref, b2_ref,
                    out_ref):
    gates = gate_ref[...]
    ffn = None
    for k, y_ref in enumerate((y0_ref, y1_ref, y2_ref, y3_ref)):
        rows = jnp.concatenate(
            [y_ref[pl.ds(c, TL, stride=ROW_TILE), :] for c in range(ROW_TILE)], axis=1)
        term = gates[:, k:k + 1] * rows
        ffn = term if ffn is None else ffn + term
    if nseg == 1:
        gate2 = mod_ref[0, 5:6, :]
    else:
        gate2 = jnp.concatenate(
            [jnp.broadcast_to(mod_ref[s, 5:6, :], (seg, D_MODEL)) for s in range(nseg)], axis=0)
    xa = DEEPNORM_ALPHA * x1_ref[...] + (1.0 + gate2) * ffn
    out_ref[...] = _layer_norm(xa, g2_ref[...], b2_ref[...])


def _combine_call(yk, x1, gates, mod, ln2_g, ln2_b, nseg, seg, n_tiles, tile0, tiles_per_group, n_tok):
    tiles_total = n_tok // TL

    def yk_spec(k):
        return pl.BlockSpec((TL * ROW_TILE, LANES), lambda i, k=k: (k * tiles_total + tile0 + i, 0))

    in_specs = [yk_spec(k) for k in range(TOP_K)] + [
        pl.BlockSpec((TL, D_MODEL), lambda i: (tile0 + i, 0)),
        pl.BlockSpec((TL, LANES), lambda i: (tile0 + i, 0)),
        pl.BlockSpec((nseg, 6, D_MODEL), lambda i: (i // tiles_per_group, 0, 0)),
        pl.BlockSpec((1, D_MODEL), lambda i: (0, 0)),
        pl.BlockSpec((1, D_MODEL), lambda i: (0, 0)),
    ]
    return pl.pallas_call(
        functools.partial(_combine_kernel, nseg, seg),
        grid=(n_tiles,),
        in_specs=in_specs,
        out_specs=pl.BlockSpec((TL, D_MODEL), lambda i: (i, 0)),
        out_shape=jax.ShapeDtypeStruct((n_tiles * TL, D_MODEL), F32),
        compiler_params=pltpu.CompilerParams(dimension_semantics=("arbitrary",),
                                             vmem_limit_bytes=VMEM_LIMIT),
        name="combine_prompt" if nseg == 1 else "combine_sample",
    )(yk, yk, yk, yk, x1, gates, mod, ln2_g, ln2_b)


def _s5_tables(lam_re, lam_im, log_dt, b_re, b_im, c_re, c_im):
    dt = jnp.exp(log_dt.astype(F32))[:, None]
    lam = lax.complex(lam_re.astype(F32), lam_im.astype(F32))
    lam_dt = lam * dt
    lam_bar = jnp.exp(lam_dt)
    b_bar = ((lam_bar - 1.0) / lam)[..., None] * lax.complex(b_re.astype(F32), b_im.astype(F32))
    gl = S5_GROUPS // N_SLAB
    eye = jnp.eye(gl, dtype=F32)

    def b_slab(part):
        a = part.reshape(N_SLAB, gl, S5_STATE, S5_GROUP)
        return jnp.einsum('sgph,gk->sghkp', a, eye).reshape(N_SLAB, gl * S5_GROUP, gl * S5_STATE)

    def b_rows(b):
        return jnp.concatenate([b_slab(b.real), b_slab(b.imag)], axis=-1)

    bmat = jnp.concatenate([b_rows(b_bar), b_rows(lam_bar[..., None] * b_bar)], axis=1).astype(BF16)

    def c_slab(part):
        a = part.reshape(N_SLAB, gl, S5_GROUP, S5_STATE)
        return jnp.einsum('sghp,gk->sgpkh', a, eye).reshape(N_SLAB, gl * S5_STATE, gl * S5_GROUP)

    cmat = jnp.concatenate([c_slab(c_re.astype(F32)), -c_slab(c_im.astype(F32))], axis=1).astype(BF16)

    row = jnp.arange(SUBLANES, dtype=F32)[:, None, None]

    def power(k):
        return jnp.exp(lam_dt[None] * k)

    tabs = []
    for d in (2, 4):
        pw = power(jnp.full_like(row, float(d)))
        mask = (row >= d).astype(F32)
        tabs += [pw.real * mask, pw.imag * mask]
    pw = power(row + 1.0)
    tabs += [pw.real, pw.imag]
    tab = jnp.stack(tabs, axis=0)
    tab = tab.reshape(len(tabs), SUBLANES, N_SLAB, gl * S5_STATE).transpose(2, 0, 1, 3)
    return bmat, cmat, tab


def _state_to_slab(re, im):
    n = re.shape[0]
    return jnp.concatenate([re.reshape(n, N_SLAB, HALF), im.reshape(n, N_SLAB, HALF)], axis=-1).astype(F32)


def _slab_to_state(s):
    n = s.shape[0]
    re = s[:, :, :HALF].reshape(n, S5_GROUPS, S5_STATE)
    im = s[:, :, HALF:].reshape(n, S5_GROUPS, S5_STATE)
    return re, im


def _block_table(counts, n_tok):
    n_assign = n_tok * TOP_K
    n_blocks = (n_assign + N_EXPERTS * (E_BLK - 1) + E_BLK - 1) // E_BLK + 1
    cap_blocks = n_tok // E_BLK
    nblk = (counts + E_BLK - 1) // E_BLK
    cum = jnp.cumsum(nblk)
    start = cum - nblk
    n_used = cum[-1]
    i = jnp.arange(n_blocks, dtype=jnp.int32)
    ii = jnp.maximum(jnp.minimum(i, n_used - 1), 0)
    e = jnp.minimum(jnp.sum((cum[None, :] <= ii[:, None]).astype(jnp.int32), axis=1), N_EXPERTS - 1)
    b = ii - start[e]
    blk_row = (e * cap_blocks + b).astype(jnp.int32)
    valid = jnp.where(i < n_used, jnp.minimum(E_BLK, counts[e] - b * E_BLK), 0).astype(jnp.int32)
    return blk_row, e, valid, n_used.astype(jnp.int32).reshape(1)


def kernel(x_prompt, x_sample, c_prompt, c_sample, state_s5_re, state_s5_im, state_conv, w_ada, b_ada, w_in,
           s5_lam_re, s5_lam_im, s5_log_dt, s5_b_re, s5_b_im, s5_c_re, s5_c_im, s5_d, w_glu, b_glu, conv_w,
           w_out, ln1_g, ln1_b, w_router, b_router, w_gu, b_gu, w_down, b_down, ln2_g, ln2_b):
    assert DEPTH == 1 and w_ada.shape[0] == 1
    n_p, l_p, _ = x_prompt.shape
    n_s, l_s, _ = x_sample.shape
    t_p, t_s = n_p * l_p, n_s * l_s
    n_tok = t_p + t_s
    tiles_p, tiles_s = t_p // TL, t_s // TL
    seq_per_tile = TL // l_s
    assert l_p % TL == 0 and TL % l_s == 0 and n_s % seq_per_tile == 0

    c_all = jnp.concatenate([c_prompt, c_sample], axis=0).astype(F32)
    pad = (-c_all.shape[0]) % SUBLANES
    c_all = jnp.pad(c_all, ((0, pad), (0, 0)))
    mod = _ada_call(c_all, w_ada[0], b_ada[0].reshape(1, -1)).reshape(-1, 6, D_MODEL)
    mod_p, mod_s = mod[:n_p], mod[n_p:n_p + n_s]

    bmat, cmat, tab = _s5_tables(s5_lam_re[0], s5_lam_im[0], s5_log_dt[0], s5_b_re[0], s5_b_im[0],
                                 s5_c_re[0], s5_c_im[0])
    wr = jnp.pad(w_router[0], ((0, 0), (0, LANES - N_EXPERTS))).astype(BF16)
    br = jnp.pad(b_router[0].astype(F32), (0, LANES - N_EXPERTS), constant_values=NEG_BIG).reshape(1, LANES)
    weights = (w_in[0].astype(BF16), bmat, cmat, tab, s5_d[0].reshape(1, D_S5).astype(F32),
               w_glu[0].astype(BF16), b_glu[0].reshape(1, D_S5).astype(F32), conv_w[0].astype(F32),
               w_out[0].astype(BF16), ln1_g[0].reshape(1, D_MODEL).astype(F32),
               ln1_b[0].reshape(1, D_MODEL).astype(F32), wr, br)

    h0_p = jnp.zeros((n_p, N_SLAB, SLAB_W), F32)
    cb_p = jnp.zeros((n_p, CONV_W - 1, D_CONV), F32)
    h0_s = _state_to_slab(state_s5_re[0], state_s5_im[0])
    cb_s = state_conv[0].astype(F32)

    assert n_tok % E_BLK == 0 and E_BLK % TL == 0
    cnt0 = jnp.zeros((1, LANES), F32)
    outs_p = _mixer_call(x_prompt, mod_p, h0_p, cb_p, weights, cnt0, 1, TL, l_p // TL, 0, n_tok, ())
    x1, gates, xs, cnt_p, s_p, conv_p = outs_p
    xs3 = x_sample.reshape(n_s // seq_per_tile, TL, D_MODEL)
    outs_s = _mixer_call(xs3, mod_s, h0_s, cb_s, weights, cnt_p, seq_per_tile, l_s, 1, tiles_p, n_tok,
                         (x1, gates, xs))
    x1, gates, xs, cnt, s_s, conv_s = outs_s

    counts = cnt[0, :N_EXPERTS].astype(jnp.int32)
    blk_row, blk_e, blk_valid, n_used = _block_table(counts, n_tok)
    yk = _expert_call(blk_row, blk_e, blk_valid, n_used, xs, w_gu[0].astype(F32),
                      b_gu[0].reshape(N_EXPERTS, 1, 2 * D_FF).astype(F32), w_down[0].astype(F32),
                      b_down[0].reshape(N_EXPERTS, 1, D_MODEL).astype(F32), n_tok)
    g2 = ln2_g[0].reshape(1, D_MODEL).astype(F32)
    b2 = ln2_b[0].reshape(1, D_MODEL).astype(F32)
    y_p = _combine_call(yk, x1, gates, mod_p, g2, b2, 1, TL, tiles_p, 0, l_p // TL, n_tok)
    y_s = _combine_call(yk, x1, gates, mod_s, g2, b2, seq_per_tile, l_s, tiles_s, tiles_p, 1, n_tok)

    p_re, p_im = _slab_to_state(s_p)
    s_re, s_im = _slab_to_state(s_s)
    return (y_p.reshape(n_p, l_p, D_MODEL), y_s.reshape(n_s, l_s, D_MODEL),
            p_re[None], p_im[None], conv_p[None], s_re[None], s_im[None], conv_s[None])
```

```python
import functools
import math

import jax
import jax.numpy as jnp
from jax import lax
from jax.experimental import pallas as pl
from jax.experimental.pallas import tpu as pltpu

F32 = jnp.float32
BF16 = jnp.bfloat16

D_MODEL = 1024
DEPTH = 1
D_S5 = 512
D_CONV = 512
S5_GROUP = 16
S5_GROUPS = 32
S5_STATE = 64
CONV_W = 3
N_EXPERTS = 32
TOP_K = 4
D_FF = 1024
SWIGLU_LIMIT = 7.0
SWIGLU_ALPHA = 1.702
LN_EPS = 1e-5
DEEPNORM_ALPHA = (2.0 * DEPTH) ** 0.25

TL = 256
SUBLANES = 8
LANES = 128
N_SLAB = 4
SLAB_W = 1024
HALF = 512
E_BLK = 512
ROW_TILE = D_MODEL // LANES
PACK_ROWS = HALF // LANES
HI16_MASK = -65536
DMA_THREADS = 2
NEG_BIG = -1e30
VMEM_LIMIT = 56 * 1024 * 1024


def _dot(a, b):
    return jnp.dot(a, b, preferred_element_type=F32)


def _layer_norm(x, g, b):
    mu = jnp.mean(x, axis=-1, keepdims=True)
    xc = x - mu
    var = jnp.mean(xc * xc, axis=-1, keepdims=True)
    return xc * lax.rsqrt(var + LN_EPS) * g + b


def _ada_kernel(c_ref, w_ref, b_ref, o_ref):
    c = c_ref[...]
    s = c * jax.nn.sigmoid(c)
    o_ref[...] = _dot(s.astype(BF16), w_ref[...].astype(BF16)) + b_ref[...]


def _ada_call(c_all, w_ada, b_ada):
    rows = c_all.shape[0]
    n_out = w_ada.shape[1]
    tn = 768
    return pl.pallas_call(
        _ada_kernel,
        grid=(n_out // tn,),
        in_specs=[pl.BlockSpec((rows, D_MODEL), lambda i: (0, 0)),
                  pl.BlockSpec((D_MODEL, tn), lambda i: (0, i)),
                  pl.BlockSpec((1, tn), lambda i: (0, i))],
        out_specs=pl.BlockSpec((rows, tn), lambda i: (0, i)),
        out_shape=jax.ShapeDtypeStruct((rows, n_out), F32),
        compiler_params=pltpu.CompilerParams(dimension_semantics=("arbitrary",),
                                             vmem_limit_bytes=VMEM_LIMIT),
        name="ada_mod",
    )(c_all, w_ada, b_ada)


def _mixer_kernel(nseg, seg, carry_tiles, n_alias, cap_rows, tile0,
                  x_ref, xn_ref, mod_ref, modn_ref, h0_ref, cbuf_ref, win_ref, bmat_ref, cmat_ref, tab_ref, d_ref,
                  wglu_ref, bglu_ref, convw_ref, wout_ref, g1_ref, b1_ref, wr_ref, br_ref, cnt0_ref, *rest):
    rest = rest[n_alias:]
    (x1_ref, gate_ref, xs_hbm, cnt_out_ref, sout_ref, cout_ref,
     p_ref, bu_ref, hs_ref, xcs_ref, carry_ref, h2s_ref, posv_ref, poss_ref, cnt_ref, sem_s, sem_v) = rest
    g_id = pl.program_id(0)
    j = pl.program_id(1)
    step = g_id * pl.num_programs(1) + j
    is_last = step == pl.num_programs(0) * pl.num_programs(1) - 1
    slot = step % 2

    def rows_of(m_ref, k):
        if nseg == 1:
            return m_ref[0, k:k + 1, :]
        return jnp.concatenate(
            [jnp.broadcast_to(m_ref[s, k:k + 1, :], (seg, D_MODEL)) for s in range(nseg)], axis=0)

    def mod_rows(k):
        return rows_of(mod_ref, k)

    def front_pieces(xt_ref, m_ref, dst):
        def adaln():
            h = xt_ref[0] * (1.0 + rows_of(m_ref, 1)) + rows_of(m_ref, 0)
            hs_ref[...] = h.astype(BF16)

        def in_proj(c):
            cols = slice(c * HALF, (c + 1) * HALF)
            p_ref[dst, :, cols] = _dot(hs_ref[...], win_ref[:, cols])

        def s5_in(i):
            u_i = p_ref[dst, :, i * LANES:(i + 1) * LANES]
            row_in_tile = lax.broadcasted_iota(jnp.int32, u_i.shape, 0) % SUBLANES
            u_prev = jnp.where(row_in_tile == 0, 0.0, pltpu.roll(u_i, 1, 0))
            lhs = jnp.concatenate([u_i, u_prev], axis=1).astype(BF16)
            bu_ref[dst * N_SLAB + i] = _dot(lhs, bmat_ref[i])

        return ([adaln] + [functools.partial(in_proj, c) for c in range(2 * D_MODEL // HALF)]
                + [functools.partial(s5_in, i) for i in range(N_SLAB)])

    @pl.when(step == 0)
    def _():
        for piece in front_pieces(x_ref, mod_ref, 0):
            piece()

    nxt_front = front_pieces(xn_ref, modn_ref, 1 - slot)

    def run_front(n):
        for _ in range(n):
            nxt_front.pop(0)()

    x = x_ref[0]
    bu0 = slot * N_SLAB

    prev = 1 - slot
    smem_copy = pltpu.make_async_copy(posv_ref, poss_ref, sem_v.at[0])

    @pl.when(step == 0)
    def _():
        k_i = lax.broadcasted_iota(jnp.int32, (SUBLANES, TL), 0)
        t_i = lax.broadcasted_iota(jnp.int32, (SUBLANES, TL), 1)
        posv_ref[...] = (N_EXPERTS * cap_rows + k_i * TL + t_i) * ROW_TILE
        smem_copy.start()
        h2s_ref[...] = jnp.zeros(h2s_ref.shape, F32)

    smem_copy.wait()

    def issue_rows(src_slot, t0, n_rows):
        for q in range(n_rows):
            t = t0 + q
            src = pl.multiple_of(t * ROW_TILE, ROW_TILE)
            for k in range(TOP_K):
                dst = pl.multiple_of(poss_ref[k, t], ROW_TILE)
                pltpu.make_async_copy(h2s_ref.at[src_slot, pl.ds(src, ROW_TILE), :],
                                      xs_hbm.at[pl.ds(dst, ROW_TILE), :],
                                      sem_s.at[src_slot]).start(priority=k % DMA_THREADS)

    def wait_scatter(sl):
        for _ in range(TOP_K):
            pltpu.make_async_copy(h2s_ref.at[sl], xs_hbm.at[pl.ds(0, TL * ROW_TILE), :], sem_s.at[sl]).wait()

    if carry_tiles:
        @pl.when(j == 0)
        def _():
            carry_ref[...] = h0_ref[0]

    for s in range(nseg):
        if carry_tiles:
            init = tuple(carry_ref[i:i + 1, :] for i in range(N_SLAB))
        else:
            init = tuple(h0_ref[s, i:i + 1, :] for i in range(N_SLAB))

        def scan_body(r, carry, s=s):
            issue_rows(prev, s * seg + r * (2 * SUBLANES), 2 * SUBLANES)
            new = []
            for i in range(N_SLAB):
                cr = carry[i][:, :HALF]
                ci = carry[i][:, HALF:]
                for half in range(2):
                    row0 = pl.multiple_of(s * seg + r * (2 * SUBLANES) + half * SUBLANES, SUBLANES)
                    blk = bu_ref[bu0 + i, pl.ds(row0, SUBLANES), :]
                    xr = blk[:, :HALF]
                    xi = blk[:, HALF:]
                    for k, d in enumerate((2, 4)):
                        lr = tab_ref[i, 2 * k]
                        li = tab_ref[i, 2 * k + 1]
                        rr = pltpu.roll(xr, d, 0)
                        ri = pltpu.roll(xi, d, 0)
                        xr, xi = xr + (lr * rr - li * ri), xi + (lr * ri + li * rr)
                    pr = tab_ref[i, 4]
                    pi_ = tab_ref[i, 5]
                    crb = jnp.broadcast_to(cr, (SUBLANES, HALF))
                    cib = jnp.broadcast_to(ci, (SUBLANES, HALF))
                    xr, xi = xr + (pr * crb - pi_ * cib), xi + (pr * cib + pi_ * crb)
                    bu_ref[bu0 + i, pl.ds(row0, SUBLANES), :] = jnp.concatenate([xr, xi], axis=1)
                    cr = xr[SUBLANES - 1:SUBLANES, :]
                    ci = xi[SUBLANES - 1:SUBLANES, :]
                new.append(jnp.concatenate([cr, ci], axis=1))
            return tuple(new)

        final = lax.fori_loop(0, seg // (2 * SUBLANES), scan_body, init)
        for i in range(N_SLAB):
            sout_ref[s, i:i + 1, :] = final[i]
            if carry_tiles:
                carry_ref[i:i + 1, :] = final[i]

    y = jnp.concatenate([_dot(bu_ref[bu0 + i].astype(BF16), cmat_ref[i]) for i in range(N_SLAB)], axis=1)
    y = y + d_ref[...] * p_ref[slot, :, :D_S5]
    z = jax.nn.gelu(y)
    z = z * jax.nn.sigmoid(_dot(z.astype(BF16), wglu_ref[...]) + bglu_ref[...])

    xc = p_ref[slot, :, D_S5 + D_CONV:D_S5 + 2 * D_CONV] * p_ref[slot, :, D_S5 + 2 * D_CONV:]
    w0 = convw_ref[0:1, :]
    w1 = convw_ref[1:2, :]
    w2 = convw_ref[2:3, :]
    conv_parts = []
    for s in range(nseg):
        base = s * (seg + SUBLANES)
        xc_s = xc[s * seg:(s + 1) * seg]
        tail = xc_s[seg - 2:seg]
        xcs_ref[pl.ds(base + SUBLANES, seg), :] = xc_s
        if carry_tiles:
            @pl.when(j == 0)
            def _(s=s, base=base):
                xcs_ref[pl.ds(base + SUBLANES - 2, 2), :] = cbuf_ref[s]
        else:
            xcs_ref[pl.ds(base + SUBLANES - 2, 2), :] = cbuf_ref[s]
        xm1 = xcs_ref[pl.ds(base + SUBLANES - 1, seg), :]
        xm2 = xcs_ref[pl.ds(base + SUBLANES - 2, seg), :]
        conv_parts.append(w0 * xm2 + w1 * xm1 + w2 * xc_s)
        cout_ref[s] = tail
        if carry_tiles:
            xcs_ref[pl.ds(base + SUBLANES - 2, 2), :] = tail
    conv = conv_parts[0] if nseg == 1 else jnp.concatenate(conv_parts, axis=0)
    y_b = p_ref[slot, :, D_S5:D_S5 + D_CONV] * conv

    run_front(2)
    mix = _dot(jnp.concatenate([z, y_b], axis=1).astype(BF16), wout_ref[...])
    x1 = _layer_norm(DEEPNORM_ALPHA * x + (1.0 + mod_rows(2)) * mix, g1_ref[...], b1_ref[...])
    x1_ref[...] = x1
    run_front(1)
    h2 = x1 * (1.0 + mod_rows(4)) + mod_rows(3)
    lo_bits = lax.bitcast_convert_type(h2[:, :HALF].astype(BF16).astype(F32), jnp.int32)
    hi_bits = lax.bitcast_convert_type(h2[:, HALF:].astype(BF16).astype(F32), jnp.int32)
    packed = lax.bitcast_convert_type(hi_bits | lax.shift_right_logical(lo_bits, 16), F32)
    for c in range(PACK_ROWS):
        h2s_ref[slot, pl.ds(c, TL, stride=ROW_TILE), :] = packed[:, c * LANES:(c + 1) * LANES]

    logits = _dot(h2.astype(BF16), wr_ref[...]) + br_ref[...]
    lane = lax.broadcasted_iota(jnp.int32, logits.shape, 1)
    lane_f = lane.astype(F32)
    vals, sels, hots = [], [], []
    cur = logits
    for _ in range(TOP_K):
        m = jnp.max(cur, axis=-1, keepdims=True)
        am = jnp.min(jnp.where(cur == m, lane_f, float(LANES)), axis=-1, keepdims=True)
        hot = lane_f == am
        vals.append(m)
        sels.append(am)
        hots.append(hot)
        cur = jnp.where(hot, -jnp.inf, cur)
    exps = [jnp.exp(v - vals[0]) for v in vals]
    inv = 1.0 / (exps[0] + exps[1] + exps[2] + exps[3])
    gate_out = jnp.zeros(logits.shape, F32)
    for k in range(TOP_K):
        gate_out = jnp.where(lane == k, exps[k] * inv, gate_out)
    gate_ref[...] = gate_out
    run_front(1)

    tok = ((tile0 + step) * TL + lax.broadcasted_iota(jnp.int32, logits.shape, 0)).astype(F32)
    record = jnp.zeros(logits.shape, F32)
    for k in range(TOP_K):
        record = jnp.where(lane == k, sels[k], record)
        record = jnp.where(lane == TOP_K + k, tok + float(k * cap_rows), record)
    h2s_ref[slot, pl.ds(PACK_ROWS, TL, stride=ROW_TILE), :] = record
    run_front(2)

    @pl.when(step == 0)
    def _():
        cnt_ref[...] = cnt0_ref[...]

    chosen = jnp.zeros(logits.shape, F32)
    for k in range(TOP_K):
        chosen = jnp.where(hots[k], 1.0, chosen)
    r_i = lax.broadcasted_iota(jnp.int32, (TL, TL), 0)
    c_i = lax.broadcasted_iota(jnp.int32, (TL, TL), 1)
    before = jnp.where(c_i < r_i, 1.0, 0.0).astype(BF16)
    rank_base = _dot(before, chosen.astype(BF16)) + cnt_ref[...]
    pos_mat = jnp.zeros(logits.shape, F32)
    for k in range(TOP_K):
        rank_k = jnp.sum(jnp.where(hots[k], rank_base, 0.0), axis=-1, keepdims=True)
        pos_mat = jnp.where(lane == k, (sels[k] * float(cap_rows) + rank_k) * float(ROW_TILE), pos_mat)
    cnt_ref[...] = cnt_ref[...] + jnp.sum(chosen, axis=0, keepdims=True)
    cnt_out_ref[...] = cnt_ref[...]
    run_front(len(nxt_front))
    pos_t =pos_mat.T[:SUBLANES, :].astype(jnp.int32)
    posv_ref[...] = pos_t
    smem_copy.start()
    wait_scatter(prev)

    @pl.when(is_last)
    def _():
        smem_copy.wait()

        def tail_body(t8, c):
            issue_rows(slot, t8 * SUBLANES, SUBLANES)
            return c
        lax.fori_loop(0, TL // SUBLANES, tail_body, 0)
        wait_scatter(slot)


def _mixer_call(x3, mod, h0, cbuf, weights, cnt0, nseg, seg, tiles_per_group, tile0, n_tok, aliased):
    groups = x3.shape[0]
    nseq = mod.shape[0]
    carry_tiles = nseg == 1
    n_alias = len(aliased)
    n_tiles = n_tok // TL
    cap_rows = n_tok

    def full(a):
        nd = a.ndim
        return pl.BlockSpec(a.shape, lambda g, j, nd=nd: (0,) * nd)

    def tile_map(g, j):
        return (tile0 + g * tiles_per_group + j, 0)

    def next_g(g, j):
        return jnp.where(j + 1 < tiles_per_group, g, jnp.minimum(g + 1, groups - 1))

    def next_j(g, j):
        return jnp.where(j + 1 < tiles_per_group, j + 1, 0)

    in_specs = [pl.BlockSpec((1, TL, D_MODEL), lambda g, j: (g, j, 0)),
                pl.BlockSpec((1, TL, D_MODEL), lambda g, j: (next_g(g, j), next_j(g, j), 0)),
                pl.BlockSpec((nseg, 6, D_MODEL), lambda g, j: (g, 0, 0)),
                pl.BlockSpec((nseg, 6, D_MODEL), lambda g, j: (next_g(g, j), 0, 0)),
                pl.BlockSpec((nseg, N_SLAB, SLAB_W), lambda g, j: (g, 0, 0)),
                pl.BlockSpec((nseg, CONV_W - 1, D_CONV), lambda g, j: (g, 0, 0))]
    in_specs += [full(w) for w in weights]
    in_specs += [full(cnt0)]
    in_specs += [pl.BlockSpec(memory_space=pl.ANY)] * n_alias
    out_shape = [jax.ShapeDtypeStruct((n_tok, D_MODEL), F32),
                 jax.ShapeDtypeStruct((n_tok, LANES), F32),
                 jax.ShapeDtypeStruct(((N_EXPERTS * cap_rows + TOP_K * TL) * ROW_TILE, LANES), F32),
                 jax.ShapeDtypeStruct((1, LANES), F32),
                 jax.ShapeDtypeStruct((nseq, N_SLAB, SLAB_W), F32),
                 jax.ShapeDtypeStruct((nseq, CONV_W - 1, D_CONV), F32)]
    out_specs = [pl.BlockSpec((TL, D_MODEL), tile_map),
                 pl.BlockSpec((TL, LANES), tile_map),
                 pl.BlockSpec(memory_space=pl.ANY),
                 pl.BlockSpec((1, LANES), lambda g, j: (0, 0)),
                 pl.BlockSpec((nseg, N_SLAB, SLAB_W), lambda g, j: (g, 0, 0)),
                 pl.BlockSpec((nseg, CONV_W - 1, D_CONV), lambda g, j: (g, 0, 0))]
    n_in = 6 + len(weights) + 1
    aliases = {n_in + k: k for k in range(n_alias)}
    scratch = [pltpu.VMEM((2, TL, 2 * D_MODEL), F32),
               pltpu.VMEM((2 * N_SLAB, TL, SLAB_W), F32),
               pltpu.VMEM((TL, D_MODEL), BF16),
               pltpu.VMEM((nseg * (seg + SUBLANES), D_CONV), F32),
               pltpu.VMEM((N_SLAB, SLAB_W), F32),
               pltpu.VMEM((2, TL * ROW_TILE, LANES), F32),
               pltpu.VMEM((SUBLANES, TL), jnp.int32),
               pltpu.SMEM((SUBLANES, TL), jnp.int32),
               pltpu.VMEM((1, LANES), F32),
               pltpu.SemaphoreType.DMA((2,)),
               pltpu.SemaphoreType.DMA((1,))]
    return pl.pallas_call(
        functools.partial(_mixer_kernel, nseg, seg, carry_tiles, n_alias, cap_rows, tile0),
        grid=(groups, tiles_per_group),
        in_specs=in_specs, out_specs=out_specs, out_shape=out_shape,
        scratch_shapes=scratch,
        input_output_aliases=aliases,
        compiler_params=pltpu.CompilerParams(dimension_semantics=("arbitrary", "arbitrary"),
                                             vmem_limit_bytes=VMEM_LIMIT),
        name="mixer_prompt" if carry_tiles else "mixer_sample",
    )(x3, x3, mod, mod, h0, cbuf, *weights, cnt0, *aliased)


W_CAST_ROWS = 128

def _expert_kernel(n_tok, row_ref, be_ref, nv_ref, nu_ref, xs0_ref, xsn_ref, wgu_ref, bgu_ref, wd_ref, bd_ref,
                   yk_hbm, wgu_bf, wd_bf, xb_ref, out_a, out_b, slotv_ref, slots_ref, sem_o, sem_v):
    i = pl.program_id(0)
    n_used = nu_ref[0]
    outs = (out_a, out_b)
    prev = jnp.maximum(i - 1, 0)
    new_expert = jnp.logical_or(i == 0, be_ref[i] != be_ref[prev])

    @pl.when(jnp.logical_and(new_expert, i < n_used))
    def _():
        def cast_gu(r, c):
            rows = pl.ds(pl.multiple_of(r * W_CAST_ROWS, W_CAST_ROWS), W_CAST_ROWS)
            wgu_bf[rows, :] = wgu_ref[0, rows, :].astype(BF16)
            return c
        lax.fori_loop(0, D_MODEL // W_CAST_ROWS, cast_gu, 0)

        def cast_d(r, c):
            rows = pl.ds(pl.multiple_of(r * W_CAST_ROWS, W_CAST_ROWS), W_CAST_ROWS)
            wd_bf[rows, :] = wd_ref[0, rows, :].astype(BF16)
            return c
        lax.fori_loop(0, D_FF // W_CAST_ROWS, cast_d, 0)

    def issue_rows(src, slot_buf, t0, count):
        for q in range(count):
            t = t0 + q
            dst = pl.multiple_of(slots_ref[slot_buf, 0, t], ROW_TILE)
            row0 = pl.multiple_of(t * ROW_TILE, ROW_TILE)
            pltpu.make_async_copy(outs[src].at[pl.ds(row0, ROW_TILE), :],
                                  yk_hbm.at[pl.ds(dst, ROW_TILE), :],
                                  sem_o.at[src]).start(priority=q % DMA_THREADS)

    def wait_rows(sl):
        pltpu.make_async_copy(outs[sl], yk_hbm.at[pl.ds(0, E_BLK * ROW_TILE), :], sem_o.at[sl]).wait()

    lane = lax.broadcasted_iota(jnp.int32, (E_BLK, LANES), 1)
    row = lax.broadcasted_iota(jnp.int32, (E_BLK, 1), 0)
    spare0 = float(TOP_K * n_tok)

    def smem_copy(buf):
        return pltpu.make_async_copy(slotv_ref, slots_ref.at[buf], sem_v.at[0])

    def prepare(x_ref, blk, xb_slot):
        live = row < nv_ref[blk]
        record = x_ref[pl.ds(PACK_ROWS, E_BLK, stride=ROW_TILE), :]
        hit = jnp.logical_and(record.astype(jnp.int32) == be_ref[blk], lane < TOP_K)
        slot_f = jnp.sum(jnp.where(hit, pltpu.roll(record, LANES - TOP_K, 1), 0.0), axis=-1, keepdims=True)
        slot_f = jnp.where(live, slot_f, spare0 + row.astype(F32))
        slot_mat = jnp.where(lane == 0, slot_f * float(ROW_TILE), 0.0)
        slotv_ref[...] = slot_mat.T[:SUBLANES, :].astype(jnp.int32)
        smem_copy(blk % 3).start()
        words = [lax.bitcast_convert_type(x_ref[pl.ds(c, E_BLK, stride=ROW_TILE), :], jnp.int32)
                 for c in range(PACK_ROWS)]
        lo = [lax.bitcast_convert_type(lax.shift_left(w, 16), F32) for w in words]
        hi = [lax.bitcast_convert_type(w & HI16_MASK, F32) for w in words]
        xb_ref[xb_slot] = jnp.where(live, jnp.concatenate(lo + hi, axis=1), 0.0).astype(BF16)

    @pl.when(i == 0)
    def _():
        t_i = lax.broadcasted_iota(jnp.int32, (SUBLANES, E_BLK), 1)
        slotv_ref[...] = (TOP_K * n_tok + t_i) * ROW_TILE
        smem_copy(2).start()
        smem_copy(2).wait()
        out_b[...] = jnp.zeros(out_b.shape, F32)
        prepare(xs0_ref, 0, 0)

    @pl.when(jnp.logical_and(i > 0, i <= n_used))
    def _():
        smem_copy(0).wait()

    nxt_blk = jnp.minimum(i + 1, pl.num_programs(0) - 1)
    sent = (i + 2) % 3

    def block(cur):
        prv = 1 - cur

        @pl.when(i < n_used)
        def _():
            @pl.when(i == 0)
            def _():
                smem_copy(0).wait()

            issue_rows(prv, sent, 0, E_BLK)
            hgu = _dot(xb_ref[cur], wgu_bf[...]) + bgu_ref[0]
            prepare(xsn_ref, nxt_blk, prv)
            g = jnp.minimum(hgu[:, :D_FF], SWIGLU_LIMIT)
            up = jnp.clip(hgu[:, D_FF:], -SWIGLU_LIMIT, SWIGLU_LIMIT)
            act = (up + 1.0) * (g * jax.nn.sigmoid(SWIGLU_ALPHA * g))
            y = _dot(act.astype(BF16), wd_bf[...]) + bd_ref[0]

            @pl.when(i > 0)
            def _():
                wait_rows(cur)

            for c in range(ROW_TILE):
                outs[cur][pl.ds(c, E_BLK, stride=ROW_TILE), :] = y[:, c * LANES:(c + 1) * LANES]

        @pl.when(i == n_used)
        def _():
            def tail(t8, c):
                issue_rows(prv, sent, t8 * SUBLANES, SUBLANES)
                return c
            lax.fori_loop(0, E_BLK // SUBLANES, tail, 0)
            wait_rows(cur)
            wait_rows(prv)

    for parity in range(2):
        @pl.when(i % 2 == parity)
        def _(parity=parity):
            block(parity)


def _expert_call(blk_row, blk_e, blk_valid, n_used, xs, w_gu, b_gu, w_down, b_down, n_tok):
    nb = blk_row.shape[0]
    grid_spec = pltpu.PrefetchScalarGridSpec(
        num_scalar_prefetch=4,
        grid=(nb,),
        in_specs=[
            pl.BlockSpec((E_BLK * ROW_TILE, LANES), lambda i, br, be, nv, nu: (br[0], 0)),
            pl.BlockSpec((E_BLK * ROW_TILE, LANES),
                         lambda i, br, be, nv, nu: (br[jnp.minimum(i + 1, nb - 1)], 0)),
            pl.BlockSpec((1, D_MODEL, 2 * D_FF), lambda i, br, be, nv, nu: (be[i], 0, 0)),
            pl.BlockSpec((1, 1, 2 * D_FF), lambda i, br, be, nv, nu: (be[i], 0, 0)),
            pl.BlockSpec((1, D_FF, D_MODEL), lambda i, br, be, nv, nu: (be[i], 0, 0)),
            pl.BlockSpec((1, 1, D_MODEL), lambda i, br, be, nv, nu: (be[i], 0, 0)),
        ],
        out_specs=pl.BlockSpec(memory_space=pl.ANY),
        scratch_shapes=[pltpu.VMEM((D_MODEL, 2 * D_FF), BF16),
                        pltpu.VMEM((D_FF, D_MODEL), BF16),
                        pltpu.VMEM((2, E_BLK, D_MODEL), BF16),
                        pltpu.VMEM((E_BLK * ROW_TILE, LANES), F32),
                        pltpu.VMEM((E_BLK * ROW_TILE, LANES), F32),
                        pltpu.VMEM((SUBLANES, E_BLK), jnp.int32),
                        pltpu.SMEM((3, SUBLANES, E_BLK), jnp.int32),
                        pltpu.SemaphoreType.DMA((2,)),
                        pltpu.SemaphoreType.DMA((1,))],
    )
    return pl.pallas_call(
        functools.partial(_expert_kernel, n_tok),
        grid_spec=grid_spec,
        out_shape=jax.ShapeDtypeStruct(((TOP_K * n_tok + E_BLK) * ROW_TILE, LANES), F32),
        compiler_params=pltpu.CompilerParams(dimension_semantics=("arbitrary",),
                                             vmem_limit_bytes=VMEM_LIMIT),
        name="experts",
    )(blk_row, blk_e, blk_valid, n_used, xs, xs, w_gu, b_gu, w_down, b_down)


def _combine_kernel(nseg, seg, y0_ref, y1_ref, y2_ref, y3_ref, x1_ref, gate_ref, mod_ref, g2_ref, b2_ref,
                    out_ref):
    gates = gate_ref[...]
    ffn = None
    for k, y_ref in enumerate((y0_ref, y1_ref, y2_ref, y3_ref)):
        rows = jnp.concatenate(
            [y_ref[pl.ds(c, TL, stride=ROW_TILE), :] for c in range(ROW_TILE)], axis=1)
        term = gates[:, k:k + 1] * rows
        ffn = term if ffn is None else ffn + term
    if nseg == 1:
        gate2 = mod_ref[0, 5:6, :]
    else:
        gate2 = jnp.concatenate(
            [jnp.broadcast_to(mod_ref[s, 5:6, :], (seg, D_MODEL)) for s in range(nseg)], axis=0)
    xa = DEEPNORM_ALPHA * x1_ref[...] + (1.0 + gate2) * ffn
    out_ref[...] = _layer_norm(xa, g2_ref[...], b2_ref[...])


def _combine_call(yk, x1, gates, mod, ln2_g, ln2_b, nseg, seg, n_tiles, tile0, tiles_per_group, n_tok):
    tiles_total = n_tok // TL

    def yk_spec(k):
        return pl.BlockSpec((TL * ROW_TILE, LANES), lambda i, k=k: (k * tiles_total + tile0 + i, 0))

    in_specs = [yk_spec(k) for k in range(TOP_K)] + [
        pl.BlockSpec((TL, D_MODEL), lambda i: (tile0 + i, 0)),
        pl.BlockSpec((TL, LANES), lambda i: (tile0 + i, 0)),
        pl.BlockSpec((nseg, 6, D_MODEL), lambda i: (i // tiles_per_group, 0, 0)),
        pl.BlockSpec((1, D_MODEL), lambda i: (0, 0)),
        pl.BlockSpec((1, D_MODEL), lambda i: (0, 0)),
    ]
    return pl.pallas_call(
        functools.partial(_combine_kernel, nseg, seg),
        grid=(n_tiles,),
        in_specs=in_specs,
        out_specs=pl.BlockSpec((TL, D_MODEL), lambda i: (i, 0)),
        out_shape=jax.ShapeDtypeStruct((n_tiles * TL, D_MODEL), F32),
        compiler_params=pltpu.CompilerParams(dimension_semantics=("arbitrary",),
                                             vmem_limit_bytes=VMEM_LIMIT),
        name="combine_prompt" if nseg == 1 else "combine_sample",
    )(yk, yk, yk, yk, x1, gates, mod, ln2_g, ln2_b)


def _s5_tables(lam_re, lam_im, log_dt, b_re, b_im, c_re, c_im):
    dt = jnp.exp(log_dt.astype(F32))[:, None]
    lam = lax.complex(lam_re.astype(F32), lam_im.astype(F32))
    lam_dt = lam * dt
    lam_bar = jnp.exp(lam_dt)
    b_bar = ((lam_bar - 1.0) / lam)[..., None] * lax.complex(b_re.astype(F32), b_im.astype(F32))
    gl = S5_GROUPS // N_SLAB
    eye = jnp.eye(gl, dtype=F32)

    def b_slab(part):
        a = part.reshape(N_SLAB, gl, S5_STATE, S5_GROUP)
        return jnp.einsum('sgph,gk->sghkp', a, eye).reshape(N_SLAB, gl * S5_GROUP, gl * S5_STATE)

    def b_rows(b):
        return jnp.concatenate([b_slab(b.real), b_slab(b.imag)], axis=-1)

    bmat = jnp.concatenate([b_rows(b_bar), b_rows(lam_bar[..., None] * b_bar)], axis=1).astype(BF16)

    def c_slab(part):
        a = part.reshape(N_SLAB, gl, S5_GROUP, S5_STATE)
        return jnp.einsum('sghp,gk->sgpkh', a, eye).reshape(N_SLAB, gl * S5_STATE, gl * S5_GROUP)

    cmat = jnp.concatenate([c_slab(c_re.astype(F32)), -c_slab(c_im.astype(F32))], axis=1).astype(BF16)

    row = jnp.arange(SUBLANES, dtype=F32)[:, None, None]

    def power(k):
        return jnp.exp(lam_dt[None] * k)

    tabs = []
    for d in (2, 4):
        pw = power(jnp.full_like(row, float(d)))
        mask = (row >= d).astype(F32)
        tabs += [pw.real * mask, pw.imag * mask]
    pw = power(row + 1.0)
    tabs += [pw.real, pw.imag]
    tab = jnp.stack(tabs, axis=0)
    tab = tab.reshape(len(tabs), SUBLANES, N_SLAB, gl * S5_STATE).transpose(2, 0, 1, 3)
    return bmat, cmat, tab


def _state_to_slab(re, im):
    n = re.shape[0]
    return jnp.concatenate([re.reshape(n, N_SLAB, HALF), im.reshape(n, N_SLAB, HALF)], axis=-1).astype(F32)


def _slab_to_state(s):
    n = s.shape[0]
    re = s[:, :, :HALF].reshape(n, S5_GROUPS, S5_STATE)
    im = s[:, :, HALF:].reshape(n, S5_GROUPS, S5_STATE)
    return re, im


def _block_table(counts, n_tok):
    n_assign = n_tok * TOP_K
    n_blocks = (n_assign + N_EXPERTS * (E_BLK - 1) + E_BLK - 1) // E_BLK + 1
    cap_blocks = n_tok // E_BLK
    nblk = (counts + E_BLK - 1) // E_BLK
    cum = jnp.cumsum(nblk)
    start = cum - nblk
    n_used = cum[-1]
    i = jnp.arange(n_blocks, dtype=jnp.int32)
    ii = jnp.maximum(jnp.minimum(i, n_used - 1), 0)
    e = jnp.minimum(jnp.sum((cum[None, :] <= ii[:, None]).astype(jnp.int32), axis=1), N_EXPERTS - 1)
    b = ii - start[e]
    blk_row = (e * cap_blocks + b).astype(jnp.int32)
    valid = jnp.where(i < n_used, jnp.minimum(E_BLK, counts[e] - b * E_BLK), 0).astype(jnp.int32)
    return blk_row, e, valid, n_used.astype(jnp.int32).reshape(1)


def kernel(x_prompt, x_sample, c_prompt, c_sample, state_s5_re, state_s5_im, state_conv, w_ada, b_ada, w_in,
           s5_lam_re, s5_lam_im, s5_log_dt, s5_b_re, s5_b_im, s5_c_re, s5_c_im, s5_d, w_glu, b_glu, conv_w,
           w_out, ln1_g, ln1_b, w_router, b_router, w_gu, b_gu, w_down, b_down, ln2_g, ln2_b):
    assert DEPTH == 1 and w_ada.shape[0] == 1
    n_p, l_p, _ = x_prompt.shape
    n_s, l_s, _ = x_sample.shape
    t_p, t_s = n_p * l_p, n_s * l_s
    n_tok = t_p + t_s
    tiles_p, tiles_s = t_p // TL, t_s // TL
    seq_per_tile = TL // l_s
    assert l_p % TL == 0 and TL % l_s == 0 and n_s % seq_per_tile == 0

    c_all = jnp.concatenate([c_prompt, c_sample], axis=0).astype(F32)
    pad = (-c_all.shape[0]) % SUBLANES
    c_all = jnp.pad(c_all, ((0, pad), (0, 0)))
    mod = _ada_call(c_all, w_ada[0], b_ada[0].reshape(1, -1)).reshape(-1, 6, D_MODEL)
    mod_p, mod_s = mod[:n_p], mod[n_p:n_p + n_s]

    bmat, cmat, tab = _s5_tables(s5_lam_re[0], s5_lam_im[0], s5_log_dt[0], s5_b_re[0], s5_b_im[0],
                                 s5_c_re[0], s5_c_im[0])
    wr = jnp.pad(w_router[0], ((0, 0), (0, LANES - N_EXPERTS))).astype(BF16)
    br = jnp.pad(b_router[0].astype(F32), (0, LANES - N_EXPERTS), constant_values=NEG_BIG).reshape(1, LANES)
    weights = (w_in[0].astype(BF16), bmat, cmat, tab, s5_d[0].reshape(1, D_S5).astype(F32),
               w_glu[0].astype(BF16), b_glu[0].reshape(1, D_S5).astype(F32), conv_w[0].astype(F32),
               w_out[0].astype(BF16), ln1_g[0].reshape(1, D_MODEL).astype(F32),
               ln1_b[0].reshape(1, D_MODEL).astype(F32), wr, br)

    h0_p = jnp.zeros((n_p, N_SLAB, SLAB_W), F32)
    cb_p = jnp.zeros((n_p, CONV_W - 1, D_CONV), F32)
    h0_s = _state_to_slab(state_s5_re[0], state_s5_im[0])
    cb_s = state_conv[0].astype(F32)

    assert n_tok % E_BLK == 0 and E_BLK % TL == 0
    cnt0 = jnp.zeros((1, LANES), F32)
    outs_p = _mixer_call(x_prompt, mod_p, h0_p, cb_p, weights, cnt0, 1, TL, l_p // TL, 0, n_tok, ())
    x1, gates, xs, cnt_p, s_p, conv_p = outs_p
    xs3 = x_sample.reshape(n_s // seq_per_tile, TL, D_MODEL)
    outs_s = _mixer_call(xs3, mod_s, h0_s, cb_s, weights, cnt_p, seq_per_tile, l_s, 1, tiles_p, n_tok,
                         (x1, gates, xs))
    x1, gates, xs, cnt, s_s, conv_s = outs_s

    counts = cnt[0, :N_EXPERTS].astype(jnp.int32)
    blk_row, blk_e, blk_valid, n_used = _block_table(counts, n_tok)
    yk = _expert_call(blk_row, blk_e, blk_valid, n_used, xs, w_gu[0].astype(F32),
                      b_gu[0].reshape(N_EXPERTS, 1, 2 * D_FF).astype(F32), w_down[0].astype(F32),
                      b_down[0].reshape(N_EXPERTS, 1, D_MODEL).astype(F32), n_tok)
    g2 = ln2_g[0].reshape(1, D_MODEL).astype(F32)
    b2 = ln2_b[0].reshape(1, D_MODEL).astype(F32)
    y_p = _combine_call(yk, x1, gates, mod_p, g2, b2, 1, TL, tiles_p, 0, l_p // TL, n_tok)
    y_s = _combine_call(yk, x1, gates, mod_s, g2, b2, seq_per_tile, l_s, tiles_s, tiles_p, 1, n_tok)

    p_re, p_im = _slab_to_state(s_p)
    s_re, s_im = _slab_to_state(s_s)
    return (y_p.reshape(n_p, l_p, D_MODEL), y_s.reshape(n_s, l_s, D_MODEL),
            p_re[None], p_im[None], conv_p[None], s_re[None], s_im[None], conv_s[None])
```

```python
import functools
import math

import jax
import jax.numpy as jnp
from jax import lax
from jax.experimental import pallas as pl
from jax.experimental.pallas import tpu as pltpu

F32 = jnp.float32
BF16 = jnp.bfloat16

D_MODEL = 1024
DEPTH = 1
D_S5 = 512
D_CONV = 512
S5_GROUP = 16
S5_GROUPS = 32
S5_STATE = 64
CONV_W = 3
N_EXPERTS = 32
TOP_K = 4
D_FF = 1024
SWIGLU_LIMIT = 7.0
SWIGLU_ALPHA = 1.702
LN_EPS = 1e-5
DEEPNORM_ALPHA = (2.0 * DEPTH) ** 0.25

TL = 256
SUBLANES = 8
LANES = 128
N_SLAB = 4
SLAB_W = 1024
HALF = 512
E_BLK = 512
ROW_TILE = D_MODEL // LANES
PACK_ROWS = HALF // LANES
HI16_MASK = -65536
DMA_THREADS = 2
NEG_BIG = -1e30
VMEM_LIMIT = 56 * 1024 * 1024


def _dot(a, b):
    return jnp.dot(a, b, preferred_element_type=F32)


def _layer_norm(x, g, b):
    mu = jnp.mean(x, axis=-1, keepdims=True)
    xc = x - mu
    var = jnp.mean(xc * xc, axis=-1, keepdims=True)
    return xc * lax.rsqrt(var + LN_EPS) * g + b


def _ada_kernel(c_ref, w_ref, b_ref, o_ref):
    c = c_ref[...]
    s = c * jax.nn.sigmoid(c)
    o_ref[...] = _dot(s.astype(BF16), w_ref[...].astype(BF16)) + b_ref[...]


def _ada_call(c_all, w_ada, b_ada):
    rows = c_all.shape[0]
    n_out = w_ada.shape[1]
    tn = 768
    return pl.pallas_call(
        _ada_kernel,
        grid=(n_out // tn,),
        in_specs=[pl.BlockSpec((rows, D_MODEL), lambda i: (0, 0)),
                  pl.BlockSpec((D_MODEL, tn), lambda i: (0, i)),
                  pl.BlockSpec((1, tn), lambda i: (0, i))],
        out_specs=pl.BlockSpec((rows, tn), lambda i: (0, i)),
        out_shape=jax.ShapeDtypeStruct((rows, n_out), F32),
        compiler_params=pltpu.CompilerParams(dimension_semantics=("arbitrary",),
                                             vmem_limit_bytes=VMEM_LIMIT),
        name="ada_mod",
    )(c_all, w_ada, b_ada)


def _mixer_kernel(nseg, seg, carry_tiles, n_alias, cap_rows, tile0,
                  x_ref, xn_ref, mod_ref, modn_ref, h0_ref, cbuf_ref, win_ref, bmat_ref, cmat_ref, tab_ref, d_ref,
                  wglu_ref, bglu_ref, convw_ref, wout_ref, g1_ref, b1_ref, wr_ref, br_ref, cnt0_ref, *rest):
    rest = rest[n_alias:]
    (x1_ref, gate_ref, xs_hbm, cnt_out_ref, sout_ref, cout_ref,
     p_ref, bu_ref, hs_ref, xcs_ref, carry_ref, h2s_ref, posv_ref, poss_ref, cnt_ref, sem_s, sem_v) = rest
    g_id = pl.program_id(0)
    j = pl.program_id(1)
    step = g_id * pl.num_programs(1) + j
    is_last = step == pl.num_programs(0) * pl.num_programs(1) - 1
    slot = step % 2

    def rows_of(m_ref, k):
        if nseg == 1:
            return m_ref[0, k:k + 1, :]
        return jnp.concatenate(
            [jnp.broadcast_to(m_ref[s, k:k + 1, :], (seg, D_MODEL)) for s in range(nseg)], axis=0)

    def mod_rows(k):
        return rows_of(mod_ref, k)

    def front_pieces(xt_ref, m_ref, dst):
        def adaln():
            h = xt_ref[0] * (1.0 + rows_of(m_ref, 1)) + rows_of(m_ref, 0)
            hs_ref[...] = h.astype(BF16)

        def in_proj(c):
            cols = slice(c * HALF, (c + 1) * HALF)
            p_ref[dst, :, cols] = _dot(hs_ref[...], win_ref[:, cols])

        def s5_in(i):
            u_i = p_ref[dst, :, i * LANES:(i + 1) * LANES]
            row_in_tile = lax.broadcasted_iota(jnp.int32, u_i.shape, 0) % SUBLANES
            u_prev = jnp.where(row_in_tile == 0, 0.0, pltpu.roll(u_i, 1, 0))
            lhs = jnp.concatenate([u_i, u_prev], axis=1).astype(BF16)
            bu_ref[dst * N_SLAB + i] = _dot(lhs, bmat_ref[i])

        return ([adaln] + [functools.partial(in_proj, c) for c in range(2 * D_MODEL // HALF)]
                + [functools.partial(s5_in, i) for i in range(N_SLAB)])

    @pl.when(step == 0)
    def _():
        for piece in front_pieces(x_ref, mod_ref, 0):
            piece()

    nxt_front = front_pieces(xn_ref, modn_ref, 1 - slot)

    def run_front(n):
        for _ in range(n):
            nxt_front.pop(0)()

    x = x_ref[0]
    bu0 = slot * N_SLAB

    prev = 1 - slot
    smem_copy = pltpu.make_async_copy(posv_ref, poss_ref, sem_v.at[0])

    @pl.when(step == 0)
    def _():
        k_i = lax.broadcasted_iota(jnp.int32, (SUBLANES, TL), 0)
        t_i = lax.broadcasted_iota(jnp.int32, (SUBLANES, TL), 1)
        posv_ref[...] = (N_EXPERTS * cap_rows + k_i * TL + t_i) * ROW_TILE
        smem_copy.start()
        h2s_ref[...] = jnp.zeros(h2s_ref.shape, F32)

    smem_copy.wait()

    def issue_rows(src_slot, t0, n_rows):
        for q in range(n_rows):
            t = t0 + q
            src = pl.multiple_of(t * ROW_TILE, ROW_TILE)
            for k in range(TOP_K):
                dst = pl.multiple_of(poss_ref[k, t], ROW_TILE)
                pltpu.make_async_copy(h2s_ref.at[src_slot, pl.ds(src, ROW_TILE), :],
                                      xs_hbm.at[pl.ds(dst, ROW_TILE), :],
                                      sem_s.at[src_slot]).start(priority=k % DMA_THREADS)

    def wait_scatter(sl):
        for _ in range(TOP_K):
            pltpu.make_async_copy(h2s_ref.at[sl], xs_hbm.at[pl.ds(0, TL * ROW_TILE), :], sem_s.at[sl]).wait()

    if carry_tiles:
        @pl.when(j == 0)
        def _():
            carry_ref[...] = h0_ref[0]

    for s in range(nseg):
        if carry_tiles:
            init = tuple(carry_ref[i:i + 1, :] for i in range(N_SLAB))
        else:
            init = tuple(h0_ref[s, i:i + 1, :] for i in range(N_SLAB))

        def scan_body(r, carry, s=s):
            issue_rows(prev, s * seg + r * (2 * SUBLANES), 2 * SUBLANES)
            new = []
            for i in range(N_SLAB):
                cr = carry[i][:, :HALF]
                ci = carry[i][:, HALF:]
                for half in range(2):
                    row0 = pl.multiple_of(s * seg + r * (2 * SUBLANES) + half * SUBLANES, SUBLANES)
                    blk = bu_ref[bu0 + i, pl.ds(row0, SUBLANES), :]
                    xr = blk[:, :HALF]
                    xi = blk[:, HALF:]
                    for k, d in enumerate((2, 4)):
                        lr = tab_ref[i, 2 * k]
                        li = tab_ref[i, 2 * k + 1]
                        rr = pltpu.roll(xr, d, 0)
                        ri = pltpu.roll(xi, d, 0)
                        xr, xi = xr + (lr * rr - li * ri), xi + (lr * ri + li * rr)
                    pr = tab_ref[i, 4]
                    pi_ = tab_ref[i, 5]
                    crb = jnp.broadcast_to(cr, (SUBLANES, HALF))
                    cib = jnp.broadcast_to(ci, (SUBLANES, HALF))
                    xr, xi = xr + (pr * crb - pi_ * cib), xi + (pr * cib + pi_ * crb)
                    bu_ref[bu0 + i, pl.ds(row0, SUBLANES), :] = jnp.concatenate([xr, xi], axis=1)
                    cr = xr[SUBLANES - 1:SUBLANES, :]
                    ci = xi[SUBLANES - 1:SUBLANES, :]
                new.append(jnp.concatenate([cr, ci], axis=1))
            return tuple(new)

        final = lax.fori_loop(0, seg // (2 * SUBLANES), scan_body, init)
        for i in range(N_SLAB):
            sout_ref[s, i:i + 1, :] = final[i]
            if carry_tiles:
                carry_ref[i:i + 1, :] = final[i]

    y = jnp.concatenate([_dot(bu_ref[bu0 + i].astype(BF16), cmat_ref[i]) for i in range(N_SLAB)], axis=1)
    y = y + d_ref[...] * p_ref[slot, :, :D_S5]
    z = jax.nn.gelu(y)
    z = z * jax.nn.sigmoid(_dot(z.astype(BF16), wglu_ref[...]) + bglu_ref[...])

    xc = p_ref[slot, :, D_S5 + D_CONV:D_S5 + 2 * D_CONV] * p_ref[slot, :, D_S5 + 2 * D_CONV:]
    w0 = convw_ref[0:1, :]
    w1 = convw_ref[1:2, :]
    w2 = convw_ref[2:3, :]
    conv_parts = []
    for s in range(nseg):
        base = s * (seg + SUBLANES)
        xc_s = xc[s * seg:(s + 1) * seg]
        tail = xc_s[seg - 2:seg]
        xcs_ref[pl.ds(base + SUBLANES, seg), :] = xc_s
        if carry_tiles:
            @pl.when(j == 0)
            def _(s=s, base=base):
                xcs_ref[pl.ds(base + SUBLANES - 2, 2), :] = cbuf_ref[s]
        else:
            xcs_ref[pl.ds(base + SUBLANES - 2, 2), :] = cbuf_ref[s]
        xm1 = xcs_ref[pl.ds(base + SUBLANES - 1, seg), :]
        xm2 = xcs_ref[pl.ds(base + SUBLANES - 2, seg), :]
        conv_parts.append(w0 * xm2 + w1 * xm1 + w2 * xc_s)
        cout_ref[s] = tail
        if carry_tiles:
            xcs_ref[pl.ds(base + SUBLANES - 2, 2), :] = tail
    conv = conv_parts[0] if nseg == 1 else jnp.concatenate(conv_parts, axis=0)
    y_b = p_ref[slot, :, D_S5:D_S5 + D_CONV] * conv

    run_front(2)
    mix = _dot(jnp.concatenate([z, y_b], axis=1).astype(BF16), wout_ref[...])
    x1 = _layer_norm(DEEPNORM_ALPHA * x + (1.0 + mod_rows(2)) * mix, g1_ref[...], b1_ref[...])
    x1_ref[...] = x1
    run_front(1)
    h2 = x1 * (1.0 + mod_rows(4)) + mod_rows(3)
    lo_bits = lax.bitcast_convert_type(h2[:, :HALF].astype(BF16).astype(F32), jnp.int32)
    hi_bits = lax.bitcast_convert_type(h2[:, HALF:].astype(BF16).astype(F32), jnp.int32)
    packed = lax.bitcast_convert_type(hi_bits | lax.shift_right_logical(lo_bits, 16), F32)
    for c in range(PACK_ROWS):
        h2s_ref[slot, pl.ds(c, TL, stride=ROW_TILE), :] = packed[:, c * LANES:(c + 1) * LANES]

    logits = _dot(h2.astype(BF16), wr_ref[...]) + br_ref[...]
    lane = lax.broadcasted_iota(jnp.int32, logits.shape, 1)
    lane_f = lane.astype(F32)
    vals, sels, hots = [], [], []
    cur = logits
    for _ in range(TOP_K):
        m = jnp.max(cur, axis=-1, keepdims=True)
        am = jnp.min(jnp.where(cur == m, lane_f, float(LANES)), axis=-1, keepdims=True)
        hot = lane_f == am
        vals.append(m)
        sels.append(am)
        hots.append(hot)
        cur = jnp.where(hot, -jnp.inf, cur)
    exps = [jnp.exp(v - vals[0]) for v in vals]
    inv = 1.0 / (exps[0] + exps[1] + exps[2] + exps[3])
    gate_out = jnp.zeros(logits.shape, F32)
    for k in range(TOP_K):
        gate_out = jnp.where(lane == k, exps[k] * inv, gate_out)
    gate_ref[...] = gate_out
    run_front(1)

    tok = ((tile0 + step) * TL + lax.broadcasted_iota(jnp.int32, logits.shape, 0)).astype(F32)
    record = jnp.zeros(logits.shape, F32)
    for k in range(TOP_K):
        record = jnp.where(lane == k, sels[k], record)
        record = jnp.where(lane == TOP_K + k, tok + float(k * cap_rows), record)
    h2s_ref[slot, pl.ds(PACK_ROWS, TL, stride=ROW_TILE), :] = record
    run_front(2)

    @pl.when(step == 0)
    def _():
        cnt_ref[...] = cnt0_ref[...]

    chosen = jnp.zeros(logits.shape, F32)
    for k in range(TOP_K):
        chosen = jnp.where(hots[k], 1.0, chosen)
    r_i = lax.broadcasted_iota(jnp.int32, (TL, TL), 0)
    c_i = lax.broadcasted_iota(jnp.int32, (TL, TL), 1)
    before = jnp.where(c_i < r_i, 1.0, 0.0).astype(BF16)
    rank_base = _dot(before, chosen.astype(BF16)) + cnt_ref[...]
    pos_mat = jnp.zeros(logits.shape, F32)
    for k in range(TOP_K):
        rank_k = jnp.sum(jnp.where(hots[k], rank_base, 0.0), axis=-1, keepdims=True)
        pos_mat = jnp.where(lane == k, (sels[k] * float(cap_rows) + rank_k) * float(ROW_TILE), pos_mat)
    cnt_ref[...] = cnt_ref[...] + jnp.sum(chosen, axis=0, keepdims=True)
    cnt_out_ref[...] = cnt_ref[...]
    run_front(len(nxt_front))
    pos_t =pos_mat.T[:SUBLANES, :].astype(jnp.int32)
    posv_ref[...] = pos_t
    smem_copy.start()
    wait_scatter(prev)

    @pl.when(is_last)
    def _():
        smem_copy.wait()

        def tail_body(t8, c):
            issue_rows(slot, t8 * SUBLANES, SUBLANES)
            return c
        lax.fori_loop(0, TL // SUBLANES, tail_body, 0)
        wait_scatter(slot)


def _mixer_call(x3, mod, h0, cbuf, weights, cnt0, nseg, seg, tiles_per_group, tile0, n_tok, aliased):
    groups = x3.shape[0]
    nseq = mod.shape[0]
    carry_tiles = nseg == 1
    n_alias = len(aliased)
    n_tiles = n_tok // TL
    cap_rows = n_tok

    def full(a):
        nd = a.ndim
        return pl.BlockSpec(a.shape, lambda g, j, nd=nd: (0,) * nd)

    def tile_map(g, j):
        return (tile0 + g * tiles_per_group + j, 0)

    def next_g(g, j):
        return jnp.where(j + 1 < tiles_per_group, g, jnp.minimum(g + 1, groups - 1))

    def next_j(g, j):
        return jnp.where(j + 1 < tiles_per_group, j + 1, 0)

    in_specs = [pl.BlockSpec((1, TL, D_MODEL), lambda g, j: (g, j, 0)),
                pl.BlockSpec((1, TL, D_MODEL), lambda g, j: (next_g(g, j), next_j(g, j), 0)),
                pl.BlockSpec((nseg, 6, D_MODEL), lambda g, j: (g, 0, 0)),
                pl.BlockSpec((nseg, 6, D_MODEL), lambda g, j: (next_g(g, j), 0, 0)),
                pl.BlockSpec((nseg, N_SLAB, SLAB_W), lambda g, j: (g, 0, 0)),
                pl.BlockSpec((nseg, CONV_W - 1, D_CONV), lambda g, j: (g, 0, 0))]
    in_specs += [full(w) for w in weights]
    in_specs += [full(cnt0)]
    in_specs += [pl.BlockSpec(memory_space=pl.ANY)] * n_alias
    out_shape = [jax.ShapeDtypeStruct((n_tok, D_MODEL), F32),
                 jax.ShapeDtypeStruct((n_tok, LANES), F32),
                 jax.ShapeDtypeStruct(((N_EXPERTS * cap_rows + TOP_K * TL) * ROW_TILE, LANES), F32),
                 jax.ShapeDtypeStruct((1, LANES), F32),
                 jax.ShapeDtypeStruct((nseq, N_SLAB, SLAB_W), F32),
                 jax.ShapeDtypeStruct((nseq, CONV_W - 1, D_CONV), F32)]
    out_specs = [pl.BlockSpec((TL, D_MODEL), tile_map),
                 pl.BlockSpec((TL, LANES), tile_map),
                 pl.BlockSpec(memory_space=pl.ANY),
                 pl.BlockSpec((1, LANES), lambda g, j: (0, 0)),
                 pl.BlockSpec((nseg, N_SLAB, SLAB_W), lambda g, j: (g, 0, 0)),
                 pl.BlockSpec((nseg, CONV_W - 1, D_CONV), lambda g, j: (g, 0, 0))]
    n_in = 6 + len(weights) + 1
    aliases = {n_in + k: k for k in range(n_alias)}
    scratch = [pltpu.VMEM((2, TL, 2 * D_MODEL), F32),
               pltpu.VMEM((2 * N_SLAB, TL, SLAB_W), F32),
               pltpu.VMEM((TL, D_MODEL), BF16),
               pltpu.VMEM((nseg * (seg + SUBLANES), D_CONV), F32),
               pltpu.VMEM((N_SLAB, SLAB_W), F32),
               pltpu.VMEM((2, TL * ROW_TILE, LANES), F32),
               pltpu.VMEM((SUBLANES, TL), jnp.int32),
               pltpu.SMEM((SUBLANES, TL), jnp.int32),
               pltpu.VMEM((1, LANES), F32),
               pltpu.SemaphoreType.DMA((2,)),
               pltpu.SemaphoreType.DMA((1,))]
    return pl.pallas_call(
        functools.partial(_mixer_kernel, nseg, seg, carry_tiles, n_alias, cap_rows, tile0),
        grid=(groups, tiles_per_group),
        in_specs=in_specs, out_specs=out_specs, out_shape=out_shape,
        scratch_shapes=scratch,
        input_output_aliases=aliases,
        compiler_params=pltpu.CompilerParams(dimension_semantics=("arbitrary", "arbitrary"),
                                             vmem_limit_bytes=VMEM_LIMIT),
        name="mixer_prompt" if carry_tiles else "mixer_sample",
    )(x3, x3, mod, mod, h0, cbuf, *weights, cnt0, *aliased)


W_CAST_ROWS = 128

def _expert_kernel(n_tok, row_ref, be_ref, nv_ref, nu_ref, xs_ref, wgu_ref, bgu_ref, wd_ref, bd_ref,
                   yk_hbm, wgu_bf, wd_bf, out_a, out_b, slotv_ref, slots_ref, sem_o, sem_v):
    i = pl.program_id(0)
    n_used = nu_ref[0]
    outs = (out_a, out_b)
    prev = jnp.maximum(i - 1, 0)
    new_expert = jnp.logical_or(i == 0, be_ref[i] != be_ref[prev])

    @pl.when(jnp.logical_and(new_expert, i < n_used))
    def _():
        def cast_gu(r, c):
            rows = pl.ds(pl.multiple_of(r * W_CAST_ROWS, W_CAST_ROWS), W_CAST_ROWS)
            wgu_bf[rows, :] = wgu_ref[0, rows, :].astype(BF16)
            return c
        lax.fori_loop(0, D_MODEL // W_CAST_ROWS, cast_gu, 0)

        def cast_d(r, c):
            rows = pl.ds(pl.multiple_of(r * W_CAST_ROWS, W_CAST_ROWS), W_CAST_ROWS)
            wd_bf[rows, :] = wd_ref[0, rows, :].astype(BF16)
            return c
        lax.fori_loop(0, D_FF // W_CAST_ROWS, cast_d, 0)

    def issue_rows(src, t0, count):
        for q in range(count):
            t = t0 + q
            dst = pl.multiple_of(slots_ref[src, 0, t], PACK_ROWS)
            row0 = pl.multiple_of(t * PACK_ROWS, PACK_ROWS)
            pltpu.make_async_copy(outs[src].at[pl.ds(row0, PACK_ROWS), :],
                                  yk_hbm.at[pl.ds(dst, PACK_ROWS), :],
                                  sem_o.at[src]).start(priority=q % DMA_THREADS)

    def wait_rows(sl):
        pltpu.make_async_copy(outs[sl], yk_hbm.at[pl.ds(0, E_BLK * PACK_ROWS), :], sem_o.at[sl]).wait()

    lane = lax.broadcasted_iota(jnp.int32, (E_BLK, LANES), 1)
    row = lax.broadcasted_iota(jnp.int32, (E_BLK, 1), 0)
    spare0 = float(TOP_K * n_tok)

    @pl.when(i == 0)
    def _():
        t_i = lax.broadcasted_iota(jnp.int32, (SUBLANES, E_BLK), 1)
        slotv_ref[...] = (TOP_K * n_tok + t_i) * PACK_ROWS
        pltpu.make_async_copy(slotv_ref, slots_ref.at[1], sem_v.at[0]).start()
        out_b[...] = jnp.zeros(out_b.shape, F32)

    def block(cur):
        prv = 1 - cur

        def smem_copy(dst):
            return pltpu.make_async_copy(slotv_ref, slots_ref.at[dst], sem_v.at[0])

        @pl.when(i < n_used)
        def _():
            smem_copy(prv).wait()
            record = xs_ref[pl.ds(PACK_ROWS, E_BLK, stride=ROW_TILE), :]
            hit = jnp.logical_and(record.astype(jnp.int32) == be_ref[i], lane < TOP_K)
            slot_f = jnp.sum(jnp.where(hit, pltpu.roll(record, LANES - TOP_K, 1), 0.0), axis=-1, keepdims=True)
            slot_f = jnp.where(row < nv_ref[i], slot_f, spare0 + row.astype(F32))
            slot_mat = jnp.where(lane == 0, slot_f * float(PACK_ROWS), 0.0)
            slotv_ref[...] = slot_mat.T[:SUBLANES, :].astype(jnp.int32)
            smem_copy(cur).start()

        @pl.when(i < n_used)
        def _():
            live = row < nv_ref[i]
            words = [lax.bitcast_convert_type(xs_ref[pl.ds(c, E_BLK, stride=ROW_TILE), :], jnp.int32)
                     for c in range(PACK_ROWS)]
            lo = [lax.bitcast_convert_type(lax.shift_left(w, 16), F32) for w in words]
            hi = [lax.bitcast_convert_type(w & HI16_MASK, F32) for w in words]
            xb = jnp.where(live, jnp.concatenate(lo + hi, axis=1), 0.0).astype(BF16)

            issue_rows(prv, 0, E_BLK)
            hgu = _dot(xb, wgu_bf[...]) + bgu_ref[0]
            g = jnp.minimum(hgu[:, :D_FF], SWIGLU_LIMIT)
            up = jnp.clip(hgu[:, D_FF:], -SWIGLU_LIMIT, SWIGLU_LIMIT)
            act = (up + 1.0) * (g * jax.nn.sigmoid(SWIGLU_ALPHA * g))
            y = _dot(act.astype(BF16), wd_bf[...]) + bd_ref[0]

            @pl.when(i > 0)
            def _():
                wait_rows(cur)

            y_lo = lax.bitcast_convert_type(y[:, :HALF].astype(BF16).astype(F32), jnp.int32)
            y_hi = lax.bitcast_convert_type(y[:, HALF:].astype(BF16).astype(F32), jnp.int32)
            y_pk = lax.bitcast_convert_type(y_hi | lax.shift_right_logical(y_lo, 16), F32)
            for c in range(PACK_ROWS):
                outs[cur][pl.ds(c, E_BLK, stride=PACK_ROWS), :] = y_pk[:, c * LANES:(c + 1) * LANES]

        @pl.when(i == n_used)
        def _():
            smem_copy(prv).wait()

            def tail(t8, c):
                issue_rows(prv, t8 * SUBLANES, SUBLANES)
                return c
            lax.fori_loop(0, E_BLK // SUBLANES, tail, 0)
            wait_rows(cur)
            wait_rows(prv)

    for parity in range(2):
        @pl.when(i % 2 == parity)
        def _(parity=parity):
            block(parity)


def _expert_call(blk_row, blk_e, blk_valid, n_used, xs, w_gu, b_gu, w_down, b_down, n_tok):
    nb = blk_row.shape[0]
    grid_spec = pltpu.PrefetchScalarGridSpec(
        num_scalar_prefetch=4,
        grid=(nb,),
        in_specs=[
            pl.BlockSpec((E_BLK * ROW_TILE, LANES), lambda i, br, be, nv, nu: (br[i], 0)),
            pl.BlockSpec((1, D_MODEL, 2 * D_FF), lambda i, br, be, nv, nu: (be[i], 0, 0)),
            pl.BlockSpec((1, 1, 2 * D_FF), lambda i, br, be, nv, nu: (be[i], 0, 0)),
            pl.BlockSpec((1, D_FF, D_MODEL), lambda i, br, be, nv, nu: (be[i], 0, 0)),
            pl.BlockSpec((1, 1, D_MODEL), lambda i, br, be, nv, nu: (be[i], 0, 0)),
        ],
        out_specs=pl.BlockSpec(memory_space=pl.ANY),
        scratch_shapes=[pltpu.VMEM((D_MODEL, 2 * D_FF), BF16),
                        pltpu.VMEM((D_FF, D_MODEL), BF16),
                        pltpu.VMEM((E_BLK * PACK_ROWS, LANES), F32),
                        pltpu.VMEM((E_BLK * PACK_ROWS, LANES), F32),
                        pltpu.VMEM((SUBLANES, E_BLK), jnp.int32),
                        pltpu.SMEM((2, SUBLANES, E_BLK), jnp.int32),
                        pltpu.SemaphoreType.DMA((2,)),
                        pltpu.SemaphoreType.DMA((1,))],
    )
    return pl.pallas_call(
        functools.partial(_expert_kernel, n_tok),
        grid_spec=grid_spec,
        out_shape=jax.ShapeDtypeStruct(((TOP_K * n_tok + E_BLK) * PACK_ROWS, LANES), F32),
        compiler_params=pltpu.CompilerParams(dimension_semantics=("arbitrary",),
                                             vmem_limit_bytes=VMEM_LIMIT),
        name="experts",
    )(blk_row, blk_e, blk_valid, n_used, xs, w_gu, b_gu, w_down, b_down)


def _combine_kernel(nseg, seg, y0_ref, y1_ref, y2_ref, y3_ref, x1_ref, gate_ref, mod_ref, g2_ref, b2_ref,
                    out_ref):
    gates = gate_ref[...]
    ffn = None
    for k, y_ref in enumerate((y0_ref, y1_ref, y2_ref, y3_ref)):
        words = [lax.bitcast_convert_type(y_ref[pl.ds(c, TL, stride=PACK_ROWS), :], jnp.int32)
                 for c in range(PACK_ROWS)]
        rows = jnp.concatenate(
            [lax.bitcast_convert_type(lax.shift_left(w, 16), F32) for w in words]
            + [lax.bitcast_convert_type(w & HI16_MASK, F32) for w in words], axis=1)
        term = gates[:, k:k + 1] * rows
        ffn = term if ffn is None else ffn + term
    if nseg == 1:
        gate2 = mod_ref[0, 5:6, :]
    else:
        gate2 = jnp.concatenate(
            [jnp.broadcast_to(mod_ref[s, 5:6, :], (seg, D_MODEL)) for s in range(nseg)], axis=0)
    xa = DEEPNORM_ALPHA * x1_ref[...] + (1.0 + gate2) * ffn
    out_ref[...] = _layer_norm(xa, g2_ref[...], b2_ref[...])


def _combine_call(yk, x1, gates, mod, ln2_g, ln2_b, nseg, seg, n_tiles, tile0, tiles_per_group, n_tok):
    tiles_total = n_tok // TL

    def yk_spec(k):
        return pl.BlockSpec((TL * PACK_ROWS, LANES), lambda i, k=k: (k * tiles_total + tile0 + i, 0))

    in_specs = [yk_spec(k) for k in range(TOP_K)] + [
        pl.BlockSpec((TL, D_MODEL), lambda i: (tile0 + i, 0)),
        pl.BlockSpec((TL, LANES), lambda i: (tile0 + i, 0)),
        pl.BlockSpec((nseg, 6, D_MODEL), lambda i: (i // tiles_per_group, 0, 0)),
        pl.BlockSpec((1, D_MODEL), lambda i: (0, 0)),
        pl.BlockSpec((1, D_MODEL), lambda i: (0, 0)),
    ]
    return pl.pallas_call(
        functools.partial(_combine_kernel, nseg, seg),
        grid=(n_tiles,),
        in_specs=in_specs,
        out_specs=pl.BlockSpec((TL, D_MODEL), lambda i: (i, 0)),
        out_shape=jax.ShapeDtypeStruct((n_tiles * TL, D_MODEL), F32),
        compiler_params=pltpu.CompilerParams(dimension_semantics=("arbitrary",),
                                             vmem_limit_bytes=VMEM_LIMIT),
        name="combine_prompt" if nseg == 1 else "combine_sample",
    )(yk, yk, yk, yk, x1, gates, mod, ln2_g, ln2_b)


def _s5_tables(lam_re, lam_im, log_dt, b_re, b_im, c_re, c_im):
    dt = jnp.exp(log_dt.astype(F32))[:, None]
    lam = lax.complex(lam_re.astype(F32), lam_im.astype(F32))
    lam_dt = lam * dt
    lam_bar = jnp.exp(lam_dt)
    b_bar = ((lam_bar - 1.0) / lam)[..., None] * lax.complex(b_re.astype(F32), b_im.astype(F32))
    gl = S5_GROUPS // N_SLAB
    eye = jnp.eye(gl, dtype=F32)

    def b_slab(part):
        a = part.reshape(N_SLAB, gl, S5_STATE, S5_GROUP)
        return jnp.einsum('sgph,gk->sghkp', a, eye).reshape(N_SLAB, gl * S5_GROUP, gl * S5_STATE)

    def b_rows(b):
        return jnp.concatenate([b_slab(b.real), b_slab(b.imag)], axis=-1)

    bmat = jnp.concatenate([b_rows(b_bar), b_rows(lam_bar[..., None] * b_bar)], axis=1).astype(BF16)

    def c_slab(part):
        a = part.reshape(N_SLAB, gl, S5_GROUP, S5_STATE)
        return jnp.einsum('sghp,gk->sgpkh', a, eye).reshape(N_SLAB, gl * S5_STATE, gl * S5_GROUP)

    cmat = jnp.concatenate([c_slab(c_re.astype(F32)), -c_slab(c_im.astype(F32))], axis=1).astype(BF16)

    row = jnp.arange(SUBLANES, dtype=F32)[:, None, None]

    def power(k):
        return jnp.exp(lam_dt[None] * k)

    tabs = []
    for d in (2, 4):
        pw = power(jnp.full_like(row, float(d)))
        mask = (row >= d).astype(F32)
        tabs += [pw.real * mask, pw.imag * mask]
    pw = power(row + 1.0)
    tabs += [pw.real, pw.imag]
    tab = jnp.stack(tabs, axis=0)
    tab = tab.reshape(len(tabs), SUBLANES, N_SLAB, gl * S5_STATE).transpose(2, 0, 1, 3)
    return bmat, cmat, tab


def _state_to_slab(re, im):
    n = re.shape[0]
    return jnp.concatenate([re.reshape(n, N_SLAB, HALF), im.reshape(n, N_SLAB, HALF)], axis=-1).astype(F32)


def _slab_to_state(s):
    n = s.shape[0]
    re = s[:, :, :HALF].reshape(n, S5_GROUPS, S5_STATE)
    im = s[:, :, HALF:].reshape(n, S5_GROUPS, S5_STATE)
    return re, im


def _block_table(counts, n_tok):
    n_assign = n_tok * TOP_K
    n_blocks = (n_assign + N_EXPERTS * (E_BLK - 1) + E_BLK - 1) // E_BLK + 1
    cap_blocks = n_tok // E_BLK
    nblk = (counts + E_BLK - 1) // E_BLK
    cum = jnp.cumsum(nblk)
    start = cum - nblk
    n_used = cum[-1]
    i = jnp.arange(n_blocks, dtype=jnp.int32)
    ii = jnp.maximum(jnp.minimum(i, n_used - 1), 0)
    e = jnp.minimum(jnp.sum((cum[None, :] <= ii[:, None]).astype(jnp.int32), axis=1), N_EXPERTS - 1)
    b = ii - start[e]
    blk_row = (e * cap_blocks + b).astype(jnp.int32)
    valid = jnp.where(i < n_used, jnp.minimum(E_BLK, counts[e] - b * E_BLK), 0).astype(jnp.int32)
    return blk_row, e, valid, n_used.astype(jnp.int32).reshape(1)


def kernel(x_prompt, x_sample, c_prompt, c_sample, state_s5_re, state_s5_im, state_conv, w_ada, b_ada, w_in,
           s5_lam_re, s5_lam_im, s5_log_dt, s5_b_re, s5_b_im, s5_c_re, s5_c_im, s5_d, w_glu, b_glu, conv_w,
           w_out, ln1_g, ln1_b, w_router, b_router, w_gu, b_gu, w_down, b_down, ln2_g, ln2_b):
    assert DEPTH == 1 and w_ada.shape[0] == 1
    n_p, l_p, _ = x_prompt.shape
    n_s, l_s, _ = x_sample.shape
    t_p, t_s = n_p * l_p, n_s * l_s
    n_tok = t_p + t_s
    tiles_p, tiles_s = t_p // TL, t_s // TL
    seq_per_tile = TL // l_s
    assert l_p % TL == 0 and TL % l_s == 0 and n_s % seq_per_tile == 0

    c_all = jnp.concatenate([c_prompt, c_sample], axis=0).astype(F32)
    pad = (-c_all.shape[0]) % SUBLANES
    c_all = jnp.pad(c_all, ((0, pad), (0, 0)))
    mod = _ada_call(c_all, w_ada[0], b_ada[0].reshape(1, -1)).reshape(-1, 6, D_MODEL)
    mod_p, mod_s = mod[:n_p], mod[n_p:n_p + n_s]

    bmat, cmat, tab = _s5_tables(s5_lam_re[0], s5_lam_im[0], s5_log_dt[0], s5_b_re[0], s5_b_im[0],
                                 s5_c_re[0], s5_c_im[0])
    wr = jnp.pad(w_router[0], ((0, 0), (0, LANES - N_EXPERTS))).astype(BF16)
    br = jnp.pad(b_router[0].astype(F32), (0, LANES - N_EXPERTS), constant_values=NEG_BIG).reshape(1, LANES)
    weights = (w_in[0].astype(BF16), bmat, cmat, tab, s5_d[0].reshape(1, D_S5).astype(F32),
               w_glu[0].astype(BF16), b_glu[0].reshape(1, D_S5).astype(F32), conv_w[0].astype(F32),
               w_out[0].astype(BF16), ln1_g[0].reshape(1, D_MODEL).astype(F32),
               ln1_b[0].reshape(1, D_MODEL).astype(F32), wr, br)

    h0_p = jnp.zeros((n_p, N_SLAB, SLAB_W), F32)
    cb_p = jnp.zeros((n_p, CONV_W - 1, D_CONV), F32)
    h0_s = _state_to_slab(state_s5_re[0], state_s5_im[0])
    cb_s = state_conv[0].astype(F32)

    assert n_tok % E_BLK == 0 and E_BLK % TL == 0
    cnt0 = jnp.zeros((1, LANES), F32)
    outs_p = _mixer_call(x_prompt, mod_p, h0_p, cb_p, weights, cnt0, 1, TL, l_p // TL, 0, n_tok, ())
    x1, gates, xs, cnt_p, s_p, conv_p = outs_p
    xs3 = x_sample.reshape(n_s // seq_per_tile, TL, D_MODEL)
    outs_s = _mixer_call(xs3, mod_s, h0_s, cb_s, weights, cnt_p, seq_per_tile, l_s, 1, tiles_p, n_tok,
                         (x1, gates, xs))
    x1, gates, xs, cnt, s_s, conv_s = outs_s

    counts = cnt[0, :N_EXPERTS].astype(jnp.int32)
    blk_row, blk_e, blk_valid, n_used = _block_table(counts, n_tok)
    yk = _expert_call(blk_row, blk_e, blk_valid, n_used, xs, w_gu[0].astype(F32),
                      b_gu[0].reshape(N_EXPERTS, 1, 2 * D_FF).astype(F32), w_down[0].astype(F32),
                      b_down[0].reshape(N_EXPERTS, 1, D_MODEL).astype(F32), n_tok)
    g2 = ln2_g[0].reshape(1, D_MODEL).astype(F32)
    b2 = ln2_b[0].reshape(1, D_MODEL).astype(F32)
    y_p = _combine_call(yk, x1, gates, mod_p, g2, b2, 1, TL, tiles_p, 0, l_p // TL, n_tok)
    y_s = _combine_call(yk, x1, gates, mod_s, g2, b2, seq_per_tile, l_s, tiles_s, tiles_p, 1, n_tok)

    p_re, p_im = _slab_to_state(s_p)
    s_re, s_im = _slab_to_state(s_s)
    return (y_p.reshape(n_p, l_p, D_MODEL), y_s.reshape(n_s, l_s, D_MODEL),
            p_re[None], p_im[None], conv_p[None], s_re[None], s_im[None], conv_s[None])
```

```python
import functools
import math

import jax
import jax.numpy as jnp
from jax import lax
from jax.experimental import pallas as pl
from jax.experimental.pallas import tpu as pltpu

F32 = jnp.float32
BF16 = jnp.bfloat16

D_MODEL = 1024
DEPTH = 1
D_S5 = 512
D_CONV = 512
S5_GROUP = 16
S5_GROUPS = 32
S5_STATE = 64
CONV_W = 3
N_EXPERTS = 32
TOP_K = 4
D_FF = 1024
SWIGLU_LIMIT = 7.0
SWIGLU_ALPHA = 1.702
LN_EPS = 1e-5
DEEPNORM_ALPHA = (2.0 * DEPTH) ** 0.25

TL = 256
SUBLANES = 8
LANES = 128
N_SLAB = 4
SLAB_W = 1024
HALF = 512
E_BLK = 512
ROW_TILE = D_MODEL // LANES
PACK_ROWS = HALF // LANES
HI16_MASK = -65536
DMA_THREADS = 2
NEG_BIG = -1e30
VMEM_LIMIT = 56 * 1024 * 1024


def _dot(a, b):
    return jnp.dot(a, b, preferred_element_type=F32)


def _layer_norm(x, g, b):
    mu = jnp.mean(x, axis=-1, keepdims=True)
    xc = x - mu
    var = jnp.mean(xc * xc, axis=-1, keepdims=True)
    return xc * lax.rsqrt(var + LN_EPS) * g + b


def _ada_kernel(c_ref, w_ref, b_ref, o_ref):
    c = c_ref[...]
    s = c * jax.nn.sigmoid(c)
    o_ref[...] = _dot(s.astype(BF16), w_ref[...].astype(BF16)) + b_ref[...]


def _ada_call(c_all, w_ada, b_ada):
    rows = c_all.shape[0]
    n_out = w_ada.shape[1]
    tn = 768
    return pl.pallas_call(
        _ada_kernel,
        grid=(n_out // tn,),
        in_specs=[pl.BlockSpec((rows, D_MODEL), lambda i: (0, 0)),
                  pl.BlockSpec((D_MODEL, tn), lambda i: (0, i)),
                  pl.BlockSpec((1, tn), lambda i: (0, i))],
        out_specs=pl.BlockSpec((rows, tn), lambda i: (0, i)),
        out_shape=jax.ShapeDtypeStruct((rows, n_out), F32),
        compiler_params=pltpu.CompilerParams(dimension_semantics=("arbitrary",),
                                             vmem_limit_bytes=VMEM_LIMIT),
        name="ada_mod",
    )(c_all, w_ada, b_ada)


def _mixer_kernel(nseg, seg, carry_tiles, n_alias, cap_rows, tile0,
                  x_ref, xn_ref, mod_ref, modn_ref, h0_ref, cbuf_ref, win_ref, bmat_ref, cmat_ref, tab_ref, d_ref,
                  wglu_ref, bglu_ref, convw_ref, wout_ref, g1_ref, b1_ref, wr_ref, br_ref, cnt0_ref, *rest):
    rest = rest[n_alias:]
    (x1_ref, gate_ref, xs_hbm, cnt_out_ref, sout_ref, cout_ref,
     p_ref, bu_ref, hs_ref, xcs_ref, carry_ref, h2s_ref, posv_ref, poss_ref, cnt_ref, sem_s, sem_v) = rest
    g_id = pl.program_id(0)
    j = pl.program_id(1)
    step = g_id * pl.num_programs(1) + j
    is_last = step == pl.num_programs(0) * pl.num_programs(1) - 1
    slot = step % 2

    def rows_of(m_ref, k):
        if nseg == 1:
            return m_ref[0, k:k + 1, :]
        return jnp.concatenate(
            [jnp.broadcast_to(m_ref[s, k:k + 1, :], (seg, D_MODEL)) for s in range(nseg)], axis=0)

    def mod_rows(k):
        return rows_of(mod_ref, k)

    def front_pieces(xt_ref, m_ref, dst):
        def adaln():
            h = xt_ref[0] * (1.0 + rows_of(m_ref, 1)) + rows_of(m_ref, 0)
            hs_ref[...] = h.astype(BF16)

        def in_proj(c):
            cols = slice(c * HALF, (c + 1) * HALF)
            p_ref[dst, :, cols] = _dot(hs_ref[...], win_ref[:, cols])

        def s5_in(i):
            u_i = p_ref[dst, :, i * LANES:(i + 1) * LANES]
            row_in_tile = lax.broadcasted_iota(jnp.int32, u_i.shape, 0) % SUBLANES
            u_prev = jnp.where(row_in_tile == 0, 0.0, pltpu.roll(u_i, 1, 0))
            lhs = jnp.concatenate([u_i, u_prev], axis=1).astype(BF16)
            bu_ref[dst * N_SLAB + i] = _dot(lhs, bmat_ref[i])

        return ([adaln] + [functools.partial(in_proj, c) for c in range(2 * D_MODEL // HALF)]
                + [functools.partial(s5_in, i) for i in range(N_SLAB)])

    @pl.when(step == 0)
    def _():
        for piece in front_pieces(x_ref, mod_ref, 0):
            piece()

    nxt_front = front_pieces(xn_ref, modn_ref, 1 - slot)

    def run_front(n):
        for _ in range(n):
            nxt_front.pop(0)()

    x = x_ref[0]
    bu0 = slot * N_SLAB

    prev = 1 - slot
    smem_copy = pltpu.make_async_copy(posv_ref, poss_ref, sem_v.at[0])

    @pl.when(step == 0)
    def _():
        k_i = lax.broadcasted_iota(jnp.int32, (SUBLANES, TL), 0)
        t_i = lax.broadcasted_iota(jnp.int32, (SUBLANES, TL), 1)
        posv_ref[...] = (N_EXPERTS * cap_rows + k_i * TL + t_i) * ROW_TILE
        smem_copy.start()
        h2s_ref[...] = jnp.zeros(h2s_ref.shape, F32)

    smem_copy.wait()

    def issue_rows(src_slot, t0, n_rows):
        for q in range(n_rows):
            t = t0 + q
            src = pl.multiple_of(t * ROW_TILE, ROW_TILE)
            for k in range(TOP_K):
                dst = pl.multiple_of(poss_ref[k, t], ROW_TILE)
                pltpu.make_async_copy(h2s_ref.at[src_slot, pl.ds(src, ROW_TILE), :],
                                      xs_hbm.at[pl.ds(dst, ROW_TILE), :],
                                      sem_s.at[src_slot]).start(priority=k % DMA_THREADS)

    def wait_scatter(sl):
        for _ in range(TOP_K):
            pltpu.make_async_copy(h2s_ref.at[sl], xs_hbm.at[pl.ds(0, TL * ROW_TILE), :], sem_s.at[sl]).wait()

    if carry_tiles:
        @pl.when(j == 0)
        def _():
            carry_ref[...] = h0_ref[0]

    for s in range(nseg):
        if carry_tiles:
            init = tuple(carry_ref[i:i + 1, :] for i in range(N_SLAB))
        else:
            init = tuple(h0_ref[s, i:i + 1, :] for i in range(N_SLAB))

        def scan_body(r, carry, s=s):
            issue_rows(prev, s * seg + r * (2 * SUBLANES), 2 * SUBLANES)
            new = []
            for i in range(N_SLAB):
                cr = carry[i][:, :HALF]
                ci = carry[i][:, HALF:]
                for half in range(2):
                    row0 = pl.multiple_of(s * seg + r * (2 * SUBLANES) + half * SUBLANES, SUBLANES)
                    blk = bu_ref[bu0 + i, pl.ds(row0, SUBLANES), :]
                    xr = blk[:, :HALF]
                    xi = blk[:, HALF:]
                    for k, d in enumerate((2, 4)):
                        lr = tab_ref[i, 2 * k]
                        li = tab_ref[i, 2 * k + 1]
                        rr = pltpu.roll(xr, d, 0)
                        ri = pltpu.roll(xi, d, 0)
                        xr, xi = xr + (lr * rr - li * ri), xi + (lr * ri + li * rr)
                    pr = tab_ref[i, 4]
                    pi_ = tab_ref[i, 5]
                    crb = jnp.broadcast_to(cr, (SUBLANES, HALF))
                    cib = jnp.broadcast_to(ci, (SUBLANES, HALF))
                    xr, xi = xr + (pr * crb - pi_ * cib), xi + (pr * cib + pi_ * crb)
                    bu_ref[bu0 + i, pl.ds(row0, SUBLANES), :] = jnp.concatenate([xr, xi], axis=1)
                    cr = xr[SUBLANES - 1:SUBLANES, :]
                    ci = xi[SUBLANES - 1:SUBLANES, :]
                new.append(jnp.concatenate([cr, ci], axis=1))
            return tuple(new)

        final = lax.fori_loop(0, seg // (2 * SUBLANES), scan_body, init)
        for i in range(N_SLAB):
            sout_ref[s, i:i + 1, :] = final[i]
            if carry_tiles:
                carry_ref[i:i + 1, :] = final[i]

    y = jnp.concatenate([_dot(bu_ref[bu0 + i].astype(BF16), cmat_ref[i]) for i in range(N_SLAB)], axis=1)
    y = y + d_ref[...] * p_ref[slot, :, :D_S5]
    z = jax.nn.gelu(y)
    z = z * jax.nn.sigmoid(_dot(z.astype(BF16), wglu_ref[...]) + bglu_ref[...])

    xc = p_ref[slot, :, D_S5 + D_CONV:D_S5 + 2 * D_CONV] * p_ref[slot, :, D_S5 + 2 * D_CONV:]
    w0 = convw_ref[0:1, :]
    w1 = convw_ref[1:2, :]
    w2 = convw_ref[2:3, :]
    conv_parts = []
    for s in range(nseg):
        base = s * (seg + SUBLANES)
        xc_s = xc[s * seg:(s + 1) * seg]
        tail = xc_s[seg - 2:seg]
        xcs_ref[pl.ds(base + SUBLANES, seg), :] = xc_s
        if carry_tiles:
            @pl.when(j == 0)
            def _(s=s, base=base):
                xcs_ref[pl.ds(base + SUBLANES - 2, 2), :] = cbuf_ref[s]
        else:
            xcs_ref[pl.ds(base + SUBLANES - 2, 2), :] = cbuf_ref[s]
        xm1 = xcs_ref[pl.ds(base + SUBLANES - 1, seg), :]
        xm2 = xcs_ref[pl.ds(base + SUBLANES - 2, seg), :]
        conv_parts.append(w0 * xm2 + w1 * xm1 + w2 * xc_s)
        cout_ref[s] = tail
        if carry_tiles:
            xcs_ref[pl.ds(base + SUBLANES - 2, 2), :] = tail
    conv = conv_parts[0] if nseg == 1 else jnp.concatenate(conv_parts, axis=0)
    y_b = p_ref[slot, :, D_S5:D_S5 + D_CONV] * conv

    run_front(2)
    mix = _dot(jnp.concatenate([z, y_b], axis=1).astype(BF16), wout_ref[...])
    x1 = _layer_norm(DEEPNORM_ALPHA * x + (1.0 + mod_rows(2)) * mix, g1_ref[...], b1_ref[...])
    x1_ref[...] = x1
    run_front(1)
    h2 = x1 * (1.0 + mod_rows(4)) + mod_rows(3)
    lo_bits = lax.bitcast_convert_type(h2[:, :HALF].astype(BF16).astype(F32), jnp.int32)
    hi_bits = lax.bitcast_convert_type(h2[:, HALF:].astype(BF16).astype(F32), jnp.int32)
    packed = lax.bitcast_convert_type(hi_bits | lax.shift_right_logical(lo_bits, 16), F32)
    for c in range(PACK_ROWS):
        h2s_ref[slot, pl.ds(c, TL, stride=ROW_TILE), :] = packed[:, c * LANES:(c + 1) * LANES]

    logits = _dot(h2.astype(BF16), wr_ref[...]) + br_ref[...]
    lane = lax.broadcasted_iota(jnp.int32, logits.shape, 1)
    lane_f = lane.astype(F32)
    vals, sels, hots = [], [], []
    cur = logits
    for _ in range(TOP_K):
        m = jnp.max(cur, axis=-1, keepdims=True)
        am = jnp.min(jnp.where(cur == m, lane_f, float(LANES)), axis=-1, keepdims=True)
        hot = lane_f == am
        vals.append(m)
        sels.append(am)
        hots.append(hot)
        cur = jnp.where(hot, -jnp.inf, cur)
    exps = [jnp.exp(v - vals[0]) for v in vals]
    inv = 1.0 / (exps[0] + exps[1] + exps[2] + exps[3])
    gate_out = jnp.zeros(logits.shape, F32)
    for k in range(TOP_K):
        gate_out = jnp.where(lane == k, exps[k] * inv, gate_out)
    gate_ref[...] = gate_out
    run_front(1)

    tok = ((tile0 + step) * TL + lax.broadcasted_iota(jnp.int32, logits.shape, 0)).astype(F32)
    record = jnp.zeros(logits.shape, F32)
    for k in range(TOP_K):
        record = jnp.where(lane == k, sels[k], record)
        record = jnp.where(lane == TOP_K + k, tok + float(k * cap_rows), record)
    h2s_ref[slot, pl.ds(PACK_ROWS, TL, stride=ROW_TILE), :] = record
    run_front(2)

    @pl.when(step == 0)
    def _():
        cnt_ref[...] = cnt0_ref[...]

    chosen = jnp.zeros(logits.shape, F32)
    for k in range(TOP_K):
        chosen = jnp.where(hots[k], 1.0, chosen)
    r_i = lax.broadcasted_iota(jnp.int32, (TL, TL), 0)
    c_i = lax.broadcasted_iota(jnp.int32, (TL, TL), 1)
    before = jnp.where(c_i < r_i, 1.0, 0.0).astype(BF16)
    rank_base = _dot(before, chosen.astype(BF16)) + cnt_ref[...]
    pos_mat = jnp.zeros(logits.shape, F32)
    for k in range(TOP_K):
        rank_k = jnp.sum(jnp.where(hots[k], rank_base, 0.0), axis=-1, keepdims=True)
        pos_mat = jnp.where(lane == k, (sels[k] * float(cap_rows) + rank_k) * float(ROW_TILE), pos_mat)
    cnt_ref[...] = cnt_ref[...] + jnp.sum(chosen, axis=0, keepdims=True)
    cnt_out_ref[...] = cnt_ref[...]
    run_front(len(nxt_front))
    pos_t =pos_mat.T[:SUBLANES, :].astype(jnp.int32)
    posv_ref[...] = pos_t
    smem_copy.start()
    wait_scatter(prev)

    @pl.when(is_last)
    def _():
        smem_copy.wait()

        def tail_body(t8, c):
            issue_rows(slot, t8 * SUBLANES, SUBLANES)
            return c
        lax.fori_loop(0, TL // SUBLANES, tail_body, 0)
        wait_scatter(slot)


def _mixer_call(x3, mod, h0, cbuf, weights, cnt0, nseg, seg, tiles_per_group, tile0, n_tok, aliased):
    groups = x3.shape[0]
    nseq = mod.shape[0]
    carry_tiles = nseg == 1
    n_alias = len(aliased)
    n_tiles = n_tok // TL
    cap_rows = n_tok

    def full(a):
        nd = a.ndim
        return pl.BlockSpec(a.shape, lambda g, j, nd=nd: (0,) * nd)

    def tile_map(g, j):
        return (tile0 + g * tiles_per_group + j, 0)

    def next_g(g, j):
        return jnp.where(j + 1 < tiles_per_group, g, jnp.minimum(g + 1, groups - 1))

    def next_j(g, j):
        return jnp.where(j + 1 < tiles_per_group, j + 1, 0)

    in_specs = [pl.BlockSpec((1, TL, D_MODEL), lambda g, j: (g, j, 0)),
                pl.BlockSpec((1, TL, D_MODEL), lambda g, j: (next_g(g, j), next_j(g, j), 0)),
                pl.BlockSpec((nseg, 6, D_MODEL), lambda g, j: (g, 0, 0)),
                pl.BlockSpec((nseg, 6, D_MODEL), lambda g, j: (next_g(g, j), 0, 0)),
                pl.BlockSpec((nseg, N_SLAB, SLAB_W), lambda g, j: (g, 0, 0)),
                pl.BlockSpec((nseg, CONV_W - 1, D_CONV), lambda g, j: (g, 0, 0))]
    in_specs += [full(w) for w in weights]
    in_specs += [full(cnt0)]
    in_specs += [pl.BlockSpec(memory_space=pl.ANY)] * n_alias
    out_shape = [jax.ShapeDtypeStruct((n_tok, D_MODEL), F32),
                 jax.ShapeDtypeStruct((n_tok, LANES), F32),
                 jax.ShapeDtypeStruct(((N_EXPERTS * cap_rows + TOP_K * TL) * ROW_TILE, LANES), F32),
                 jax.ShapeDtypeStruct((1, LANES), F32),
                 jax.ShapeDtypeStruct((nseq, N_SLAB, SLAB_W), F32),
                 jax.ShapeDtypeStruct((nseq, CONV_W - 1, D_CONV), F32)]
    out_specs = [pl.BlockSpec((TL, D_MODEL), tile_map),
                 pl.BlockSpec((TL, LANES), tile_map),
                 pl.BlockSpec(memory_space=pl.ANY),
                 pl.BlockSpec((1, LANES), lambda g, j: (0, 0)),
                 pl.BlockSpec((nseg, N_SLAB, SLAB_W), lambda g, j: (g, 0, 0)),
                 pl.BlockSpec((nseg, CONV_W - 1, D_CONV), lambda g, j: (g, 0, 0))]
    n_in = 6 + len(weights) + 1
    aliases = {n_in + k: k for k in range(n_alias)}
    scratch = [pltpu.VMEM((2, TL, 2 * D_MODEL), F32),
               pltpu.VMEM((2 * N_SLAB, TL, SLAB_W), F32),
               pltpu.VMEM((TL, D_MODEL), BF16),
               pltpu.VMEM((nseg * (seg + SUBLANES), D_CONV), F32),
               pltpu.VMEM((N_SLAB, SLAB_W), F32),
               pltpu.VMEM((2, TL * ROW_TILE, LANES), F32),
               pltpu.VMEM((SUBLANES, TL), jnp.int32),
               pltpu.SMEM((SUBLANES, TL), jnp.int32),
               pltpu.VMEM((1, LANES), F32),
               pltpu.SemaphoreType.DMA((2,)),
               pltpu.SemaphoreType.DMA((1,))]
    return pl.pallas_call(
        functools.partial(_mixer_kernel, nseg, seg, carry_tiles, n_alias, cap_rows, tile0),
        grid=(groups, tiles_per_group),
        in_specs=in_specs, out_specs=out_specs, out_shape=out_shape,
        scratch_shapes=scratch,
        input_output_aliases=aliases,
        compiler_params=pltpu.CompilerParams(dimension_semantics=("arbitrary", "arbitrary"),
                                             vmem_limit_bytes=VMEM_LIMIT),
        name="mixer_prompt" if carry_tiles else "mixer_sample",
    )(x3, x3, mod, mod, h0, cbuf, *weights, cnt0, *aliased)


W_CAST_ROWS = 128

def _expert_kernel(n_tok, row_ref, be_ref, nv_ref, nu_ref, xs_ref, wgu_ref, bgu_ref, wd_ref, bd_ref,
                   yk_hbm, wgu_bf, wd_bf, out_a, out_b, slotv_ref, slots_ref, sem_o, sem_v):
    i = pl.program_id(0)
    n_used = nu_ref[0]
    outs = (out_a, out_b)
    prev = jnp.maximum(i - 1, 0)
    new_expert = jnp.logical_or(i == 0, be_ref[i] != be_ref[prev])

    @pl.when(jnp.logical_and(new_expert, i < n_used))
    def _():
        def cast_gu(r, c):
            rows = pl.ds(pl.multiple_of(r * W_CAST_ROWS, W_CAST_ROWS), W_CAST_ROWS)
            wgu_bf[rows, :] = wgu_ref[0, rows, :].astype(BF16)
            return c
        lax.fori_loop(0, D_MODEL // W_CAST_ROWS, cast_gu, 0)

        def cast_d(r, c):
            rows = pl.ds(pl.multiple_of(r * W_CAST_ROWS, W_CAST_ROWS), W_CAST_ROWS)
            wd_bf[rows, :] = wd_ref[0, rows, :].astype(BF16)
            return c
        lax.fori_loop(0, D_FF // W_CAST_ROWS, cast_d, 0)

    def issue_rows(src, t0, count):
        for q in range(count):
            t = t0 + q
            dst = pl.multiple_of(slots_ref[src, 0, t], PACK_ROWS)
            row0 = pl.multiple_of(t * PACK_ROWS, PACK_ROWS)
            pltpu.make_async_copy(outs[src].at[pl.ds(row0, PACK_ROWS), :],
                                  yk_hbm.at[pl.ds(dst, PACK_ROWS), :],
                                  sem_o.at[src]).start(priority=q % DMA_THREADS)

    def wait_rows(sl):
        pltpu.make_async_copy(outs[sl], yk_hbm.at[pl.ds(0, E_BLK * PACK_ROWS), :], sem_o.at[sl]).wait()

    lane = lax.broadcasted_iota(jnp.int32, (E_BLK, LANES), 1)
    row = lax.broadcasted_iota(jnp.int32, (E_BLK, 1), 0)
    spare0 = float(TOP_K * n_tok)

    @pl.when(i == 0)
    def _():
        t_i = lax.broadcasted_iota(jnp.int32, (SUBLANES, E_BLK), 1)
        slotv_ref[...] = (TOP_K * n_tok + t_i) * PACK_ROWS
        pltpu.make_async_copy(slotv_ref, slots_ref.at[1], sem_v.at[0]).start()
        out_b[...] = jnp.zeros(out_b.shape, F32)

    def block(cur):
        prv = 1 - cur

        def smem_copy(dst):
            return pltpu.make_async_copy(slotv_ref, slots_ref.at[dst], sem_v.at[0])

        @pl.when(i < n_used)
        def _():
            smem_copy(prv).wait()
            record = xs_ref[pl.ds(PACK_ROWS, E_BLK, stride=ROW_TILE), :]
            hit = jnp.logical_and(record.astype(jnp.int32) == be_ref[i], lane < TOP_K)
            slot_f = jnp.sum(jnp.where(hit, pltpu.roll(record, LANES - TOP_K, 1), 0.0), axis=-1, keepdims=True)
            slot_f = jnp.where(row < nv_ref[i], slot_f, spare0 + row.astype(F32))
            slot_mat = jnp.where(lane == 0, slot_f * float(PACK_ROWS), 0.0)
            slotv_ref[...] = slot_mat.T[:SUBLANES, :].astype(jnp.int32)
            smem_copy(cur).start()

        @pl.when(i < n_used)
        def _():
            live = row < nv_ref[i]
            words = [lax.bitcast_convert_type(xs_ref[pl.ds(c, E_BLK, stride=ROW_TILE), :], jnp.int32)
                     for c in range(PACK_ROWS)]
            lo = [lax.bitcast_convert_type(lax.shift_left(w, 16), F32) for w in words]
            hi = [lax.bitcast_convert_type(w & HI16_MASK, F32) for w in words]
            xb = jnp.where(live, jnp.concatenate(lo + hi, axis=1), 0.0).astype(BF16)

            issue_rows(prv, 0, E_BLK)
            hgu = _dot(xb, wgu_bf[...]) + bgu_ref[0]
            g = jnp.minimum(hgu[:, :D_FF], SWIGLU_LIMIT)
            up = jnp.clip(hgu[:, D_FF:], -SWIGLU_LIMIT, SWIGLU_LIMIT)
            act = (up + 1.0) * (g * jax.nn.sigmoid(SWIGLU_ALPHA * g))
            y = _dot(act.astype(BF16), wd_bf[...]) + bd_ref[0]

            @pl.when(i > 0)
            def _():
                wait_rows(cur)

            y_lo = lax.bitcast_convert_type(y[:, :HALF].astype(BF16).astype(F32), jnp.int32)
            y_hi = lax.bitcast_convert_type(y[:, HALF:].astype(BF16).astype(F32), jnp.int32)
            y_pk = lax.bitcast_convert_type(y_hi | lax.shift_right_logical(y_lo, 16), F32)
            for c in range(PACK_ROWS):
                outs[cur][pl.ds(c, E_BLK, stride=PACK_ROWS), :] = y_pk[:, c * LANES:(c + 1) * LANES]

        @pl.when(i == n_used)
        def _():
            smem_copy(prv).wait()

            def tail(t8, c):
                issue_rows(prv, t8 * SUBLANES, SUBLANES)
                return c
            lax.fori_loop(0, E_BLK // SUBLANES, tail, 0)
            wait_rows(cur)
            wait_rows(prv)

    for parity in range(2):
        @pl.when(i % 2 == parity)
        def _(parity=parity):
            block(parity)


def _expert_call(blk_row, blk_e, blk_valid, n_used, xs, w_gu, b_gu, w_down, b_down, n_tok):
    nb = blk_row.shape[0]
    grid_spec = pltpu.PrefetchScalarGridSpec(
        num_scalar_prefetch=4,
        grid=(nb,),
        in_specs=[
            pl.BlockSpec((E_BLK * ROW_TILE, LANES), lambda i, br, be, nv, nu: (br[i], 0)),
            pl.BlockSpec((1, D_MODEL, 2 * D_FF), lambda i, br, be, nv, nu: (be[i], 0, 0)),
            pl.BlockSpec((1, 1, 2 * D_FF), lambda i, br, be, nv, nu: (be[i], 0, 0)),
            pl.BlockSpec((1, D_FF, D_MODEL), lambda i, br, be, nv, nu: (be[i], 0, 0)),
            pl.BlockSpec((1, 1, D_MODEL), lambda i, br, be, nv, nu: (be[i], 0, 0)),
        ],
        out_specs=pl.BlockSpec(memory_space=pl.ANY),
        scratch_shapes=[pltpu.VMEM((D_MODEL, 2 * D_FF), BF16),
                        pltpu.VMEM((D_FF, D_MODEL), BF16),
                        pltpu.VMEM((E_BLK * PACK_ROWS, LANES), F32),
                        pltpu.VMEM((E_BLK * PACK_ROWS, LANES), F32),
                        pltpu.VMEM((SUBLANES, E_BLK), jnp.int32),
                        pltpu.SMEM((2, SUBLANES, E_BLK), jnp.int32),
                        pltpu.SemaphoreType.DMA((2,)),
                        pltpu.SemaphoreType.DMA((1,))],
    )
    return pl.pallas_call(
        functools.partial(_expert_kernel, n_tok),
        grid_spec=grid_spec,
        out_shape=jax.ShapeDtypeStruct(((TOP_K * n_tok + E_BLK) * PACK_ROWS, LANES), F32),
        compiler_params=pltpu.CompilerParams(dimension_semantics=("arbitrary",),
                                             vmem_limit_bytes=VMEM_LIMIT),
        name="experts",
    )(blk_row, blk_e, blk_valid, n_used, xs, w_gu, b_gu, w_down, b_down)


def _combine_kernel(nseg, seg, y0_ref, y1_ref, y2_ref, y3_ref, x1_ref, gate_ref, mod_ref, g2_ref, b2_ref,
                    out_ref):
    gates = gate_ref[...]
    ffn = None
    for k, y_ref in enumerate((y0_ref, y1_ref, y2_ref, y3_ref)):
        words = [lax.bitcast_convert_type(y_ref[pl.ds(c, TL, stride=PACK_ROWS), :], jnp.int32)
                 for c in range(PACK_ROWS)]
        rows = jnp.concatenate(
            [lax.bitcast_convert_type(lax.shift_left(w, 16), F32) for w in words]
            + [lax.bitcast_convert_type(w & HI16_MASK, F32) for w in words], axis=1)
        term = gates[:, k:k + 1] * rows
        ffn = term if ffn is None else ffn + term
    if nseg == 1:
        gate2 = mod_ref[0, 5:6, :]
    else:
        gate2 = jnp.concatenate(
            [jnp.broadcast_to(mod_ref[s, 5:6, :], (seg, D_MODEL)) for s in range(nseg)], axis=0)
    xa = DEEPNORM_ALPHA * x1_ref[...] + (1.0 + gate2) * ffn
    out_ref[...] = _layer_norm(xa, g2_ref[...], b2_ref[...])


def _combine_call(yk, x1, gates, mod, ln2_g, ln2_b, nseg, seg, n_tiles, tile0, tiles_per_group, n_tok):
    tiles_total = n_tok // TL

    def yk_spec(k):
        return pl.BlockSpec((TL * PACK_ROWS, LANES), lambda i, k=k: (k * tiles_total + tile0 + i, 0))

    in_specs = [yk_spec(k) for k in range(TOP_K)] + [
        pl.BlockSpec((TL, D_MODEL), lambda i: (tile0 + i, 0)),
        pl.BlockSpec((TL, LANES), lambda i: (tile0 + i, 0)),
        pl.BlockSpec((nseg, 6, D_MODEL), lambda i: (i // tiles_per_group, 0, 0)),
        pl.BlockSpec((1, D_MODEL), lambda i: (0, 0)),
        pl.BlockSpec((1, D_MODEL), lambda i: (0, 0)),
    ]
    return pl.pallas_call(
        functools.partial(_combine_kernel, nseg, seg),
        grid=(n_tiles,),
        in_specs=in_specs,
        out_specs=pl.BlockSpec((TL, D_MODEL), lambda i: (i, 0)),
        out_shape=jax.ShapeDtypeStruct((n_tiles * TL, D_MODEL), F32),
        compiler_params=pltpu.CompilerParams(dimension_semantics=("arbitrary",),
                                             vmem_limit_bytes=VMEM_LIMIT),
        name="combine_prompt" if nseg == 1 else "combine_sample",
    )(yk, yk, yk, yk, x1, gates, mod, ln2_g, ln2_b)


def _s5_tables(lam_re, lam_im, log_dt, b_re, b_im, c_re, c_im):
    dt = jnp.exp(log_dt.astype(F32))[:, None]
    lam = lax.complex(lam_re.astype(F32), lam_im.astype(F32))
    lam_dt = lam * dt
    lam_bar = jnp.exp(lam_dt)
    b_bar = ((lam_bar - 1.0) / lam)[..., None] * lax.complex(b_re.astype(F32), b_im.astype(F32))
    gl = S5_GROUPS // N_SLAB
    eye = jnp.eye(gl, dtype=F32)

    def b_slab(part):
        a = part.reshape(N_SLAB, gl, S5_STATE, S5_GROUP)
        return jnp.einsum('sgph,gk->sghkp', a, eye).reshape(N_SLAB, gl * S5_GROUP, gl * S5_STATE)

    def b_rows(b):
        return jnp.concatenate([b_slab(b.real), b_slab(b.imag)], axis=-1)

    bmat = jnp.concatenate([b_rows(b_bar), b_rows(lam_bar[..., None] * b_bar)], axis=1).astype(BF16)

    def c_slab(part):
        a = part.reshape(N_SLAB, gl, S5_GROUP, S5_STATE)
        return jnp.einsum('sghp,gk->sgpkh', a, eye).reshape(N_SLAB, gl * S5_STATE, gl * S5_GROUP)

    cmat = jnp.concatenate([c_slab(c_re.astype(F32)), -c_slab(c_im.astype(F32))], axis=1).astype(BF16)

    row = jnp.arange(SUBLANES, dtype=F32)[:, None, None]

    def power(k):
        return jnp.exp(lam_dt[None] * k)

    tabs = []
    for d in (2, 4):
        pw = power(jnp.full_like(row, float(d)))
        mask = (row >= d).astype(F32)
        tabs += [pw.real * mask, pw.imag * mask]
    pw = power(row + 1.0)
    tabs += [pw.real, pw.imag]
    tab = jnp.stack(tabs, axis=0)
    tab = tab.reshape(len(tabs), SUBLANES, N_SLAB, gl * S5_STATE).transpose(2, 0, 1, 3)
    return bmat, cmat, tab


def _state_to_slab(re, im):
    n = re.shape[0]
    return jnp.concatenate([re.reshape(n, N_SLAB, HALF), im.reshape(n, N_SLAB, HALF)], axis=-1).astype(F32)


def _slab_to_state(s):
    n = s.shape[0]
    re = s[:, :, :HALF].reshape(n, S5_GROUPS, S5_STATE)
    im = s[:, :, HALF:].reshape(n, S5_GROUPS, S5_STATE)
    return re, im


def _block_table(counts, n_tok):
    n_assign = n_tok * TOP_K
    n_blocks = (n_assign + N_EXPERTS * (E_BLK - 1) + E_BLK - 1) // E_BLK + 1
    cap_blocks = n_tok // E_BLK
    nblk = (counts + E_BLK - 1) // E_BLK
    cum = jnp.cumsum(nblk)
    n_used = cum[-1]
    i = jnp.arange(n_blocks, dtype=jnp.int32)
    ii = jnp.maximum(jnp.minimum(i, n_used - 1), 0)
    before = (cum[None, :] <= ii[:, None]).astype(jnp.int32)
    e = jnp.minimum(jnp.sum(before, axis=1), N_EXPERTS - 1)
    start_e = jnp.sum(before * nblk[None, :], axis=1)
    is_e = jnp.concatenate([jnp.ones((n_blocks, 1), jnp.int32), before[:, :-1]], axis=1) - before
    count_e = jnp.sum(is_e * counts[None, :], axis=1)
    b = ii - start_e
    blk_row = (e * cap_blocks + b).astype(jnp.int32)
    valid = jnp.where(i < n_used, jnp.minimum(E_BLK, count_e - b * E_BLK), 0).astype(jnp.int32)
    return blk_row, e, valid, n_used.astype(jnp.int32).reshape(1)


def kernel(x_prompt, x_sample, c_prompt, c_sample, state_s5_re, state_s5_im, state_conv, w_ada, b_ada, w_in,
           s5_lam_re, s5_lam_im, s5_log_dt, s5_b_re, s5_b_im, s5_c_re, s5_c_im, s5_d, w_glu, b_glu, conv_w,
           w_out, ln1_g, ln1_b, w_router, b_router, w_gu, b_gu, w_down, b_down, ln2_g, ln2_b):
    assert DEPTH == 1 and w_ada.shape[0] == 1
    n_p, l_p, _ = x_prompt.shape
    n_s, l_s, _ = x_sample.shape
    t_p, t_s = n_p * l_p, n_s * l_s
    n_tok = t_p + t_s
    tiles_p, tiles_s = t_p // TL, t_s // TL
    seq_per_tile = TL // l_s
    assert l_p % TL == 0 and TL % l_s == 0 and n_s % seq_per_tile == 0

    c_all = jnp.concatenate([c_prompt, c_sample], axis=0).astype(F32)
    pad = (-c_all.shape[0]) % SUBLANES
    c_all = jnp.pad(c_all, ((0, pad), (0, 0)))
    mod = _ada_call(c_all, w_ada[0], b_ada[0].reshape(1, -1)).reshape(-1, 6, D_MODEL)
    mod_p, mod_s = mod[:n_p], mod[n_p:n_p + n_s]

    bmat, cmat, tab = _s5_tables(s5_lam_re[0], s5_lam_im[0], s5_log_dt[0], s5_b_re[0], s5_b_im[0],
                                 s5_c_re[0], s5_c_im[0])
    wr = jnp.pad(w_router[0], ((0, 0), (0, LANES - N_EXPERTS))).astype(BF16)
    br = jnp.pad(b_router[0].astype(F32), (0, LANES - N_EXPERTS), constant_values=NEG_BIG).reshape(1, LANES)
    weights = (w_in[0].astype(BF16), bmat, cmat, tab, s5_d[0].reshape(1, D_S5).astype(F32),
               w_glu[0].astype(BF16), b_glu[0].reshape(1, D_S5).astype(F32), conv_w[0].astype(F32),
               w_out[0].astype(BF16), ln1_g[0].reshape(1, D_MODEL).astype(F32),
               ln1_b[0].reshape(1, D_MODEL).astype(F32), wr, br)

    h0_p = jnp.zeros((n_p, N_SLAB, SLAB_W), F32)
    cb_p = jnp.zeros((n_p, CONV_W - 1, D_CONV), F32)
    h0_s = _state_to_slab(state_s5_re[0], state_s5_im[0])
    cb_s = state_conv[0].astype(F32)

    assert n_tok % E_BLK == 0 and E_BLK % TL == 0
    cnt0 = jnp.zeros((1, LANES), F32)
    outs_p = _mixer_call(x_prompt, mod_p, h0_p, cb_p, weights, cnt0, 1, TL, l_p // TL, 0, n_tok, ())
    x1, gates, xs, cnt_p, s_p, conv_p = outs_p
    xs3 = x_sample.reshape(n_s // seq_per_tile, TL, D_MODEL)
    outs_s = _mixer_call(xs3, mod_s, h0_s, cb_s, weights, cnt_p, seq_per_tile, l_s, 1, tiles_p, n_tok,
                         (x1, gates, xs))
    x1, gates, xs, cnt, s_s, conv_s = outs_s

    counts = cnt[0, :N_EXPERTS].astype(jnp.int32)
    blk_row, blk_e, blk_valid, n_used = _block_table(counts, n_tok)
    yk = _expert_call(blk_row, blk_e, blk_valid, n_used, xs, w_gu[0].astype(F32),
                      b_gu[0].reshape(N_EXPERTS, 1, 2 * D_FF).astype(F32), w_down[0].astype(F32),
                      b_down[0].reshape(N_EXPERTS, 1, D_MODEL).astype(F32), n_tok)
    g2 = ln2_g[0].reshape(1, D_MODEL).astype(F32)
    b2 = ln2_b[0].reshape(1, D_MODEL).astype(F32)
    y_p = _combine_call(yk, x1, gates, mod_p, g2, b2, 1, TL, tiles_p, 0, l_p // TL, n_tok)
    y_s = _combine_call(yk, x1, gates, mod_s, g2, b2, seq_per_tile, l_s, tiles_s, tiles_p, 1, n_tok)

    p_re, p_im = _slab_to_state(s_p)
    s_re, s_im = _slab_to_state(s_s)
    return (y_p.reshape(n_p, l_p, D_MODEL), y_s.reshape(n_s, l_s, D_MODEL),
            p_re[None], p_im[None], conv_p[None], s_re[None], s_im[None], conv_s[None])
```

```python
import functools
import math

import jax
import jax.numpy as jnp
from jax import lax
from jax.experimental import pallas as pl
from jax.experimental.pallas import tpu as pltpu

F32 = jnp.float32
BF16 = jnp.bfloat16

D_MODEL = 1024
DEPTH = 1
D_S5 = 512
D_CONV = 512
S5_GROUP = 16
S5_GROUPS = 32
S5_STATE = 64
CONV_W = 3
N_EXPERTS = 32
TOP_K = 4
D_FF = 1024
SWIGLU_LIMIT = 7.0
SWIGLU_ALPHA = 1.702
LN_EPS = 1e-5
DEEPNORM_ALPHA = (2.0 * DEPTH) ** 0.25

TL = 256
SUBLANES = 8
LANES = 128
N_SLAB = 4
SLAB_W = 1024
HALF = 512
E_BLK = 512
ROW_TILE = D_MODEL // LANES
PACK_ROWS = HALF // LANES
HI16_MASK = -65536
DMA_THREADS = 2
NEG_BIG = -1e30
VMEM_LIMIT = 56 * 1024 * 1024


def _dot(a, b):
    return jnp.dot(a, b, preferred_element_type=F32)


def _layer_norm(x, g, b):
    mu = jnp.mean(x, axis=-1, keepdims=True)
    xc = x - mu
    var = jnp.mean(xc * xc, axis=-1, keepdims=True)
    return xc * lax.rsqrt(var + LN_EPS) * g + b


def _ada_kernel(c_ref, w_ref, b_ref, o_ref):
    c = c_ref[...]
    s = c * jax.nn.sigmoid(c)
    o_ref[...] = _dot(s.astype(BF16), w_ref[...].astype(BF16)) + b_ref[...]


def _ada_call(c_all, w_ada, b_ada):
    rows = c_all.shape[0]
    n_out = w_ada.shape[1]
    tn = 768
    return pl.pallas_call(
        _ada_kernel,
        grid=(n_out // tn,),
        in_specs=[pl.BlockSpec((rows, D_MODEL), lambda i: (0, 0)),
                  pl.BlockSpec((D_MODEL, tn), lambda i: (0, i)),
                  pl.BlockSpec((1, tn), lambda i: (0, i))],
        out_specs=pl.BlockSpec((rows, tn), lambda i: (0, i)),
        out_shape=jax.ShapeDtypeStruct((rows, n_out), F32),
        compiler_params=pltpu.CompilerParams(dimension_semantics=("arbitrary",),
                                             vmem_limit_bytes=VMEM_LIMIT),
        name="ada_mod",
    )(c_all, w_ada, b_ada)


def _mixer_kernel(nseg, seg, carry_tiles, n_alias, cap_rows, tile0,
                  x_ref, xn_ref, mod_ref, modn_ref, h0_ref, cbuf_ref, win_ref, bmat_ref, cmat_ref, tab_ref, d_ref,
                  wglu_ref, bglu_ref, convw_ref, wout_ref, g1_ref, b1_ref, wr_ref, br_ref, cnt0_ref, *rest):
    rest = rest[n_alias:]
    (x1_ref, gate_ref, xs_hbm, cnt_out_ref, sout_ref, cout_ref,
     p_ref, bu_ref, hs_ref, xcs_ref, carry_ref, h2s_ref, posv_ref, poss_ref, cnt_ref, sem_s, sem_v) = rest
    g_id = pl.program_id(0)
    j = pl.program_id(1)
    step = g_id * pl.num_programs(1) + j
    is_last = step == pl.num_programs(0) * pl.num_programs(1) - 1
    slot = step % 2

    def rows_of(m_ref, k):
        if nseg == 1:
            return m_ref[0, k:k + 1, :]
        return jnp.concatenate(
            [jnp.broadcast_to(m_ref[s, k:k + 1, :], (seg, D_MODEL)) for s in range(nseg)], axis=0)

    def mod_rows(k):
        return rows_of(mod_ref, k)

    def front_pieces(xt_ref, m_ref, dst):
        def adaln():
            h = xt_ref[0] * (1.0 + rows_of(m_ref, 1)) + rows_of(m_ref, 0)
            hs_ref[...] = h.astype(BF16)

        def in_proj(c):
            cols = slice(c * HALF, (c + 1) * HALF)
            p_ref[dst, :, cols] = _dot(hs_ref[...], win_ref[:, cols])

        def s5_in(i):
            u_i = p_ref[dst, :, i * LANES:(i + 1) * LANES]
            row_in_tile = lax.broadcasted_iota(jnp.int32, u_i.shape, 0) % SUBLANES
            u_prev = jnp.where(row_in_tile == 0, 0.0, pltpu.roll(u_i, 1, 0))
            lhs = jnp.concatenate([u_i, u_prev], axis=1).astype(BF16)
            bu_ref[dst * N_SLAB + i] = _dot(lhs, bmat_ref[i])

        return ([adaln] + [functools.partial(in_proj, c) for c in range(2 * D_MODEL // HALF)]
                + [functools.partial(s5_in, i) for i in range(N_SLAB)])

    @pl.when(step == 0)
    def _():
        for piece in front_pieces(x_ref, mod_ref, 0):
            piece()

    nxt_front = front_pieces(xn_ref, modn_ref, 1 - slot)

    def run_front(n):
        for _ in range(n):
            nxt_front.pop(0)()

    x = x_ref[0]
    bu0 = slot * N_SLAB

    prev = 1 - slot
    smem_copy = pltpu.make_async_copy(posv_ref, poss_ref, sem_v.at[0])

    @pl.when(step == 0)
    def _():
        k_i = lax.broadcasted_iota(jnp.int32, (SUBLANES, TL), 0)
        t_i = lax.broadcasted_iota(jnp.int32, (SUBLANES, TL), 1)
        posv_ref[...] = (N_EXPERTS * cap_rows + k_i * TL + t_i) * ROW_TILE
        smem_copy.start()
        h2s_ref[...] = jnp.zeros(h2s_ref.shape, F32)

    smem_copy.wait()

    def issue_rows(src_slot, t0, n_rows):
        for q in range(n_rows):
            t = t0 + q
            src = pl.multiple_of(t * ROW_TILE, ROW_TILE)
            for k in range(TOP_K):
                dst = pl.multiple_of(poss_ref[k, t], ROW_TILE)
                pltpu.make_async_copy(h2s_ref.at[src_slot, pl.ds(src, ROW_TILE), :],
                                      xs_hbm.at[pl.ds(dst, ROW_TILE), :],
                                      sem_s.at[src_slot]).start(priority=k % DMA_THREADS)

    def wait_scatter(sl):
        for _ in range(TOP_K):
            pltpu.make_async_copy(h2s_ref.at[sl], xs_hbm.at[pl.ds(0, TL * ROW_TILE), :], sem_s.at[sl]).wait()

    if carry_tiles:
        @pl.when(j == 0)
        def _():
            carry_ref[...] = h0_ref[0]

    for s in range(nseg):
        if carry_tiles:
            init = tuple(carry_ref[i:i + 1, :] for i in range(N_SLAB))
        else:
            init = tuple(h0_ref[s, i:i + 1, :] for i in range(N_SLAB))

        def scan_body(r, carry, s=s):
            issue_rows(prev, s * seg + r * (2 * SUBLANES), 2 * SUBLANES)
            new = []
            for i in range(N_SLAB):
                cr = carry[i][:, :HALF]
                ci = carry[i][:, HALF:]
                for half in range(2):
                    row0 = pl.multiple_of(s * seg + r * (2 * SUBLANES) + half * SUBLANES, SUBLANES)
                    blk = bu_ref[bu0 + i, pl.ds(row0, SUBLANES), :]
                    xr = blk[:, :HALF]
                    xi = blk[:, HALF:]
                    for k, d in enumerate((2, 4)):
                        lr = tab_ref[i, 2 * k]
                        li = tab_ref[i, 2 * k + 1]
                        rr = pltpu.roll(xr, d, 0)
                        ri = pltpu.roll(xi, d, 0)
                        xr, xi = xr + (lr * rr - li * ri), xi + (lr * ri + li * rr)
                    pr = tab_ref[i, 4]
                    pi_ = tab_ref[i, 5]
                    crb = jnp.broadcast_to(cr, (SUBLANES, HALF))
                    cib = jnp.broadcast_to(ci, (SUBLANES, HALF))
                    xr, xi = xr + (pr * crb - pi_ * cib), xi + (pr * cib + pi_ * crb)
                    bu_ref[bu0 + i, pl.ds(row0, SUBLANES), :] = jnp.concatenate([xr, xi], axis=1)
                    cr = xr[SUBLANES - 1:SUBLANES, :]
                    ci = xi[SUBLANES - 1:SUBLANES, :]
                new.append(jnp.concatenate([cr, ci], axis=1))
            return tuple(new)

        final = lax.fori_loop(0, seg // (2 * SUBLANES), scan_body, init)
        for i in range(N_SLAB):
            sout_ref[s, i:i + 1, :] = final[i]
            if carry_tiles:
                carry_ref[i:i + 1, :] = final[i]

    y = jnp.concatenate([_dot(bu_ref[bu0 + i].astype(BF16), cmat_ref[i]) for i in range(N_SLAB)], axis=1)
    y = y + d_ref[...] * p_ref[slot, :, :D_S5]
    z = jax.nn.gelu(y)
    z = z * jax.nn.sigmoid(_dot(z.astype(BF16), wglu_ref[...]) + bglu_ref[...])

    xc = p_ref[slot, :, D_S5 + D_CONV:D_S5 + 2 * D_CONV] * p_ref[slot, :, D_S5 + 2 * D_CONV:]
    w0 = convw_ref[0:1, :]
    w1 = convw_ref[1:2, :]
    w2 = convw_ref[2:3, :]
    conv_parts = []
    for s in range(nseg):
        base = s * (seg + SUBLANES)
        xc_s = xc[s * seg:(s + 1) * seg]
        tail = xc_s[seg - 2:seg]
        xcs_ref[pl.ds(base + SUBLANES, seg), :] = xc_s
        if carry_tiles:
            @pl.when(j == 0)
            def _(s=s, base=base):
                xcs_ref[pl.ds(base + SUBLANES - 2, 2), :] = cbuf_ref[s]
        else:
            xcs_ref[pl.ds(base + SUBLANES - 2, 2), :] = cbuf_ref[s]
        xm1 = xcs_ref[pl.ds(base + SUBLANES - 1, seg), :]
        xm2 = xcs_ref[pl.ds(base + SUBLANES - 2, seg), :]
        conv_parts.append(w0 * xm2 + w1 * xm1 + w2 * xc_s)
        cout_ref[s] = tail
        if carry_tiles:
            xcs_ref[pl.ds(base + SUBLANES - 2, 2), :] = tail
    conv = conv_parts[0] if nseg == 1 else jnp.concatenate(conv_parts, axis=0)
    y_b = p_ref[slot, :, D_S5:D_S5 + D_CONV] * conv

    run_front(2)
    mix = _dot(jnp.concatenate([z, y_b], axis=1).astype(BF16), wout_ref[...])
    x1 = _layer_norm(DEEPNORM_ALPHA * x + (1.0 + mod_rows(2)) * mix, g1_ref[...], b1_ref[...])
    x1_ref[...] = x1
    run_front(1)
    h2 = x1 * (1.0 + mod_rows(4)) + mod_rows(3)
    lo_bits = lax.bitcast_convert_type(h2[:, :HALF].astype(BF16).astype(F32), jnp.int32)
    hi_bits = lax.bitcast_convert_type(h2[:, HALF:].astype(BF16).astype(F32), jnp.int32)
    packed = lax.bitcast_convert_type(hi_bits | lax.shift_right_logical(lo_bits, 16), F32)
    for c in range(PACK_ROWS):
        h2s_ref[slot, pl.ds(c, TL, stride=ROW_TILE), :] = packed[:, c * LANES:(c + 1) * LANES]

    logits = _dot(h2.astype(BF16), wr_ref[...]) + br_ref[...]
    lane = lax.broadcasted_iota(jnp.int32, logits.shape, 1)
    lane_f = lane.astype(F32)
    vals, sels, hots = [], [], []
    cur = logits
    for _ in range(TOP_K):
        m = jnp.max(cur, axis=-1, keepdims=True)
        am = jnp.min(jnp.where(cur == m, lane_f, float(LANES)), axis=-1, keepdims=True)
        hot = lane_f == am
        vals.append(m)
        sels.append(am)
        hots.append(hot)
        cur = jnp.where(hot, -jnp.inf, cur)
    exps = [jnp.exp(v - vals[0]) for v in vals]
    inv = 1.0 / (exps[0] + exps[1] + exps[2] + exps[3])
    gate_out = jnp.zeros(logits.shape, F32)
    for k in range(TOP_K):
        gate_out = jnp.where(lane == k, exps[k] * inv, gate_out)
    gate_ref[...] = gate_out
    run_front(1)

    tok = ((tile0 + step) * TL + lax.broadcasted_iota(jnp.int32, logits.shape, 0)).astype(F32)
    record = jnp.zeros(logits.shape, F32)
    for k in range(TOP_K):
        record = jnp.where(hots[k], tok + float(k * cap_rows), record)
    h2s_ref[slot, pl.ds(PACK_ROWS, TL, stride=ROW_TILE), :] = record
    run_front(2)

    @pl.when(step == 0)
    def _():
        cnt_ref[...] = cnt0_ref[...]

    chosen = jnp.zeros(logits.shape, F32)
    for k in range(TOP_K):
        chosen = jnp.where(hots[k], 1.0, chosen)
    r_i = lax.broadcasted_iota(jnp.int32, (TL, TL), 0)
    c_i = lax.broadcasted_iota(jnp.int32, (TL, TL), 1)
    before = jnp.where(c_i < r_i, 1.0, 0.0).astype(BF16)
    rank_base = _dot(before, chosen.astype(BF16)) + cnt_ref[...]
    pos_mat = jnp.zeros(logits.shape, F32)
    for k in range(TOP_K):
        rank_k = jnp.sum(jnp.where(hots[k], rank_base, 0.0), axis=-1, keepdims=True)
        pos_mat = jnp.where(lane == k, (sels[k] * float(cap_rows) + rank_k) * float(ROW_TILE), pos_mat)
    cnt_ref[...] = cnt_ref[...] + jnp.sum(chosen, axis=0, keepdims=True)
    cnt_out_ref[...] = cnt_ref[...]
    run_front(len(nxt_front))
    pos_t =pos_mat.T[:SUBLANES, :].astype(jnp.int32)
    posv_ref[...] = pos_t
    smem_copy.start()
    wait_scatter(prev)

    @pl.when(is_last)
    def _():
        smem_copy.wait()

        def tail_body(t8, c):
            issue_rows(slot, t8 * SUBLANES, SUBLANES)
            return c
        lax.fori_loop(0, TL // SUBLANES, tail_body, 0)
        wait_scatter(slot)


def _mixer_call(x3, mod, h0, cbuf, weights, cnt0, nseg, seg, tiles_per_group, tile0, n_tok, aliased):
    groups = x3.shape[0]
    nseq = mod.shape[0]
    carry_tiles = nseg == 1
    n_alias = len(aliased)
    n_tiles = n_tok // TL
    cap_rows = n_tok

    def full(a):
        nd = a.ndim
        return pl.BlockSpec(a.shape, lambda g, j, nd=nd: (0,) * nd)

    def tile_map(g, j):
        return (tile0 + g * tiles_per_group + j, 0)

    def next_g(g, j):
        return jnp.where(j + 1 < tiles_per_group, g, jnp.minimum(g + 1, groups - 1))

    def next_j(g, j):
        return jnp.where(j + 1 < tiles_per_group, j + 1, 0)

    in_specs = [pl.BlockSpec((1, TL, D_MODEL), lambda g, j: (g, j, 0)),
                pl.BlockSpec((1, TL, D_MODEL), lambda g, j: (next_g(g, j), next_j(g, j), 0)),
                pl.BlockSpec((nseg, 6, D_MODEL), lambda g, j: (g, 0, 0)),
                pl.BlockSpec((nseg, 6, D_MODEL), lambda g, j: (next_g(g, j), 0, 0)),
                pl.BlockSpec((nseg, N_SLAB, SLAB_W), lambda g, j: (g, 0, 0)),
                pl.BlockSpec((nseg, CONV_W - 1, D_CONV), lambda g, j: (g, 0, 0))]
    in_specs += [full(w) for w in weights]
    in_specs += [full(cnt0)]
    in_specs += [pl.BlockSpec(memory_space=pl.ANY)] * n_alias
    out_shape = [jax.ShapeDtypeStruct((n_tok, D_MODEL), F32),
                 jax.ShapeDtypeStruct((n_tok, LANES), F32),
                 jax.ShapeDtypeStruct(((N_EXPERTS * cap_rows + TOP_K * TL) * ROW_TILE, LANES), F32),
                 jax.ShapeDtypeStruct((1, LANES), F32),
                 jax.ShapeDtypeStruct((nseq, N_SLAB, SLAB_W), F32),
                 jax.ShapeDtypeStruct((nseq, CONV_W - 1, D_CONV), F32)]
    out_specs = [pl.BlockSpec((TL, D_MODEL), tile_map),
                 pl.BlockSpec((TL, LANES), tile_map),
                 pl.BlockSpec(memory_space=pl.ANY),
                 pl.BlockSpec((1, LANES), lambda g, j: (0, 0)),
                 pl.BlockSpec((nseg, N_SLAB, SLAB_W), lambda g, j: (g, 0, 0)),
                 pl.BlockSpec((nseg, CONV_W - 1, D_CONV), lambda g, j: (g, 0, 0))]
    n_in = 6 + len(weights) + 1
    aliases = {n_in + k: k for k in range(n_alias)}
    scratch = [pltpu.VMEM((2, TL, 2 * D_MODEL), F32),
               pltpu.VMEM((2 * N_SLAB, TL, SLAB_W), F32),
               pltpu.VMEM((TL, D_MODEL), BF16),
               pltpu.VMEM((nseg * (seg + SUBLANES), D_CONV), F32),
               pltpu.VMEM((N_SLAB, SLAB_W), F32),
               pltpu.VMEM((2, TL * ROW_TILE, LANES), F32),
               pltpu.VMEM((SUBLANES, TL), jnp.int32),
               pltpu.SMEM((SUBLANES, TL), jnp.int32),
               pltpu.VMEM((1, LANES), F32),
               pltpu.SemaphoreType.DMA((2,)),
               pltpu.SemaphoreType.DMA((1,))]
    return pl.pallas_call(
        functools.partial(_mixer_kernel, nseg, seg, carry_tiles, n_alias, cap_rows, tile0),
        grid=(groups, tiles_per_group),
        in_specs=in_specs, out_specs=out_specs, out_shape=out_shape,
        scratch_shapes=scratch,
        input_output_aliases=aliases,
        compiler_params=pltpu.CompilerParams(dimension_semantics=("arbitrary", "arbitrary"),
                                             vmem_limit_bytes=VMEM_LIMIT),
        name="mixer_prompt" if carry_tiles else "mixer_sample",
    )(x3, x3, mod, mod, h0, cbuf, *weights, cnt0, *aliased)


W_CAST_ROWS = 128

def _expert_kernel(n_tok, row_ref, be_ref, nv_ref, nu_ref, xs_ref, wgu_ref, bgu_ref, wd_ref, bd_ref,
                   yk_hbm, wgu_bf, wd_bf, out_a, out_b, slotv_ref, slots_ref, sem_o, sem_v):
    i = pl.program_id(0)
    n_used = nu_ref[0]
    outs = (out_a, out_b)
    prev = jnp.maximum(i - 1, 0)
    new_expert = jnp.logical_or(i == 0, be_ref[i] != be_ref[prev])

    @pl.when(jnp.logical_and(new_expert, i < n_used))
    def _():
        def cast_gu(r, c):
            rows = pl.ds(pl.multiple_of(r * W_CAST_ROWS, W_CAST_ROWS), W_CAST_ROWS)
            wgu_bf[rows, :] = wgu_ref[0, rows, :].astype(BF16)
            return c
        lax.fori_loop(0, D_MODEL // W_CAST_ROWS, cast_gu, 0)

        def cast_d(r, c):
            rows = pl.ds(pl.multiple_of(r * W_CAST_ROWS, W_CAST_ROWS), W_CAST_ROWS)
            wd_bf[rows, :] = wd_ref[0, rows, :].astype(BF16)
            return c
        lax.fori_loop(0, D_FF // W_CAST_ROWS, cast_d, 0)

    def issue_rows(src, t0, count):
        for q in range(count):
            t = t0 + q
            dst = pl.multiple_of(slots_ref[src, 0, t], PACK_ROWS)
            row0 = pl.multiple_of(t * PACK_ROWS, PACK_ROWS)
            pltpu.make_async_copy(outs[src].at[pl.ds(row0, PACK_ROWS), :],
                                  yk_hbm.at[pl.ds(dst, PACK_ROWS), :],
                                  sem_o.at[src]).start(priority=q % DMA_THREADS)

    def wait_rows(sl):
        pltpu.make_async_copy(outs[sl], yk_hbm.at[pl.ds(0, E_BLK * PACK_ROWS), :], sem_o.at[sl]).wait()

    row = lax.broadcasted_iota(jnp.int32, (E_BLK, 1), 0)
    spare0 = float(TOP_K * n_tok)

    @pl.when(i == 0)
    def _():
        t_i = lax.broadcasted_iota(jnp.int32, (SUBLANES, E_BLK), 1)
        slotv_ref[...] = (TOP_K * n_tok + t_i) * PACK_ROWS
        pltpu.make_async_copy(slotv_ref, slots_ref.at[1], sem_v.at[0]).start()
        out_b[...] = jnp.zeros(out_b.shape, F32)

    def block(cur):
        prv = 1 - cur

        def smem_copy(dst):
            return pltpu.make_async_copy(slotv_ref, slots_ref.at[dst], sem_v.at[0])

        @pl.when(i < n_used)
        def _():
            smem_copy(prv).wait()
            record = xs_ref[pl.ds(PACK_ROWS, E_BLK, stride=ROW_TILE), :]
            pick = (lax.broadcasted_iota(jnp.int32, (SUBLANES, LANES), 1) == be_ref[i]).astype(F32)
            slot_row = lax.dot_general(pick, record, (((1,), (1,)), ((), ())),
                                       precision=lax.Precision.HIGHEST, preferred_element_type=F32)
            t_row = lax.broadcasted_iota(jnp.int32, (SUBLANES, E_BLK), 1)
            slot_row = jnp.where(t_row < nv_ref[i], slot_row, spare0 + t_row.astype(F32))
            slotv_ref[...] = (slot_row * float(PACK_ROWS)).astype(jnp.int32)
            smem_copy(cur).start()

        @pl.when(i < n_used)
        def _():
            live = row < nv_ref[i]
            words = [lax.bitcast_convert_type(xs_ref[pl.ds(c, E_BLK, stride=ROW_TILE), :], jnp.int32)
                     for c in range(PACK_ROWS)]
            lo = [lax.bitcast_convert_type(lax.shift_left(w, 16), F32) for w in words]
            hi = [lax.bitcast_convert_type(w & HI16_MASK, F32) for w in words]
            xb = jnp.where(live, jnp.concatenate(lo + hi, axis=1), 0.0).astype(BF16)

            issue_rows(prv, 0, E_BLK)
            hgu = _dot(xb, wgu_bf[...]) + bgu_ref[0]
            g = jnp.minimum(hgu[:, :D_FF], SWIGLU_LIMIT)
            up = jnp.clip(hgu[:, D_FF:], -SWIGLU_LIMIT, SWIGLU_LIMIT)
            act = (up + 1.0) * (g * jax.nn.sigmoid(SWIGLU_ALPHA * g))
            y = _dot(act.astype(BF16), wd_bf[...]) + bd_ref[0]

            @pl.when(i > 0)
            def _():
                wait_rows(cur)

            y_lo = lax.bitcast_convert_type(y[:, :HALF].astype(BF16).astype(F32), jnp.int32)
            y_hi = lax.bitcast_convert_type(y[:, HALF:].astype(BF16).astype(F32), jnp.int32)
            y_pk = lax.bitcast_convert_type(y_hi | lax.shift_right_logical(y_lo, 16), F32)
            for c in range(PACK_ROWS):
                outs[cur][pl.ds(c, E_BLK, stride=PACK_ROWS), :] = y_pk[:, c * LANES:(c + 1) * LANES]

        @pl.when(i == n_used)
        def _():
            smem_copy(prv).wait()

            def tail(t8, c):
                issue_rows(prv, t8 * SUBLANES, SUBLANES)
                return c
            lax.fori_loop(0, E_BLK // SUBLANES, tail, 0)
            wait_rows(cur)
            wait_rows(prv)

    for parity in range(2):
        @pl.when(i % 2 == parity)
        def _(parity=parity):
            block(parity)


def _expert_call(blk_row, blk_e, blk_valid, n_used, xs, w_gu, b_gu, w_down, b_down, n_tok):
    nb = blk_row.shape[0]
    grid_spec = pltpu.PrefetchScalarGridSpec(
        num_scalar_prefetch=4,
        grid=(nb,),
        in_specs=[
            pl.BlockSpec((E_BLK * ROW_TILE, LANES), lambda i, br, be, nv, nu: (br[i], 0)),
            pl.BlockSpec((1, D_MODEL, 2 * D_FF), lambda i, br, be, nv, nu: (be[i], 0, 0)),
            pl.BlockSpec((1, 1, 2 * D_FF), lambda i, br, be, nv, nu: (be[i], 0, 0)),
            pl.BlockSpec((1, D_FF, D_MODEL), lambda i, br, be, nv, nu: (be[i], 0, 0)),
            pl.BlockSpec((1, 1, D_MODEL), lambda i, br, be, nv, nu: (be[i], 0, 0)),
        ],
        out_specs=pl.BlockSpec(memory_space=pl.ANY),
        scratch_shapes=[pltpu.VMEM((D_MODEL, 2 * D_FF), BF16),
                        pltpu.VMEM((D_FF, D_MODEL), BF16),
                        pltpu.VMEM((E_BLK * PACK_ROWS, LANES), F32),
                        pltpu.VMEM((E_BLK * PACK_ROWS, LANES), F32),
                        pltpu.VMEM((SUBLANES, E_BLK), jnp.int32),
                        pltpu.SMEM((2, SUBLANES, E_BLK), jnp.int32),
                        pltpu.SemaphoreType.DMA((2,)),
                        pltpu.SemaphoreType.DMA((1,))],
    )
    return pl.pallas_call(
        functools.partial(_expert_kernel, n_tok),
        grid_spec=grid_spec,
        out_shape=jax.ShapeDtypeStruct(((TOP_K * n_tok + E_BLK) * PACK_ROWS, LANES), F32),
        compiler_params=pltpu.CompilerParams(dimension_semantics=("arbitrary",),
                                             vmem_limit_bytes=VMEM_LIMIT),
        name="experts",
    )(blk_row, blk_e, blk_valid, n_used, xs, w_gu, b_gu, w_down, b_down)


def _combine_kernel(nseg, seg, y0_ref, y1_ref, y2_ref, y3_ref, x1_ref, gate_ref, mod_ref, g2_ref, b2_ref,
                    out_ref):
    gates = gate_ref[...]
    ffn = None
    for k, y_ref in enumerate((y0_ref, y1_ref, y2_ref, y3_ref)):
        words = [lax.bitcast_convert_type(y_ref[pl.ds(c, TL, stride=PACK_ROWS), :], jnp.int32)
                 for c in range(PACK_ROWS)]
        rows = jnp.concatenate(
            [lax.bitcast_convert_type(lax.shift_left(w, 16), F32) for w in words]
            + [lax.bitcast_convert_type(w & HI16_MASK, F32) for w in words], axis=1)
        term = gates[:, k:k + 1] * rows
        ffn = term if ffn is None else ffn + term
    if nseg == 1:
        gate2 = mod_ref[0, 5:6, :]
    else:
        gate2 = jnp.concatenate(
            [jnp.broadcast_to(mod_ref[s, 5:6, :], (seg, D_MODEL)) for s in range(nseg)], axis=0)
    xa = DEEPNORM_ALPHA * x1_ref[...] + (1.0 + gate2) * ffn
    out_ref[...] = _layer_norm(xa, g2_ref[...], b2_ref[...])


def _combine_call(yk, x1, gates, mod, ln2_g, ln2_b, nseg, seg, n_tiles, tile0, tiles_per_group, n_tok):
    tiles_total = n_tok // TL

    def yk_spec(k):
        return pl.BlockSpec((TL * PACK_ROWS, LANES), lambda i, k=k: (k * tiles_total + tile0 + i, 0))

    in_specs = [yk_spec(k) for k in range(TOP_K)] + [
        pl.BlockSpec((TL, D_MODEL), lambda i: (tile0 + i, 0)),
        pl.BlockSpec((TL, LANES), lambda i: (tile0 + i, 0)),
        pl.BlockSpec((nseg, 6, D_MODEL), lambda i: (i // tiles_per_group, 0, 0)),
        pl.BlockSpec((1, D_MODEL), lambda i: (0, 0)),
        pl.BlockSpec((1, D_MODEL), lambda i: (0, 0)),
    ]
    return pl.pallas_call(
        functools.partial(_combine_kernel, nseg, seg),
        grid=(n_tiles,),
        in_specs=in_specs,
        out_specs=pl.BlockSpec((TL, D_MODEL), lambda i: (i, 0)),
        out_shape=jax.ShapeDtypeStruct((n_tiles * TL, D_MODEL), F32),
        compiler_params=pltpu.CompilerParams(dimension_semantics=("arbitrary",),
                                             vmem_limit_bytes=VMEM_LIMIT),
        name="combine_prompt" if nseg == 1 else "combine_sample",
    )(yk, yk, yk, yk, x1, gates, mod, ln2_g, ln2_b)


def _s5_tables(lam_re, lam_im, log_dt, b_re, b_im, c_re, c_im):
    dt = jnp.exp(log_dt.astype(F32))[:, None]
    lam = lax.complex(lam_re.astype(F32), lam_im.astype(F32))
    lam_dt = lam * dt
    lam_bar = jnp.exp(lam_dt)
    b_bar = ((lam_bar - 1.0) / lam)[..., None] * lax.complex(b_re.astype(F32), b_im.astype(F32))
    gl = S5_GROUPS // N_SLAB
    eye = jnp.eye(gl, dtype=F32)

    def b_slab(part):
        a = part.reshape(N_SLAB, gl, S5_STATE, S5_GROUP)
        return jnp.einsum('sgph,gk->sghkp', a, eye).reshape(N_SLAB, gl * S5_GROUP, gl * S5_STATE)

    def b_rows(b):
        return jnp.concatenate([b_slab(b.real), b_slab(b.imag)], axis=-1)

    bmat = jnp.concatenate([b_rows(b_bar), b_rows(lam_bar[..., None] * b_bar)], axis=1).astype(BF16)

    def c_slab(part):
        a = part.reshape(N_SLAB, gl, S5_GROUP, S5_STATE)
        return jnp.einsum('sghp,gk->sgpkh', a, eye).reshape(N_SLAB, gl * S5_STATE, gl * S5_GROUP)

    cmat = jnp.concatenate([c_slab(c_re.astype(F32)), -c_slab(c_im.astype(F32))], axis=1).astype(BF16)

    row = jnp.arange(SUBLANES, dtype=F32)[:, None, None]

    def power(k):
        return jnp.exp(lam_dt[None] * k)

    tabs = []
    for d in (2, 4):
        pw = power(jnp.full_like(row, float(d)))
        mask = (row >= d).astype(F32)
        tabs += [pw.real * mask, pw.imag * mask]
    pw = power(row + 1.0)
    tabs += [pw.real, pw.imag]
    tab = jnp.stack(tabs, axis=0)
    tab = tab.reshape(len(tabs), SUBLANES, N_SLAB, gl * S5_STATE).transpose(2, 0, 1, 3)
    return bmat, cmat, tab


def _state_to_slab(re, im):
    n = re.shape[0]
    return jnp.concatenate([re.reshape(n, N_SLAB, HALF), im.reshape(n, N_SLAB, HALF)], axis=-1).astype(F32)


def _slab_to_state(s):
    n = s.shape[0]
    re = s[:, :, :HALF].reshape(n, S5_GROUPS, S5_STATE)
    im = s[:, :, HALF:].reshape(n, S5_GROUPS, S5_STATE)
    return re, im


def _block_table(counts, n_tok):
    n_assign = n_tok * TOP_K
    n_blocks = (n_assign + N_EXPERTS * (E_BLK - 1) + E_BLK - 1) // E_BLK + 1
    cap_blocks = n_tok // E_BLK
    nblk = (counts + E_BLK - 1) // E_BLK
    cum = jnp.cumsum(nblk)
    n_used = cum[-1]
    i = jnp.arange(n_blocks, dtype=jnp.int32)
    ii = jnp.maximum(jnp.minimum(i, n_used - 1), 0)
    before = (cum[None, :] <= ii[:, None]).astype(jnp.int32)
    e = jnp.minimum(jnp.sum(before, axis=1), N_EXPERTS - 1)
    start_e = jnp.sum(before * nblk[None, :], axis=1)
    is_e = jnp.concatenate([jnp.ones((n_blocks, 1), jnp.int32), before[:, :-1]], axis=1) - before
    count_e = jnp.sum(is_e * counts[None, :], axis=1)
    b = ii - start_e
    blk_row = (e * cap_blocks + b).astype(jnp.int32)
    valid = jnp.where(i < n_used, jnp.minimum(E_BLK, count_e - b * E_BLK), 0).astype(jnp.int32)
    return blk_row, e, valid, n_used.astype(jnp.int32).reshape(1)


def kernel(x_prompt, x_sample, c_prompt, c_sample, state_s5_re, state_s5_im, state_conv, w_ada, b_ada, w_in,
           s5_lam_re, s5_lam_im, s5_log_dt, s5_b_re, s5_b_im, s5_c_re, s5_c_im, s5_d, w_glu, b_glu, conv_w,
           w_out, ln1_g, ln1_b, w_router, b_router, w_gu, b_gu, w_down, b_down, ln2_g, ln2_b):
    assert DEPTH == 1 and w_ada.shape[0] == 1
    n_p, l_p, _ = x_prompt.shape
    n_s, l_s, _ = x_sample.shape
    t_p, t_s = n_p * l_p, n_s * l_s
    n_tok = t_p + t_s
    tiles_p, tiles_s = t_p // TL, t_s // TL
    seq_per_tile = TL // l_s
    assert l_p % TL == 0 and TL % l_s == 0 and n_s % seq_per_tile == 0

    c_all = jnp.concatenate([c_prompt, c_sample], axis=0).astype(F32)
    pad = (-c_all.shape[0]) % SUBLANES
    c_all = jnp.pad(c_all, ((0, pad), (0, 0)))
    mod = _ada_call(c_all, w_ada[0], b_ada[0].reshape(1, -1)).reshape(-1, 6, D_MODEL)
    mod_p, mod_s = mod[:n_p], mod[n_p:n_p + n_s]

    bmat, cmat, tab = _s5_tables(s5_lam_re[0], s5_lam_im[0], s5_log_dt[0], s5_b_re[0], s5_b_im[0],
                                 s5_c_re[0], s5_c_im[0])
    wr = jnp.pad(w_router[0], ((0, 0), (0, LANES - N_EXPERTS))).astype(BF16)
    br = jnp.pad(b_router[0].astype(F32), (0, LANES - N_EXPERTS), constant_values=NEG_BIG).reshape(1, LANES)
    weights = (w_in[0].astype(BF16), bmat, cmat, tab, s5_d[0].reshape(1, D_S5).astype(F32),
               w_glu[0].astype(BF16), b_glu[0].reshape(1, D_S5).astype(F32), conv_w[0].astype(F32),
               w_out[0].astype(BF16), ln1_g[0].reshape(1, D_MODEL).astype(F32),
               ln1_b[0].reshape(1, D_MODEL).astype(F32), wr, br)

    h0_p = jnp.zeros((n_p, N_SLAB, SLAB_W), F32)
    cb_p = jnp.zeros((n_p, CONV_W - 1, D_CONV), F32)
    h0_s = _state_to_slab(state_s5_re[0], state_s5_im[0])
    cb_s = state_conv[0].astype(F32)

    assert n_tok % E_BLK == 0 and E_BLK % TL == 0
    cnt0 = jnp.zeros((1, LANES), F32)
    outs_p = _mixer_call(x_prompt, mod_p, h0_p, cb_p, weights, cnt0, 1, TL, l_p // TL, 0, n_tok, ())
    x1, gates, xs, cnt_p, s_p, conv_p = outs_p
    xs3 = x_sample.reshape(n_s // seq_per_tile, TL, D_MODEL)
    outs_s = _mixer_call(xs3, mod_s, h0_s, cb_s, weights, cnt_p, seq_per_tile, l_s, 1, tiles_p, n_tok,
                         (x1, gates, xs))
    x1, gates, xs, cnt, s_s, conv_s = outs_s

    counts = cnt[0, :N_EXPERTS].astype(jnp.int32)
    blk_row, blk_e, blk_valid, n_used = _block_table(counts, n_tok)
    yk = _expert_call(blk_row, blk_e, blk_valid, n_used, xs, w_gu[0].astype(F32),
                      b_gu[0].reshape(N_EXPERTS, 1, 2 * D_FF).astype(F32), w_down[0].astype(F32),
                      b_down[0].reshape(N_EXPERTS, 1, D_MODEL).astype(F32), n_tok)
    g2 = ln2_g[0].reshape(1, D_MODEL).astype(F32)
    b2 = ln2_b[0].reshape(1, D_MODEL).astype(F32)
    y_p = _combine_call(yk, x1, gates, mod_p, g2, b2, 1, TL, tiles_p, 0, l_p // TL, n_tok)
    y_s = _combine_call(yk, x1, gates, mod_s, g2, b2, seq_per_tile, l_s, tiles_s, tiles_p, 1, n_tok)

    p_re, p_im = _slab_to_state(s_p)
    s_re, s_im = _slab_to_state(s_s)
    return (y_p.reshape(n_p, l_p, D_MODEL), y_s.reshape(n_s, l_s, D_MODEL),
            p_re[None], p_im[None], conv_p[None], s_re[None], s_im[None], conv_s[None])
```

```python
import functools
import math

import jax
import jax.numpy as jnp
from jax import lax
from jax.experimental import pallas as pl
from jax.experimental.pallas import tpu as pltpu

F32 = jnp.float32
BF16 = jnp.bfloat16

D_MODEL = 1024
DEPTH = 1
D_S5 = 512
D_CONV = 512
S5_GROUP = 16
S5_GROUPS = 32
S5_STATE = 64
CONV_W = 3
N_EXPERTS = 32
TOP_K = 4
D_FF = 1024
SWIGLU_LIMIT = 7.0
SWIGLU_ALPHA = 1.702
LN_EPS = 1e-5
DEEPNORM_ALPHA = (2.0 * DEPTH) ** 0.25

TL = 256
C_TL = 512
SUBLANES = 8
LANES = 128
N_SLAB = 4
SLAB_W = 1024
HALF = 512
E_BLK = 512
ROW_TILE = D_MODEL // LANES
PACK_ROWS = HALF // LANES
HI16_MASK = -65536
DMA_THREADS = 2
NEG_BIG = -1e30
VMEM_LIMIT = 56 * 1024 * 1024


def _dot(a, b):
    return jnp.dot(a, b, preferred_element_type=F32)


def _layer_norm(x, g, b):
    mu = jnp.mean(x, axis=-1, keepdims=True)
    xc = x - mu
    var = jnp.mean(xc * xc, axis=-1, keepdims=True)
    return xc * lax.rsqrt(var + LN_EPS) * g + b


def _ada_kernel(c_ref, w_ref, b_ref, o_ref):
    c = c_ref[...]
    s = c * jax.nn.sigmoid(c)
    o_ref[...] = _dot(s.astype(BF16), w_ref[...].astype(BF16)) + b_ref[...]


def _ada_call(c_all, w_ada, b_ada):
    rows = c_all.shape[0]
    n_out = w_ada.shape[1]
    tn = 768
    return pl.pallas_call(
        _ada_kernel,
        grid=(n_out // tn,),
        in_specs=[pl.BlockSpec((rows, D_MODEL), lambda i: (0, 0)),
                  pl.BlockSpec((D_MODEL, tn), lambda i: (0, i)),
                  pl.BlockSpec((1, tn), lambda i: (0, i))],
        out_specs=pl.BlockSpec((rows, tn), lambda i: (0, i)),
        out_shape=jax.ShapeDtypeStruct((rows, n_out), F32),
        compiler_params=pltpu.CompilerParams(dimension_semantics=("arbitrary",),
                                             vmem_limit_bytes=VMEM_LIMIT),
        name="ada_mod",
    )(c_all, w_ada, b_ada)


def _mixer_kernel(nseg, seg, carry_tiles, n_alias, cap_rows, tile0,
                  x_ref, xn_ref, mod_ref, modn_ref, h0_ref, cbuf_ref, win_ref, bmat_ref, cmat_ref, tab_ref, d_ref,
                  wglu_ref, bglu_ref, convw_ref, wout_ref, g1_ref, b1_ref, wr_ref, br_ref, cnt0_ref, *rest):
    rest = rest[n_alias:]
    (x1_ref, gate_ref, xs_hbm, cnt_out_ref, sout_ref, cout_ref,
     p_ref, bu_ref, hs_ref, xcs_ref, carry_ref, h2s_ref, posv_ref, poss_ref, cnt_ref, sem_s, sem_v) = rest
    g_id = pl.program_id(0)
    j = pl.program_id(1)
    step = g_id * pl.num_programs(1) + j
    is_last = step == pl.num_programs(0) * pl.num_programs(1) - 1
    slot = step % 2

    def rows_of(m_ref, k):
        if nseg == 1:
            return m_ref[0, k:k + 1, :]
        return jnp.concatenate(
            [jnp.broadcast_to(m_ref[s, k:k + 1, :], (seg, D_MODEL)) for s in range(nseg)], axis=0)

    def mod_rows(k):
        return rows_of(mod_ref, k)

    def front_pieces(xt_ref, m_ref, dst):
        def adaln():
            h = xt_ref[0] * (1.0 + rows_of(m_ref, 1)) + rows_of(m_ref, 0)
            hs_ref[...] = h.astype(BF16)

        def in_proj(c):
            cols = slice(c * HALF, (c + 1) * HALF)
            p_ref[dst, :, cols] = _dot(hs_ref[...], win_ref[:, cols])

        def s5_in(i):
            u_i = p_ref[dst, :, i * LANES:(i + 1) * LANES]
            row_in_tile = lax.broadcasted_iota(jnp.int32, u_i.shape, 0) % SUBLANES
            u_prev = jnp.where(row_in_tile == 0, 0.0, pltpu.roll(u_i, 1, 0))
            lhs = jnp.concatenate([u_i, u_prev], axis=1).astype(BF16)
            bu_ref[dst * N_SLAB + i] = _dot(lhs, bmat_ref[i])

        return ([adaln] + [functools.partial(in_proj, c) for c in range(2 * D_MODEL // HALF)]
                + [functools.partial(s5_in, i) for i in range(N_SLAB)])

    @pl.when(step == 0)
    def _():
        for piece in front_pieces(x_ref, mod_ref, 0):
            piece()

    nxt_front = front_pieces(xn_ref, modn_ref, 1 - slot)

    def run_front(n):
        for _ in range(n):
            nxt_front.pop(0)()

    x = x_ref[0]
    bu0 = slot * N_SLAB

    prev = 1 - slot
    smem_copy = pltpu.make_async_copy(posv_ref, poss_ref, sem_v.at[0])

    @pl.when(step == 0)
    def _():
        k_i = lax.broadcasted_iota(jnp.int32, (SUBLANES, TL), 0)
        t_i = lax.broadcasted_iota(jnp.int32, (SUBLANES, TL), 1)
        posv_ref[...] = (N_EXPERTS * cap_rows + k_i * TL + t_i) * ROW_TILE
        smem_copy.start()
        h2s_ref[...] = jnp.zeros(h2s_ref.shape, F32)

    smem_copy.wait()

    def issue_rows(src_slot, t0, n_rows):
        for q in range(n_rows):
            t = t0 + q
            src = pl.multiple_of(t * ROW_TILE, ROW_TILE)
            for k in range(TOP_K):
                dst = pl.multiple_of(poss_ref[k, t], ROW_TILE)
                pltpu.make_async_copy(h2s_ref.at[src_slot, pl.ds(src, ROW_TILE), :],
                                      xs_hbm.at[pl.ds(dst, ROW_TILE), :],
                                      sem_s.at[src_slot]).start(priority=k % DMA_THREADS)

    def wait_scatter(sl):
        for _ in range(TOP_K):
            pltpu.make_async_copy(h2s_ref.at[sl], xs_hbm.at[pl.ds(0, TL * ROW_TILE), :], sem_s.at[sl]).wait()

    if carry_tiles:
        @pl.when(j == 0)
        def _():
            carry_ref[...] = h0_ref[0]

    for s in range(nseg):
        if carry_tiles:
            init = tuple(carry_ref[i:i + 1, :] for i in range(N_SLAB))
        else:
            init = tuple(h0_ref[s, i:i + 1, :] for i in range(N_SLAB))

        def scan_body(r, carry, s=s):
            issue_rows(prev, s * seg + r * (2 * SUBLANES), 2 * SUBLANES)
            new = []
            for i in range(N_SLAB):
                cr = carry[i][:, :HALF]
                ci = carry[i][:, HALF:]
                for half in range(2):
                    row0 = pl.multiple_of(s * seg + r * (2 * SUBLANES) + half * SUBLANES, SUBLANES)
                    blk = bu_ref[bu0 + i, pl.ds(row0, SUBLANES), :]
                    xr = blk[:, :HALF]
                    xi = blk[:, HALF:]
                    for k, d in enumerate((2, 4)):
                        lr = tab_ref[i, 2 * k]
                        li = tab_ref[i, 2 * k + 1]
                        rr = pltpu.roll(xr, d, 0)
                        ri = pltpu.roll(xi, d, 0)
                        xr, xi = xr + (lr * rr - li * ri), xi + (lr * ri + li * rr)
                    pr = tab_ref[i, 4]
                    pi_ = tab_ref[i, 5]
                    crb = jnp.broadcast_to(cr, (SUBLANES, HALF))
                    cib = jnp.broadcast_to(ci, (SUBLANES, HALF))
                    xr, xi = xr + (pr * crb - pi_ * cib), xi + (pr * cib + pi_ * crb)
                    bu_ref[bu0 + i, pl.ds(row0, SUBLANES), :] = jnp.concatenate([xr, xi], axis=1)
                    cr = xr[SUBLANES - 1:SUBLANES, :]
                    ci = xi[SUBLANES - 1:SUBLANES, :]
                new.append(jnp.concatenate([cr, ci], axis=1))
            return tuple(new)

        final = lax.fori_loop(0, seg // (2 * SUBLANES), scan_body, init)
        for i in range(N_SLAB):
            sout_ref[s, i:i + 1, :] = final[i]
            if carry_tiles:
                carry_ref[i:i + 1, :] = final[i]

    y = jnp.concatenate([_dot(bu_ref[bu0 + i].astype(BF16), cmat_ref[i]) for i in range(N_SLAB)], axis=1)
    y = y + d_ref[...] * p_ref[slot, :, :D_S5]
    z = jax.nn.gelu(y)
    z = z * jax.nn.sigmoid(_dot(z.astype(BF16), wglu_ref[...]) + bglu_ref[...])

    xc = p_ref[slot, :, D_S5 + D_CONV:D_S5 + 2 * D_CONV] * p_ref[slot, :, D_S5 + 2 * D_CONV:]
    w0 = convw_ref[0:1, :]
    w1 = convw_ref[1:2, :]
    w2 = convw_ref[2:3, :]
    conv_parts = []
    for s in range(nseg):
        base = s * (seg + SUBLANES)
        xc_s = xc[s * seg:(s + 1) * seg]
        tail = xc_s[seg - 2:seg]
        xcs_ref[pl.ds(base + SUBLANES, seg), :] = xc_s
        if carry_tiles:
            @pl.when(j == 0)
            def _(s=s, base=base):
                xcs_ref[pl.ds(base + SUBLANES - 2, 2), :] = cbuf_ref[s]
        else:
            xcs_ref[pl.ds(base + SUBLANES - 2, 2), :] = cbuf_ref[s]
        xm1 = xcs_ref[pl.ds(base + SUBLANES - 1, seg), :]
        xm2 = xcs_ref[pl.ds(base + SUBLANES - 2, seg), :]
        conv_parts.append(w0 * xm2 + w1 * xm1 + w2 * xc_s)
        cout_ref[s] = tail
        if carry_tiles:
            xcs_ref[pl.ds(base + SUBLANES - 2, 2), :] = tail
    conv = conv_parts[0] if nseg == 1 else jnp.concatenate(conv_parts, axis=0)
    y_b = p_ref[slot, :, D_S5:D_S5 + D_CONV] * conv

    run_front(2)
    mix = _dot(jnp.concatenate([z, y_b], axis=1).astype(BF16), wout_ref[...])
    x1 = _layer_norm(DEEPNORM_ALPHA * x + (1.0 + mod_rows(2)) * mix, g1_ref[...], b1_ref[...])
    x1_ref[...] = x1
    run_front(1)
    h2 = x1 * (1.0 + mod_rows(4)) + mod_rows(3)
    lo_bits = lax.bitcast_convert_type(h2[:, :HALF].astype(BF16).astype(F32), jnp.int32)
    hi_bits = lax.bitcast_convert_type(h2[:, HALF:].astype(BF16).astype(F32), jnp.int32)
    packed = lax.bitcast_convert_type(hi_bits | lax.shift_right_logical(lo_bits, 16), F32)
    for c in range(PACK_ROWS):
        h2s_ref[slot, pl.ds(c, TL, stride=ROW_TILE), :] = packed[:, c * LANES:(c + 1) * LANES]

    logits = _dot(h2.astype(BF16), wr_ref[...]) + br_ref[...]
    lane = lax.broadcasted_iota(jnp.int32, logits.shape, 1)
    lane_f = lane.astype(F32)
    vals, sels, hots = [], [], []
    cur = logits
    for _ in range(TOP_K):
        m = jnp.max(cur, axis=-1, keepdims=True)
        am = jnp.min(jnp.where(cur == m, lane_f, float(LANES)), axis=-1, keepdims=True)
        hot = lane_f == am
        vals.append(m)
        sels.append(am)
        hots.append(hot)
        cur = jnp.where(hot, -jnp.inf, cur)
    exps = [jnp.exp(v - vals[0]) for v in vals]
    inv = 1.0 / (exps[0] + exps[1] + exps[2] + exps[3])
    gate_out = jnp.zeros(logits.shape, F32)
    for k in range(TOP_K):
        gate_out = jnp.where(lane == k, exps[k] * inv, gate_out)
    gate_ref[...] = gate_out
    run_front(1)

    tok = ((tile0 + step) * TL + lax.broadcasted_iota(jnp.int32, logits.shape, 0)).astype(F32)
    record = jnp.zeros(logits.shape, F32)
    for k in range(TOP_K):
        record = jnp.where(hots[k], tok + float(k * cap_rows), record)
    h2s_ref[slot, pl.ds(PACK_ROWS, TL, stride=ROW_TILE), :] = record
    run_front(2)

    @pl.when(step == 0)
    def _():
        cnt_ref[...] = cnt0_ref[...]

    chosen = jnp.zeros(logits.shape, F32)
    for k in range(TOP_K):
        chosen = jnp.where(hots[k], 1.0, chosen)
    r_i = lax.broadcasted_iota(jnp.int32, (TL, TL), 0)
    c_i = lax.broadcasted_iota(jnp.int32, (TL, TL), 1)
    before = jnp.where(c_i < r_i, 1.0, 0.0).astype(BF16)
    rank_base = _dot(before, chosen.astype(BF16)) + cnt_ref[...]
    pos_mat = jnp.zeros(logits.shape, F32)
    for k in range(TOP_K):
        rank_k = jnp.sum(jnp.where(hots[k], rank_base, 0.0), axis=-1, keepdims=True)
        pos_mat = jnp.where(lane == k, (sels[k] * float(cap_rows) + rank_k) * float(ROW_TILE), pos_mat)
    cnt_ref[...] = cnt_ref[...] + jnp.sum(chosen, axis=0, keepdims=True)
    cnt_out_ref[...] = cnt_ref[...]
    run_front(len(nxt_front))
    pos_t =pos_mat.T[:SUBLANES, :].astype(jnp.int32)
    posv_ref[...] = pos_t
    smem_copy.start()
    wait_scatter(prev)

    @pl.when(is_last)
    def _():
        smem_copy.wait()

        def tail_body(t8, c):
            issue_rows(slot, t8 * SUBLANES, SUBLANES)
            return c
        lax.fori_loop(0, TL // SUBLANES, tail_body, 0)
        wait_scatter(slot)


def _mixer_call(x3, mod, h0, cbuf, weights, cnt0, nseg, seg, tiles_per_group, tile0, n_tok, aliased):
    groups = x3.shape[0]
    nseq = mod.shape[0]
    carry_tiles = nseg == 1
    n_alias = len(aliased)
    n_tiles = n_tok // TL
    cap_rows = n_tok

    def full(a):
        nd = a.ndim
        return pl.BlockSpec(a.shape, lambda g, j, nd=nd: (0,) * nd)

    def tile_map(g, j):
        return (tile0 + g * tiles_per_group + j, 0)

    def next_g(g, j):
        return jnp.where(j + 1 < tiles_per_group, g, jnp.minimum(g + 1, groups - 1))

    def next_j(g, j):
        return jnp.where(j + 1 < tiles_per_group, j + 1, 0)

    in_specs = [pl.BlockSpec((1, TL, D_MODEL), lambda g, j: (g, j, 0)),
                pl.BlockSpec((1, TL, D_MODEL), lambda g, j: (next_g(g, j), next_j(g, j), 0)),
                pl.BlockSpec((nseg, 6, D_MODEL), lambda g, j: (g, 0, 0)),
                pl.BlockSpec((nseg, 6, D_MODEL), lambda g, j: (next_g(g, j), 0, 0)),
                pl.BlockSpec((nseg, N_SLAB, SLAB_W), lambda g, j: (g, 0, 0)),
                pl.BlockSpec((nseg, CONV_W - 1, D_CONV), lambda g, j: (g, 0, 0))]
    in_specs += [full(w) for w in weights]
    in_specs += [full(cnt0)]
    in_specs += [pl.BlockSpec(memory_space=pl.ANY)] * n_alias
    out_shape = [jax.ShapeDtypeStruct((n_tok, D_MODEL), F32),
                 jax.ShapeDtypeStruct((n_tok, LANES), F32),
                 jax.ShapeDtypeStruct(((N_EXPERTS * cap_rows + TOP_K * TL) * ROW_TILE, LANES), F32),
                 jax.ShapeDtypeStruct((1, LANES), F32),
                 jax.ShapeDtypeStruct((nseq, N_SLAB, SLAB_W), F32),
                 jax.ShapeDtypeStruct((nseq, CONV_W - 1, D_CONV), F32)]
    out_specs = [pl.BlockSpec((TL, D_MODEL), tile_map),
                 pl.BlockSpec((TL, LANES), tile_map),
                 pl.BlockSpec(memory_space=pl.ANY),
                 pl.BlockSpec((1, LANES), lambda g, j: (0, 0)),
                 pl.BlockSpec((nseg, N_SLAB, SLAB_W), lambda g, j: (g, 0, 0)),
                 pl.BlockSpec((nseg, CONV_W - 1, D_CONV), lambda g, j: (g, 0, 0))]
    n_in = 6 + len(weights) + 1
    aliases = {n_in + k: k for k in range(n_alias)}
    scratch = [pltpu.VMEM((2, TL, 2 * D_MODEL), F32),
               pltpu.VMEM((2 * N_SLAB, TL, SLAB_W), F32),
               pltpu.VMEM((TL, D_MODEL), BF16),
               pltpu.VMEM((nseg * (seg + SUBLANES), D_CONV), F32),
               pltpu.VMEM((N_SLAB, SLAB_W), F32),
               pltpu.VMEM((2, TL * ROW_TILE, LANES), F32),
               pltpu.VMEM((SUBLANES, TL), jnp.int32),
               pltpu.SMEM((SUBLANES, TL), jnp.int32),
               pltpu.VMEM((1, LANES), F32),
               pltpu.SemaphoreType.DMA((2,)),
               pltpu.SemaphoreType.DMA((1,))]
    return pl.pallas_call(
        functools.partial(_mixer_kernel, nseg, seg, carry_tiles, n_alias, cap_rows, tile0),
        grid=(groups, tiles_per_group),
        in_specs=in_specs, out_specs=out_specs, out_shape=out_shape,
        scratch_shapes=scratch,
        input_output_aliases=aliases,
        compiler_params=pltpu.CompilerParams(dimension_semantics=("arbitrary", "arbitrary"),
                                             vmem_limit_bytes=VMEM_LIMIT),
        name="mixer_prompt" if carry_tiles else "mixer_sample",
    )(x3, x3, mod, mod, h0, cbuf, *weights, cnt0, *aliased)


W_CAST_ROWS = 128

def _expert_kernel(n_tok, row_ref, be_ref, nv_ref, nu_ref, xs_ref, wgu_ref, bgu_ref, wd_ref, bd_ref,
                   yk_hbm, wgu_bf, wd_bf, out_a, out_b, slotv_ref, slots_ref, sem_o, sem_v):
    i = pl.program_id(0)
    n_used = nu_ref[0]
    outs = (out_a, out_b)
    prev = jnp.maximum(i - 1, 0)
    new_expert = jnp.logical_or(i == 0, be_ref[i] != be_ref[prev])

    @pl.when(jnp.logical_and(new_expert, i < n_used))
    def _():
        def cast_gu(r, c):
            rows = pl.ds(pl.multiple_of(r * W_CAST_ROWS, W_CAST_ROWS), W_CAST_ROWS)
            wgu_bf[rows, :] = wgu_ref[0, rows, :].astype(BF16)
            return c
        lax.fori_loop(0, D_MODEL // W_CAST_ROWS, cast_gu, 0)

        def cast_d(r, c):
            rows = pl.ds(pl.multiple_of(r * W_CAST_ROWS, W_CAST_ROWS), W_CAST_ROWS)
            wd_bf[rows, :] = wd_ref[0, rows, :].astype(BF16)
            return c
        lax.fori_loop(0, D_FF // W_CAST_ROWS, cast_d, 0)

    def issue_rows(src, t0, count):
        for q in range(count):
            t = t0 + q
            dst = pl.multiple_of(slots_ref[src, 0, t], PACK_ROWS)
            row0 = pl.multiple_of(t * PACK_ROWS, PACK_ROWS)
            pltpu.make_async_copy(outs[src].at[pl.ds(row0, PACK_ROWS), :],
                                  yk_hbm.at[pl.ds(dst, PACK_ROWS), :],
                                  sem_o.at[src]).start(priority=q % DMA_THREADS)

    def wait_rows(sl):
        pltpu.make_async_copy(outs[sl], yk_hbm.at[pl.ds(0, E_BLK * PACK_ROWS), :], sem_o.at[sl]).wait()

    row = lax.broadcasted_iota(jnp.int32, (E_BLK, 1), 0)
    spare0 = float(TOP_K * n_tok)

    @pl.when(i == 0)
    def _():
        t_i = lax.broadcasted_iota(jnp.int32, (SUBLANES, E_BLK), 1)
        slotv_ref[...] = (TOP_K * n_tok + t_i) * PACK_ROWS
        pltpu.make_async_copy(slotv_ref, slots_ref.at[1], sem_v.at[0]).start()
        out_b[...] = jnp.zeros(out_b.shape, F32)

    def block(cur):
        prv = 1 - cur

        def smem_copy(dst):
            return pltpu.make_async_copy(slotv_ref, slots_ref.at[dst], sem_v.at[0])

        @pl.when(i < n_used)
        def _():
            smem_copy(prv).wait()
            record = xs_ref[pl.ds(PACK_ROWS, E_BLK, stride=ROW_TILE), :]
            pick = (lax.broadcasted_iota(jnp.int32, (SUBLANES, LANES), 1) == be_ref[i]).astype(F32)
            slot_row = lax.dot_general(pick, record, (((1,), (1,)), ((), ())),
                                       precision=lax.Precision.HIGHEST, preferred_element_type=F32)
            t_row = lax.broadcasted_iota(jnp.int32, (SUBLANES, E_BLK), 1)
            slot_row = jnp.where(t_row < nv_ref[i], slot_row, spare0 + t_row.astype(F32))
            slotv_ref[...] = (slot_row * float(PACK_ROWS)).astype(jnp.int32)
            smem_copy(cur).start()

        @pl.when(i < n_used)
        def _():
            live = row < nv_ref[i]
            words = [lax.bitcast_convert_type(xs_ref[pl.ds(c, E_BLK, stride=ROW_TILE), :], jnp.int32)
                     for c in range(PACK_ROWS)]
            lo = [lax.bitcast_convert_type(lax.shift_left(w, 16), F32) for w in words]
            hi = [lax.bitcast_convert_type(w & HI16_MASK, F32) for w in words]
            xb = jnp.where(live, jnp.concatenate(lo + hi, axis=1), 0.0).astype(BF16)

            issue_rows(prv, 0, E_BLK)
            hgu = _dot(xb, wgu_bf[...]) + bgu_ref[0]
            g = jnp.minimum(hgu[:, :D_FF], SWIGLU_LIMIT)
            up = jnp.clip(hgu[:, D_FF:], -SWIGLU_LIMIT, SWIGLU_LIMIT)
            act = (up + 1.0) * (g * jax.nn.sigmoid(SWIGLU_ALPHA * g))
            y = _dot(act.astype(BF16), wd_bf[...]) + bd_ref[0]

            @pl.when(i > 0)
            def _():
                wait_rows(cur)

            y_lo = lax.bitcast_convert_type(y[:, :HALF].astype(BF16).astype(F32), jnp.int32)
            y_hi = lax.bitcast_convert_type(y[:, HALF:].astype(BF16).astype(F32), jnp.int32)
            y_pk = lax.bitcast_convert_type(y_hi | lax.shift_right_logical(y_lo, 16), F32)
            for c in range(PACK_ROWS):
                outs[cur][pl.ds(c, E_BLK, stride=PACK_ROWS), :] = y_pk[:, c * LANES:(c + 1) * LANES]

        @pl.when(i == n_used)
        def _():
            smem_copy(prv).wait()

            def tail(t8, c):
                issue_rows(prv, t8 * SUBLANES, SUBLANES)
                return c
            lax.fori_loop(0, E_BLK // SUBLANES, tail, 0)
            wait_rows(cur)
            wait_rows(prv)

    for parity in range(2):
        @pl.when(i % 2 == parity)
        def _(parity=parity):
            block(parity)


def _expert_call(blk_row, blk_e, blk_valid, n_used, xs, w_gu, b_gu, w_down, b_down, n_tok):
    nb = blk_row.shape[0]
    grid_spec = pltpu.PrefetchScalarGridSpec(
        num_scalar_prefetch=4,
        grid=(nb,),
        in_specs=[
            pl.BlockSpec((E_BLK * ROW_TILE, LANES), lambda i, br, be, nv, nu: (br[i], 0)),
            pl.BlockSpec((1, D_MODEL, 2 * D_FF), lambda i, br, be, nv, nu: (be[i], 0, 0)),
            pl.BlockSpec((1, 1, 2 * D_FF), lambda i, br, be, nv, nu: (be[i], 0, 0)),
            pl.BlockSpec((1, D_FF, D_MODEL), lambda i, br, be, nv, nu: (be[i], 0, 0)),
            pl.BlockSpec((1, 1, D_MODEL), lambda i, br, be, nv, nu: (be[i], 0, 0)),
        ],
        out_specs=pl.BlockSpec(memory_space=pl.ANY),
        scratch_shapes=[pltpu.VMEM((D_MODEL, 2 * D_FF), BF16),
                        pltpu.VMEM((D_FF, D_MODEL), BF16),
                        pltpu.VMEM((E_BLK * PACK_ROWS, LANES), F32),
                        pltpu.VMEM((E_BLK * PACK_ROWS, LANES), F32),
                        pltpu.VMEM((SUBLANES, E_BLK), jnp.int32),
                        pltpu.SMEM((2, SUBLANES, E_BLK), jnp.int32),
                        pltpu.SemaphoreType.DMA((2,)),
                        pltpu.SemaphoreType.DMA((1,))],
    )
    return pl.pallas_call(
        functools.partial(_expert_kernel, n_tok),
        grid_spec=grid_spec,
        out_shape=jax.ShapeDtypeStruct(((TOP_K * n_tok + E_BLK) * PACK_ROWS, LANES), F32),
        compiler_params=pltpu.CompilerParams(dimension_semantics=("arbitrary",),
                                             vmem_limit_bytes=VMEM_LIMIT),
        name="experts",
    )(blk_row, blk_e, blk_valid, n_used, xs, w_gu, b_gu, w_down, b_down)


def _combine_kernel(nseg, seg, y0_ref, y1_ref, y2_ref, y3_ref, x1_ref, gate_ref, mod_ref, g2_ref, b2_ref,
                    out_ref):
    gates = gate_ref[...]
    ffn = None
    for k, y_ref in enumerate((y0_ref, y1_ref, y2_ref, y3_ref)):
        words = [lax.bitcast_convert_type(y_ref[pl.ds(c, C_TL, stride=PACK_ROWS), :], jnp.int32)
                 for c in range(PACK_ROWS)]
        rows = jnp.concatenate(
            [lax.bitcast_convert_type(lax.shift_left(w, 16), F32) for w in words]
            + [lax.bitcast_convert_type(w & HI16_MASK, F32) for w in words], axis=1)
        term = gates[:, k:k + 1] * rows
        ffn = term if ffn is None else ffn + term
    if nseg == 1:
        gate2 = mod_ref[0, 5:6, :]
    else:
        gate2 = jnp.concatenate(
            [jnp.broadcast_to(mod_ref[s, 5:6, :], (seg, D_MODEL)) for s in range(nseg)], axis=0)
    xa = DEEPNORM_ALPHA * x1_ref[...] + (1.0 + gate2) * ffn
    out_ref[...] = _layer_norm(xa, g2_ref[...], b2_ref[...])


def _combine_call(yk, x1, gates, mod, ln2_g, ln2_b, nseg, seg, n_tiles, tile0, tiles_per_group, n_tok):
    tiles_total = n_tok // C_TL

    def yk_spec(k):
        return pl.BlockSpec((C_TL * PACK_ROWS, LANES), lambda i, k=k: (k * tiles_total + tile0 + i, 0))

    in_specs = [yk_spec(k) for k in range(TOP_K)] + [
        pl.BlockSpec((C_TL, D_MODEL), lambda i: (tile0 + i, 0)),
        pl.BlockSpec((C_TL, LANES), lambda i: (tile0 + i, 0)),
        pl.BlockSpec((nseg, 6, D_MODEL), lambda i: (i // tiles_per_group, 0, 0)),
        pl.BlockSpec((1, D_MODEL), lambda i: (0, 0)),
        pl.BlockSpec((1, D_MODEL), lambda i: (0, 0)),
    ]
    return pl.pallas_call(
        functools.partial(_combine_kernel, nseg, seg),
        grid=(n_tiles,),
        in_specs=in_specs,
        out_specs=pl.BlockSpec((C_TL, D_MODEL), lambda i: (i, 0)),
        out_shape=jax.ShapeDtypeStruct((n_tiles * C_TL, D_MODEL), F32),
        compiler_params=pltpu.CompilerParams(dimension_semantics=("arbitrary",),
                                             vmem_limit_bytes=VMEM_LIMIT),
        name="combine_prompt" if nseg == 1 else "combine_sample",
    )(yk, yk, yk, yk, x1, gates, mod, ln2_g, ln2_b)


def _s5_tables(lam_re, lam_im, log_dt, b_re, b_im, c_re, c_im):
    dt = jnp.exp(log_dt.astype(F32))[:, None]
    lam = lax.complex(lam_re.astype(F32), lam_im.astype(F32))
    lam_dt = lam * dt
    lam_bar = jnp.exp(lam_dt)
    b_bar = ((lam_bar - 1.0) / lam)[..., None] * lax.complex(b_re.astype(F32), b_im.astype(F32))
    gl = S5_GROUPS // N_SLAB
    eye = jnp.eye(gl, dtype=F32)

    def b_slab(part):
        a = part.reshape(N_SLAB, gl, S5_STATE, S5_GROUP)
        return jnp.einsum('sgph,gk->sghkp', a, eye).reshape(N_SLAB, gl * S5_GROUP, gl * S5_STATE)

    def b_rows(b):
        return jnp.concatenate([b_slab(b.real), b_slab(b.imag)], axis=-1)

    bmat = jnp.concatenate([b_rows(b_bar), b_rows(lam_bar[..., None] * b_bar)], axis=1).astype(BF16)

    def c_slab(part):
        a = part.reshape(N_SLAB, gl, S5_GROUP, S5_STATE)
        return jnp.einsum('sghp,gk->sgpkh', a, eye).reshape(N_SLAB, gl * S5_STATE, gl * S5_GROUP)

    cmat = jnp.concatenate([c_slab(c_re.astype(F32)), -c_slab(c_im.astype(F32))], axis=1).astype(BF16)

    row = jnp.arange(SUBLANES, dtype=F32)[:, None, None]

    def power(k):
        return jnp.exp(lam_dt[None] * k)

    tabs = []
    for d in (2, 4):
        pw = power(jnp.full_like(row, float(d)))
        mask = (row >= d).astype(F32)
        tabs += [pw.real * mask, pw.imag * mask]
    pw = power(row + 1.0)
    tabs += [pw.real, pw.imag]
    tab = jnp.stack(tabs, axis=0)
    tab = tab.reshape(len(tabs), SUBLANES, N_SLAB, gl * S5_STATE).transpose(2, 0, 1, 3)
    return bmat, cmat, tab


def _state_to_slab(re, im):
    n = re.shape[0]
    return jnp.concatenate([re.reshape(n, N_SLAB, HALF), im.reshape(n, N_SLAB, HALF)], axis=-1).astype(F32)


def _slab_to_state(s):
    n = s.shape[0]
    re = s[:, :, :HALF].reshape(n, S5_GROUPS, S5_STATE)
    im = s[:, :, HALF:].reshape(n, S5_GROUPS, S5_STATE)
    return re, im


def _block_table(counts, n_tok):
    n_assign = n_tok * TOP_K
    n_blocks = (n_assign + N_EXPERTS * (E_BLK - 1) + E_BLK - 1) // E_BLK + 1
    cap_blocks = n_tok // E_BLK
    nblk = (counts + E_BLK - 1) // E_BLK
    cum = jnp.cumsum(nblk)
    n_used = cum[-1]
    i = jnp.arange(n_blocks, dtype=jnp.int32)
    ii = jnp.maximum(jnp.minimum(i, n_used - 1), 0)
    before = (cum[None, :] <= ii[:, None]).astype(jnp.int32)
    e = jnp.minimum(jnp.sum(before, axis=1), N_EXPERTS - 1)
    start_e = jnp.sum(before * nblk[None, :], axis=1)
    is_e = jnp.concatenate([jnp.ones((n_blocks, 1), jnp.int32), before[:, :-1]], axis=1) - before
    count_e = jnp.sum(is_e * counts[None, :], axis=1)
    b = ii - start_e
    blk_row = (e * cap_blocks + b).astype(jnp.int32)
    valid = jnp.where(i < n_used, jnp.minimum(E_BLK, count_e - b * E_BLK), 0).astype(jnp.int32)
    return blk_row, e, valid, n_used.astype(jnp.int32).reshape(1)


def kernel(x_prompt, x_sample, c_prompt, c_sample, state_s5_re, state_s5_im, state_conv, w_ada, b_ada, w_in,
           s5_lam_re, s5_lam_im, s5_log_dt, s5_b_re, s5_b_im, s5_c_re, s5_c_im, s5_d, w_glu, b_glu, conv_w,
           w_out, ln1_g, ln1_b, w_router, b_router, w_gu, b_gu, w_down, b_down, ln2_g, ln2_b):
    assert DEPTH == 1 and w_ada.shape[0] == 1
    n_p, l_p, _ = x_prompt.shape
    n_s, l_s, _ = x_sample.shape
    t_p, t_s = n_p * l_p, n_s * l_s
    n_tok = t_p + t_s
    tiles_p, tiles_s = t_p // TL, t_s // TL
    seq_per_tile = TL // l_s
    assert l_p % TL == 0 and TL % l_s == 0 and n_s % seq_per_tile == 0

    c_all = jnp.concatenate([c_prompt, c_sample], axis=0).astype(F32)
    pad = (-c_all.shape[0]) % SUBLANES
    c_all = jnp.pad(c_all, ((0, pad), (0, 0)))
    mod = _ada_call(c_all, w_ada[0], b_ada[0].reshape(1, -1)).reshape(-1, 6, D_MODEL)
    mod_p, mod_s = mod[:n_p], mod[n_p:n_p + n_s]

    bmat, cmat, tab = _s5_tables(s5_lam_re[0], s5_lam_im[0], s5_log_dt[0], s5_b_re[0], s5_b_im[0],
                                 s5_c_re[0], s5_c_im[0])
    wr = jnp.pad(w_router[0], ((0, 0), (0, LANES - N_EXPERTS))).astype(BF16)
    br = jnp.pad(b_router[0].astype(F32), (0, LANES - N_EXPERTS), constant_values=NEG_BIG).reshape(1, LANES)
    weights = (w_in[0].astype(BF16), bmat, cmat, tab, s5_d[0].reshape(1, D_S5).astype(F32),
               w_glu[0].astype(BF16), b_glu[0].reshape(1, D_S5).astype(F32), conv_w[0].astype(F32),
               w_out[0].astype(BF16), ln1_g[0].reshape(1, D_MODEL).astype(F32),
               ln1_b[0].reshape(1, D_MODEL).astype(F32), wr, br)

    h0_p = jnp.zeros((n_p, N_SLAB, SLAB_W), F32)
    cb_p = jnp.zeros((n_p, CONV_W - 1, D_CONV), F32)
    h0_s = _state_to_slab(state_s5_re[0], state_s5_im[0])
    cb_s = state_conv[0].astype(F32)

    assert n_tok % E_BLK == 0 and E_BLK % TL == 0
    cnt0 = jnp.zeros((1, LANES), F32)
    outs_p = _mixer_call(x_prompt, mod_p, h0_p, cb_p, weights, cnt0, 1, TL, l_p // TL, 0, n_tok, ())
    x1, gates, xs, cnt_p, s_p, conv_p = outs_p
    xs3 = x_sample.reshape(n_s // seq_per_tile, TL, D_MODEL)
    outs_s = _mixer_call(xs3, mod_s, h0_s, cb_s, weights, cnt_p, seq_per_tile, l_s, 1, tiles_p, n_tok,
                         (x1, gates, xs))
    x1, gates, xs, cnt, s_s, conv_s = outs_s

    counts = cnt[0, :N_EXPERTS].astype(jnp.int32)
    blk_row, blk_e, blk_valid, n_used = _block_table(counts, n_tok)
    yk = _expert_call(blk_row, blk_e, blk_valid, n_used, xs, w_gu[0].astype(F32),
                      b_gu[0].reshape(N_EXPERTS, 1, 2 * D_FF).astype(F32), w_down[0].astype(F32),
                      b_down[0].reshape(N_EXPERTS, 1, D_MODEL).astype(F32), n_tok)
    g2 = ln2_g[0].reshape(1, D_MODEL).astype(F32)
    b2 = ln2_b[0].reshape(1, D_MODEL).astype(F32)
    assert l_p % C_TL == 0 and C_TL % l_s == 0 and t_s % C_TL == 0
    y_p = _combine_call(yk, x1, gates, mod_p, g2, b2, 1, C_TL, t_p // C_TL, 0, l_p // C_TL, n_tok)
    y_s = _combine_call(yk, x1, gates, mod_s, g2, b2, C_TL // l_s, l_s, t_s // C_TL, t_p // C_TL, 1, n_tok)

    p_re, p_im = _slab_to_state(s_p)
    s_re, s_im = _slab_to_state(s_s)
    return (y_p.reshape(n_p, l_p, D_MODEL), y_s.reshape(n_s, l_s, D_MODEL),
            p_re[None], p_im[None], conv_p[None], s_re[None], s_im[None], conv_s[None])
```

```python
import functools
import math

import jax
import jax.numpy as jnp
from jax import lax
from jax.experimental import pallas as pl
from jax.experimental.pallas import tpu as pltpu

F32 = jnp.float32
BF16 = jnp.bfloat16

D_MODEL = 1024
DEPTH = 1
D_S5 = 512
D_CONV = 512
S5_GROUP = 16
S5_GROUPS = 32
S5_STATE = 64
CONV_W = 3
N_EXPERTS = 32
TOP_K = 4
D_FF = 1024
SWIGLU_LIMIT = 7.0
SWIGLU_ALPHA = 1.702
LN_EPS = 1e-5
DEEPNORM_ALPHA = (2.0 * DEPTH) ** 0.25

TL = 256
C_TL = 1024
SUBLANES = 8
LANES = 128
N_SLAB = 4
SLAB_W = 1024
HALF = 512
E_BLK = 512
ROW_TILE = D_MODEL // LANES
PACK_ROWS = HALF // LANES
HI16_MASK = -65536
DMA_THREADS = 2
NEG_BIG = -1e30
VMEM_LIMIT = 56 * 1024 * 1024


def _dot(a, b):
    return jnp.dot(a, b, preferred_element_type=F32)


def _layer_norm(x, g, b):
    mu = jnp.mean(x, axis=-1, keepdims=True)
    xc = x - mu
    var = jnp.mean(xc * xc, axis=-1, keepdims=True)
    return xc * lax.rsqrt(var + LN_EPS) * g + b


def _ada_kernel(c_ref, w_ref, b_ref, o_ref):
    c = c_ref[...]
    s = c * jax.nn.sigmoid(c)
    o_ref[...] = _dot(s.astype(BF16), w_ref[...].astype(BF16)) + b_ref[...]


def _ada_call(c_all, w_ada, b_ada):
    rows = c_all.shape[0]
    n_out = w_ada.shape[1]
    tn = 768
    return pl.pallas_call(
        _ada_kernel,
        grid=(n_out // tn,),
        in_specs=[pl.BlockSpec((rows, D_MODEL), lambda i: (0, 0)),
                  pl.BlockSpec((D_MODEL, tn), lambda i: (0, i)),
                  pl.BlockSpec((1, tn), lambda i: (0, i))],
        out_specs=pl.BlockSpec((rows, tn), lambda i: (0, i)),
        out_shape=jax.ShapeDtypeStruct((rows, n_out), F32),
        compiler_params=pltpu.CompilerParams(dimension_semantics=("arbitrary",),
                                             vmem_limit_bytes=VMEM_LIMIT),
        name="ada_mod",
    )(c_all, w_ada, b_ada)


def _mixer_kernel(nseg, seg, carry_tiles, n_alias, cap_rows, tile0,
                  x_ref, xn_ref, mod_ref, modn_ref, h0_ref, cbuf_ref, win_ref, bmat_ref, cmat_ref, tab_ref, d_ref,
                  wglu_ref, bglu_ref, convw_ref, wout_ref, g1_ref, b1_ref, wr_ref, br_ref, cnt0_ref, *rest):
    rest = rest[n_alias:]
    (x1_ref, gate_ref, xs_hbm, cnt_out_ref, sout_ref, cout_ref,
     p_ref, bu_ref, hs_ref, xcs_ref, carry_ref, h2s_ref, posv_ref, poss_ref, cnt_ref, sem_s, sem_v) = rest
    g_id = pl.program_id(0)
    j = pl.program_id(1)
    step = g_id * pl.num_programs(1) + j
    is_last = step == pl.num_programs(0) * pl.num_programs(1) - 1
    slot = step % 2

    def rows_of(m_ref, k):
        if nseg == 1:
            return m_ref[0, k:k + 1, :]
        return jnp.concatenate(
            [jnp.broadcast_to(m_ref[s, k:k + 1, :], (seg, D_MODEL)) for s in range(nseg)], axis=0)

    def mod_rows(k):
        return rows_of(mod_ref, k)

    def front_pieces(xt_ref, m_ref, dst):
        def adaln():
            h = xt_ref[0] * (1.0 + rows_of(m_ref, 1)) + rows_of(m_ref, 0)
            hs_ref[...] = h.astype(BF16)

        def in_proj(c):
            cols = slice(c * HALF, (c + 1) * HALF)
            p_ref[dst, :, cols] = _dot(hs_ref[...], win_ref[:, cols])

        def s5_in(i):
            u_i = p_ref[dst, :, i * LANES:(i + 1) * LANES]
            row_in_tile = lax.broadcasted_iota(jnp.int32, u_i.shape, 0) % SUBLANES
            u_prev = jnp.where(row_in_tile == 0, 0.0, pltpu.roll(u_i, 1, 0))
            lhs = jnp.concatenate([u_i, u_prev], axis=1).astype(BF16)
            bu_ref[dst * N_SLAB + i] = _dot(lhs, bmat_ref[i])

        return ([adaln] + [functools.partial(in_proj, c) for c in range(2 * D_MODEL // HALF)]
                + [functools.partial(s5_in, i) for i in range(N_SLAB)])

    @pl.when(step == 0)
    def _():
        for piece in front_pieces(x_ref, mod_ref, 0):
            piece()

    nxt_front = front_pieces(xn_ref, modn_ref, 1 - slot)

    def run_front(n):
        for _ in range(n):
            nxt_front.pop(0)()

    x = x_ref[0]
    bu0 = slot * N_SLAB

    prev = 1 - slot
    smem_copy = pltpu.make_async_copy(posv_ref, poss_ref, sem_v.at[0])

    @pl.when(step == 0)
    def _():
        k_i = lax.broadcasted_iota(jnp.int32, (SUBLANES, TL), 0)
        t_i = lax.broadcasted_iota(jnp.int32, (SUBLANES, TL), 1)
        posv_ref[...] = (N_EXPERTS * cap_rows + k_i * TL + t_i) * ROW_TILE
        smem_copy.start()
        h2s_ref[...] = jnp.zeros(h2s_ref.shape, F32)

    smem_copy.wait()

    def issue_rows(src_slot, t0, n_rows):
        for q in range(n_rows):
            t = t0 + q
            src = pl.multiple_of(t * ROW_TILE, ROW_TILE)
            for k in range(TOP_K):
                dst = pl.multiple_of(poss_ref[k, t], ROW_TILE)
                pltpu.make_async_copy(h2s_ref.at[src_slot, pl.ds(src, ROW_TILE), :],
                                      xs_hbm.at[pl.ds(dst, ROW_TILE), :],
                                      sem_s.at[src_slot]).start(priority=k % DMA_THREADS)

    def wait_scatter(sl):
        for _ in range(TOP_K):
            pltpu.make_async_copy(h2s_ref.at[sl], xs_hbm.at[pl.ds(0, TL * ROW_TILE), :], sem_s.at[sl]).wait()

    if carry_tiles:
        @pl.when(j == 0)
        def _():
            carry_ref[...] = h0_ref[0]

    for s in range(nseg):
        if carry_tiles:
            init = tuple(carry_ref[i:i + 1, :] for i in range(N_SLAB))
        else:
            init = tuple(h0_ref[s, i:i + 1, :] for i in range(N_SLAB))

        def scan_body(r, carry, s=s):
            issue_rows(prev, s * seg + r * (2 * SUBLANES), 2 * SUBLANES)
            new = []
            for i in range(N_SLAB):
                cr = carry[i][:, :HALF]
                ci = carry[i][:, HALF:]
                for half in range(2):
                    row0 = pl.multiple_of(s * seg + r * (2 * SUBLANES) + half * SUBLANES, SUBLANES)
                    blk = bu_ref[bu0 + i, pl.ds(row0, SUBLANES), :]
                    xr = blk[:, :HALF]
                    xi = blk[:, HALF:]
                    for k, d in enumerate((2, 4)):
                        lr = tab_ref[i, 2 * k]
                        li = tab_ref[i, 2 * k + 1]
                        rr = pltpu.roll(xr, d, 0)
                        ri = pltpu.roll(xi, d, 0)
                        xr, xi = xr + (lr * rr - li * ri), xi + (lr * ri + li * rr)
                    pr = tab_ref[i, 4]
                    pi_ = tab_ref[i, 5]
                    crb = jnp.broadcast_to(cr, (SUBLANES, HALF))
                    cib = jnp.broadcast_to(ci, (SUBLANES, HALF))
                    xr, xi = xr + (pr * crb - pi_ * cib), xi + (pr * cib + pi_ * crb)
                    bu_ref[bu0 + i, pl.ds(row0, SUBLANES), :] = jnp.concatenate([xr, xi], axis=1)
                    cr = xr[SUBLANES - 1:SUBLANES, :]
                    ci = xi[SUBLANES - 1:SUBLANES, :]
                new.append(jnp.concatenate([cr, ci], axis=1))
            return tuple(new)

        final = lax.fori_loop(0, seg // (2 * SUBLANES), scan_body, init)
        for i in range(N_SLAB):
            sout_ref[s, i:i + 1, :] = final[i]
            if carry_tiles:
                carry_ref[i:i + 1, :] = final[i]

    y = jnp.concatenate([_dot(bu_ref[bu0 + i].astype(BF16), cmat_ref[i]) for i in range(N_SLAB)], axis=1)
    y = y + d_ref[...] * p_ref[slot, :, :D_S5]
    z = jax.nn.gelu(y)
    z = z * jax.nn.sigmoid(_dot(z.astype(BF16), wglu_ref[...]) + bglu_ref[...])

    xc = p_ref[slot, :, D_S5 + D_CONV:D_S5 + 2 * D_CONV] * p_ref[slot, :, D_S5 + 2 * D_CONV:]
    w0 = convw_ref[0:1, :]
    w1 = convw_ref[1:2, :]
    w2 = convw_ref[2:3, :]
    conv_parts = []
    for s in range(nseg):
        base = s * (seg + SUBLANES)
        xc_s = xc[s * seg:(s + 1) * seg]
        tail = xc_s[seg - 2:seg]
        xcs_ref[pl.ds(base + SUBLANES, seg), :] = xc_s
        if carry_tiles:
            @pl.when(j == 0)
            def _(s=s, base=base):
                xcs_ref[pl.ds(base + SUBLANES - 2, 2), :] = cbuf_ref[s]
        else:
            xcs_ref[pl.ds(base + SUBLANES - 2, 2), :] = cbuf_ref[s]
        xm1 = xcs_ref[pl.ds(base + SUBLANES - 1, seg), :]
        xm2 = xcs_ref[pl.ds(base + SUBLANES - 2, seg), :]
        conv_parts.append(w0 * xm2 + w1 * xm1 + w2 * xc_s)
        cout_ref[s] = tail
        if carry_tiles:
            xcs_ref[pl.ds(base + SUBLANES - 2, 2), :] = tail
    conv = conv_parts[0] if nseg == 1 else jnp.concatenate(conv_parts, axis=0)
    y_b = p_ref[slot, :, D_S5:D_S5 + D_CONV] * conv

    run_front(2)
    mix = _dot(jnp.concatenate([z, y_b], axis=1).astype(BF16), wout_ref[...])
    x1 = _layer_norm(DEEPNORM_ALPHA * x + (1.0 + mod_rows(2)) * mix, g1_ref[...], b1_ref[...])
    x1_ref[...] = x1
    run_front(1)
    h2 = x1 * (1.0 + mod_rows(4)) + mod_rows(3)
    lo_bits = lax.bitcast_convert_type(h2[:, :HALF].astype(BF16).astype(F32), jnp.int32)
    hi_bits = lax.bitcast_convert_type(h2[:, HALF:].astype(BF16).astype(F32), jnp.int32)
    packed = lax.bitcast_convert_type(hi_bits | lax.shift_right_logical(lo_bits, 16), F32)
    for c in range(PACK_ROWS):
        h2s_ref[slot, pl.ds(c, TL, stride=ROW_TILE), :] = packed[:, c * LANES:(c + 1) * LANES]

    logits = _dot(h2.astype(BF16), wr_ref[...]) + br_ref[...]
    lane = lax.broadcasted_iota(jnp.int32, logits.shape, 1)
    lane_f = lane.astype(F32)
    vals, sels, hots = [], [], []
    cur = logits
    for _ in range(TOP_K):
        m = jnp.max(cur, axis=-1, keepdims=True)
        am = jnp.min(jnp.where(cur == m, lane_f, float(LANES)), axis=-1, keepdims=True)
        hot = lane_f == am
        vals.append(m)
        sels.append(am)
        hots.append(hot)
        cur = jnp.where(hot, -jnp.inf, cur)
    exps = [jnp.exp(v - vals[0]) for v in vals]
    inv = 1.0 / (exps[0] + exps[1] + exps[2] + exps[3])
    gate_out = jnp.zeros(logits.shape, F32)
    for k in range(TOP_K):
        gate_out = jnp.where(lane == k, exps[k] * inv, gate_out)
    gate_ref[...] = gate_out
    run_front(1)

    tok = ((tile0 + step) * TL + lax.broadcasted_iota(jnp.int32, logits.shape, 0)).astype(F32)
    record = jnp.zeros(logits.shape, F32)
    for k in range(TOP_K):
        record = jnp.where(hots[k], tok + float(k * cap_rows), record)
    h2s_ref[slot, pl.ds(PACK_ROWS, TL, stride=ROW_TILE), :] = record
    run_front(2)

    @pl.when(step == 0)
    def _():
        cnt_ref[...] = cnt0_ref[...]

    chosen = jnp.zeros(logits.shape, F32)
    for k in range(TOP_K):
        chosen = jnp.where(hots[k], 1.0, chosen)
    r_i = lax.broadcasted_iota(jnp.int32, (TL, TL), 0)
    c_i = lax.broadcasted_iota(jnp.int32, (TL, TL), 1)
    before = jnp.where(c_i < r_i, 1.0, 0.0).astype(BF16)
    rank_base = _dot(before, chosen.astype(BF16)) + cnt_ref[...]
    pos_mat = jnp.zeros(logits.shape, F32)
    for k in range(TOP_K):
        rank_k = jnp.sum(jnp.where(hots[k], rank_base, 0.0), axis=-1, keepdims=True)
        pos_mat = jnp.where(lane == k, (sels[k] * float(cap_rows) + rank_k) * float(ROW_TILE), pos_mat)
    cnt_ref[...] = cnt_ref[...] + jnp.sum(chosen, axis=0, keepdims=True)
    cnt_out_ref[...] = cnt_ref[...]
    run_front(len(nxt_front))
    pos_t =pos_mat.T[:SUBLANES, :].astype(jnp.int32)
    posv_ref[...] = pos_t
    smem_copy.start()
    wait_scatter(prev)

    @pl.when(is_last)
    def _():
        smem_copy.wait()

        def tail_body(t8, c):
            issue_rows(slot, t8 * SUBLANES, SUBLANES)
            return c
        lax.fori_loop(0, TL // SUBLANES, tail_body, 0)
        wait_scatter(slot)


def _mixer_call(x3, mod, h0, cbuf, weights, cnt0, nseg, seg, tiles_per_group, tile0, n_tok, aliased):
    groups = x3.shape[0]
    nseq = mod.shape[0]
    carry_tiles = nseg == 1
    n_alias = len(aliased)
    n_tiles = n_tok // TL
    cap_rows = n_tok

    def full(a):
        nd = a.ndim
        return pl.BlockSpec(a.shape, lambda g, j, nd=nd: (0,) * nd)

    def tile_map(g, j):
        return (tile0 + g * tiles_per_group + j, 0)

    def next_g(g, j):
        return jnp.where(j + 1 < tiles_per_group, g, jnp.minimum(g + 1, groups - 1))

    def next_j(g, j):
        return jnp.where(j + 1 < tiles_per_group, j + 1, 0)

    in_specs = [pl.BlockSpec((1, TL, D_MODEL), lambda g, j: (g, j, 0)),
                pl.BlockSpec((1, TL, D_MODEL), lambda g, j: (next_g(g, j), next_j(g, j), 0)),
                pl.BlockSpec((nseg, 6, D_MODEL), lambda g, j: (g, 0, 0)),
                pl.BlockSpec((nseg, 6, D_MODEL), lambda g, j: (next_g(g, j), 0, 0)),
                pl.BlockSpec((nseg, N_SLAB, SLAB_W), lambda g, j: (g, 0, 0)),
                pl.BlockSpec((nseg, CONV_W - 1, D_CONV), lambda g, j: (g, 0, 0))]
    in_specs += [full(w) for w in weights]
    in_specs += [full(cnt0)]
    in_specs += [pl.BlockSpec(memory_space=pl.ANY)] * n_alias
    out_shape = [jax.ShapeDtypeStruct((n_tok, D_MODEL), F32),
                 jax.ShapeDtypeStruct((n_tok, LANES), F32),
                 jax.ShapeDtypeStruct(((N_EXPERTS * cap_rows + TOP_K * TL) * ROW_TILE, LANES), F32),
                 jax.ShapeDtypeStruct((1, LANES), F32),
                 jax.ShapeDtypeStruct((nseq, N_SLAB, SLAB_W), F32),
                 jax.ShapeDtypeStruct((nseq, CONV_W - 1, D_CONV), F32)]
    out_specs = [pl.BlockSpec((TL, D_MODEL), tile_map),
                 pl.BlockSpec((TL, LANES), tile_map),
                 pl.BlockSpec(memory_space=pl.ANY),
                 pl.BlockSpec((1, LANES), lambda g, j: (0, 0)),
                 pl.BlockSpec((nseg, N_SLAB, SLAB_W), lambda g, j: (g, 0, 0)),
                 pl.BlockSpec((nseg, CONV_W - 1, D_CONV), lambda g, j: (g, 0, 0))]
    n_in = 6 + len(weights) + 1
    aliases = {n_in + k: k for k in range(n_alias)}
    scratch = [pltpu.VMEM((2, TL, 2 * D_MODEL), F32),
               pltpu.VMEM((2 * N_SLAB, TL, SLAB_W), F32),
               pltpu.VMEM((TL, D_MODEL), BF16),
               pltpu.VMEM((nseg * (seg + SUBLANES), D_CONV), F32),
               pltpu.VMEM((N_SLAB, SLAB_W), F32),
               pltpu.VMEM((2, TL * ROW_TILE, LANES), F32),
               pltpu.VMEM((SUBLANES, TL), jnp.int32),
               pltpu.SMEM((SUBLANES, TL), jnp.int32),
               pltpu.VMEM((1, LANES), F32),
               pltpu.SemaphoreType.DMA((2,)),
               pltpu.SemaphoreType.DMA((1,))]
    return pl.pallas_call(
        functools.partial(_mixer_kernel, nseg, seg, carry_tiles, n_alias, cap_rows, tile0),
        grid=(groups, tiles_per_group),
        in_specs=in_specs, out_specs=out_specs, out_shape=out_shape,
        scratch_shapes=scratch,
        input_output_aliases=aliases,
        compiler_params=pltpu.CompilerParams(dimension_semantics=("arbitrary", "arbitrary"),
                                             vmem_limit_bytes=VMEM_LIMIT),
        name="mixer_prompt" if carry_tiles else "mixer_sample",
    )(x3, x3, mod, mod, h0, cbuf, *weights, cnt0, *aliased)


W_CAST_ROWS = 128

def _expert_kernel(n_tok, row_ref, be_ref, nv_ref, nu_ref, xs_ref, wgu_ref, bgu_ref, wd_ref, bd_ref,
                   yk_hbm, wgu_bf, wd_bf, out_a, out_b, slotv_ref, slots_ref, sem_o, sem_v):
    i = pl.program_id(0)
    n_used = nu_ref[0]
    outs = (out_a, out_b)
    prev = jnp.maximum(i - 1, 0)
    new_expert = jnp.logical_or(i == 0, be_ref[i] != be_ref[prev])

    @pl.when(jnp.logical_and(new_expert, i < n_used))
    def _():
        def cast_gu(r, c):
            rows = pl.ds(pl.multiple_of(r * W_CAST_ROWS, W_CAST_ROWS), W_CAST_ROWS)
            wgu_bf[rows, :] = wgu_ref[0, rows, :].astype(BF16)
            return c
        lax.fori_loop(0, D_MODEL // W_CAST_ROWS, cast_gu, 0)

        def cast_d(r, c):
            rows = pl.ds(pl.multiple_of(r * W_CAST_ROWS, W_CAST_ROWS), W_CAST_ROWS)
            wd_bf[rows, :] = wd_ref[0, rows, :].astype(BF16)
            return c
        lax.fori_loop(0, D_FF // W_CAST_ROWS, cast_d, 0)

    def issue_rows(src, t0, count):
        for q in range(count):
            t = t0 + q
            dst = pl.multiple_of(slots_ref[src, 0, t], PACK_ROWS)
            row0 = pl.multiple_of(t * PACK_ROWS, PACK_ROWS)
            pltpu.make_async_copy(outs[src].at[pl.ds(row0, PACK_ROWS), :],
                                  yk_hbm.at[pl.ds(dst, PACK_ROWS), :],
                                  sem_o.at[src]).start(priority=q % DMA_THREADS)

    def wait_rows(sl):
        pltpu.make_async_copy(outs[sl], yk_hbm.at[pl.ds(0, E_BLK * PACK_ROWS), :], sem_o.at[sl]).wait()

    row = lax.broadcasted_iota(jnp.int32, (E_BLK, 1), 0)
    spare0 = float(TOP_K * n_tok)

    @pl.when(i == 0)
    def _():
        t_i = lax.broadcasted_iota(jnp.int32, (SUBLANES, E_BLK), 1)
        slotv_ref[...] = (TOP_K * n_tok + t_i) * PACK_ROWS
        pltpu.make_async_copy(slotv_ref, slots_ref.at[1], sem_v.at[0]).start()
        out_b[...] = jnp.zeros(out_b.shape, F32)

    def block(cur):
        prv = 1 - cur

        def smem_copy(dst):
            return pltpu.make_async_copy(slotv_ref, slots_ref.at[dst], sem_v.at[0])

        @pl.when(i < n_used)
        def _():
            smem_copy(prv).wait()
            record = xs_ref[pl.ds(PACK_ROWS, E_BLK, stride=ROW_TILE), :]
            pick = (lax.broadcasted_iota(jnp.int32, (SUBLANES, LANES), 1) == be_ref[i]).astype(F32)
            slot_row = lax.dot_general(pick, record, (((1,), (1,)), ((), ())),
                                       precision=lax.Precision.HIGHEST, preferred_element_type=F32)
            t_row = lax.broadcasted_iota(jnp.int32, (SUBLANES, E_BLK), 1)
            slot_row = jnp.where(t_row < nv_ref[i], slot_row, spare0 + t_row.astype(F32))
            slotv_ref[...] = (slot_row * float(PACK_ROWS)).astype(jnp.int32)
            smem_copy(cur).start()

        @pl.when(i < n_used)
        def _():
            live = row < nv_ref[i]
            words = [lax.bitcast_convert_type(xs_ref[pl.ds(c, E_BLK, stride=ROW_TILE), :], jnp.int32)
                     for c in range(PACK_ROWS)]
            lo = [lax.bitcast_convert_type(lax.shift_left(w, 16), F32) for w in words]
            hi = [lax.bitcast_convert_type(w & HI16_MASK, F32) for w in words]
            xb = jnp.where(live, jnp.concatenate(lo + hi, axis=1), 0.0).astype(BF16)

            issue_rows(prv, 0, E_BLK)
            hgu = _dot(xb, wgu_bf[...]) + bgu_ref[0]
            g = jnp.minimum(hgu[:, :D_FF], SWIGLU_LIMIT)
            up = jnp.clip(hgu[:, D_FF:], -SWIGLU_LIMIT, SWIGLU_LIMIT)
            act = (up + 1.0) * (g * jax.nn.sigmoid(SWIGLU_ALPHA * g))
            y = _dot(act.astype(BF16), wd_bf[...]) + bd_ref[0]

            @pl.when(i > 0)
            def _():
                wait_rows(cur)

            y_lo = lax.bitcast_convert_type(y[:, :HALF].astype(BF16).astype(F32), jnp.int32)
            y_hi = lax.bitcast_convert_type(y[:, HALF:].astype(BF16).astype(F32), jnp.int32)
            y_pk = lax.bitcast_convert_type(y_hi | lax.shift_right_logical(y_lo, 16), F32)
            for c in range(PACK_ROWS):
                outs[cur][pl.ds(c, E_BLK, stride=PACK_ROWS), :] = y_pk[:, c * LANES:(c + 1) * LANES]

        @pl.when(i == n_used)
        def _():
            smem_copy(prv).wait()

            def tail(t8, c):
                issue_rows(prv, t8 * SUBLANES, SUBLANES)
                return c
            lax.fori_loop(0, E_BLK // SUBLANES, tail, 0)
            wait_rows(cur)
            wait_rows(prv)

    for parity in range(2):
        @pl.when(i % 2 == parity)
        def _(parity=parity):
            block(parity)


def _expert_call(blk_row, blk_e, blk_valid, n_used, xs, w_gu, b_gu, w_down, b_down, n_tok):
    nb = blk_row.shape[0]
    grid_spec = pltpu.PrefetchScalarGridSpec(
        num_scalar_prefetch=4,
        grid=(nb,),
        in_specs=[
            pl.BlockSpec((E_BLK * ROW_TILE, LANES), lambda i, br, be, nv, nu: (br[i], 0)),
            pl.BlockSpec((1, D_MODEL, 2 * D_FF), lambda i, br, be, nv, nu: (be[i], 0, 0)),
            pl.BlockSpec((1, 1, 2 * D_FF), lambda i, br, be, nv, nu: (be[i], 0, 0)),
            pl.BlockSpec((1, D_FF, D_MODEL), lambda i, br, be, nv, nu: (be[i], 0, 0)),
            pl.BlockSpec((1, 1, D_MODEL), lambda i, br, be, nv, nu: (be[i], 0, 0)),
        ],
        out_specs=pl.BlockSpec(memory_space=pl.ANY),
        scratch_shapes=[pltpu.VMEM((D_MODEL, 2 * D_FF), BF16),
                        pltpu.VMEM((D_FF, D_MODEL), BF16),
                        pltpu.VMEM((E_BLK * PACK_ROWS, LANES), F32),
                        pltpu.VMEM((E_BLK * PACK_ROWS, LANES), F32),
                        pltpu.VMEM((SUBLANES, E_BLK), jnp.int32),
                        pltpu.SMEM((2, SUBLANES, E_BLK), jnp.int32),
                        pltpu.SemaphoreType.DMA((2,)),
                        pltpu.SemaphoreType.DMA((1,))],
    )
    return pl.pallas_call(
        functools.partial(_expert_kernel, n_tok),
        grid_spec=grid_spec,
        out_shape=jax.ShapeDtypeStruct(((TOP_K * n_tok + E_BLK) * PACK_ROWS, LANES), F32),
        compiler_params=pltpu.CompilerParams(dimension_semantics=("arbitrary",),
                                             vmem_limit_bytes=VMEM_LIMIT),
        name="experts",
    )(blk_row, blk_e, blk_valid, n_used, xs, w_gu, b_gu, w_down, b_down)


def _combine_kernel(nseg, seg, y0_ref, y1_ref, y2_ref, y3_ref, x1_ref, gate_ref, mod_ref, g2_ref, b2_ref,
                    out_ref):
    gates = gate_ref[...]
    ffn = None
    for k, y_ref in enumerate((y0_ref, y1_ref, y2_ref, y3_ref)):
        words = [lax.bitcast_convert_type(y_ref[pl.ds(c, C_TL, stride=PACK_ROWS), :], jnp.int32)
                 for c in range(PACK_ROWS)]
        rows = jnp.concatenate(
            [lax.bitcast_convert_type(lax.shift_left(w, 16), F32) for w in words]
            + [lax.bitcast_convert_type(w & HI16_MASK, F32) for w in words], axis=1)
        term = gates[:, k:k + 1] * rows
        ffn = term if ffn is None else ffn + term
    if nseg == 1:
        gate2 = mod_ref[0, 5:6, :]
    else:
        gate2 = jnp.concatenate(
            [jnp.broadcast_to(mod_ref[s, 5:6, :], (seg, D_MODEL)) for s in range(nseg)], axis=0)
    xa = DEEPNORM_ALPHA * x1_ref[...] + (1.0 + gate2) * ffn
    out_ref[...] = _layer_norm(xa, g2_ref[...], b2_ref[...])


def _combine_call(yk, x1, gates, mod, ln2_g, ln2_b, nseg, seg, n_tiles, tile0, tiles_per_group, n_tok):
    tiles_total = n_tok // C_TL

    def yk_spec(k):
        return pl.BlockSpec((C_TL * PACK_ROWS, LANES), lambda i, k=k: (k * tiles_total + tile0 + i, 0))

    in_specs = [yk_spec(k) for k in range(TOP_K)] + [
        pl.BlockSpec((C_TL, D_MODEL), lambda i: (tile0 + i, 0)),
        pl.BlockSpec((C_TL, LANES), lambda i: (tile0 + i, 0)),
        pl.BlockSpec((nseg, 6, D_MODEL), lambda i: (i // tiles_per_group, 0, 0)),
        pl.BlockSpec((1, D_MODEL), lambda i: (0, 0)),
        pl.BlockSpec((1, D_MODEL), lambda i: (0, 0)),
    ]
    return pl.pallas_call(
        functools.partial(_combine_kernel, nseg, seg),
        grid=(n_tiles,),
        in_specs=in_specs,
        out_specs=pl.BlockSpec((C_TL, D_MODEL), lambda i: (i, 0)),
        out_shape=jax.ShapeDtypeStruct((n_tiles * C_TL, D_MODEL), F32),
        compiler_params=pltpu.CompilerParams(dimension_semantics=("arbitrary",),
                                             vmem_limit_bytes=VMEM_LIMIT),
        name="combine_prompt" if nseg == 1 else "combine_sample",
    )(yk, yk, yk, yk, x1, gates, mod, ln2_g, ln2_b)


def _s5_tables(lam_re, lam_im, log_dt, b_re, b_im, c_re, c_im):
    dt = jnp.exp(log_dt.astype(F32))[:, None]
    lam = lax.complex(lam_re.astype(F32), lam_im.astype(F32))
    lam_dt = lam * dt
    lam_bar = jnp.exp(lam_dt)
    b_bar = ((lam_bar - 1.0) / lam)[..., None] * lax.complex(b_re.astype(F32), b_im.astype(F32))
    gl = S5_GROUPS // N_SLAB
    eye = jnp.eye(gl, dtype=F32)

    def b_slab(part):
        a = part.reshape(N_SLAB, gl, S5_STATE, S5_GROUP)
        return jnp.einsum('sgph,gk->sghkp', a, eye).reshape(N_SLAB, gl * S5_GROUP, gl * S5_STATE)

    def b_rows(b):
        return jnp.concatenate([b_slab(b.real), b_slab(b.imag)], axis=-1)

    bmat = jnp.concatenate([b_rows(b_bar), b_rows(lam_bar[..., None] * b_bar)], axis=1).astype(BF16)

    def c_slab(part):
        a = part.reshape(N_SLAB, gl, S5_GROUP, S5_STATE)
        return jnp.einsum('sghp,gk->sgpkh', a, eye).reshape(N_SLAB, gl * S5_STATE, gl * S5_GROUP)

    cmat = jnp.concatenate([c_slab(c_re.astype(F32)), -c_slab(c_im.astype(F32))], axis=1).astype(BF16)

    row = jnp.arange(SUBLANES, dtype=F32)[:, None, None]

    def power(k):
        return jnp.exp(lam_dt[None] * k)

    tabs = []
    for d in (2, 4):
        pw = power(jnp.full_like(row, float(d)))
        mask = (row >= d).astype(F32)
        tabs += [pw.real * mask, pw.imag * mask]
    pw = power(row + 1.0)
    tabs += [pw.real, pw.imag]
    tab = jnp.stack(tabs, axis=0)
    tab = tab.reshape(len(tabs), SUBLANES, N_SLAB, gl * S5_STATE).transpose(2, 0, 1, 3)
    return bmat, cmat, tab


def _state_to_slab(re, im):
    n = re.shape[0]
    return jnp.concatenate([re.reshape(n, N_SLAB, HALF), im.reshape(n, N_SLAB, HALF)], axis=-1).astype(F32)


def _slab_to_state(s):
    n = s.shape[0]
    re = s[:, :, :HALF].reshape(n, S5_GROUPS, S5_STATE)
    im = s[:, :, HALF:].reshape(n, S5_GROUPS, S5_STATE)
    return re, im


def _block_table(counts, n_tok):
    n_assign = n_tok * TOP_K
    n_blocks = (n_assign + N_EXPERTS * (E_BLK - 1) + E_BLK - 1) // E_BLK + 1
    cap_blocks = n_tok // E_BLK
    nblk = (counts + E_BLK - 1) // E_BLK
    cum = jnp.cumsum(nblk)
    n_used = cum[-1]
    i = jnp.arange(n_blocks, dtype=jnp.int32)
    ii = jnp.maximum(jnp.minimum(i, n_used - 1), 0)
    before = (cum[None, :] <= ii[:, None]).astype(jnp.int32)
    e = jnp.minimum(jnp.sum(before, axis=1), N_EXPERTS - 1)
    start_e = jnp.sum(before * nblk[None, :], axis=1)
    is_e = jnp.concatenate([jnp.ones((n_blocks, 1), jnp.int32), before[:, :-1]], axis=1) - before
    count_e = jnp.sum(is_e * counts[None, :], axis=1)
    b = ii - start_e
    blk_row = (e * cap_blocks + b).astype(jnp.int32)
    valid = jnp.where(i < n_used, jnp.minimum(E_BLK, count_e - b * E_BLK), 0).astype(jnp.int32)
    return blk_row, e, valid, n_used.astype(jnp.int32).reshape(1)


def kernel(x_prompt, x_sample, c_prompt, c_sample, state_s5_re, state_s5_im, state_conv, w_ada, b_ada, w_in,
           s5_lam_re, s5_lam_im, s5_log_dt, s5_b_re, s5_b_im, s5_c_re, s5_c_im, s5_d, w_glu, b_glu, conv_w,
           w_out, ln1_g, ln1_b, w_router, b_router, w_gu, b_gu, w_down, b_down, ln2_g, ln2_b):
    assert DEPTH == 1 and w_ada.shape[0] == 1
    n_p, l_p, _ = x_prompt.shape
    n_s, l_s, _ = x_sample.shape
    t_p, t_s = n_p * l_p, n_s * l_s
    n_tok = t_p + t_s
    tiles_p, tiles_s = t_p // TL, t_s // TL
    seq_per_tile = TL // l_s
    assert l_p % TL == 0 and TL % l_s == 0 and n_s % seq_per_tile == 0

    c_all = jnp.concatenate([c_prompt, c_sample], axis=0).astype(F32)
    pad = (-c_all.shape[0]) % SUBLANES
    c_all = jnp.pad(c_all, ((0, pad), (0, 0)))
    mod = _ada_call(c_all, w_ada[0], b_ada[0].reshape(1, -1)).reshape(-1, 6, D_MODEL)
    mod_p, mod_s = mod[:n_p], mod[n_p:n_p + n_s]

    bmat, cmat, tab = _s5_tables(s5_lam_re[0], s5_lam_im[0], s5_log_dt[0], s5_b_re[0], s5_b_im[0],
                                 s5_c_re[0], s5_c_im[0])
    wr = jnp.pad(w_router[0], ((0, 0), (0, LANES - N_EXPERTS))).astype(BF16)
    br = jnp.pad(b_router[0].astype(F32), (0, LANES - N_EXPERTS), constant_values=NEG_BIG).reshape(1, LANES)
    weights = (w_in[0].astype(BF16), bmat, cmat, tab, s5_d[0].reshape(1, D_S5).astype(F32),
               w_glu[0].astype(BF16), b_glu[0].reshape(1, D_S5).astype(F32), conv_w[0].astype(F32),
               w_out[0].astype(BF16), ln1_g[0].reshape(1, D_MODEL).astype(F32),
               ln1_b[0].reshape(1, D_MODEL).astype(F32), wr, br)

    h0_p = jnp.zeros((n_p, N_SLAB, SLAB_W), F32)
    cb_p = jnp.zeros((n_p, CONV_W - 1, D_CONV), F32)
    h0_s = _state_to_slab(state_s5_re[0], state_s5_im[0])
    cb_s = state_conv[0].astype(F32)

    assert n_tok % E_BLK == 0 and E_BLK % TL == 0
    cnt0 = jnp.zeros((1, LANES), F32)
    outs_p = _mixer_call(x_prompt, mod_p, h0_p, cb_p, weights, cnt0, 1, TL, l_p // TL, 0, n_tok, ())
    x1, gates, xs, cnt_p, s_p, conv_p = outs_p
    xs3 = x_sample.reshape(n_s // seq_per_tile, TL, D_MODEL)
    outs_s = _mixer_call(xs3, mod_s, h0_s, cb_s, weights, cnt_p, seq_per_tile, l_s, 1, tiles_p, n_tok,
                         (x1, gates, xs))
    x1, gates, xs, cnt, s_s, conv_s = outs_s

    counts = cnt[0, :N_EXPERTS].astype(jnp.int32)
    blk_row, blk_e, blk_valid, n_used = _block_table(counts, n_tok)
    yk = _expert_call(blk_row, blk_e, blk_valid, n_used, xs, w_gu[0].astype(F32),
                      b_gu[0].reshape(N_EXPERTS, 1, 2 * D_FF).astype(F32), w_down[0].astype(F32),
                      b_down[0].reshape(N_EXPERTS, 1, D_MODEL).astype(F32), n_tok)
    g2 = ln2_g[0].reshape(1, D_MODEL).astype(F32)
    b2 = ln2_b[0].reshape(1, D_MODEL).astype(F32)
    assert l_p % C_TL == 0 and C_TL % l_s == 0 and t_s % C_TL == 0
    y_p = _combine_call(yk, x1, gates, mod_p, g2, b2, 1, C_TL, t_p // C_TL, 0, l_p // C_TL, n_tok)
    y_s = _combine_call(yk, x1, gates, mod_s, g2, b2, C_TL // l_s, l_s, t_s // C_TL, t_p // C_TL, 1, n_tok)

    p_re, p_im = _slab_to_state(s_p)
    s_re, s_im = _slab_to_state(s_s)
    return (y_p.reshape(n_p, l_p, D_MODEL), y_s.reshape(n_s, l_s, D_MODEL),
            p_re[None], p_im[None], conv_p[None], s_re[None], s_im[None], conv_s[None])
```

```python
import functools
import math

import jax
import jax.numpy as jnp
from jax import lax
from jax.experimental import pallas as pl
from jax.experimental.pallas import tpu as pltpu

F32 = jnp.float32
BF16 = jnp.bfloat16

D_MODEL = 1024
DEPTH = 1
D_S5 = 512
D_CONV = 512
S5_GROUP = 16
S5_GROUPS = 32
S5_STATE = 64
CONV_W = 3
N_EXPERTS = 32
TOP_K = 4
D_FF = 1024
SWIGLU_LIMIT = 7.0
SWIGLU_ALPHA = 1.702
LN_EPS = 1e-5
DEEPNORM_ALPHA = (2.0 * DEPTH) ** 0.25

TL = 256
C_TL = 1024
SUBLANES = 8
LANES = 128
N_SLAB = 4
SLAB_W = 1024
HALF = 512
E_BLK = 512
ROW_TILE = D_MODEL // LANES
PACK_ROWS = HALF // LANES
HI16_MASK = -65536
DMA_THREADS = 2
NEG_BIG = -1e30
VMEM_LIMIT = 56 * 1024 * 1024


def _dot(a, b):
    return jnp.dot(a, b, preferred_element_type=F32)


def _layer_norm(x, g, b):
    mu = jnp.mean(x, axis=-1, keepdims=True)
    xc = x - mu
    var = jnp.mean(xc * xc, axis=-1, keepdims=True)
    return xc * lax.rsqrt(var + LN_EPS) * g + b


def _ada_kernel(c_ref, w_ref, b_ref, o_ref):
    c = c_ref[...]
    s = c * jax.nn.sigmoid(c)
    o_ref[...] = _dot(s.astype(BF16), w_ref[...].astype(BF16)) + b_ref[...]


def _ada_call(c_all, w_ada, b_ada):
    rows = c_all.shape[0]
    n_out = w_ada.shape[1]
    tn = 768
    return pl.pallas_call(
        _ada_kernel,
        grid=(n_out // tn,),
        in_specs=[pl.BlockSpec((rows, D_MODEL), lambda i: (0, 0)),
                  pl.BlockSpec((D_MODEL, tn), lambda i: (0, i)),
                  pl.BlockSpec((1, tn), lambda i: (0, i))],
        out_specs=pl.BlockSpec((rows, tn), lambda i: (0, i)),
        out_shape=jax.ShapeDtypeStruct((rows, n_out), F32),
        compiler_params=pltpu.CompilerParams(dimension_semantics=("arbitrary",),
                                             vmem_limit_bytes=VMEM_LIMIT),
        name="ada_mod",
    )(c_all, w_ada, b_ada)


def _mixer_kernel(nseg, seg, carry_tiles, n_alias, cap_rows, tile0,
                  x_ref, xn_ref, mod_ref, modn_ref, h0_ref, cbuf_ref, win_ref, bmat_ref, cmat_ref, tab_ref, d_ref,
                  wglu_ref, bglu_ref, convw_ref, wout_ref, g1_ref, b1_ref, wr_ref, br_ref, cnt0_ref, *rest):
    rest = rest[n_alias:]
    (x1_ref, gate_ref, xs_hbm, cnt_out_ref, sout_ref, cout_ref,
     p_ref, bu_ref, hs_ref, xcs_ref, carry_ref, h2s_ref, posv_ref, poss_ref, cnt_ref, sem_s, sem_v) = rest
    g_id = pl.program_id(0)
    j = pl.program_id(1)
    step = g_id * pl.num_programs(1) + j
    is_last = step == pl.num_programs(0) * pl.num_programs(1) - 1
    slot = step % 2

    def rows_of(m_ref, k):
        if nseg == 1:
            return m_ref[0, k:k + 1, :]
        return jnp.concatenate(
            [jnp.broadcast_to(m_ref[s, k:k + 1, :], (seg, D_MODEL)) for s in range(nseg)], axis=0)

    def mod_rows(k):
        return rows_of(mod_ref, k)

    def front_pieces(xt_ref, m_ref, dst):
        def adaln():
            h = xt_ref[0] * (1.0 + rows_of(m_ref, 1)) + rows_of(m_ref, 0)
            hs_ref[...] = h.astype(BF16)

        def in_proj(c):
            cols = slice(c * HALF, (c + 1) * HALF)
            p_ref[dst, :, cols] = _dot(hs_ref[...], win_ref[:, cols])

        def s5_in(i):
            u_i = p_ref[dst, :, i * LANES:(i + 1) * LANES]
            row_in_tile = lax.broadcasted_iota(jnp.int32, u_i.shape, 0) % SUBLANES
            u_prev = jnp.where(row_in_tile == 0, 0.0, pltpu.roll(u_i, 1, 0))
            lhs = jnp.concatenate([u_i, u_prev], axis=1).astype(BF16)
            bu_ref[dst * N_SLAB + i] = _dot(lhs, bmat_ref[i])

        return ([adaln] + [functools.partial(in_proj, c) for c in range(2 * D_MODEL // HALF)]
                + [functools.partial(s5_in, i) for i in range(N_SLAB)])

    @pl.when(step == 0)
    def _():
        for piece in front_pieces(x_ref, mod_ref, 0):
            piece()

    nxt_front = front_pieces(xn_ref, modn_ref, 1 - slot)

    def run_front(n):
        for _ in range(n):
            nxt_front.pop(0)()

    x = x_ref[0]
    bu0 = slot * N_SLAB

    prev = 1 - slot
    smem_copy = pltpu.make_async_copy(posv_ref, poss_ref, sem_v.at[0])

    @pl.when(step == 0)
    def _():
        k_i = lax.broadcasted_iota(jnp.int32, (SUBLANES, TL), 0)
        t_i = lax.broadcasted_iota(jnp.int32, (SUBLANES, TL), 1)
        posv_ref[...] = (N_EXPERTS * cap_rows + k_i * TL + t_i) * ROW_TILE
        smem_copy.start()
        h2s_ref[...] = jnp.zeros(h2s_ref.shape, F32)

    smem_copy.wait()

    def issue_rows(src_slot, t0, n_rows):
        for q in range(n_rows):
            t = t0 + q
            src = pl.multiple_of(t * ROW_TILE, ROW_TILE)
            for k in range(TOP_K):
                dst = pl.multiple_of(poss_ref[k, t], ROW_TILE)
                pltpu.make_async_copy(h2s_ref.at[src_slot, pl.ds(src, ROW_TILE), :],
                                      xs_hbm.at[pl.ds(dst, ROW_TILE), :],
                                      sem_s.at[src_slot]).start(priority=k % DMA_THREADS)

    def wait_scatter(sl):
        for _ in range(TOP_K):
            pltpu.make_async_copy(h2s_ref.at[sl], xs_hbm.at[pl.ds(0, TL * ROW_TILE), :], sem_s.at[sl]).wait()

    if carry_tiles:
        @pl.when(j == 0)
        def _():
            carry_ref[...] = h0_ref[0]

    for s in range(nseg):
        if carry_tiles:
            init = tuple(carry_ref[i:i + 1, :] for i in range(N_SLAB))
        else:
            init = tuple(h0_ref[s, i:i + 1, :] for i in range(N_SLAB))

        def scan_body(r, carry, s=s):
            issue_rows(prev, s * seg + r * (2 * SUBLANES), 2 * SUBLANES)
            new = []
            for i in range(N_SLAB):
                cr = carry[i][:, :HALF]
                ci = carry[i][:, HALF:]
                for half in range(2):
                    row0 = pl.multiple_of(s * seg + r * (2 * SUBLANES) + half * SUBLANES, SUBLANES)
                    blk = bu_ref[bu0 + i, pl.ds(row0, SUBLANES), :]
                    xr = blk[:, :HALF]
                    xi = blk[:, HALF:]
                    for k, d in enumerate((2, 4)):
                        lr = tab_ref[i, 2 * k]
                        li = tab_ref[i, 2 * k + 1]
                        rr = pltpu.roll(xr, d, 0)
                        ri = pltpu.roll(xi, d, 0)
                        xr, xi = xr + (lr * rr - li * ri), xi + (lr * ri + li * rr)
                    pr = tab_ref[i, 4]
                    pi_ = tab_ref[i, 5]
                    crb = jnp.broadcast_to(cr, (SUBLANES, HALF))
                    cib = jnp.broadcast_to(ci, (SUBLANES, HALF))
                    xr, xi = xr + (pr * crb - pi_ * cib), xi + (pr * cib + pi_ * crb)
                    bu_ref[bu0 + i, pl.ds(row0, SUBLANES), :] = jnp.concatenate([xr, xi], axis=1)
                    cr = xr[SUBLANES - 1:SUBLANES, :]
                    ci = xi[SUBLANES - 1:SUBLANES, :]
                new.append(jnp.concatenate([cr, ci], axis=1))
            return tuple(new)

        final = lax.fori_loop(0, seg // (2 * SUBLANES), scan_body, init)
        for i in range(N_SLAB):
            sout_ref[s, i:i + 1, :] = final[i]
            if carry_tiles:
                carry_ref[i:i + 1, :] = final[i]

    y = jnp.concatenate([_dot(bu_ref[bu0 + i].astype(BF16), cmat_ref[i]) for i in range(N_SLAB)], axis=1)
    y = y + d_ref[...] * p_ref[slot, :, :D_S5]
    z = jax.nn.gelu(y)
    z = z * jax.nn.sigmoid(_dot(z.astype(BF16), wglu_ref[...]) + bglu_ref[...])

    xc = p_ref[slot, :, D_S5 + D_CONV:D_S5 + 2 * D_CONV] * p_ref[slot, :, D_S5 + 2 * D_CONV:]
    w0 = convw_ref[0:1, :]
    w1 = convw_ref[1:2, :]
    w2 = convw_ref[2:3, :]
    conv_parts = []
    for s in range(nseg):
        base = s * (seg + SUBLANES)
        xc_s = xc[s * seg:(s + 1) * seg]
        tail = xc_s[seg - 2:seg]
        xcs_ref[pl.ds(base + SUBLANES, seg), :] = xc_s
        if carry_tiles:
            @pl.when(j == 0)
            def _(s=s, base=base):
                xcs_ref[pl.ds(base + SUBLANES - 2, 2), :] = cbuf_ref[s]
        else:
            xcs_ref[pl.ds(base + SUBLANES - 2, 2), :] = cbuf_ref[s]
        xm1 = xcs_ref[pl.ds(base + SUBLANES - 1, seg), :]
        xm2 = xcs_ref[pl.ds(base + SUBLANES - 2, seg), :]
        conv_parts.append(w0 * xm2 + w1 * xm1 + w2 * xc_s)
        cout_ref[s] = tail
        if carry_tiles:
            xcs_ref[pl.ds(base + SUBLANES - 2, 2), :] = tail
    conv = conv_parts[0] if nseg == 1 else jnp.concatenate(conv_parts, axis=0)
    y_b = p_ref[slot, :, D_S5:D_S5 + D_CONV] * conv

    run_front(2)
    mix = _dot(jnp.concatenate([z, y_b], axis=1).astype(BF16), wout_ref[...])
    x1 = _layer_norm(DEEPNORM_ALPHA * x + (1.0 + mod_rows(2)) * mix, g1_ref[...], b1_ref[...])
    x1_ref[...] = x1
    run_front(1)
    h2 = x1 * (1.0 + mod_rows(4)) + mod_rows(3)
    lo_bits = lax.bitcast_convert_type(h2[:, :HALF].astype(BF16).astype(F32), jnp.int32)
    hi_bits = lax.bitcast_convert_type(h2[:, HALF:].astype(BF16).astype(F32), jnp.int32)
    packed = lax.bitcast_convert_type(hi_bits | lax.shift_right_logical(lo_bits, 16), F32)
    for c in range(PACK_ROWS):
        h2s_ref[slot, pl.ds(c, TL, stride=ROW_TILE), :] = packed[:, c * LANES:(c + 1) * LANES]

    logits = _dot(h2.astype(BF16), wr_ref[...]) + br_ref[...]
    lane = lax.broadcasted_iota(jnp.int32, logits.shape, 1)
    lane_f = lane.astype(F32)
    vals, sels, hots = [], [], []
    cur = logits
    for _ in range(TOP_K):
        m = jnp.max(cur, axis=-1, keepdims=True)
        am = jnp.min(jnp.where(cur == m, lane_f, float(LANES)), axis=-1, keepdims=True)
        hot = lane_f == am
        vals.append(m)
        sels.append(am)
        hots.append(hot)
        cur = jnp.where(hot, -jnp.inf, cur)
    exps = [jnp.exp(v - vals[0]) for v in vals]
    inv = 1.0 / (exps[0] + exps[1] + exps[2] + exps[3])
    gate_out = jnp.zeros(logits.shape, F32)
    for k in range(TOP_K):
        gate_out = jnp.where(lane == k, exps[k] * inv, gate_out)
    gate_ref[...] = gate_out
    run_front(1)

    tok = ((tile0 + step) * TL + lax.broadcasted_iota(jnp.int32, logits.shape, 0)).astype(F32)
    record = jnp.zeros(logits.shape, F32)
    for k in range(TOP_K):
        record = jnp.where(hots[k], tok + float(k * cap_rows), record)
    h2s_ref[slot, pl.ds(PACK_ROWS, TL, stride=ROW_TILE), :] = record
    run_front(2)

    @pl.when(step == 0)
    def _():
        cnt_ref[...] = cnt0_ref[...]

    chosen = jnp.zeros(logits.shape, F32)
    for k in range(TOP_K):
        chosen = jnp.where(hots[k], 1.0, chosen)
    r_i = lax.broadcasted_iota(jnp.int32, (TL, TL), 0)
    c_i = lax.broadcasted_iota(jnp.int32, (TL, TL), 1)
    before = jnp.where(c_i < r_i, 1.0, 0.0).astype(BF16)
    rank_base = _dot(before, chosen.astype(BF16)) + cnt_ref[...]
    pos_mat = jnp.zeros(logits.shape, F32)
    for k in range(TOP_K):
        rank_k = jnp.sum(jnp.where(hots[k], rank_base, 0.0), axis=-1, keepdims=True)
        pos_mat = jnp.where(lane == k, (sels[k] * float(cap_rows) + rank_k) * float(ROW_TILE), pos_mat)
    cnt_ref[...] = cnt_ref[...] + jnp.sum(chosen, axis=0, keepdims=True)
    cnt_out_ref[...] = cnt_ref[...]
    run_front(len(nxt_front))
    pos_t =pos_mat.T[:SUBLANES, :].astype(jnp.int32)
    posv_ref[...] = pos_t
    smem_copy.start()
    wait_scatter(prev)

    @pl.when(is_last)
    def _():
        smem_copy.wait()

        def tail_body(t8, c):
            issue_rows(slot, t8 * SUBLANES, SUBLANES)
            return c
        lax.fori_loop(0, TL // SUBLANES, tail_body, 0)
        wait_scatter(slot)


def _mixer_call(x3, mod, h0, cbuf, weights, cnt0, nseg, seg, tiles_per_group, tile0, n_tok, aliased):
    groups = x3.shape[0]
    nseq = mod.shape[0]
    carry_tiles = nseg == 1
    n_alias = len(aliased)
    n_tiles = n_tok // TL
    cap_rows = n_tok

    def full(a):
        nd = a.ndim
        return pl.BlockSpec(a.shape, lambda g, j, nd=nd: (0,) * nd)

    def tile_map(g, j):
        return (tile0 + g * tiles_per_group + j, 0)

    def next_g(g, j):
        return jnp.where(j + 1 < tiles_per_group, g, jnp.minimum(g + 1, groups - 1))

    def next_j(g, j):
        return jnp.where(j + 1 < tiles_per_group, j + 1, 0)

    in_specs = [pl.BlockSpec((1, TL, D_MODEL), lambda g, j: (g, j, 0)),
                pl.BlockSpec((1, TL, D_MODEL), lambda g, j: (next_g(g, j), next_j(g, j), 0)),
                pl.BlockSpec((nseg, 6, D_MODEL), lambda g, j: (g, 0, 0)),
                pl.BlockSpec((nseg, 6, D_MODEL), lambda g, j: (next_g(g, j), 0, 0)),
                pl.BlockSpec((nseg, N_SLAB, SLAB_W), lambda g, j: (g, 0, 0)),
                pl.BlockSpec((nseg, CONV_W - 1, D_CONV), lambda g, j: (g, 0, 0))]
    in_specs += [full(w) for w in weights]
    in_specs += [full(cnt0)]
    in_specs += [pl.BlockSpec(memory_space=pl.ANY)] * n_alias
    out_shape = [jax.ShapeDtypeStruct((n_tok, D_MODEL), F32),
                 jax.ShapeDtypeStruct((n_tok, LANES), F32),
                 jax.ShapeDtypeStruct(((N_EXPERTS * cap_rows + TOP_K * TL) * ROW_TILE, LANES), F32),
                 jax.ShapeDtypeStruct((1, LANES), F32),
                 jax.ShapeDtypeStruct((nseq, N_SLAB, SLAB_W), F32),
                 jax.ShapeDtypeStruct((nseq, CONV_W - 1, D_CONV), F32)]
    out_specs = [pl.BlockSpec((TL, D_MODEL), tile_map),
                 pl.BlockSpec((TL, LANES), tile_map),
                 pl.BlockSpec(memory_space=pl.ANY),
                 pl.BlockSpec((1, LANES), lambda g, j: (0, 0)),
                 pl.BlockSpec((nseg, N_SLAB, SLAB_W), lambda g, j: (g, 0, 0)),
                 pl.BlockSpec((nseg, CONV_W - 1, D_CONV), lambda g, j: (g, 0, 0))]
    n_in = 6 + len(weights) + 1
    aliases = {n_in + k: k for k in range(n_alias)}
    scratch = [pltpu.VMEM((2, TL, 2 * D_MODEL), F32),
               pltpu.VMEM((2 * N_SLAB, TL, SLAB_W), F32),
               pltpu.VMEM((TL, D_MODEL), BF16),
               pltpu.VMEM((nseg * (seg + SUBLANES), D_CONV), F32),
               pltpu.VMEM((N_SLAB, SLAB_W), F32),
               pltpu.VMEM((2, TL * ROW_TILE, LANES), F32),
               pltpu.VMEM((SUBLANES, TL), jnp.int32),
               pltpu.SMEM((SUBLANES, TL), jnp.int32),
               pltpu.VMEM((1, LANES), F32),
               pltpu.SemaphoreType.DMA((2,)),
               pltpu.SemaphoreType.DMA((1,))]
    return pl.pallas_call(
        functools.partial(_mixer_kernel, nseg, seg, carry_tiles, n_alias, cap_rows, tile0),
        grid=(groups, tiles_per_group),
        in_specs=in_specs, out_specs=out_specs, out_shape=out_shape,
        scratch_shapes=scratch,
        input_output_aliases=aliases,
        compiler_params=pltpu.CompilerParams(dimension_semantics=("arbitrary", "arbitrary"),
                                             vmem_limit_bytes=VMEM_LIMIT),
        name="mixer_prompt" if carry_tiles else "mixer_sample",
    )(x3, x3, mod, mod, h0, cbuf, *weights, cnt0, *aliased)


W_CAST_ROWS = 128

def _expert_kernel(n_tok, row_ref, be_ref, nv_ref, nu_ref, xs_ref, wgu_ref, bgu_ref, wd_ref, bd_ref,
                   yk_hbm, wgu_bf, wd_bf, out_a, out_b, slotv_ref, slots_ref, sem_o, sem_v):
    i = pl.program_id(0)
    n_used = nu_ref[0]
    outs = (out_a, out_b)
    prev = jnp.maximum(i - 1, 0)
    new_expert = jnp.logical_or(i == 0, be_ref[i] != be_ref[prev])

    @pl.when(jnp.logical_and(new_expert, i < n_used))
    def _():
        def cast_gu(r, c):
            rows = pl.ds(pl.multiple_of(r * W_CAST_ROWS, W_CAST_ROWS), W_CAST_ROWS)
            wgu_bf[rows, :] = wgu_ref[0, rows, :].astype(BF16)
            return c
        lax.fori_loop(0, D_MODEL // W_CAST_ROWS, cast_gu, 0)

        def cast_d(r, c):
            rows = pl.ds(pl.multiple_of(r * W_CAST_ROWS, W_CAST_ROWS), W_CAST_ROWS)
            wd_bf[rows, :] = wd_ref[0, rows, :].astype(BF16)
            return c
        lax.fori_loop(0, D_FF // W_CAST_ROWS, cast_d, 0)

    def issue_rows(src, t0, count):
        for q in range(count):
            t = t0 + q
            dst = pl.multiple_of(slots_ref[src, 0, t], PACK_ROWS)
            row0 = pl.multiple_of(t * PACK_ROWS, PACK_ROWS)
            pltpu.make_async_copy(outs[src].at[pl.ds(row0, PACK_ROWS), :],
                                  yk_hbm.at[pl.ds(dst, PACK_ROWS), :],
                                  sem_o.at[src]).start(priority=1)

    def wait_rows(sl):
        pltpu.make_async_copy(outs[sl], yk_hbm.at[pl.ds(0, E_BLK * PACK_ROWS), :], sem_o.at[sl]).wait()

    row = lax.broadcasted_iota(jnp.int32, (E_BLK, 1), 0)
    spare0 = float(TOP_K * n_tok)

    @pl.when(i == 0)
    def _():
        t_i = lax.broadcasted_iota(jnp.int32, (SUBLANES, E_BLK), 1)
        slotv_ref[...] = (TOP_K * n_tok + t_i) * PACK_ROWS
        pltpu.make_async_copy(slotv_ref, slots_ref.at[1], sem_v.at[0]).start()
        out_b[...] = jnp.zeros(out_b.shape, F32)

    def block(cur):
        prv = 1 - cur

        def smem_copy(dst):
            return pltpu.make_async_copy(slotv_ref, slots_ref.at[dst], sem_v.at[0])

        @pl.when(i < n_used)
        def _():
            smem_copy(prv).wait()
            record = xs_ref[pl.ds(PACK_ROWS, E_BLK, stride=ROW_TILE), :]
            pick = (lax.broadcasted_iota(jnp.int32, (SUBLANES, LANES), 1) == be_ref[i]).astype(F32)
            slot_row = lax.dot_general(pick, record, (((1,), (1,)), ((), ())),
                                       precision=lax.Precision.HIGHEST, preferred_element_type=F32)
            t_row = lax.broadcasted_iota(jnp.int32, (SUBLANES, E_BLK), 1)
            slot_row = jnp.where(t_row < nv_ref[i], slot_row, spare0 + t_row.astype(F32))
            slotv_ref[...] = (slot_row * float(PACK_ROWS)).astype(jnp.int32)
            smem_copy(cur).start()

        @pl.when(i < n_used)
        def _():
            live = row < nv_ref[i]
            words = [lax.bitcast_convert_type(xs_ref[pl.ds(c, E_BLK, stride=ROW_TILE), :], jnp.int32)
                     for c in range(PACK_ROWS)]
            lo = [lax.bitcast_convert_type(lax.shift_left(w, 16), F32) for w in words]
            hi = [lax.bitcast_convert_type(w & HI16_MASK, F32) for w in words]
            xb = jnp.where(live, jnp.concatenate(lo + hi, axis=1), 0.0).astype(BF16)

            issue_rows(prv, 0, E_BLK)
            hgu = _dot(xb, wgu_bf[...]) + bgu_ref[0]
            g = jnp.minimum(hgu[:, :D_FF], SWIGLU_LIMIT)
            up = jnp.clip(hgu[:, D_FF:], -SWIGLU_LIMIT, SWIGLU_LIMIT)
            act = (up + 1.0) * (g * jax.nn.sigmoid(SWIGLU_ALPHA * g))
            y = _dot(act.astype(BF16), wd_bf[...]) + bd_ref[0]

            @pl.when(i > 0)
            def _():
                wait_rows(cur)

            y_lo = lax.bitcast_convert_type(y[:, :HALF].astype(BF16).astype(F32), jnp.int32)
            y_hi = lax.bitcast_convert_type(y[:, HALF:].astype(BF16).astype(F32), jnp.int32)
            y_pk = lax.bitcast_convert_type(y_hi | lax.shift_right_logical(y_lo, 16), F32)
            for c in range(PACK_ROWS):
                outs[cur][pl.ds(c, E_BLK, stride=PACK_ROWS), :] = y_pk[:, c * LANES:(c + 1) * LANES]

        @pl.when(i == n_used)
        def _():
            smem_copy(prv).wait()

            def tail(t8, c):
                issue_rows(prv, t8 * SUBLANES, SUBLANES)
                return c
            lax.fori_loop(0, E_BLK // SUBLANES, tail, 0)
            wait_rows(cur)
            wait_rows(prv)

    for parity in range(2):
        @pl.when(i % 2 == parity)
        def _(parity=parity):
            block(parity)


def _expert_call(blk_row, blk_e, blk_valid, n_used, xs, w_gu, b_gu, w_down, b_down, n_tok):
    nb = blk_row.shape[0]
    grid_spec = pltpu.PrefetchScalarGridSpec(
        num_scalar_prefetch=4,
        grid=(nb,),
        in_specs=[
            pl.BlockSpec((E_BLK * ROW_TILE, LANES), lambda i, br, be, nv, nu: (br[i], 0)),
            pl.BlockSpec((1, D_MODEL, 2 * D_FF), lambda i, br, be, nv, nu: (be[i], 0, 0)),
            pl.BlockSpec((1, 1, 2 * D_FF), lambda i, br, be, nv, nu: (be[i], 0, 0)),
            pl.BlockSpec((1, D_FF, D_MODEL), lambda i, br, be, nv, nu: (be[i], 0, 0)),
            pl.BlockSpec((1, 1, D_MODEL), lambda i, br, be, nv, nu: (be[i], 0, 0)),
        ],
        out_specs=pl.BlockSpec(memory_space=pl.ANY),
        scratch_shapes=[pltpu.VMEM((D_MODEL, 2 * D_FF), BF16),
                        pltpu.VMEM((D_FF, D_MODEL), BF16),
                        pltpu.VMEM((E_BLK * PACK_ROWS, LANES), F32),
                        pltpu.VMEM((E_BLK * PACK_ROWS, LANES), F32),
                        pltpu.VMEM((SUBLANES, E_BLK), jnp.int32),
                        pltpu.SMEM((2, SUBLANES, E_BLK), jnp.int32),
                        pltpu.SemaphoreType.DMA((2,)),
                        pltpu.SemaphoreType.DMA((1,))],
    )
    return pl.pallas_call(
        functools.partial(_expert_kernel, n_tok),
        grid_spec=grid_spec,
        out_shape=jax.ShapeDtypeStruct(((TOP_K * n_tok + E_BLK) * PACK_ROWS, LANES), F32),
        compiler_params=pltpu.CompilerParams(dimension_semantics=("arbitrary",),
                                             vmem_limit_bytes=VMEM_LIMIT),
        name="experts",
    )(blk_row, blk_e, blk_valid, n_used, xs, w_gu, b_gu, w_down, b_down)


def _combine_kernel(nseg, seg, y0_ref, y1_ref, y2_ref, y3_ref, x1_ref, gate_ref, mod_ref, g2_ref, b2_ref,
                    out_ref):
    gates = gate_ref[...]
    ffn = None
    for k, y_ref in enumerate((y0_ref, y1_ref, y2_ref, y3_ref)):
        words = [lax.bitcast_convert_type(y_ref[pl.ds(c, C_TL, stride=PACK_ROWS), :], jnp.int32)
                 for c in range(PACK_ROWS)]
        rows = jnp.concatenate(
            [lax.bitcast_convert_type(lax.shift_left(w, 16), F32) for w in words]
            + [lax.bitcast_convert_type(w & HI16_MASK, F32) for w in words], axis=1)
        term = gates[:, k:k + 1] * rows
        ffn = term if ffn is None else ffn + term
    if nseg == 1:
        gate2 = mod_ref[0, 5:6, :]
    else:
        gate2 = jnp.concatenate(
            [jnp.broadcast_to(mod_ref[s, 5:6, :], (seg, D_MODEL)) for s in range(nseg)], axis=0)
    xa = DEEPNORM_ALPHA * x1_ref[...] + (1.0 + gate2) * ffn
    out_ref[...] = _layer_norm(xa, g2_ref[...], b2_ref[...])


def _combine_call(yk, x1, gates, mod, ln2_g, ln2_b, nseg, seg, n_tiles, tile0, tiles_per_group, n_tok):
    tiles_total = n_tok // C_TL

    def yk_spec(k):
        return pl.BlockSpec((C_TL * PACK_ROWS, LANES), lambda i, k=k: (k * tiles_total + tile0 + i, 0))

    in_specs = [yk_spec(k) for k in range(TOP_K)] + [
        pl.BlockSpec((C_TL, D_MODEL), lambda i: (tile0 + i, 0)),
        pl.BlockSpec((C_TL, LANES), lambda i: (tile0 + i, 0)),
        pl.BlockSpec((nseg, 6, D_MODEL), lambda i: (i // tiles_per_group, 0, 0)),
        pl.BlockSpec((1, D_MODEL), lambda i: (0, 0)),
        pl.BlockSpec((1, D_MODEL), lambda i: (0, 0)),
    ]
    return pl.pallas_call(
        functools.partial(_combine_kernel, nseg, seg),
        grid=(n_tiles,),
        in_specs=in_specs,
        out_specs=pl.BlockSpec((C_TL, D_MODEL), lambda i: (i, 0)),
        out_shape=jax.ShapeDtypeStruct((n_tiles * C_TL, D_MODEL), F32),
        compiler_params=pltpu.CompilerParams(dimension_semantics=("arbitrary",),
                                             vmem_limit_bytes=VMEM_LIMIT),
        name="combine_prompt" if nseg == 1 else "combine_sample",
    )(yk, yk, yk, yk, x1, gates, mod, ln2_g, ln2_b)


def _s5_tables(lam_re, lam_im, log_dt, b_re, b_im, c_re, c_im):
    dt = jnp.exp(log_dt.astype(F32))[:, None]
    lam = lax.complex(lam_re.astype(F32), lam_im.astype(F32))
    lam_dt = lam * dt
    lam_bar = jnp.exp(lam_dt)
    b_bar = ((lam_bar - 1.0) / lam)[..., None] * lax.complex(b_re.astype(F32), b_im.astype(F32))
    gl = S5_GROUPS // N_SLAB
    eye = jnp.eye(gl, dtype=F32)

    def b_slab(part):
        a = part.reshape(N_SLAB, gl, S5_STATE, S5_GROUP)
        return jnp.einsum('sgph,gk->sghkp', a, eye).reshape(N_SLAB, gl * S5_GROUP, gl * S5_STATE)

    def b_rows(b):
        return jnp.concatenate([b_slab(b.real), b_slab(b.imag)], axis=-1)

    bmat = jnp.concatenate([b_rows(b_bar), b_rows(lam_bar[..., None] * b_bar)], axis=1).astype(BF16)

    def c_slab(part):
        a = part.reshape(N_SLAB, gl, S5_GROUP, S5_STATE)
        return jnp.einsum('sghp,gk->sgpkh', a, eye).reshape(N_SLAB, gl * S5_STATE, gl * S5_GROUP)

    cmat = jnp.concatenate([c_slab(c_re.astype(F32)), -c_slab(c_im.astype(F32))], axis=1).astype(BF16)

    row = jnp.arange(SUBLANES, dtype=F32)[:, None, None]

    def power(k):
        return jnp.exp(lam_dt[None] * k)

    tabs = []
    for d in (2, 4):
        pw = power(jnp.full_like(row, float(d)))
        mask = (row >= d).astype(F32)
        tabs += [pw.real * mask, pw.imag * mask]
    pw = power(row + 1.0)
    tabs += [pw.real, pw.imag]
    tab = jnp.stack(tabs, axis=0)
    tab = tab.reshape(len(tabs), SUBLANES, N_SLAB, gl * S5_STATE).transpose(2, 0, 1, 3)
    return bmat, cmat, tab


def _state_to_slab(re, im):
    n = re.shape[0]
    return jnp.concatenate([re.reshape(n, N_SLAB, HALF), im.reshape(n, N_SLAB, HALF)], axis=-1).astype(F32)


def _slab_to_state(s):
    n = s.shape[0]
    re = s[:, :, :HALF].reshape(n, S5_GROUPS, S5_STATE)
    im = s[:, :, HALF:].reshape(n, S5_GROUPS, S5_STATE)
    return re, im


def _block_table(counts, n_tok):
    n_assign = n_tok * TOP_K
    n_blocks = (n_assign + N_EXPERTS * (E_BLK - 1) + E_BLK - 1) // E_BLK + 1
    cap_blocks = n_tok // E_BLK
    nblk = (counts + E_BLK - 1) // E_BLK
    cum = jnp.cumsum(nblk)
    n_used = cum[-1]
    i = jnp.arange(n_blocks, dtype=jnp.int32)
    ii = jnp.maximum(jnp.minimum(i, n_used - 1), 0)
    before = (cum[None, :] <= ii[:, None]).astype(jnp.int32)
    e = jnp.minimum(jnp.sum(before, axis=1), N_EXPERTS - 1)
    start_e = jnp.sum(before * nblk[None, :], axis=1)
    is_e = jnp.concatenate([jnp.ones((n_blocks, 1), jnp.int32), before[:, :-1]], axis=1) - before
    count_e = jnp.sum(is_e * counts[None, :], axis=1)
    b = ii - start_e
    blk_row = (e * cap_blocks + b).astype(jnp.int32)
    valid = jnp.where(i < n_used, jnp.minimum(E_BLK, count_e - b * E_BLK), 0).astype(jnp.int32)
    return blk_row, e, valid, n_used.astype(jnp.int32).reshape(1)


def kernel(x_prompt, x_sample, c_prompt, c_sample, state_s5_re, state_s5_im, state_conv, w_ada, b_ada, w_in,
           s5_lam_re, s5_lam_im, s5_log_dt, s5_b_re, s5_b_im, s5_c_re, s5_c_im, s5_d, w_glu, b_glu, conv_w,
           w_out, ln1_g, ln1_b, w_router, b_router, w_gu, b_gu, w_down, b_down, ln2_g, ln2_b):
    assert DEPTH == 1 and w_ada.shape[0] == 1
    n_p, l_p, _ = x_prompt.shape
    n_s, l_s, _ = x_sample.shape
    t_p, t_s = n_p * l_p, n_s * l_s
    n_tok = t_p + t_s
    tiles_p, tiles_s = t_p // TL, t_s // TL
    seq_per_tile = TL // l_s
    assert l_p % TL == 0 and TL % l_s == 0 and n_s % seq_per_tile == 0

    c_all = jnp.concatenate([c_prompt, c_sample], axis=0).astype(F32)
    pad = (-c_all.shape[0]) % SUBLANES
    c_all = jnp.pad(c_all, ((0, pad), (0, 0)))
    mod = _ada_call(c_all, w_ada[0], b_ada[0].reshape(1, -1)).reshape(-1, 6, D_MODEL)
    mod_p, mod_s = mod[:n_p], mod[n_p:n_p + n_s]

    bmat, cmat, tab = _s5_tables(s5_lam_re[0], s5_lam_im[0], s5_log_dt[0], s5_b_re[0], s5_b_im[0],
                                 s5_c_re[0], s5_c_im[0])
    wr = jnp.pad(w_router[0], ((0, 0), (0, LANES - N_EXPERTS))).astype(BF16)
    br = jnp.pad(b_router[0].astype(F32), (0, LANES - N_EXPERTS), constant_values=NEG_BIG).reshape(1, LANES)
    weights = (w_in[0].astype(BF16), bmat, cmat, tab, s5_d[0].reshape(1, D_S5).astype(F32),
               w_glu[0].astype(BF16), b_glu[0].reshape(1, D_S5).astype(F32), conv_w[0].astype(F32),
               w_out[0].astype(BF16), ln1_g[0].reshape(1, D_MODEL).astype(F32),
               ln1_b[0].reshape(1, D_MODEL).astype(F32), wr, br)

    h0_p = jnp.zeros((n_p, N_SLAB, SLAB_W), F32)
    cb_p = jnp.zeros((n_p, CONV_W - 1, D_CONV), F32)
    h0_s = _state_to_slab(state_s5_re[0], state_s5_im[0])
    cb_s = state_conv[0].astype(F32)

    assert n_tok % E_BLK == 0 and E_BLK % TL == 0
    cnt0 = jnp.zeros((1, LANES), F32)
    outs_p = _mixer_call(x_prompt, mod_p, h0_p, cb_p, weights, cnt0, 1, TL, l_p // TL, 0, n_tok, ())
    x1, gates, xs, cnt_p, s_p, conv_p = outs_p
    xs3 = x_sample.reshape(n_s // seq_per_tile, TL, D_MODEL)
    outs_s = _mixer_call(xs3, mod_s, h0_s, cb_s, weights, cnt_p, seq_per_tile, l_s, 1, tiles_p, n_tok,
                         (x1, gates, xs))
    x1, gates, xs, cnt, s_s, conv_s = outs_s

    counts = cnt[0, :N_EXPERTS].astype(jnp.int32)
    blk_row, blk_e, blk_valid, n_used = _block_table(counts, n_tok)
    yk = _expert_call(blk_row, blk_e, blk_valid, n_used, xs, w_gu[0].astype(F32),
                      b_gu[0].reshape(N_EXPERTS, 1, 2 * D_FF).astype(F32), w_down[0].astype(F32),
                      b_down[0].reshape(N_EXPERTS, 1, D_MODEL).astype(F32), n_tok)
    g2 = ln2_g[0].reshape(1, D_MODEL).astype(F32)
    b2 = ln2_b[0].reshape(1, D_MODEL).astype(F32)
    assert l_p % C_TL == 0 and C_TL % l_s == 0 and t_s % C_TL == 0
    y_p = _combine_call(yk, x1, gates, mod_p, g2, b2, 1, C_TL, t_p // C_TL, 0, l_p // C_TL, n_tok)
    y_s = _combine_call(yk, x1, gates, mod_s, g2, b2, C_TL // l_s, l_s, t_s // C_TL, t_p // C_TL, 1, n_tok)

    p_re, p_im = _slab_to_state(s_p)
    s_re, s_im = _slab_to_state(s_s)
    return (y_p.reshape(n_p, l_p, D_MODEL), y_s.reshape(n_s, l_s, D_MODEL),
            p_re[None], p_im[None], conv_p[None], s_re[None], s_im[None], conv_s[None])
```

```python
import functools

import jax
import jax.numpy as jnp
from jax import lax
from jax.experimental import pallas as pl
from jax.experimental.pallas import tpu as pltpu

F32 = jnp.float32
BF16 = jnp.bfloat16

D_MODEL = 1024
DEPTH = 1
D_S5 = 512
D_CONV = 512
S5_GROUP = 16
S5_GROUPS = 32
S5_STATE = 64
CONV_W = 3
N_EXPERTS = 32
TOP_K = 4
D_FF = 1024
SWIGLU_LIMIT = 7.0
SWIGLU_ALPHA = 1.702
LN_EPS = 1e-5
DEEPNORM_ALPHA = (2.0 * DEPTH) ** 0.25

TL = 256
C_TL = 1024
SUBLANES = 8
LANES = 128
N_SLAB = 4
SLAB_W = 1024
HALF = 512
E_BLK = 512
ROW_TILE = D_MODEL // LANES
PACK_ROWS = HALF // LANES
HI16_MASK = -65536
DMA_THREADS = 2
NEG_BIG = -1e30
VMEM_LIMIT = 56 * 1024 * 1024


def _dot(a, b):
    return jnp.dot(a, b, preferred_element_type=F32)


def _layer_norm(x, g, b):
    mu = jnp.mean(x, axis=-1, keepdims=True)
    xc = x - mu
    var = jnp.mean(xc * xc, axis=-1, keepdims=True)
    return xc * lax.rsqrt(var + LN_EPS) * g + b


def _ada_kernel(c_ref, w_ref, b_ref, o_ref):
    c = c_ref[...]
    s = c * jax.nn.sigmoid(c)
    o_ref[...] = _dot(s.astype(BF16), w_ref[...].astype(BF16)) + b_ref[...]


def _ada_call(c_all, w_ada, b_ada):
    rows = c_all.shape[0]
    n_out = w_ada.shape[1]
    tn = 768
    return pl.pallas_call(
        _ada_kernel,
        grid=(n_out // tn,),
        in_specs=[pl.BlockSpec((rows, D_MODEL), lambda i: (0, 0)),
                  pl.BlockSpec((D_MODEL, tn), lambda i: (0, i)),
                  pl.BlockSpec((1, tn), lambda i: (0, i))],
        out_specs=pl.BlockSpec((rows, tn), lambda i: (0, i)),
        out_shape=jax.ShapeDtypeStruct((rows, n_out), F32),
        compiler_params=pltpu.CompilerParams(dimension_semantics=("arbitrary",),
                                             vmem_limit_bytes=VMEM_LIMIT),
        name="ada_mod",
    )(c_all, w_ada, b_ada)


def _mixer_kernel(nseg, seg, carry_tiles, n_alias, cap_rows, tile0,
                  x_ref, xn_ref, mod_ref, modn_ref, h0_ref, cbuf_ref, win_ref, bmat_ref, cmat_ref, tab_ref, d_ref,
                  wglu_ref, bglu_ref, convw_ref, wout_ref, g1_ref, b1_ref, wr_ref, br_ref, cnt0_ref, *rest):
    rest = rest[n_alias:]
    (x1_ref, gate_ref, xs_hbm, cnt_out_ref, sout_ref, cout_ref,
     p_ref, bu_ref, hs_ref, xcs_ref, carry_ref, h2s_ref, posv_ref, poss_ref, cnt_ref, sem_s, sem_v) = rest
    g_id = pl.program_id(0)
    j = pl.program_id(1)
    step = g_id * pl.num_programs(1) + j
    is_last = step == pl.num_programs(0) * pl.num_programs(1) - 1
    slot = step % 2

    def rows_of(m_ref, k):
        if nseg == 1:
            return m_ref[0, k:k + 1, :]
        return jnp.concatenate(
            [jnp.broadcast_to(m_ref[s, k:k + 1, :], (seg, D_MODEL)) for s in range(nseg)], axis=0)

    def mod_rows(k):
        return rows_of(mod_ref, k)

    def front_pieces(xt_ref, m_ref, dst):
        def adaln():
            h = xt_ref[0] * (1.0 + rows_of(m_ref, 1)) + rows_of(m_ref, 0)
            hs_ref[...] = h.astype(BF16)

        def in_proj(c):
            cols = slice(c * HALF, (c + 1) * HALF)
            p_ref[dst, :, cols] = _dot(hs_ref[...], win_ref[:, cols])

        def s5_in(i):
            u_i = p_ref[dst, :, i * LANES:(i + 1) * LANES]
            row_in_tile = lax.broadcasted_iota(jnp.int32, u_i.shape, 0) % SUBLANES
            u_prev = jnp.where(row_in_tile == 0, 0.0, pltpu.roll(u_i, 1, 0))
            lhs = jnp.concatenate([u_i, u_prev], axis=1).astype(BF16)
            bu_ref[dst * N_SLAB + i] = _dot(lhs, bmat_ref[i])

        return ([adaln] + [functools.partial(in_proj, c) for c in range(2 * D_MODEL // HALF)]
                + [functools.partial(s5_in, i) for i in range(N_SLAB)])

    @pl.when(step == 0)
    def _():
        for piece in front_pieces(x_ref, mod_ref, 0):
            piece()

    nxt_front = front_pieces(xn_ref, modn_ref, 1 - slot)

    def run_front(n):
        for _ in range(n):
            nxt_front.pop(0)()

    x = x_ref[0]
    bu0 = slot * N_SLAB

    prev = 1 - slot
    smem_copy = pltpu.make_async_copy(posv_ref, poss_ref, sem_v.at[0])

    @pl.when(step == 0)
    def _():
        k_i = lax.broadcasted_iota(jnp.int32, (SUBLANES, TL), 0)
        t_i = lax.broadcasted_iota(jnp.int32, (SUBLANES, TL), 1)
        posv_ref[...] = (N_EXPERTS * cap_rows + k_i * TL + t_i) * ROW_TILE
        smem_copy.start()
        h2s_ref[...] = jnp.zeros(h2s_ref.shape, F32)

    smem_copy.wait()

    def issue_rows(src_slot, t0, n_rows):
        for q in range(n_rows):
            t = t0 + q
            src = pl.multiple_of(t * ROW_TILE, ROW_TILE)
            for k in range(TOP_K):
                dst = pl.multiple_of(poss_ref[k, t], ROW_TILE)
                pltpu.make_async_copy(h2s_ref.at[src_slot, pl.ds(src, ROW_TILE), :],
                                      xs_hbm.at[pl.ds(dst, ROW_TILE), :],
                                      sem_s.at[src_slot]).start(priority=k % DMA_THREADS)

    def wait_scatter(sl):
        for _ in range(TOP_K):
            pltpu.make_async_copy(h2s_ref.at[sl], xs_hbm.at[pl.ds(0, TL * ROW_TILE), :], sem_s.at[sl]).wait()

    if carry_tiles:
        @pl.when(j == 0)
        def _():
            carry_ref[...] = h0_ref[0]

    for s in range(nseg):
        if carry_tiles:
            init = tuple(carry_ref[i:i + 1, :] for i in range(N_SLAB))
        else:
            init = tuple(h0_ref[s, i:i + 1, :] for i in range(N_SLAB))

        def scan_body(r, carry, s=s):
            issue_rows(prev, s * seg + r * (2 * SUBLANES), 2 * SUBLANES)
            new = []
            for i in range(N_SLAB):
                cr = carry[i][:, :HALF]
                ci = carry[i][:, HALF:]
                for half in range(2):
                    row0 = pl.multiple_of(s * seg + r * (2 * SUBLANES) + half * SUBLANES, SUBLANES)
                    blk = bu_ref[bu0 + i, pl.ds(row0, SUBLANES), :]
                    xr = blk[:, :HALF]
                    xi = blk[:, HALF:]
                    for k, d in enumerate((2, 4)):
                        lr = tab_ref[i, 2 * k]
                        li = tab_ref[i, 2 * k + 1]
                        rr = pltpu.roll(xr, d, 0)
                        ri = pltpu.roll(xi, d, 0)
                        xr, xi = xr + (lr * rr - li * ri), xi + (lr * ri + li * rr)
                    pr = tab_ref[i, 4]
                    pi_ = tab_ref[i, 5]
                    crb = jnp.broadcast_to(cr, (SUBLANES, HALF))
                    cib = jnp.broadcast_to(ci, (SUBLANES, HALF))
                    xr, xi = xr + (pr * crb - pi_ * cib), xi + (pr * cib + pi_ * crb)
                    bu_ref[bu0 + i, pl.ds(row0, SUBLANES), :] = jnp.concatenate([xr, xi], axis=1)
                    cr = xr[SUBLANES - 1:SUBLANES, :]
                    ci = xi[SUBLANES - 1:SUBLANES, :]
                new.append(jnp.concatenate([cr, ci], axis=1))
            return tuple(new)

        final = lax.fori_loop(0, seg // (2 * SUBLANES), scan_body, init)
        for i in range(N_SLAB):
            sout_ref[s, i:i + 1, :] = final[i]
            if carry_tiles:
                carry_ref[i:i + 1, :] = final[i]

    y = jnp.concatenate([_dot(bu_ref[bu0 + i].astype(BF16), cmat_ref[i]) for i in range(N_SLAB)], axis=1)
    y = y + d_ref[...] * p_ref[slot, :, :D_S5]
    z = jax.nn.gelu(y)
    z = z * jax.nn.sigmoid(_dot(z.astype(BF16), wglu_ref[...]) + bglu_ref[...])

    xc = p_ref[slot, :, D_S5 + D_CONV:D_S5 + 2 * D_CONV] * p_ref[slot, :, D_S5 + 2 * D_CONV:]
    w0 = convw_ref[0:1, :]
    w1 = convw_ref[1:2, :]
    w2 = convw_ref[2:3, :]
    conv_parts = []
    for s in range(nseg):
        base = s * (seg + SUBLANES)
        xc_s = xc[s * seg:(s + 1) * seg]
        tail = xc_s[seg - 2:seg]
        xcs_ref[pl.ds(base + SUBLANES, seg), :] = xc_s
        if carry_tiles:
            @pl.when(j == 0)
            def _(s=s, base=base):
                xcs_ref[pl.ds(base + SUBLANES - 2, 2), :] = cbuf_ref[s]
        else:
            xcs_ref[pl.ds(base + SUBLANES - 2, 2), :] = cbuf_ref[s]
        xm1 = xcs_ref[pl.ds(base + SUBLANES - 1, seg), :]
        xm2 = xcs_ref[pl.ds(base + SUBLANES - 2, seg), :]
        conv_parts.append(w0 * xm2 + w1 * xm1 + w2 * xc_s)
        cout_ref[s] = tail
        if carry_tiles:
            xcs_ref[pl.ds(base + SUBLANES - 2, 2), :] = tail
    conv = conv_parts[0] if nseg == 1 else jnp.concatenate(conv_parts, axis=0)
    y_b = p_ref[slot, :, D_S5:D_S5 + D_CONV] * conv

    run_front(2)
    mix = _dot(jnp.concatenate([z, y_b], axis=1).astype(BF16), wout_ref[...])
    x1 = _layer_norm(DEEPNORM_ALPHA * x + (1.0 + mod_rows(2)) * mix, g1_ref[...], b1_ref[...])
    x1_ref[...] = x1
    run_front(1)
    h2 = x1 * (1.0 + mod_rows(4)) + mod_rows(3)
    lo_bits = lax.bitcast_convert_type(h2[:, :HALF].astype(BF16).astype(F32), jnp.int32)
    hi_bits = lax.bitcast_convert_type(h2[:, HALF:].astype(BF16).astype(F32), jnp.int32)
    packed = lax.bitcast_convert_type(hi_bits | lax.shift_right_logical(lo_bits, 16), F32)
    for c in range(PACK_ROWS):
        h2s_ref[slot, pl.ds(c, TL, stride=ROW_TILE), :] = packed[:, c * LANES:(c + 1) * LANES]

    logits = _dot(h2.astype(BF16), wr_ref[...]) + br_ref[...]
    lane = lax.broadcasted_iota(jnp.int32, logits.shape, 1)
    lane_f = lane.astype(F32)
    vals, sels, hots = [], [], []
    cur = logits
    for _ in range(TOP_K):
        m = jnp.max(cur, axis=-1, keepdims=True)
        am = jnp.min(jnp.where(cur == m, lane_f, float(LANES)), axis=-1, keepdims=True)
        hot = lane_f == am
        vals.append(m)
        sels.append(am)
        hots.append(hot)
        cur = jnp.where(hot, -jnp.inf, cur)
    exps = [jnp.exp(v - vals[0]) for v in vals]
    inv = 1.0 / (exps[0] + exps[1] + exps[2] + exps[3])
    gate_out = jnp.zeros(logits.shape, F32)
    for k in range(TOP_K):
        gate_out = jnp.where(lane == k, exps[k] * inv, gate_out)
    gate_ref[...] = gate_out
    run_front(1)

    tok = ((tile0 + step) * TL + lax.broadcasted_iota(jnp.int32, logits.shape, 0)).astype(F32)
    record = jnp.zeros(logits.shape, F32)
    for k in range(TOP_K):
        record = jnp.where(hots[k], tok + float(k * cap_rows), record)
    h2s_ref[slot, pl.ds(PACK_ROWS, TL, stride=ROW_TILE), :] = record
    run_front(2)

    @pl.when(step == 0)
    def _():
        cnt_ref[...] = cnt0_ref[...]

    chosen = jnp.zeros(logits.shape, F32)
    for k in range(TOP_K):
        chosen = jnp.where(hots[k], 1.0, chosen)
    r_i = lax.broadcasted_iota(jnp.int32, (TL, TL), 0)
    c_i = lax.broadcasted_iota(jnp.int32, (TL, TL), 1)
    before = jnp.where(c_i < r_i, 1.0, 0.0).astype(BF16)
    rank_base = _dot(before, chosen.astype(BF16)) + cnt_ref[...]
    pos_mat = jnp.zeros(logits.shape, F32)
    for k in range(TOP_K):
        rank_k = jnp.sum(jnp.where(hots[k], rank_base, 0.0), axis=-1, keepdims=True)
        pos_mat = jnp.where(lane == k, (sels[k] * float(cap_rows) + rank_k) * float(ROW_TILE), pos_mat)
    cnt_ref[...] = cnt_ref[...] + jnp.sum(chosen, axis=0, keepdims=True)
    cnt_out_ref[...] = cnt_ref[...]
    run_front(len(nxt_front))
    pos_t =pos_mat.T[:SUBLANES, :].astype(jnp.int32)
    posv_ref[...] = pos_t
    smem_copy.start()
    wait_scatter(prev)

    @pl.when(is_last)
    def _():
        smem_copy.wait()

        def tail_body(t8, c):
            issue_rows(slot, t8 * SUBLANES, SUBLANES)
            return c
        lax.fori_loop(0, TL // SUBLANES, tail_body, 0)
        wait_scatter(slot)


def _mixer_call(x3, mod, h0, cbuf, weights, cnt0, nseg, seg, tiles_per_group, tile0, n_tok, aliased):
    groups = x3.shape[0]
    nseq = mod.shape[0]
    carry_tiles = nseg == 1
    n_alias = len(aliased)
    cap_rows = n_tok

    def full(a):
        nd = a.ndim
        return pl.BlockSpec(a.shape, lambda g, j, nd=nd: (0,) * nd)

    def tile_map(g, j):
        return (tile0 + g * tiles_per_group + j, 0)

    def next_g(g, j):
        return jnp.where(j + 1 < tiles_per_group, g, jnp.minimum(g + 1, groups - 1))

    def next_j(g, j):
        return jnp.where(j + 1 < tiles_per_group, j + 1, 0)

    in_specs = [pl.BlockSpec((1, TL, D_MODEL), lambda g, j: (g, j, 0)),
                pl.BlockSpec((1, TL, D_MODEL), lambda g, j: (next_g(g, j), next_j(g, j), 0)),
                pl.BlockSpec((nseg, 6, D_MODEL), lambda g, j: (g, 0, 0)),
                pl.BlockSpec((nseg, 6, D_MODEL), lambda g, j: (next_g(g, j), 0, 0)),
                pl.BlockSpec((nseg, N_SLAB, SLAB_W), lambda g, j: (g, 0, 0)),
                pl.BlockSpec((nseg, CONV_W - 1, D_CONV), lambda g, j: (g, 0, 0))]
    in_specs += [full(w) for w in weights]
    in_specs += [full(cnt0)]
    in_specs += [pl.BlockSpec(memory_space=pl.ANY)] * n_alias
    out_shape = [jax.ShapeDtypeStruct((n_tok, D_MODEL), F32),
                 jax.ShapeDtypeStruct((n_tok, LANES), F32),
                 jax.ShapeDtypeStruct(((N_EXPERTS * cap_rows + TOP_K * TL) * ROW_TILE, LANES), F32),
                 jax.ShapeDtypeStruct((1, LANES), F32),
                 jax.ShapeDtypeStruct((nseq, N_SLAB, SLAB_W), F32),
                 jax.ShapeDtypeStruct((nseq, CONV_W - 1, D_CONV), F32)]
    out_specs = [pl.BlockSpec((TL, D_MODEL), tile_map),
                 pl.BlockSpec((TL, LANES), tile_map),
                 pl.BlockSpec(memory_space=pl.ANY),
                 pl.BlockSpec((1, LANES), lambda g, j: (0, 0)),
                 pl.BlockSpec((nseg, N_SLAB, SLAB_W), lambda g, j: (g, 0, 0)),
                 pl.BlockSpec((nseg, CONV_W - 1, D_CONV), lambda g, j: (g, 0, 0))]
    n_in = 6 + len(weights) + 1
    aliases = {n_in + k: k for k in range(n_alias)}
    scratch = [pltpu.VMEM((2, TL, 2 * D_MODEL), F32),
               pltpu.VMEM((2 * N_SLAB, TL, SLAB_W), F32),
               pltpu.VMEM((TL, D_MODEL), BF16),
               pltpu.VMEM((nseg * (seg + SUBLANES), D_CONV), F32),
               pltpu.VMEM((N_SLAB, SLAB_W), F32),
               pltpu.VMEM((2, TL * ROW_TILE, LANES), F32),
               pltpu.VMEM((SUBLANES, TL), jnp.int32),
               pltpu.SMEM((SUBLANES, TL), jnp.int32),
               pltpu.VMEM((1, LANES), F32),
               pltpu.SemaphoreType.DMA((2,)),
               pltpu.SemaphoreType.DMA((1,))]
    return pl.pallas_call(
        functools.partial(_mixer_kernel, nseg, seg, carry_tiles, n_alias, cap_rows, tile0),
        grid=(groups, tiles_per_group),
        in_specs=in_specs, out_specs=out_specs, out_shape=out_shape,
        scratch_shapes=scratch,
        input_output_aliases=aliases,
        compiler_params=pltpu.CompilerParams(dimension_semantics=("arbitrary", "arbitrary"),
                                             vmem_limit_bytes=VMEM_LIMIT),
        name="mixer_prompt" if carry_tiles else "mixer_sample",
    )(x3, x3, mod, mod, h0, cbuf, *weights, cnt0, *aliased)


W_CAST_ROWS = 128

def _expert_kernel(n_tok, row_ref, be_ref, nv_ref, nu_ref, xs_ref, wgu_ref, bgu_ref, wd_ref, bd_ref,
                   yk_hbm, wgu_bf, wd_bf, out_a, out_b, slotv_ref, slots_ref, sem_o, sem_v):
    i = pl.program_id(0)
    n_used = nu_ref[0]
    outs = (out_a, out_b)
    prev = jnp.maximum(i - 1, 0)
    new_expert = jnp.logical_or(i == 0, be_ref[i] != be_ref[prev])

    @pl.when(jnp.logical_and(new_expert, i < n_used))
    def _():
        def cast_gu(r, c):
            rows = pl.ds(pl.multiple_of(r * W_CAST_ROWS, W_CAST_ROWS), W_CAST_ROWS)
            wgu_bf[rows, :] = wgu_ref[0, rows, :].astype(BF16)
            return c
        lax.fori_loop(0, D_MODEL // W_CAST_ROWS, cast_gu, 0)

        def cast_d(r, c):
            rows = pl.ds(pl.multiple_of(r * W_CAST_ROWS, W_CAST_ROWS), W_CAST_ROWS)
            wd_bf[rows, :] = wd_ref[0, rows, :].astype(BF16)
            return c
        lax.fori_loop(0, D_FF // W_CAST_ROWS, cast_d, 0)

    def issue_rows(src, t0, count):
        for q in range(count):
            t = t0 + q
            dst = pl.multiple_of(slots_ref[src, 0, t], PACK_ROWS)
            row0 = pl.multiple_of(t * PACK_ROWS, PACK_ROWS)
            pltpu.make_async_copy(outs[src].at[pl.ds(row0, PACK_ROWS), :],
                                  yk_hbm.at[pl.ds(dst, PACK_ROWS), :],
                                  sem_o.at[src]).start(priority=q % DMA_THREADS)

    def wait_rows(sl):
        pltpu.make_async_copy(outs[sl], yk_hbm.at[pl.ds(0, E_BLK * PACK_ROWS), :], sem_o.at[sl]).wait()

    row = lax.broadcasted_iota(jnp.int32, (E_BLK, 1), 0)
    spare0 = float(TOP_K * n_tok)

    @pl.when(i == 0)
    def _():
        t_i = lax.broadcasted_iota(jnp.int32, (SUBLANES, E_BLK), 1)
        slotv_ref[...] = (TOP_K * n_tok + t_i) * PACK_ROWS
        pltpu.make_async_copy(slotv_ref, slots_ref.at[1], sem_v.at[0]).start()
        out_b[...] = jnp.zeros(out_b.shape, F32)

    def block(cur):
        prv = 1 - cur

        def smem_copy(dst):
            return pltpu.make_async_copy(slotv_ref, slots_ref.at[dst], sem_v.at[0])

        @pl.when(i < n_used)
        def _():
            smem_copy(prv).wait()
            record = xs_ref[pl.ds(PACK_ROWS, E_BLK, stride=ROW_TILE), :]
            pick = (lax.broadcasted_iota(jnp.int32, (SUBLANES, LANES), 1) == be_ref[i]).astype(F32)
            slot_row = lax.dot_general(pick, record, (((1,), (1,)), ((), ())),
                                       precision=lax.Precision.HIGHEST, preferred_element_type=F32)
            t_row = lax.broadcasted_iota(jnp.int32, (SUBLANES, E_BLK), 1)
            slot_row = jnp.where(t_row < nv_ref[i], slot_row, spare0 + t_row.astype(F32))
            slotv_ref[...] = (slot_row * float(PACK_ROWS)).astype(jnp.int32)
            smem_copy(cur).start()

        @pl.when(i < n_used)
        def _():
            live = row < nv_ref[i]
            words = [lax.bitcast_convert_type(xs_ref[pl.ds(c, E_BLK, stride=ROW_TILE), :], jnp.int32)
                     for c in range(PACK_ROWS)]
            lo = [lax.bitcast_convert_type(lax.shift_left(w, 16), F32) for w in words]
            hi = [lax.bitcast_convert_type(w & HI16_MASK, F32) for w in words]
            xb = jnp.where(live, jnp.concatenate(lo + hi, axis=1), 0.0).astype(BF16)

            issue_rows(prv, 0, E_BLK)
            hgu = _dot(xb, wgu_bf[...]) + bgu_ref[0]
            g = jnp.minimum(hgu[:, :D_FF], SWIGLU_LIMIT)
            up = jnp.clip(hgu[:, D_FF:], -SWIGLU_LIMIT, SWIGLU_LIMIT)
            act = (up + 1.0) * (g * jax.nn.sigmoid(SWIGLU_ALPHA * g))
            y = _dot(act.astype(BF16), wd_bf[...]) + bd_ref[0]

            @pl.when(i > 0)
            def _():
                wait_rows(cur)

            y_lo = lax.bitcast_convert_type(y[:, :HALF].astype(BF16).astype(F32), jnp.int32)
            y_hi = lax.bitcast_convert_type(y[:, HALF:].astype(BF16).astype(F32), jnp.int32)
            y_pk = lax.bitcast_convert_type(y_hi | lax.shift_right_logical(y_lo, 16), F32)
            for c in range(PACK_ROWS):
                outs[cur][pl.ds(c, E_BLK, stride=PACK_ROWS), :] = y_pk[:, c * LANES:(c + 1) * LANES]

        @pl.when(i == n_used)
        def _():
            smem_copy(prv).wait()

            def tail(t8, c):
                issue_rows(prv, t8 * SUBLANES, SUBLANES)
                return c
            lax.fori_loop(0, E_BLK // SUBLANES, tail, 0)
            wait_rows(cur)
            wait_rows(prv)

    for parity in range(2):
        @pl.when(i % 2 == parity)
        def _(parity=parity):
            block(parity)


def _expert_call(blk_row, blk_e, blk_valid, n_used, xs, w_gu, b_gu, w_down, b_down, n_tok):
    nb = blk_row.shape[0]
    grid_spec = pltpu.PrefetchScalarGridSpec(
        num_scalar_prefetch=4,
        grid=(nb,),
        in_specs=[
            pl.BlockSpec((E_BLK * ROW_TILE, LANES), lambda i, br, be, nv, nu: (br[i], 0)),
            pl.BlockSpec((1, D_MODEL, 2 * D_FF), lambda i, br, be, nv, nu: (be[i], 0, 0)),
            pl.BlockSpec((1, 1, 2 * D_FF), lambda i, br, be, nv, nu: (be[i], 0, 0)),
            pl.BlockSpec((1, D_FF, D_MODEL), lambda i, br, be, nv, nu: (be[i], 0, 0)),
            pl.BlockSpec((1, 1, D_MODEL), lambda i, br, be, nv, nu: (be[i], 0, 0)),
        ],
        out_specs=pl.BlockSpec(memory_space=pl.ANY),
        scratch_shapes=[pltpu.VMEM((D_MODEL, 2 * D_FF), BF16),
                        pltpu.VMEM((D_FF, D_MODEL), BF16),
                        pltpu.VMEM((E_BLK * PACK_ROWS, LANES), F32),
                        pltpu.VMEM((E_BLK * PACK_ROWS, LANES), F32),
                        pltpu.VMEM((SUBLANES, E_BLK), jnp.int32),
                        pltpu.SMEM((2, SUBLANES, E_BLK), jnp.int32),
                        pltpu.SemaphoreType.DMA((2,)),
                        pltpu.SemaphoreType.DMA((1,))],
    )
    return pl.pallas_call(
        functools.partial(_expert_kernel, n_tok),
        grid_spec=grid_spec,
        out_shape=jax.ShapeDtypeStruct(((TOP_K * n_tok + E_BLK) * PACK_ROWS, LANES), F32),
        compiler_params=pltpu.CompilerParams(dimension_semantics=("arbitrary",),
                                             vmem_limit_bytes=VMEM_LIMIT),
        name="experts",
    )(blk_row, blk_e, blk_valid, n_used, xs, w_gu, b_gu, w_down, b_down)


def _combine_kernel(nseg, seg, y0_ref, y1_ref, y2_ref, y3_ref, x1_ref, gate_ref, mod_ref, g2_ref, b2_ref,
                    out_ref):
    gates = gate_ref[...]
    ffn = None
    for k, y_ref in enumerate((y0_ref, y1_ref, y2_ref, y3_ref)):
        words = [lax.bitcast_convert_type(y_ref[pl.ds(c, C_TL, stride=PACK_ROWS), :], jnp.int32)
                 for c in range(PACK_ROWS)]
        rows = jnp.concatenate(
            [lax.bitcast_convert_type(lax.shift_left(w, 16), F32) for w in words]
            + [lax.bitcast_convert_type(w & HI16_MASK, F32) for w in words], axis=1)
        term = gates[:, k:k + 1] * rows
        ffn = term if ffn is None else ffn + term
    if nseg == 1:
        gate2 = mod_ref[0, 5:6, :]
    else:
        gate2 = jnp.concatenate(
            [jnp.broadcast_to(mod_ref[s, 5:6, :], (seg, D_MODEL)) for s in range(nseg)], axis=0)
    xa = DEEPNORM_ALPHA * x1_ref[...] + (1.0 + gate2) * ffn
    out_ref[...] = _layer_norm(xa, g2_ref[...], b2_ref[...])


def _combine_call(yk, x1, gates, mod, ln2_g, ln2_b, nseg, seg, n_tiles, tile0, tiles_per_group, n_tok):
    tiles_total = n_tok // C_TL

    def yk_spec(k):
        return pl.BlockSpec((C_TL * PACK_ROWS, LANES), lambda i, k=k: (k * tiles_total + tile0 + i, 0))

    in_specs = [yk_spec(k) for k in range(TOP_K)] + [
        pl.BlockSpec((C_TL, D_MODEL), lambda i: (tile0 + i, 0)),
        pl.BlockSpec((C_TL, LANES), lambda i: (tile0 + i, 0)),
        pl.BlockSpec((nseg, 6, D_MODEL), lambda i: (i // tiles_per_group, 0, 0)),
        pl.BlockSpec((1, D_MODEL), lambda i: (0, 0)),
        pl.BlockSpec((1, D_MODEL), lambda i: (0, 0)),
    ]
    return pl.pallas_call(
        functools.partial(_combine_kernel, nseg, seg),
        grid=(n_tiles,),
        in_specs=in_specs,
        out_specs=pl.BlockSpec((C_TL, D_MODEL), lambda i: (i, 0)),
        out_shape=jax.ShapeDtypeStruct((n_tiles * C_TL, D_MODEL), F32),
        compiler_params=pltpu.CompilerParams(dimension_semantics=("arbitrary",),
                                             vmem_limit_bytes=VMEM_LIMIT),
        name="combine_prompt" if nseg == 1 else "combine_sample",
    )(yk, yk, yk, yk, x1, gates, mod, ln2_g, ln2_b)


def _s5_tables(lam_re, lam_im, log_dt, b_re, b_im, c_re, c_im):
    dt = jnp.exp(log_dt.astype(F32))[:, None]
    lam = lax.complex(lam_re.astype(F32), lam_im.astype(F32))
    lam_dt = lam * dt
    lam_bar = jnp.exp(lam_dt)
    b_bar = ((lam_bar - 1.0) / lam)[..., None] * lax.complex(b_re.astype(F32), b_im.astype(F32))
    gl = S5_GROUPS // N_SLAB
    eye = jnp.eye(gl, dtype=F32)

    def b_slab(part):
        a = part.reshape(N_SLAB, gl, S5_STATE, S5_GROUP)
        return jnp.einsum('sgph,gk->sghkp', a, eye).reshape(N_SLAB, gl * S5_GROUP, gl * S5_STATE)

    def b_rows(b):
        return jnp.concatenate([b_slab(b.real), b_slab(b.imag)], axis=-1)

    bmat = jnp.concatenate([b_rows(b_bar), b_rows(lam_bar[..., None] * b_bar)], axis=1).astype(BF16)

    def c_slab(part):
        a = part.reshape(N_SLAB, gl, S5_GROUP, S5_STATE)
        return jnp.einsum('sghp,gk->sgpkh', a, eye).reshape(N_SLAB, gl * S5_STATE, gl * S5_GROUP)

    cmat = jnp.concatenate([c_slab(c_re.astype(F32)), -c_slab(c_im.astype(F32))], axis=1).astype(BF16)

    row = jnp.arange(SUBLANES, dtype=F32)[:, None, None]

    def power(k):
        return jnp.exp(lam_dt[None] * k)

    tabs = []
    for d in (2, 4):
        pw = power(jnp.full_like(row, float(d)))
        mask = (row >= d).astype(F32)
        tabs += [pw.real * mask, pw.imag * mask]
    pw = power(row + 1.0)
    tabs += [pw.real, pw.imag]
    tab = jnp.stack(tabs, axis=0)
    tab = tab.reshape(len(tabs), SUBLANES, N_SLAB, gl * S5_STATE).transpose(2, 0, 1, 3)
    return bmat, cmat, tab


def _state_to_slab(re, im):
    n = re.shape[0]
    return jnp.concatenate([re.reshape(n, N_SLAB, HALF), im.reshape(n, N_SLAB, HALF)], axis=-1).astype(F32)


def _slab_to_state(s):
    n = s.shape[0]
    re = s[:, :, :HALF].reshape(n, S5_GROUPS, S5_STATE)
    im = s[:, :, HALF:].reshape(n, S5_GROUPS, S5_STATE)
    return re, im


def _block_table(counts, n_tok):
    n_assign = n_tok * TOP_K
    n_blocks = (n_assign + N_EXPERTS * (E_BLK - 1) + E_BLK - 1) // E_BLK + 1
    cap_blocks = n_tok // E_BLK
    nblk = (counts + E_BLK - 1) // E_BLK
    cum = jnp.cumsum(nblk)
    n_used = cum[-1]
    i = jnp.arange(n_blocks, dtype=jnp.int32)
    ii = jnp.maximum(jnp.minimum(i, n_used - 1), 0)
    before = (cum[None, :] <= ii[:, None]).astype(jnp.int32)
    e = jnp.minimum(jnp.sum(before, axis=1), N_EXPERTS - 1)
    start_e = jnp.sum(before * nblk[None, :], axis=1)
    is_e = jnp.concatenate([jnp.ones((n_blocks, 1), jnp.int32), before[:, :-1]], axis=1) - before
    count_e = jnp.sum(is_e * counts[None, :], axis=1)
    b = ii - start_e
    blk_row = (e * cap_blocks + b).astype(jnp.int32)
    valid = jnp.where(i < n_used, jnp.minimum(E_BLK, count_e - b * E_BLK), 0).astype(jnp.int32)
    return blk_row, e, valid, n_used.astype(jnp.int32).reshape(1)


def kernel(x_prompt, x_sample, c_prompt, c_sample, state_s5_re, state_s5_im, state_conv, w_ada, b_ada, w_in,
           s5_lam_re, s5_lam_im, s5_log_dt, s5_b_re, s5_b_im, s5_c_re, s5_c_im, s5_d, w_glu, b_glu, conv_w,
           w_out, ln1_g, ln1_b, w_router, b_router, w_gu, b_gu, w_down, b_down, ln2_g, ln2_b):
    assert DEPTH == 1 and w_ada.shape[0] == 1
    n_p, l_p, _ = x_prompt.shape
    n_s, l_s, _ = x_sample.shape
    t_p, t_s = n_p * l_p, n_s * l_s
    n_tok = t_p + t_s
    tiles_p = t_p // TL
    seq_per_tile = TL // l_s
    assert l_p % TL == 0 and TL % l_s == 0 and n_s % seq_per_tile == 0

    c_all = jnp.concatenate([c_prompt, c_sample], axis=0).astype(F32)
    pad = (-c_all.shape[0]) % SUBLANES
    c_all = jnp.pad(c_all, ((0, pad), (0, 0)))
    mod = _ada_call(c_all, w_ada[0], b_ada[0].reshape(1, -1)).reshape(-1, 6, D_MODEL)
    mod_p, mod_s = mod[:n_p], mod[n_p:n_p + n_s]

    bmat, cmat, tab = _s5_tables(s5_lam_re[0], s5_lam_im[0], s5_log_dt[0], s5_b_re[0], s5_b_im[0],
                                 s5_c_re[0], s5_c_im[0])
    wr = jnp.pad(w_router[0], ((0, 0), (0, LANES - N_EXPERTS))).astype(BF16)
    br = jnp.pad(b_router[0].astype(F32), (0, LANES - N_EXPERTS), constant_values=NEG_BIG).reshape(1, LANES)
    weights = (w_in[0].astype(BF16), bmat, cmat, tab, s5_d[0].reshape(1, D_S5).astype(F32),
               w_glu[0].astype(BF16), b_glu[0].reshape(1, D_S5).astype(F32), conv_w[0].astype(F32),
               w_out[0].astype(BF16), ln1_g[0].reshape(1, D_MODEL).astype(F32),
               ln1_b[0].reshape(1, D_MODEL).astype(F32), wr, br)

    h0_p = jnp.zeros((n_p, N_SLAB, SLAB_W), F32)
    cb_p = jnp.zeros((n_p, CONV_W - 1, D_CONV), F32)
    h0_s = _state_to_slab(state_s5_re[0], state_s5_im[0])
    cb_s = state_conv[0].astype(F32)

    assert n_tok % E_BLK == 0 and E_BLK % TL == 0
    cnt0 = jnp.zeros((1, LANES), F32)
    outs_p = _mixer_call(x_prompt, mod_p, h0_p, cb_p, weights, cnt0, 1, TL, l_p // TL, 0, n_tok, ())
    x1, gates, xs, cnt_p, s_p, conv_p = outs_p
    xs3 = x_sample.reshape(n_s // seq_per_tile, TL, D_MODEL)
    outs_s = _mixer_call(xs3, mod_s, h0_s, cb_s, weights, cnt_p, seq_per_tile, l_s, 1, tiles_p, n_tok,
                         (x1, gates, xs))
    x1, gates, xs, cnt, s_s, conv_s = outs_s

    counts = cnt[0, :N_EXPERTS].astype(jnp.int32)
    blk_row, blk_e, blk_valid, n_used = _block_table(counts, n_tok)
    yk = _expert_call(blk_row, blk_e, blk_valid, n_used, xs, w_gu[0].astype(F32),
                      b_gu[0].reshape(N_EXPERTS, 1, 2 * D_FF).astype(F32), w_down[0].astype(F32),
                      b_down[0].reshape(N_EXPERTS, 1, D_MODEL).astype(F32), n_tok)
    g2 = ln2_g[0].reshape(1, D_MODEL).astype(F32)
    b2 = ln2_b[0].reshape(1, D_MODEL).astype(F32)
    assert l_p % C_TL == 0 and C_TL % l_s == 0 and t_s % C_TL == 0
    y_p = _combine_call(yk, x1, gates, mod_p, g2, b2, 1, C_TL, t_p // C_TL, 0, l_p // C_TL, n_tok)
    y_s = _combine_call(yk, x1, gates, mod_s, g2, b2, C_TL // l_s, l_s, t_s // C_TL, t_p // C_TL, 1, n_tok)

    p_re, p_im = _slab_to_state(s_p)
    s_re, s_im = _slab_to_state(s_s)
    return (y_p.reshape(n_p, l_p, D_MODEL), y_s.reshape(n_s, l_s, D_MODEL),
            p_re[None], p_im[None], conv_p[None], s_re[None], s_im[None], conv_s[None])
```
